```python
import math
import jax, jax.numpy as jnp
from jax import lax
import numpy as np

D_MODEL = 1024
BATCH = 8
SEQ = 4096
DEPTH = 1

CHUNK = 64
Q_BLOCK = 128

MIX_WIDTH = D_MODEL
MLA_HEADS = 8
MLA_NOPE = 64
MLA_ROPE = 32
MLA_QK = MLA_NOPE + MLA_ROPE
MLA_V = 64
MLA_WIDTH = MLA_HEADS * MLA_V
Q_LORA = 256
KV_LORA = 128
ROPE_THETA = 10000.0
SB_HEADS = 8
SB_HEAD_DIM = 64
SB_WIDTH = SB_HEADS * SB_HEAD_DIM
IN_COLS = Q_LORA + KV_LORA + MLA_ROPE + 3 * SB_WIDTH
N_EXPERTS = 64
TOP_K = 8
EXPERT_FF = 256
SHARED_FF = 256
ROUTED_SCALE = 2.5
EPS = 1e-6
NEG = -1e30

kernel_name = "hybrid_mla_stickbreaking_moe_block"


def rmsnorm(x, g):
    xf = x.astype(jnp.float32)
    y = xf * lax.rsqrt(jnp.mean(xf * xf, axis=-1, keepdims=True) + EPS)
    return (y * g.astype(jnp.float32)).astype(x.dtype)


def rope(x, pos):
    half = x.shape[-1] // 2
    inv = ROPE_THETA ** (-jnp.arange(half, dtype=jnp.float32) / half)
    ang = pos.astype(jnp.float32)[..., None] * inv
    ang = ang.reshape(ang.shape[:2] + (1,) * (x.ndim - 3) + (half,))
    cos, sin = jnp.cos(ang), jnp.sin(ang)
    xf = x.astype(jnp.float32)
    x1, x2 = xf[..., :half], xf[..., half:]
    return jnp.concatenate([x1 * cos - x2 * sin, x2 * cos + x1 * sin], axis=-1).astype(x.dtype)


def to_blocks(t):
    B, H, S, d = t.shape
    return t.reshape(B, H, S // Q_BLOCK, Q_BLOCK, d).transpose(2, 0, 1, 3, 4)


def from_blocks(t):
    nb, B, H, QB, d = t.shape
    return t.transpose(1, 2, 0, 3, 4).reshape(B, H, nb * QB, d)


def mla_attention(q, k, v, pos):
    B, _, S, _ = q.shape
    scale = 1.0 / math.sqrt(MLA_QK)
    kf, vf = k.astype(jnp.float32), v.astype(jnp.float32)
    kchunk = (pos // CHUNK)[:, None, None, :]
    pb = pos.reshape(B, S // Q_BLOCK, Q_BLOCK).transpose(1, 0, 2)

    def step(args):
        qi, pi = args
        s = jnp.einsum('bhqd,bhkd->bhqk', qi.astype(jnp.float32), kf) * scale
        mask = kchunk <= (pi // CHUNK)[:, None, :, None]
        p = jax.nn.softmax(jnp.where(mask, s, NEG), axis=-1)
        return jnp.einsum('bhqk,bhkd->bhqd', p, vf)

    return from_blocks(lax.map(step, (to_blocks(q), pb)))


def stick_breaking_attention(q, k, v, pos):
    B, _, S, _ = q.shape
    scale = 1.0 / math.sqrt(SB_HEAD_DIM)
    kf, vf = k.astype(jnp.float32), v.astype(jnp.float32)
    kpos = pos[:, None, None, :]
    pb = pos.reshape(B, S // Q_BLOCK, Q_BLOCK).transpose(1, 0, 2)

    def step(args):
        qi, pi = args
        z = jnp.einsum('bhqd,bhkd->bhqk', qi.astype(jnp.float32), kf) * scale
        causal = kpos < pi[:, None, :, None]
        log_1m = jnp.where(causal, jax.nn.log_sigmoid(-z), 0.0)
        suffix = lax.cumsum(log_1m, axis=3, reverse=True)
        log_a = jax.nn.log_sigmoid(z) + (suffix - log_1m)
        a = jnp.where(causal, jnp.exp(log_a), 0.0)
        return jnp.einsum('bhqk,bhkd->bhqd', a, vf)

    return from_blocks(lax.map(step, (to_blocks(q), pb)))


def moe(h, w_router, router_bias, w_gate_up, w_down, w_shared_gate_up, w_shared_down):
    B, S, D = h.shape
    xf = h.reshape(B * S, D)
    scores = jax.nn.sigmoid((xf @ w_router).astype(jnp.float32))
    _, idx = lax.top_k(scores + router_bias.astype(jnp.float32), TOP_K)
    gates = jnp.take_along_axis(scores, idx, axis=-1)
    gates = gates / jnp.sum(gates, axis=-1, keepdims=True) * ROUTED_SCALE
    flat_e = idx.reshape(-1)
    order = jnp.argsort(flat_e)
    tok = order // TOP_K
    sizes = jnp.bincount(flat_e, length=N_EXPERTS).astype(jnp.int32)
    xs = xf[tok]
    gu = lax.ragged_dot(xs, w_gate_up, sizes)
    act = jax.nn.silu(gu[:, :EXPERT_FF]) * gu[:, EXPERT_FF:]
    y = lax.ragged_dot(act, w_down, sizes)
    y = y * gates.reshape(-1)[order][:, None].astype(y.dtype)
    routed = jnp.zeros_like(xf).at[tok].add(y.astype(xf.dtype))
    sgu = xf @ w_shared_gate_up
    shared = (jax.nn.silu(sgu[:, :SHARED_FF]) * sgu[:, SHARED_FF:]) @ w_shared_down
    return (routed + shared).reshape(B, S, D)


def setup_inputs(seed: int = 0) -> dict:
    key = jax.random.key(seed)
    ks = jax.random.split(key, 24)
    f32 = jnp.float32

    def nrm(k, shape, fan_in):
        return jax.random.normal(k, shape, f32) * fan_in ** -0.5

    def gain(k, n):
        return 1.0 + 0.02 * jax.random.normal(k, (n,), f32)

    x = jax.random.normal(ks[0], (BATCH, SEQ, D_MODEL), f32)
    offset = jax.random.randint(ks[1], (BATCH, 1), 0, 64, dtype=jnp.int32) * CHUNK
    positions = (offset + jnp.arange(SEQ, dtype=jnp.int32)[None, :]).astype(jnp.int32)
    return {
        "x": x,
        "positions": positions,
        "norm1": gain(ks[2], D_MODEL),
        "w_in": nrm(ks[3], (D_MODEL, IN_COLS), D_MODEL),
        "q_a_norm": gain(ks[4], Q_LORA),
        "w_uq": nrm(ks[5], (Q_LORA, MLA_HEADS * MLA_QK), Q_LORA),
        "kv_a_norm": gain(ks[6], KV_LORA),
        "w_ukv": nrm(ks[7], (KV_LORA, MLA_HEADS * (MLA_NOPE + MLA_V)), KV_LORA),
        "q_norm": gain(ks[8], MLA_QK),
        "k_norm": gain(ks[9], MLA_QK),
        "out_norm_mla": gain(ks[10], MLA_WIDTH),
        "out_norm_sb": gain(ks[11], SB_WIDTH),
        "w_o": nrm(ks[12], (MIX_WIDTH, D_MODEL), MIX_WIDTH),
        "norm2": gain(ks[13], D_MODEL),
        "w_router": nrm(ks[14], (D_MODEL, N_EXPERTS), D_MODEL),
        "router_bias": 0.01 * jax.random.normal(ks[15], (N_EXPERTS,), f32),
        "w_gate_up": nrm(ks[16], (N_EXPERTS, D_MODEL, 2 * EXPERT_FF), D_MODEL),
        "w_down": nrm(ks[17], (N_EXPERTS, EXPERT_FF, D_MODEL), EXPERT_FF),
        "w_shared_gate_up": nrm(ks[18], (D_MODEL, 2 * SHARED_FF), D_MODEL),
        "w_shared_down": nrm(ks[19], (SHARED_FF, D_MODEL), SHARED_FF),
    }


def reference(x, positions, norm1, w_in, q_a_norm, w_uq, kv_a_norm, w_ukv, q_norm, k_norm,
              out_norm_mla, out_norm_sb, w_o, norm2, w_router, router_bias, w_gate_up, w_down,
              w_shared_gate_up, w_shared_down):
    B, S, _ = x.shape
    h = x
    for _layer in range(DEPTH):
        xn = rmsnorm(h, norm1)
        proj = xn @ w_in
        c0 = Q_LORA
        c1 = c0 + KV_LORA
        c2 = c1 + MLA_ROPE
        c3 = c2 + SB_WIDTH
        c4 = c3 + SB_WIDTH
        cq = rmsnorm(proj[..., :c0], q_a_norm)
        q = (cq @ w_uq).reshape(B, S, MLA_HEADS, MLA_QK)
        ckv = rmsnorm(proj[..., c0:c1], kv_a_norm)
        kv = (ckv @ w_ukv).reshape(B, S, MLA_HEADS, MLA_NOPE + MLA_V)
        k_nope, v_mla = kv[..., :MLA_NOPE], kv[..., MLA_NOPE:]
        k_rope = jnp.broadcast_to(proj[..., None, c1:c2], (B, S, MLA_HEADS, MLA_ROPE))
        k = jnp.concatenate([k_nope, k_rope], axis=-1)
        q = rmsnorm(q, q_norm)
        k = rmsnorm(k, k_norm)
        q = jnp.concatenate([q[..., :MLA_NOPE], rope(q[..., MLA_NOPE:], positions)], axis=-1)
        k = jnp.concatenate([k[..., :MLA_NOPE], rope(k[..., MLA_NOPE:], positions)], axis=-1)
        o_mla = mla_attention(q.transpose(0, 2, 1, 3), k.transpose(0, 2, 1, 3),
                              v_mla.transpose(0, 2, 1, 3), positions)
        o_mla = o_mla.transpose(0, 2, 1, 3).reshape(B, S, MLA_WIDTH).astype(h.dtype)
        def heads(t):
            return t.reshape(B, S, SB_HEADS, SB_HEAD_DIM).transpose(0, 2, 1, 3)
        o_sb = stick_breaking_attention(heads(proj[..., c2:c3]), heads(proj[..., c3:c4]),
                                        heads(proj[..., c4:]), positions)
        o_sb = o_sb.transpose(0, 2, 1, 3).reshape(B, S, SB_WIDTH).astype(h.dtype)
        mixed = jnp.concatenate([rmsnorm(o_mla, out_norm_mla), rmsnorm(o_sb, out_norm_sb)], axis=-1)
        h = h + mixed @ w_o
        h = h + moe(rmsnorm(h, norm2), w_router, router_bias, w_gate_up, w_down,
                    w_shared_gate_up, w_shared_down)
    return h
```

```python
import functools
import math

import jax
import jax.numpy as jnp
from jax import lax
from jax.experimental import pallas as pl
from jax.experimental.pallas import tpu as pltpu

D_MODEL = 1024
CHUNK = 64
MLA_HEADS = 8
MLA_NOPE = 64
MLA_ROPE = 32
MLA_QK = MLA_NOPE + MLA_ROPE
MLA_V = 64
Q_LORA = 256
KV_LORA = 128
ROPE_THETA = 10000.0
SB_HEADS = 8
SB_HEAD_DIM = 64
SB_WIDTH = SB_HEADS * SB_HEAD_DIM
MLA_WIDTH = MLA_HEADS * MLA_V
N_EXPERTS = 64
TOP_K = 8
EXPERT_FF = 256
SHARED_FF = 256
ROUTED_SCALE = 2.5
EPS = 1e-6
NEG = -1e30

LANES = 128
VMEM_LIMIT_BYTES = 56 * 1024 * 1024

SLOT = LANES
HALF_ROPE = MLA_ROPE // 2

SB_LOG_ZERO = -90.0

BF16 = jnp.bfloat16
F32 = jnp.float32


def _rms(x, g):
    return x * lax.rsqrt(jnp.mean(x * x, axis=-1, keepdims=True) + EPS) * g


def _dot(a, b):
    return jnp.dot(a, b, preferred_element_type=F32)


def _dot_nt(a, b):
    return lax.dot_general(a, b, (((1,), (1,)), ((), ())), preferred_element_type=F32)


def _proj_kernel(x_ref, pos_ref, g1_ref, wcq_ref, wckv_ref, wkr_ref, wsb_ref, gqa_ref, wuq_ref,
                 gkva_ref, wuk_ref, wuv_ref, gq_ref, gk_ref, inv_ref,
                 q_ref, k_ref, v_ref, sq_ref, sk_ref, sv_ref):
    x = x_ref[...]
    xn = _rms(x, g1_ref[...]).astype(BF16)

    cq = _rms(_dot(xn, wcq_ref[...]), gqa_ref[...]).astype(BF16)
    q = _dot(cq, wuq_ref[...])
    ckv = _rms(_dot(xn, wckv_ref[...]), gkva_ref[...]).astype(BF16)
    kn = _dot(ckv, wuk_ref[...])
    v_ref[...] = _dot(ckv, wuv_ref[...]).astype(BF16)
    kr = _dot(xn, wkr_ref[...])

    sb = _dot(xn, wsb_ref[...])
    sq_ref[...] = (sb[:, :SB_WIDTH] * (1.0 / math.sqrt(SB_HEAD_DIM))).astype(BF16)
    sk_ref[...] = sb[:, SB_WIDTH:2 * SB_WIDTH].astype(BF16)
    sv_ref[...] = sb[:, 2 * SB_WIDTH:].astype(BF16)

    ang = pos_ref[...].astype(F32) * inv_ref[...]
    cos = jnp.cos(ang)
    sin = jnp.sin(ang)
    lane = lax.broadcasted_iota(jnp.int32, ang.shape, 1)
    first_half = lane < MLA_NOPE + HALF_ROPE

    def rope(t):
        up = pltpu.roll(t, SLOT - HALF_ROPE, 1)
        down = pltpu.roll(t, HALF_ROPE, 1)
        return t * cos + jnp.where(first_half, -up, down) * sin

    def head_norm(t, g):
        ss = jnp.sum(t * t, axis=-1, keepdims=True) * (1.0 / MLA_QK)
        return t * lax.rsqrt(ss + EPS) * g

    scale = 1.0 / math.sqrt(MLA_QK)
    for h in range(MLA_HEADS):
        sl = slice(h * SLOT, (h + 1) * SLOT)
        qh = rope(head_norm(q[:, sl], gq_ref[...])) * scale
        q_ref[:, sl] = qh.astype(BF16)
        kh = rope(head_norm(kn[:, sl] + kr, gk_ref[...]))
        k_ref[:, sl] = kh.astype(BF16)


def _proj_call(x2, pos_col, g1, wcq, wckv, wkr, wsb, gqa, wuq, gkva, wuk, wuv, gq, gk, inv, tm):
    T = x2.shape[0]
    full = lambda a: pl.BlockSpec(a.shape, lambda i: (0,) * a.ndim)
    row = lambda w: pl.BlockSpec((tm, w), lambda i: (i, 0))
    weights = (g1, wcq, wckv, wkr, wsb, gqa, wuq, gkva, wuk, wuv, gq, gk, inv)
    out_w = (MLA_HEADS * SLOT, MLA_HEADS * SLOT, MLA_WIDTH, SB_WIDTH, SB_WIDTH, SB_WIDTH)
    return pl.pallas_call(
        _proj_kernel,
        grid=(T // tm,),
        in_specs=[row(D_MODEL), row(1)] + [full(w) for w in weights],
        out_specs=[row(w) for w in out_w],
        out_shape=[jax.ShapeDtypeStruct((T, w), BF16) for w in out_w],
        compiler_params=pltpu.CompilerParams(dimension_semantics=("arbitrary",),
                                             vmem_limit_bytes=VMEM_LIMIT_BYTES),
        name="proj",
    )(x2, pos_col, *weights)


def _mla_kernel(flags_ref, q_ref, k_ref, v_ref, posq_ref, posk_ref, o_ref, m_ref, l_ref, acc_ref,
                *, tq, tk, nk):
    b = pl.program_id(0)
    qi = pl.program_id(2)
    qchunk = jnp.right_shift(posq_ref[0], 6)
    m_ref[...] = jnp.full(m_ref.shape, NEG, F32)
    l_ref[...] = jnp.zeros(l_ref.shape, F32)
    acc_ref[...] = jnp.zeros(acc_ref.shape, F32)

    def block(kb, masked):
        k0 = pl.multiple_of(kb * tk, tk)
        if masked:
            kchunk = jnp.right_shift(posk_ref[0, :, pl.ds(k0, tk)], 6)
            vis = kchunk <= qchunk
        vb = v_ref[0, pl.ds(k0, tk), :]
        for h in range(2):
            qh = q_ref[0, :, h * SLOT:(h + 1) * SLOT]
            kh = k_ref[0, pl.ds(k0, tk), h * SLOT:(h + 1) * SLOT]
            s = _dot_nt(qh, kh)
            if masked:
                s = jnp.where(vis, s, NEG)
            m_old = m_ref[h]
            m_new = jnp.maximum(m_old, jnp.max(s, axis=-1, keepdims=True))
            alpha = jnp.exp(m_old - m_new)
            p = jnp.exp(s - m_new)
            l_ref[h] = alpha * l_ref[h] + jnp.sum(p, axis=-1, keepdims=True)
            acc_ref[h] = alpha * acc_ref[h] + _dot(p.astype(BF16), vb)
            m_ref[h] = m_new

    def body(kb, carry):
        flag = flags_ref[b, qi, kb]

        @pl.when(flag == 1)
        def _():
            block(kb, False)

        @pl.when(flag == 2)
        def _():
            block(kb, True)

        return carry

    lax.fori_loop(0, nk, body, 0)
    lane = lax.broadcasted_iota(jnp.int32, (tq, LANES), 1)
    o0 = acc_ref[0] / l_ref[0]
    o1 = acc_ref[1] / l_ref[1]
    o_ref[0] = jnp.where(lane < MLA_V, o0, o1)


def _mla_call(flags, q, k, v, pos_col, pos_row, tq, tk):
    B, S, _ = q.shape
    nq, nk = S // tq, S // tk
    grid_spec = pltpu.PrefetchScalarGridSpec(
        num_scalar_prefetch=1,
        grid=(B, MLA_HEADS // 2, nq),
        in_specs=[
            pl.BlockSpec((1, tq, 2 * SLOT), lambda b, p, i, f: (b, i, p)),
            pl.BlockSpec((1, S, 2 * SLOT), lambda b, p, i, f: (b, 0, p)),
            pl.BlockSpec((1, S, LANES), lambda b, p, i, f: (b, 0, p)),
            pl.BlockSpec((1, tq, 1), lambda b, p, i, f: (b, i, 0)),
            pl.BlockSpec((1, 1, S), lambda b, p, i, f: (b, 0, 0)),
        ],
        out_specs=pl.BlockSpec((1, tq, LANES), lambda b, p, i, f: (b, i, p)),
        scratch_shapes=[pltpu.VMEM((2, tq, 1), F32), pltpu.VMEM((2, tq, 1), F32),
                        pltpu.VMEM((2, tq, LANES), F32)],
    )
    return pl.pallas_call(
        functools.partial(_mla_kernel, tq=tq, tk=tk, nk=nk),
        grid_spec=grid_spec,
        out_shape=jax.ShapeDtypeStruct((B, S, MLA_WIDTH), F32),
        compiler_params=pltpu.CompilerParams(
            dimension_semantics=("arbitrary", "arbitrary", "arbitrary"),
            vmem_limit_bytes=VMEM_LIMIT_BYTES),
        name="mla_attn",
    )(flags, q, k, v, pos_col, pos_row)


def _sb_kernel(flags_ref, kstart_ref, q_ref, k_ref, v_ref, posq_ref, posk_ref, o_ref,
               run_ref, acc_ref, *, tq, tk):
    b = pl.program_id(0)
    qi = pl.program_id(2)
    qpos = posq_ref[0]
    lane = lax.broadcasted_iota(jnp.int32, (tq, LANES), 1)
    row_i = lax.broadcasted_iota(jnp.int32, (tk, tk), 0)
    col_i = lax.broadcasted_iota(jnp.int32, (tk, tk), 1)
    tri = (row_i >= col_i).astype(BF16)
    q_pair = q_ref[0]
    kstart = kstart_ref[b, qi]

    for h in range(2):
        in_head = (lane >= h * SB_HEAD_DIM) & (lane < (h + 1) * SB_HEAD_DIM)
        qh = jnp.where(in_head, q_pair, jnp.zeros_like(q_pair))
        run_ref[...] = jnp.zeros(run_ref.shape, F32)
        acc_ref[h] = jnp.zeros((tq, LANES), F32)

        def block(kb, masked):
            k0 = pl.multiple_of(kb * tk, tk)
            kb_ = k_ref[0, pl.ds(k0, tk), :]
            vb = v_ref[0, pl.ds(k0, tk), :]
            z = _dot_nt(qh, kb_)
            l1m = -(jnp.maximum(z, 0.0) + jnp.log1p(jnp.exp(-jnp.abs(z))))
            if masked:
                causal = posk_ref[0, :, pl.ds(k0, tk)] < qpos
                l1m = jnp.where(causal, l1m, 0.0)
            hi = l1m.astype(BF16)
            lo = (l1m - hi.astype(F32)).astype(BF16)
            suffix = _dot(hi, tri) + _dot(lo, tri)
            run = run_ref[...]
            a = jnp.exp(z + suffix + run)
            if masked:
                a = jnp.where(causal, a, 0.0)
            acc_ref[h] += _dot(a.astype(BF16), vb)
            run_new = run + jnp.sum(l1m, axis=-1, keepdims=True)
            run_ref[...] = run_new
            return jnp.max(run_new)

        def cond(c):
            kb, top = c
            return (kb >= 0) & (top >= SB_LOG_ZERO)

        def body(c):
            kb, top = c
            flag = flags_ref[b, qi, kb]
            top = lax.cond(flag == 0, lambda: top,
                           lambda: lax.cond(flag == 1, lambda: block(kb, False),
                                            lambda: block(kb, True)))
            return kb - 1, top

        lax.while_loop(cond, body, (kstart, jnp.float32(0.0)))

    o_ref[0] = jnp.where(lane < SB_HEAD_DIM, acc_ref[0], acc_ref[1])


def _sb_call(flags, kstart, q, k, v, pos_col, pos_row, tq, tk):
    B, S, _ = q.shape
    nq = S // tq
    grid_spec = pltpu.PrefetchScalarGridSpec(
        num_scalar_prefetch=2,
        grid=(B, SB_HEADS // 2, nq),
        in_specs=[
            pl.BlockSpec((1, tq, LANES), lambda b, p, i, f, s: (b, i, p)),
            pl.BlockSpec((1, S, LANES), lambda b, p, i, f, s: (b, 0, p)),
            pl.BlockSpec((1, S, LANES), lambda b, p, i, f, s: (b, 0, p)),
            pl.BlockSpec((1, tq, 1), lambda b, p, i, f, s: (b, i, 0)),
            pl.BlockSpec((1, 1, S), lambda b, p, i, f, s: (b, 0, 0)),
        ],
        out_specs=pl.BlockSpec((1, tq, LANES), lambda b, p, i, f, s: (b, i, p)),
        scratch_shapes=[pltpu.VMEM((tq, 1), F32), pltpu.VMEM((2, tq, LANES), F32)],
    )
    return pl.pallas_call(
        functools.partial(_sb_kernel, tq=tq, tk=tk),
        grid_spec=grid_spec,
        out_shape=jax.ShapeDtypeStruct((B, S, SB_WIDTH), F32),
        compiler_params=pltpu.CompilerParams(
            dimension_semantics=("arbitrary", "arbitrary", "arbitrary"),
            vmem_limit_bytes=VMEM_LIMIT_BYTES),
        name="sb_attn",
    )(flags, kstart, q, k, v, pos_col, pos_row)


def _split_bf16(a):
    hi = a.astype(BF16)
    lo = (a - hi.astype(F32)).astype(BF16)
    return hi, lo


def _merge_kernel(x_ref, om_ref, os_ref, gm_ref, gs_ref, wo_ref, g2_ref, wrh_ref, wrl_ref,
                  bias_ref, wsgu_ref, wsd_ref, base_ref, hn_ref, gates_ref):
    mm = _rms(om_ref[...], gm_ref[...]).astype(BF16)
    ms = _rms(os_ref[...], gs_ref[...]).astype(BF16)
    h = x_ref[...] + _dot(mm, wo_ref[:MLA_WIDTH, :]) + _dot(ms, wo_ref[MLA_WIDTH:, :])
    hn = _rms(h, g2_ref[...])
    hn_hi, hn_lo = _split_bf16(hn)
    hn_ref[...] = hn_hi

    logits = (_dot_nt(wrh_ref[...], hn_hi) + _dot_nt(wrh_ref[...], hn_lo)
              + _dot_nt(wrl_ref[...], hn_hi))
    scores = jax.nn.sigmoid(logits)
    work = scores + bias_ref[...]
    eidx = lax.broadcasted_iota(jnp.int32, work.shape, 0)
    picked = jnp.zeros(work.shape, jnp.bool_)
    for _ in range(TOP_K):
        top = jnp.max(work, axis=0, keepdims=True)
        first = jnp.min(jnp.where(work == top, eidx, N_EXPERTS), axis=0, keepdims=True)
        hit = eidx == first
        picked = picked | hit
        work = jnp.where(hit, -jnp.inf, work)
    sel = jnp.where(picked, scores, 0.0)
    gates_ref[...] = sel / jnp.sum(sel, axis=0, keepdims=True) * ROUTED_SCALE

    sgu = _dot(hn_hi, wsgu_ref[...])
    act = (jax.nn.silu(sgu[:, :SHARED_FF]) * sgu[:, SHARED_FF:]).astype(BF16)
    base_ref[...] = h + _dot(act, wsd_ref[...])


def _merge_call(x2, o_mla, o_sb, gm, gs, wo, g2, wrh, wrl, bias, wsgu, wsd, tm):
    T = x2.shape[0]
    full = lambda a: pl.BlockSpec(a.shape, lambda i: (0,) * a.ndim)
    row = lambda w: pl.BlockSpec((tm, w), lambda i: (i, 0))
    weights = (gm, gs, wo, g2, wrh, wrl, bias, wsgu, wsd)
    return pl.pallas_call(
        _merge_kernel,
        grid=(T // tm,),
        in_specs=[row(D_MODEL), row(MLA_WIDTH), row(SB_WIDTH)] + [full(w) for w in weights],
        out_specs=[row(D_MODEL), row(D_MODEL), pl.BlockSpec((N_EXPERTS, tm), lambda i: (0, i))],
        out_shape=[jax.ShapeDtypeStruct((T, D_MODEL), F32),
                   jax.ShapeDtypeStruct((T, D_MODEL), BF16),
                   jax.ShapeDtypeStruct((N_EXPERTS, T), F32)],
        compiler_params=pltpu.CompilerParams(dimension_semantics=("arbitrary",),
                                             vmem_limit_bytes=VMEM_LIMIT_BYTES),
        name="merge",
    )(x2, o_mla, o_sb, *weights)


def _moe_kernel(hn_ref, gates_ref, base_ref, wgu_ref, wd_ref, o_ref, *, eg):
    g = pl.program_id(1)

    @pl.when(g == 0)
    def _():
        o_ref[...] = base_ref[...]

    hn = hn_ref[...]
    gates = gates_ref[...]
    lane = lax.broadcasted_iota(jnp.int32, gates.shape, 1)
    acc = jnp.zeros(o_ref.shape, F32)
    for j in range(eg):
        e = g * eg + j
        gate = jnp.sum(jnp.where(lane == e, gates, 0.0), axis=1, keepdims=True)
        gu = _dot(hn, wgu_ref[j])
        act = jax.nn.silu(gu[:, :EXPERT_FF]) * gu[:, EXPERT_FF:] * gate
        acc = acc + _dot(act.astype(BF16), wd_ref[j])
    o_ref[...] += acc


def _moe_call(hn, gates, base, wgu, wd, tm, eg):
    T = hn.shape[0]
    return pl.pallas_call(
        functools.partial(_moe_kernel, eg=eg),
        grid=(T // tm, N_EXPERTS // eg),
        in_specs=[
            pl.BlockSpec((tm, D_MODEL), lambda i, g: (i, 0)),
            pl.BlockSpec((tm, N_EXPERTS), lambda i, g: (i, 0)),
            pl.BlockSpec((tm, D_MODEL), lambda i, g: (i, 0)),
            pl.BlockSpec((eg, D_MODEL, 2 * EXPERT_FF), lambda i, g: (g, 0, 0)),
            pl.BlockSpec((eg, EXPERT_FF, D_MODEL), lambda i, g: (g, 0, 0)),
        ],
        out_specs=pl.BlockSpec((tm, D_MODEL), lambda i, g: (i, 0)),
        out_shape=jax.ShapeDtypeStruct((T, D_MODEL), F32),
        compiler_params=pltpu.CompilerParams(dimension_semantics=("arbitrary", "arbitrary"),
                                             vmem_limit_bytes=VMEM_LIMIT_BYTES),
        name="moe",
    )(hn, gates, base, wgu, wd)


def _slot_cols(w, width):
    k = w.shape[0]
    w = w.reshape(k, MLA_HEADS, width)
    return jnp.pad(w, ((0, 0), (0, 0), (0, SLOT - width))).reshape(k, MLA_HEADS * SLOT)


def _block_flags(pos, tq, tk, chunked):
    B, S = pos.shape
    p = jnp.right_shift(pos, 6) if chunked else pos
    qmin = p.reshape(B, S // tq, tq).min(-1)[:, :, None]
    qmax = p.reshape(B, S // tq, tq).max(-1)[:, :, None]
    kmin = p.reshape(B, S // tk, tk).min(-1)[:, None, :]
    kmax = p.reshape(B, S // tk, tk).max(-1)[:, None, :]
    if chunked:
        none, all_ = kmin > qmax, kmax <= qmin
    else:
        none, all_ = kmin >= qmax, kmax < qmin
    return jnp.where(none, 0, jnp.where(all_, 1, 2)).astype(jnp.int32)


def kernel(x, positions, norm1, w_in, q_a_norm, w_uq, kv_a_norm, w_ukv, q_norm, k_norm,
           out_norm_mla, out_norm_sb, w_o, norm2, w_router, router_bias, w_gate_up, w_down,
           w_shared_gate_up, w_shared_down):
    B, S, D = x.shape
    T = B * S
    c0 = Q_LORA
    c1 = c0 + KV_LORA
    c2 = c1 + MLA_ROPE

    row2 = lambda g: g.reshape(1, -1).astype(F32)
    wcq = w_in[:, :c0].astype(BF16)
    wckv = w_in[:, c0:c1].astype(BF16)
    wkr = jnp.pad(w_in[:, c1:c2], ((0, 0), (MLA_NOPE, SLOT - MLA_QK))).astype(BF16)
    wsb = w_in[:, c2:].astype(BF16)
    wuq = _slot_cols(w_uq, MLA_QK).astype(BF16)
    w_ukv3 = w_ukv.reshape(KV_LORA, MLA_HEADS, MLA_NOPE + MLA_V)
    wuk = _slot_cols(w_ukv3[:, :, :MLA_NOPE].reshape(KV_LORA, -1), MLA_NOPE).astype(BF16)
    wuv = w_ukv3[:, :, MLA_NOPE:].reshape(KV_LORA, MLA_WIDTH).astype(BF16)
    gq = jnp.pad(q_norm, (0, SLOT - MLA_QK)).reshape(1, SLOT)
    gk = jnp.pad(k_norm, (0, SLOT - MLA_QK)).reshape(1, SLOT)
    inv = ROPE_THETA ** (-jnp.arange(HALF_ROPE, dtype=F32) / HALF_ROPE)
    inv_slot = jnp.pad(jnp.concatenate([inv, inv]), (MLA_NOPE, SLOT - MLA_QK)).reshape(1, SLOT)

    x2 = x.reshape(T, D)
    pos_col = positions.reshape(T, 1)
    q, k, v, sq, sk, sv = _proj_call(
        x2, pos_col, row2(norm1), wcq, wckv, wkr, wsb, row2(q_a_norm), wuq, row2(kv_a_norm),
        wuk, wuv, gq, gk, inv_slot, tm=512)

    pos_c3 = positions.reshape(B, S, 1)
    pos_r3 = positions.reshape(B, 1, S)
    r3 = lambda a: a.reshape(B, S, a.shape[-1])

    tq_m, tk_m = 256, 512
    o_mla = _mla_call(_block_flags(positions, tq_m, tk_m, True), r3(q), r3(k), r3(v),
                      pos_c3, pos_r3, tq_m, tk_m)

    tq_s = tk_s = 128
    sb_flags = _block_flags(positions, tq_s, tk_s, False)
    nk_s = S // tk_s
    kstart = jnp.max(jnp.where(sb_flags != 0, jnp.arange(nk_s, dtype=jnp.int32), -1), axis=-1)
    o_sb = _sb_call(sb_flags, kstart.astype(jnp.int32), r3(sq), r3(sk), r3(sv),
                    pos_c3, pos_r3, tq_s, tk_s)

    wr_t = w_router.T
    wrh = wr_t.astype(BF16)
    wrl = (wr_t - wrh.astype(F32)).astype(BF16)
    base, hn, gates_t = _merge_call(
        x2, o_mla.reshape(T, MLA_WIDTH), o_sb.reshape(T, SB_WIDTH), row2(out_norm_mla),
        row2(out_norm_sb), w_o.astype(BF16), row2(norm2), wrh, wrl,
        router_bias.reshape(N_EXPERTS, 1).astype(F32), w_shared_gate_up.astype(BF16),
        w_shared_down.astype(BF16), tm=512)

    out = _moe_call(hn, gates_t.T, base, w_gate_up.astype(BF16), w_down.astype(BF16),
                    tm=1024, eg=4)
    return out.reshape(B, S, D)
```

```python
import functools
import math

import jax
import jax.numpy as jnp
from jax import lax
from jax.experimental import pallas as pl
from jax.experimental.pallas import tpu as pltpu

D_MODEL = 1024
CHUNK = 64
MLA_HEADS = 8
MLA_NOPE = 64
MLA_ROPE = 32
MLA_QK = MLA_NOPE + MLA_ROPE
MLA_V = 64
Q_LORA = 256
KV_LORA = 128
ROPE_THETA = 10000.0
SB_HEADS = 8
SB_HEAD_DIM = 64
SB_WIDTH = SB_HEADS * SB_HEAD_DIM
MLA_WIDTH = MLA_HEADS * MLA_V
N_EXPERTS = 64
TOP_K = 8
EXPERT_FF = 256
SHARED_FF = 256
ROUTED_SCALE = 2.5
EPS = 1e-6
NEG = -1e30

LANES = 128
VMEM_LIMIT_BYTES = 56 * 1024 * 1024

SLOT = LANES
HALF_ROPE = MLA_ROPE // 2

SB_LOG_ZERO = -90.0

BF16 = jnp.bfloat16
F32 = jnp.float32


def _rms(x, g):
    return x * lax.rsqrt(jnp.mean(x * x, axis=-1, keepdims=True) + EPS) * g


def _dot(a, b):
    return jnp.dot(a, b, preferred_element_type=F32)


def _dot_nt(a, b):
    return lax.dot_general(a, b, (((1,), (1,)), ((), ())), preferred_element_type=F32)


def _proj_kernel(x_ref, pos_ref, g1_ref, wcq_ref, wckv_ref, wkr_ref, wsb_ref, gqa_ref, wuq_ref,
                 gkva_ref, wuk_ref, wuv_ref, gq_ref, gk_ref, inv_ref,
                 q_ref, k_ref, v_ref, sq_ref, sk_ref, sv_ref):
    x = x_ref[...]
    xn = _rms(x, g1_ref[...]).astype(BF16)

    cq = _rms(_dot(xn, wcq_ref[...]), gqa_ref[...]).astype(BF16)
    q = _dot(cq, wuq_ref[...])
    ckv = _rms(_dot(xn, wckv_ref[...]), gkva_ref[...]).astype(BF16)
    kn = _dot(ckv, wuk_ref[...])
    v_ref[...] = _dot(ckv, wuv_ref[...]).astype(BF16)
    kr = _dot(xn, wkr_ref[...])

    sb = _dot(xn, wsb_ref[...])
    sq_ref[...] = (sb[:, :SB_WIDTH] * (1.0 / math.sqrt(SB_HEAD_DIM))).astype(BF16)
    sk_ref[...] = sb[:, SB_WIDTH:2 * SB_WIDTH].astype(BF16)
    sv_ref[...] = sb[:, 2 * SB_WIDTH:].astype(BF16)

    ang = pos_ref[...].astype(F32) * inv_ref[...]
    cos = jnp.cos(ang)
    sin = jnp.sin(ang)
    lane = lax.broadcasted_iota(jnp.int32, ang.shape, 1)
    first_half = lane < MLA_NOPE + HALF_ROPE

    def rope(t):
        up = pltpu.roll(t, SLOT - HALF_ROPE, 1)
        down = pltpu.roll(t, HALF_ROPE, 1)
        return t * cos + jnp.where(first_half, -up, down) * sin

    def head_norm(t, g):
        ss = jnp.sum(t * t, axis=-1, keepdims=True) * (1.0 / MLA_QK)
        return t * lax.rsqrt(ss + EPS) * g

    scale = 1.0 / math.sqrt(MLA_QK)
    for h in range(MLA_HEADS):
        sl = slice(h * SLOT, (h + 1) * SLOT)
        qh = rope(head_norm(q[:, sl], gq_ref[...])) * scale
        q_ref[:, sl] = qh.astype(BF16)
        kh = rope(head_norm(kn[:, sl] + kr, gk_ref[...]))
        k_ref[:, sl] = kh.astype(BF16)


def _proj_call(x2, pos_col, g1, wcq, wckv, wkr, wsb, gqa, wuq, gkva, wuk, wuv, gq, gk, inv, tm):
    T = x2.shape[0]
    full = lambda a: pl.BlockSpec(a.shape, lambda i: (0,) * a.ndim)
    row = lambda w: pl.BlockSpec((tm, w), lambda i: (i, 0))
    weights = (g1, wcq, wckv, wkr, wsb, gqa, wuq, gkva, wuk, wuv, gq, gk, inv)
    out_w = (MLA_HEADS * SLOT, MLA_HEADS * SLOT, MLA_WIDTH, SB_WIDTH, SB_WIDTH, SB_WIDTH)
    return pl.pallas_call(
        _proj_kernel,
        grid=(T // tm,),
        in_specs=[row(D_MODEL), row(1)] + [full(w) for w in weights],
        out_specs=[row(w) for w in out_w],
        out_shape=[jax.ShapeDtypeStruct((T, w), BF16) for w in out_w],
        compiler_params=pltpu.CompilerParams(dimension_semantics=("arbitrary",),
                                             vmem_limit_bytes=VMEM_LIMIT_BYTES),
        name="proj",
    )(x2, pos_col, *weights)


def _mla_kernel(flags_ref, q_ref, k_ref, v_ref, posq_ref, posk_ref, o_ref, m_ref, l_ref, acc_ref,
                *, tq, tk, nk):
    b = pl.program_id(0)
    qi = pl.program_id(2)
    qchunk = jnp.right_shift(posq_ref[0], 6)
    m_ref[...] = jnp.full(m_ref.shape, NEG, F32)
    l_ref[...] = jnp.zeros(l_ref.shape, F32)
    acc_ref[...] = jnp.zeros(acc_ref.shape, F32)

    def block(kb, masked):
        k0 = pl.multiple_of(kb * tk, tk)
        if masked:
            kchunk = jnp.right_shift(posk_ref[0, :, pl.ds(k0, tk)], 6)
            vis = kchunk <= qchunk
        vb = v_ref[0, pl.ds(k0, tk), :]
        for h in range(2):
            qh = q_ref[0, :, h * SLOT:(h + 1) * SLOT]
            kh = k_ref[0, pl.ds(k0, tk), h * SLOT:(h + 1) * SLOT]
            s = _dot_nt(qh, kh)
            if masked:
                s = jnp.where(vis, s, NEG)
            m_old = m_ref[h]
            m_new = jnp.maximum(m_old, jnp.max(s, axis=-1, keepdims=True))
            alpha = jnp.exp(m_old - m_new)
            p = jnp.exp(s - m_new)
            l_ref[h] = alpha * l_ref[h] + jnp.sum(p, axis=-1, keepdims=True)
            acc_ref[h] = alpha * acc_ref[h] + _dot(p.astype(BF16), vb)
            m_ref[h] = m_new

    def body(kb, carry):
        flag = flags_ref[b, qi, kb]

        @pl.when(flag == 1)
        def _():
            block(kb, False)

        @pl.when(flag == 2)
        def _():
            block(kb, True)

        return carry

    lax.fori_loop(0, nk, body, 0)
    lane = lax.broadcasted_iota(jnp.int32, (tq, LANES), 1)
    o0 = acc_ref[0] / l_ref[0]
    o1 = acc_ref[1] / l_ref[1]
    o_ref[0] = jnp.where(lane < MLA_V, o0, o1)


def _mla_call(flags, q, k, v, pos_col, pos_row, tq, tk):
    B, S, _ = q.shape
    nq, nk = S // tq, S // tk
    grid_spec = pltpu.PrefetchScalarGridSpec(
        num_scalar_prefetch=1,
        grid=(B, MLA_HEADS // 2, nq),
        in_specs=[
            pl.BlockSpec((1, tq, 2 * SLOT), lambda b, p, i, f: (b, i, p)),
            pl.BlockSpec((1, S, 2 * SLOT), lambda b, p, i, f: (b, 0, p)),
            pl.BlockSpec((1, S, LANES), lambda b, p, i, f: (b, 0, p)),
            pl.BlockSpec((1, tq, 1), lambda b, p, i, f: (b, i, 0)),
            pl.BlockSpec((1, 1, S), lambda b, p, i, f: (b, 0, 0)),
        ],
        out_specs=pl.BlockSpec((1, tq, LANES), lambda b, p, i, f: (b, i, p)),
        scratch_shapes=[pltpu.VMEM((2, tq, 1), F32), pltpu.VMEM((2, tq, 1), F32),
                        pltpu.VMEM((2, tq, LANES), F32)],
    )
    return pl.pallas_call(
        functools.partial(_mla_kernel, tq=tq, tk=tk, nk=nk),
        grid_spec=grid_spec,
        out_shape=jax.ShapeDtypeStruct((B, S, MLA_WIDTH), F32),
        compiler_params=pltpu.CompilerParams(
            dimension_semantics=("arbitrary", "arbitrary", "arbitrary"),
            vmem_limit_bytes=VMEM_LIMIT_BYTES),
        name="mla_attn",
    )(flags, q, k, v, pos_col, pos_row)


def _softplus(z):
    return jnp.maximum(z, 0.0) + jnp.log1p(jnp.exp(-jnp.abs(z)))


def _sb_kernel(kstart_ref, q_ref, k_ref, v_ref, posq_ref, posk_ref, o_ref, run_ref, acc_ref,
               *, tq, tk, win):
    b = pl.program_id(0)
    qi = pl.program_id(2)
    qpos = posq_ref[0]
    lane = lax.broadcasted_iota(jnp.int32, (tq, LANES), 1)
    row_i = lax.broadcasted_iota(jnp.int32, (tk, tk), 0)
    col_i = lax.broadcasted_iota(jnp.int32, (tk, tk), 1)
    tri = (row_i >= col_i).astype(BF16)
    q_pair = q_ref[0]
    kstart = kstart_ref[b, qi]
    q_heads = []
    for h in range(2):
        in_head = (lane >= h * SB_HEAD_DIM) & (lane < (h + 1) * SB_HEAD_DIM)
        q_heads.append(jnp.where(in_head, q_pair, jnp.zeros_like(q_pair)))

    def suffix_sums(l1m):
        hi = l1m.astype(BF16)
        lo = (l1m - hi.astype(F32)).astype(BF16)
        both = _dot(jnp.concatenate([hi, lo], axis=0), tri)
        return both[:tq] + both[tq:]

    run_ref[...] = jnp.zeros(run_ref.shape, F32)
    acc_ref[...] = jnp.zeros(acc_ref.shape, F32)

    @pl.when(kstart >= win - 1)
    def _():
        k0 = pl.multiple_of((kstart - (win - 1)) * tk, tk)
        kw = k_ref[0, pl.ds(k0, win * tk), :]
        vw = v_ref[0, pl.ds(k0, win * tk), :]
        causal = posk_ref[0, :, pl.ds(k0, win * tk)] < qpos
        for h in range(2):
            z = _dot_nt(q_heads[h], kw)
            l1m = jnp.where(causal, -_softplus(z), 0.0)
            run = jnp.zeros((tq, 1), F32)
            a_blocks = [None] * win
            for j in reversed(range(win)):
                cs = slice(j * tk, (j + 1) * tk)
                suffix = suffix_sums(l1m[:, cs])
                a = jnp.exp(z[:, cs] + suffix + run)
                a_blocks[j] = jnp.where(causal[:, cs], a, 0.0).astype(BF16)
                run = run + suffix[:, 0:1]
            acc_ref[h] = _dot(jnp.concatenate(a_blocks, axis=1), vw)
            run_ref[h] = run

    kb_first = jnp.where(kstart >= win - 1, kstart - win, kstart)
    for h in range(2):

        def block(kb):
            k0 = pl.multiple_of(kb * tk, tk)
            z = _dot_nt(q_heads[h], k_ref[0, pl.ds(k0, tk), :])
            causal = posk_ref[0, :, pl.ds(k0, tk)] < qpos
            l1m = jnp.where(causal, -_softplus(z), 0.0)
            suffix = suffix_sums(l1m)
            run = run_ref[h]
            a = jnp.where(causal, jnp.exp(z + suffix + run), 0.0)
            acc_ref[h] += _dot(a.astype(BF16), v_ref[0, pl.ds(k0, tk), :])
            run_new = run + suffix[:, 0:1]
            run_ref[h] = run_new
            return jnp.max(run_new)

        def cond(c):
            kb, top = c
            return (kb >= 0) & (top >= SB_LOG_ZERO)

        def body(c):
            kb, _ = c
            return kb - 1, block(kb)

        lax.while_loop(cond, body, (kb_first, jnp.max(run_ref[h])))

    o_ref[0] = jnp.where(lane < SB_HEAD_DIM, acc_ref[0], acc_ref[1])


def _sb_call(kstart, q, k, v, pos_col, pos_row, tq, tk, win):
    B, S, _ = q.shape
    nq = S // tq
    grid_spec = pltpu.PrefetchScalarGridSpec(
        num_scalar_prefetch=1,
        grid=(B, SB_HEADS // 2, nq),
        in_specs=[
            pl.BlockSpec((1, tq, LANES), lambda b, p, i, s: (b, i, p)),
            pl.BlockSpec((1, S, LANES), lambda b, p, i, s: (b, 0, p)),
            pl.BlockSpec((1, S, LANES), lambda b, p, i, s: (b, 0, p)),
            pl.BlockSpec((1, tq, 1), lambda b, p, i, s: (b, i, 0)),
            pl.BlockSpec((1, 1, S), lambda b, p, i, s: (b, 0, 0)),
        ],
        out_specs=pl.BlockSpec((1, tq, LANES), lambda b, p, i, s: (b, i, p)),
        scratch_shapes=[pltpu.VMEM((2, tq, 1), F32), pltpu.VMEM((2, tq, LANES), F32)],
    )
    return pl.pallas_call(
        functools.partial(_sb_kernel, tq=tq, tk=tk, win=win),
        grid_spec=grid_spec,
        out_shape=jax.ShapeDtypeStruct((B, S, SB_WIDTH), F32),
        compiler_params=pltpu.CompilerParams(
            dimension_semantics=("arbitrary", "arbitrary", "arbitrary"),
            vmem_limit_bytes=VMEM_LIMIT_BYTES),
        name="sb_attn",
    )(kstart, q, k, v, pos_col, pos_row)


def _split_bf16(a):
    hi = a.astype(BF16)
    lo = (a - hi.astype(F32)).astype(BF16)
    return hi, lo


def _merge_kernel(x_ref, om_ref, os_ref, gm_ref, gs_ref, wo_ref, g2_ref, wrh_ref, wrl_ref,
                  bias_ref, wsgu_ref, wsd_ref, base_ref, hn_ref, gates_ref):
    mm = _rms(om_ref[...], gm_ref[...]).astype(BF16)
    ms = _rms(os_ref[...], gs_ref[...]).astype(BF16)
    h = x_ref[...] + _dot(mm, wo_ref[:MLA_WIDTH, :]) + _dot(ms, wo_ref[MLA_WIDTH:, :])
    hn = _rms(h, g2_ref[...])
    hn_hi, hn_lo = _split_bf16(hn)
    hn_ref[...] = hn_hi

    logits = (_dot_nt(wrh_ref[...], hn_hi) + _dot_nt(wrh_ref[...], hn_lo)
              + _dot_nt(wrl_ref[...], hn_hi))
    scores = jax.nn.sigmoid(logits)
    work = scores + bias_ref[...]
    eidx = lax.broadcasted_iota(jnp.int32, work.shape, 0)
    picked = jnp.zeros(work.shape, jnp.bool_)
    for _ in range(TOP_K):
        top = jnp.max(work, axis=0, keepdims=True)
        first = jnp.min(jnp.where(work == top, eidx, N_EXPERTS), axis=0, keepdims=True)
        hit = eidx == first
        picked = picked | hit
        work = jnp.where(hit, -jnp.inf, work)
    sel = jnp.where(picked, scores, 0.0)
    gates_ref[...] = sel / jnp.sum(sel, axis=0, keepdims=True) * ROUTED_SCALE

    sgu = _dot(hn_hi, wsgu_ref[...])
    act = (jax.nn.silu(sgu[:, :SHARED_FF]) * sgu[:, SHARED_FF:]).astype(BF16)
    base_ref[...] = h + _dot(act, wsd_ref[...])


def _merge_call(x2, o_mla, o_sb, gm, gs, wo, g2, wrh, wrl, bias, wsgu, wsd, tm):
    T = x2.shape[0]
    full = lambda a: pl.BlockSpec(a.shape, lambda i: (0,) * a.ndim)
    row = lambda w: pl.BlockSpec((tm, w), lambda i: (i, 0))
    weights = (gm, gs, wo, g2, wrh, wrl, bias, wsgu, wsd)
    return pl.pallas_call(
        _merge_kernel,
        grid=(T // tm,),
        in_specs=[row(D_MODEL), row(MLA_WIDTH), row(SB_WIDTH)] + [full(w) for w in weights],
        out_specs=[row(D_MODEL), row(D_MODEL), pl.BlockSpec((N_EXPERTS, tm), lambda i: (0, i))],
        out_shape=[jax.ShapeDtypeStruct((T, D_MODEL), F32),
                   jax.ShapeDtypeStruct((T, D_MODEL), BF16),
                   jax.ShapeDtypeStruct((N_EXPERTS, T), F32)],
        compiler_params=pltpu.CompilerParams(dimension_semantics=("arbitrary",),
                                             vmem_limit_bytes=VMEM_LIMIT_BYTES),
        name="merge",
    )(x2, o_mla, o_sb, *weights)


def _moe_kernel(hn_ref, gates_ref, base_ref, wgu_ref, wd_ref, o_ref, *, eg):
    g = pl.program_id(1)

    @pl.when(g == 0)
    def _():
        o_ref[...] = base_ref[...]

    hn = hn_ref[...]
    gates = gates_ref[...]
    lane = lax.broadcasted_iota(jnp.int32, gates.shape, 1)
    acc = jnp.zeros(o_ref.shape, F32)
    for j in range(eg):
        e = g * eg + j
        gate = jnp.sum(jnp.where(lane == e, gates, 0.0), axis=1, keepdims=True)
        gu = _dot(hn, wgu_ref[j])
        act = jax.nn.silu(gu[:, :EXPERT_FF]) * gu[:, EXPERT_FF:] * gate
        acc = acc + _dot(act.astype(BF16), wd_ref[j])
    o_ref[...] += acc


def _moe_call(hn, gates, base, wgu, wd, tm, eg):
    T = hn.shape[0]
    return pl.pallas_call(
        functools.partial(_moe_kernel, eg=eg),
        grid=(T // tm, N_EXPERTS // eg),
        in_specs=[
            pl.BlockSpec((tm, D_MODEL), lambda i, g: (i, 0)),
            pl.BlockSpec((tm, N_EXPERTS), lambda i, g: (i, 0)),
            pl.BlockSpec((tm, D_MODEL), lambda i, g: (i, 0)),
            pl.BlockSpec((eg, D_MODEL, 2 * EXPERT_FF), lambda i, g: (g, 0, 0)),
            pl.BlockSpec((eg, EXPERT_FF, D_MODEL), lambda i, g: (g, 0, 0)),
        ],
        out_specs=pl.BlockSpec((tm, D_MODEL), lambda i, g: (i, 0)),
        out_shape=jax.ShapeDtypeStruct((T, D_MODEL), F32),
        compiler_params=pltpu.CompilerParams(dimension_semantics=("arbitrary", "arbitrary"),
                                             vmem_limit_bytes=VMEM_LIMIT_BYTES),
        name="moe",
    )(hn, gates, base, wgu, wd)


def _slot_cols(w, width):
    k = w.shape[0]
    w = w.reshape(k, MLA_HEADS, width)
    return jnp.pad(w, ((0, 0), (0, 0), (0, SLOT - width))).reshape(k, MLA_HEADS * SLOT)


def _block_flags(pos, tq, tk, chunked):
    B, S = pos.shape
    p = jnp.right_shift(pos, 6) if chunked else pos
    qmin = p.reshape(B, S // tq, tq).min(-1)[:, :, None]
    qmax = p.reshape(B, S // tq, tq).max(-1)[:, :, None]
    kmin = p.reshape(B, S // tk, tk).min(-1)[:, None, :]
    kmax = p.reshape(B, S // tk, tk).max(-1)[:, None, :]
    if chunked:
        none, all_ = kmin > qmax, kmax <= qmin
    else:
        none, all_ = kmin >= qmax, kmax < qmin
    return jnp.where(none, 0, jnp.where(all_, 1, 2)).astype(jnp.int32)


def kernel(x, positions, norm1, w_in, q_a_norm, w_uq, kv_a_norm, w_ukv, q_norm, k_norm,
           out_norm_mla, out_norm_sb, w_o, norm2, w_router, router_bias, w_gate_up, w_down,
           w_shared_gate_up, w_shared_down):
    B, S, D = x.shape
    T = B * S
    c0 = Q_LORA
    c1 = c0 + KV_LORA
    c2 = c1 + MLA_ROPE

    row2 = lambda g: g.reshape(1, -1).astype(F32)
    wcq = w_in[:, :c0].astype(BF16)
    wckv = w_in[:, c0:c1].astype(BF16)
    wkr = jnp.pad(w_in[:, c1:c2], ((0, 0), (MLA_NOPE, SLOT - MLA_QK))).astype(BF16)
    wsb = w_in[:, c2:].astype(BF16)
    wuq = _slot_cols(w_uq, MLA_QK).astype(BF16)
    w_ukv3 = w_ukv.reshape(KV_LORA, MLA_HEADS, MLA_NOPE + MLA_V)
    wuk = _slot_cols(w_ukv3[:, :, :MLA_NOPE].reshape(KV_LORA, -1), MLA_NOPE).astype(BF16)
    wuv = w_ukv3[:, :, MLA_NOPE:].reshape(KV_LORA, MLA_WIDTH).astype(BF16)
    gq = jnp.pad(q_norm, (0, SLOT - MLA_QK)).reshape(1, SLOT)
    gk = jnp.pad(k_norm, (0, SLOT - MLA_QK)).reshape(1, SLOT)
    inv = ROPE_THETA ** (-jnp.arange(HALF_ROPE, dtype=F32) / HALF_ROPE)
    inv_slot = jnp.pad(jnp.concatenate([inv, inv]), (MLA_NOPE, SLOT - MLA_QK)).reshape(1, SLOT)

    x2 = x.reshape(T, D)
    pos_col = positions.reshape(T, 1)
    q, k, v, sq, sk, sv = _proj_call(
        x2, pos_col, row2(norm1), wcq, wckv, wkr, wsb, row2(q_a_norm), wuq, row2(kv_a_norm),
        wuk, wuv, gq, gk, inv_slot, tm=512)

    pos_c3 = positions.reshape(B, S, 1)
    pos_r3 = positions.reshape(B, 1, S)
    r3 = lambda a: a.reshape(B, S, a.shape[-1])

    tq_m, tk_m = 256, 512
    o_mla = _mla_call(_block_flags(positions, tq_m, tk_m, True), r3(q), r3(k), r3(v),
                      pos_c3, pos_r3, tq_m, tk_m)

    tq_s = tk_s = 128
    sb_flags = _block_flags(positions, tq_s, tk_s, False)
    kstart = jnp.max(jnp.where(sb_flags != 0, jnp.arange(S // tk_s, dtype=jnp.int32), -1), axis=-1)
    o_sb = _sb_call(kstart.astype(jnp.int32), r3(sq), r3(sk), r3(sv), pos_c3, pos_r3,
                    tq_s, tk_s, win=3)

    wr_t = w_router.T
    wrh = wr_t.astype(BF16)
    wrl = (wr_t - wrh.astype(F32)).astype(BF16)
    base, hn, gates_t = _merge_call(
        x2, o_mla.reshape(T, MLA_WIDTH), o_sb.reshape(T, SB_WIDTH), row2(out_norm_mla),
        row2(out_norm_sb), w_o.astype(BF16), row2(norm2), wrh, wrl,
        router_bias.reshape(N_EXPERTS, 1).astype(F32), w_shared_gate_up.astype(BF16),
        w_shared_down.astype(BF16), tm=512)

    out = _moe_call(hn, gates_t.T, base, w_gate_up.astype(BF16), w_down.astype(BF16),
                    tm=1024, eg=4)
    return out.reshape(B, S, D)
```

```python
import functools
import math

import jax
import jax.numpy as jnp
from jax import lax
from jax.experimental import pallas as pl
from jax.experimental.pallas import tpu as pltpu

D_MODEL = 1024
CHUNK = 64
MLA_HEADS = 8
MLA_NOPE = 64
MLA_ROPE = 32
MLA_QK = MLA_NOPE + MLA_ROPE
MLA_V = 64
Q_LORA = 256
KV_LORA = 128
ROPE_THETA = 10000.0
SB_HEADS = 8
SB_HEAD_DIM = 64
SB_WIDTH = SB_HEADS * SB_HEAD_DIM
MLA_WIDTH = MLA_HEADS * MLA_V
N_EXPERTS = 64
TOP_K = 8
EXPERT_FF = 256
SHARED_FF = 256
ROUTED_SCALE = 2.5
EPS = 1e-6
NEG = -1e30

LANES = 128
VMEM_LIMIT_BYTES = 56 * 1024 * 1024

SLOT = LANES
HALF_ROPE = MLA_ROPE // 2

SB_LOG_ZERO = -90.0

LOG2E = 1.4426950408889634
BOUND_MARGIN = 1.02
MLA_MAX_SHIFT = 50.0

BF16 = jnp.bfloat16
F32 = jnp.float32


def _rms(x, g):
    return x * lax.rsqrt(jnp.mean(x * x, axis=-1, keepdims=True) + EPS) * g


def _dot(a, b):
    return jnp.dot(a, b, preferred_element_type=F32)


def _dot_nt(a, b):
    return lax.dot_general(a, b, (((1,), (1,)), ((), ())), preferred_element_type=F32)


def _proj_kernel(x_ref, pos_ref, g1_ref, wcq_ref, wckv_ref, wkr_ref, wsb_ref, gqa_ref, wuq_ref,
                 gkva_ref, wuk_ref, wuv_ref, vone_ref, gq_ref, gk_ref, inv_ref, qaug_ref, kaug_ref,
                 q_ref, k_ref, v_ref, sq_ref, sk_ref, sv_ref):
    x = x_ref[...]
    xn = _rms(x, g1_ref[...]).astype(BF16)

    cq = _rms(_dot(xn, wcq_ref[...]), gqa_ref[...]).astype(BF16)
    q = _dot(cq, wuq_ref[...])
    ckv = _rms(_dot(xn, wckv_ref[...]), gkva_ref[...]).astype(BF16)
    kn = _dot(ckv, wuk_ref[...])
    v_ref[...] = (_dot(ckv, wuv_ref[...]) + vone_ref[...]).astype(BF16)
    kr = _dot(xn, wkr_ref[...])

    sb = _dot(xn, wsb_ref[...])
    sq_ref[...] = (sb[:, :SB_WIDTH] * (1.0 / math.sqrt(SB_HEAD_DIM))).astype(BF16)
    sk_ref[...] = sb[:, SB_WIDTH:2 * SB_WIDTH].astype(BF16)
    sv_ref[...] = sb[:, 2 * SB_WIDTH:].astype(BF16)

    ang = pos_ref[...].astype(F32) * inv_ref[...]
    cos = jnp.cos(ang)
    sin = jnp.sin(ang)
    lane = lax.broadcasted_iota(jnp.int32, ang.shape, 1)
    first_half = lane < MLA_NOPE + HALF_ROPE

    def rope(t):
        up = pltpu.roll(t, SLOT - HALF_ROPE, 1)
        down = pltpu.roll(t, HALF_ROPE, 1)
        return t * cos + jnp.where(first_half, -up, down) * sin

    def head_norm(t, g):
        ss = jnp.sum(t * t, axis=-1, keepdims=True) * (1.0 / MLA_QK)
        return t * lax.rsqrt(ss + EPS) * g

    scale = LOG2E / math.sqrt(MLA_QK)
    for h in range(MLA_HEADS):
        sl = slice(h * SLOT, (h + 1) * SLOT)
        qh = rope(head_norm(q[:, sl], gq_ref[...])) * scale + qaug_ref[...]
        q_ref[:, sl] = qh.astype(BF16)
        kh = rope(head_norm(kn[:, sl] + kr, gk_ref[...])) + kaug_ref[...]
        k_ref[:, sl] = kh.astype(BF16)


def _proj_call(x2, pos_col, weights, tm):
    T = x2.shape[0]
    full = lambda a: pl.BlockSpec(a.shape, lambda i: (0,) * a.ndim)
    row = lambda w: pl.BlockSpec((tm, w), lambda i: (i, 0))
    out_w = (MLA_HEADS * SLOT, MLA_HEADS * SLOT, MLA_HEADS * SLOT, SB_WIDTH, SB_WIDTH, SB_WIDTH)
    return pl.pallas_call(
        _proj_kernel,
        grid=(T // tm,),
        in_specs=[row(D_MODEL), row(1)] + [full(w) for w in weights],
        out_specs=[row(w) for w in out_w],
        out_shape=[jax.ShapeDtypeStruct((T, w), BF16) for w in out_w],
        compiler_params=pltpu.CompilerParams(dimension_semantics=("arbitrary",),
                                             vmem_limit_bytes=VMEM_LIMIT_BYTES),
        name="proj",
    )(x2, pos_col, *weights)


def _mla_kernel(flags_ref, q_ref, k_ref, v_ref, posq_ref, posk_ref, o_ref, acc_ref, m_ref,
                *, tq, tk, nk, online):
    b = pl.program_id(0)
    qi = pl.program_id(2)
    qchunk = jnp.right_shift(posq_ref[0], 6)
    acc_ref[...] = jnp.zeros(acc_ref.shape, F32)
    if online:
        m_ref[...] = jnp.full(m_ref.shape, NEG, F32)

    def block(kb, masked):
        k0 = pl.multiple_of(kb * tk, tk)
        if masked:
            kchunk = jnp.right_shift(posk_ref[0, :, pl.ds(k0, tk)], 6)
            vis = kchunk <= qchunk
        for h in range(2):
            hs = slice(h * SLOT, (h + 1) * SLOT)
            s = _dot_nt(q_ref[0, :, hs], k_ref[0, pl.ds(k0, tk), hs])
            if masked:
                s = jnp.where(vis, s, NEG)
            vh = v_ref[0, pl.ds(k0, tk), hs]
            if online:
                m_old = m_ref[h]
                m_new = jnp.maximum(m_old, jnp.max(s, axis=-1, keepdims=True))
                p = jnp.exp2(s - m_new)
                acc_ref[h] = jnp.exp2(m_old - m_new) * acc_ref[h] + _dot(p.astype(BF16), vh)
                m_ref[h] = m_new
            else:
                acc_ref[h] += _dot(jnp.exp2(s).astype(BF16), vh)

    def body(kb, carry):
        flag = flags_ref[b, qi, kb]

        @pl.when(flag == 1)
        def _():
            block(kb, False)

        @pl.when(flag == 2)
        def _():
            block(kb, True)

        return carry

    lax.fori_loop(0, nk, body, 0)
    lane = lax.broadcasted_iota(jnp.int32, (tq, LANES), 1)
    o0 = acc_ref[0] / pltpu.roll(acc_ref[0], MLA_V, 1)
    o1 = acc_ref[1] / pltpu.roll(acc_ref[1], MLA_V, 1)
    o_ref[0] = jnp.where(lane < MLA_V, o0, pltpu.roll(o1, MLA_V, 1))


def _mla_call(flags, q, k, v, pos_col, pos_row, tq, tk, online):
    B, S, _ = q.shape
    nq, nk = S // tq, S // tk
    grid_spec = pltpu.PrefetchScalarGridSpec(
        num_scalar_prefetch=1,
        grid=(B, MLA_HEADS // 2, nq),
        in_specs=[
            pl.BlockSpec((1, tq, 2 * SLOT), lambda b, p, i, f: (b, i, p)),
            pl.BlockSpec((1, S, 2 * SLOT), lambda b, p, i, f: (b, 0, p)),
            pl.BlockSpec((1, S, 2 * SLOT), lambda b, p, i, f: (b, 0, p)),
            pl.BlockSpec((1, tq, 1), lambda b, p, i, f: (b, i, 0)),
            pl.BlockSpec((1, 1, S), lambda b, p, i, f: (b, 0, 0)),
        ],
        out_specs=pl.BlockSpec((1, tq, LANES), lambda b, p, i, f: (b, i, p)),
        scratch_shapes=[pltpu.VMEM((2, tq, LANES), F32), pltpu.VMEM((2, tq, 1), F32)],
    )
    return pl.pallas_call(
        functools.partial(_mla_kernel, tq=tq, tk=tk, nk=nk, online=online),
        grid_spec=grid_spec,
        out_shape=jax.ShapeDtypeStruct((B, S, MLA_WIDTH), F32),
        compiler_params=pltpu.CompilerParams(
            dimension_semantics=("arbitrary", "arbitrary", "arbitrary"),
            vmem_limit_bytes=VMEM_LIMIT_BYTES),
        name="mla_attn_online" if online else "mla_attn",
    )(flags, q, k, v, pos_col, pos_row)


def _softplus(z):
    return jnp.maximum(z, 0.0) + jnp.log1p(jnp.exp(-jnp.abs(z)))


def _sb_kernel(kstart_ref, q_ref, k_ref, v_ref, posq_ref, posk_ref, o_ref, run_ref, acc_ref,
               *, tq, tk, win):
    b = pl.program_id(0)
    qi = pl.program_id(2)
    qpos = posq_ref[0]
    lane = lax.broadcasted_iota(jnp.int32, (tq, LANES), 1)
    row_i = lax.broadcasted_iota(jnp.int32, (tk, tk), 0)
    col_i = lax.broadcasted_iota(jnp.int32, (tk, tk), 1)
    tri = (row_i >= col_i).astype(BF16)
    q_pair = q_ref[0]
    kstart = kstart_ref[b, qi]
    q_heads = []
    for h in range(2):
        in_head = (lane >= h * SB_HEAD_DIM) & (lane < (h + 1) * SB_HEAD_DIM)
        q_heads.append(jnp.where(in_head, q_pair, jnp.zeros_like(q_pair)))

    def suffix_sums(l1m):
        hi = l1m.astype(BF16)
        lo = (l1m - hi.astype(F32)).astype(BF16)
        both = _dot(jnp.concatenate([hi, lo], axis=0), tri)
        return both[:tq] + both[tq:]

    run_ref[...] = jnp.zeros(run_ref.shape, F32)
    acc_ref[...] = jnp.zeros(acc_ref.shape, F32)

    @pl.when(kstart >= win - 1)
    def _():
        k0 = pl.multiple_of((kstart - (win - 1)) * tk, tk)
        kw = k_ref[0, pl.ds(k0, win * tk), :]
        vw = v_ref[0, pl.ds(k0, win * tk), :]
        causal = posk_ref[0, :, pl.ds(k0, win * tk)] < qpos
        for h in range(2):
            z = _dot_nt(q_heads[h], kw)
            l1m = jnp.where(causal, -_softplus(z), 0.0)
            run = jnp.zeros((tq, 1), F32)
            a_blocks = [None] * win
            for j in reversed(range(win)):
                cs = slice(j * tk, (j + 1) * tk)
                suffix = suffix_sums(l1m[:, cs])
                a = jnp.exp(z[:, cs] + suffix + run)
                a_blocks[j] = jnp.where(causal[:, cs], a, 0.0).astype(BF16)
                run = run + suffix[:, 0:1]
            acc_ref[h] = _dot(jnp.concatenate(a_blocks, axis=1), vw)
            run_ref[h] = run

    kb_first = jnp.where(kstart >= win - 1, kstart - win, kstart)
    for h in range(2):

        def block(kb):
            k0 = pl.multiple_of(kb * tk, tk)
            z = _dot_nt(q_heads[h], k_ref[0, pl.ds(k0, tk), :])
            causal = posk_ref[0, :, pl.ds(k0, tk)] < qpos
            l1m = jnp.where(causal, -_softplus(z), 0.0)
            suffix = suffix_sums(l1m)
            run = run_ref[h]
            a = jnp.where(causal, jnp.exp(z + suffix + run), 0.0)
            acc_ref[h] += _dot(a.astype(BF16), v_ref[0, pl.ds(k0, tk), :])
            run_new = run + suffix[:, 0:1]
            run_ref[h] = run_new
            return jnp.max(run_new)

        def cond(c):
            kb, top = c
            return (kb >= 0) & (top >= SB_LOG_ZERO)

        def body(c):
            kb, _ = c
            return kb - 1, block(kb)

        lax.while_loop(cond, body, (kb_first, jnp.max(run_ref[h])))

    o_ref[0] = jnp.where(lane < SB_HEAD_DIM, acc_ref[0], acc_ref[1])


def _sb_call(kstart, q, k, v, pos_col, pos_row, tq, tk, win):
    B, S, _ = q.shape
    nq = S // tq
    grid_spec = pltpu.PrefetchScalarGridSpec(
        num_scalar_prefetch=1,
        grid=(B, SB_HEADS // 2, nq),
        in_specs=[
            pl.BlockSpec((1, tq, LANES), lambda b, p, i, s: (b, i, p)),
            pl.BlockSpec((1, S, LANES), lambda b, p, i, s: (b, 0, p)),
            pl.BlockSpec((1, S, LANES), lambda b, p, i, s: (b, 0, p)),
            pl.BlockSpec((1, tq, 1), lambda b, p, i, s: (b, i, 0)),
            pl.BlockSpec((1, 1, S), lambda b, p, i, s: (b, 0, 0)),
        ],
        out_specs=pl.BlockSpec((1, tq, LANES), lambda b, p, i, s: (b, i, p)),
        scratch_shapes=[pltpu.VMEM((2, tq, 1), F32), pltpu.VMEM((2, tq, LANES), F32)],
    )
    return pl.pallas_call(
        functools.partial(_sb_kernel, tq=tq, tk=tk, win=win),
        grid_spec=grid_spec,
        out_shape=jax.ShapeDtypeStruct((B, S, SB_WIDTH), F32),
        compiler_params=pltpu.CompilerParams(
            dimension_semantics=("arbitrary", "arbitrary", "arbitrary"),
            vmem_limit_bytes=VMEM_LIMIT_BYTES),
        name="sb_attn",
    )(kstart, q, k, v, pos_col, pos_row)


def _split_bf16(a):
    hi = a.astype(BF16)
    lo = (a - hi.astype(F32)).astype(BF16)
    return hi, lo


def _merge_kernel(x_ref, om_ref, os_ref, gm_ref, gs_ref, wo_ref, g2_ref, wrh_ref, wrl_ref,
                  bias_ref, wsgu_ref, wsd_ref, base_ref, hn_ref, gates_ref):
    mm = _rms(om_ref[...], gm_ref[...]).astype(BF16)
    ms = _rms(os_ref[...], gs_ref[...]).astype(BF16)
    h = x_ref[...] + _dot(mm, wo_ref[:MLA_WIDTH, :]) + _dot(ms, wo_ref[MLA_WIDTH:, :])
    hn = _rms(h, g2_ref[...])
    hn_hi, hn_lo = _split_bf16(hn)
    hn_ref[...] = hn_hi

    logits = (_dot_nt(wrh_ref[...], hn_hi) + _dot_nt(wrh_ref[...], hn_lo)
              + _dot_nt(wrl_ref[...], hn_hi))
    scores = jax.nn.sigmoid(logits)
    work = scores + bias_ref[...]
    eidx = lax.broadcasted_iota(jnp.int32, work.shape, 0)
    picked = jnp.zeros(work.shape, jnp.bool_)
    for _ in range(TOP_K):
        top = jnp.max(work, axis=0, keepdims=True)
        first = jnp.min(jnp.where(work == top, eidx, N_EXPERTS), axis=0, keepdims=True)
        hit = eidx == first
        picked = picked | hit
        work = jnp.where(hit, -jnp.inf, work)
    sel = jnp.where(picked, scores, 0.0)
    gates_ref[...] = sel / jnp.sum(sel, axis=0, keepdims=True) * ROUTED_SCALE

    sgu = _dot(hn_hi, wsgu_ref[...])
    act = (jax.nn.silu(sgu[:, :SHARED_FF]) * sgu[:, SHARED_FF:]).astype(BF16)
    base_ref[...] = h + _dot(act, wsd_ref[...])


def _merge_call(x2, o_mla, o_sb, gm, gs, wo, g2, wrh, wrl, bias, wsgu, wsd, tm):
    T = x2.shape[0]
    full = lambda a: pl.BlockSpec(a.shape, lambda i: (0,) * a.ndim)
    row = lambda w: pl.BlockSpec((tm, w), lambda i: (i, 0))
    weights = (gm, gs, wo, g2, wrh, wrl, bias, wsgu, wsd)
    return pl.pallas_call(
        _merge_kernel,
        grid=(T // tm,),
        in_specs=[row(D_MODEL), row(MLA_WIDTH), row(SB_WIDTH)] + [full(w) for w in weights],
        out_specs=[row(D_MODEL), row(D_MODEL), pl.BlockSpec((N_EXPERTS, tm), lambda i: (0, i))],
        out_shape=[jax.ShapeDtypeStruct((T, D_MODEL), F32),
                   jax.ShapeDtypeStruct((T, D_MODEL), BF16),
                   jax.ShapeDtypeStruct((N_EXPERTS, T), F32)],
        compiler_params=pltpu.CompilerParams(dimension_semantics=("arbitrary",),
                                             vmem_limit_bytes=VMEM_LIMIT_BYTES),
        name="merge",
    )(x2, o_mla, o_sb, *weights)


def _moe_kernel(hn_ref, gates_ref, base_ref, wgu_ref, wd_ref, o_ref, *, eg):
    g = pl.program_id(1)

    @pl.when(g == 0)
    def _():
        o_ref[...] = base_ref[...]

    hn = hn_ref[...]
    gates = gates_ref[...]
    lane = lax.broadcasted_iota(jnp.int32, gates.shape, 1)
    acc = jnp.zeros(o_ref.shape, F32)
    for j in range(eg):
        e = g * eg + j
        gate = jnp.sum(jnp.where(lane == e, gates, 0.0), axis=1, keepdims=True)
        gu = _dot(hn, wgu_ref[j])
        act = jax.nn.silu(gu[:, :EXPERT_FF]) * gu[:, EXPERT_FF:] * gate
        acc = acc + _dot(act.astype(BF16), wd_ref[j])
    o_ref[...] += acc


def _moe_call(hn, gates, base, wgu, wd, tm, eg):
    T = hn.shape[0]
    return pl.pallas_call(
        functools.partial(_moe_kernel, eg=eg),
        grid=(T // tm, N_EXPERTS // eg),
        in_specs=[
            pl.BlockSpec((tm, D_MODEL), lambda i, g: (i, 0)),
            pl.BlockSpec((tm, N_EXPERTS), lambda i, g: (i, 0)),
            pl.BlockSpec((tm, D_MODEL), lambda i, g: (i, 0)),
            pl.BlockSpec((eg, D_MODEL, 2 * EXPERT_FF), lambda i, g: (g, 0, 0)),
            pl.BlockSpec((eg, EXPERT_FF, D_MODEL), lambda i, g: (g, 0, 0)),
        ],
        out_specs=pl.BlockSpec((tm, D_MODEL), lambda i, g: (i, 0)),
        out_shape=jax.ShapeDtypeStruct((T, D_MODEL), F32),
        compiler_params=pltpu.CompilerParams(dimension_semantics=("arbitrary", "arbitrary"),
                                             vmem_limit_bytes=VMEM_LIMIT_BYTES),
        name="moe",
    )(hn, gates, base, wgu, wd)


def _slot_cols(w, width):
    k = w.shape[0]
    w = w.reshape(k, MLA_HEADS, width)
    return jnp.pad(w, ((0, 0), (0, 0), (0, SLOT - width))).reshape(k, MLA_HEADS * SLOT)


def _block_flags(pos, tq, tk, chunked):
    B, S = pos.shape
    p = jnp.right_shift(pos, 6) if chunked else pos
    qmin = p.reshape(B, S // tq, tq).min(-1)[:, :, None]
    qmax = p.reshape(B, S // tq, tq).max(-1)[:, :, None]
    kmin = p.reshape(B, S // tk, tk).min(-1)[:, None, :]
    kmax = p.reshape(B, S // tk, tk).max(-1)[:, None, :]
    if chunked:
        none, all_ = kmin > qmax, kmax <= qmin
    else:
        none, all_ = kmin >= qmax, kmax < qmin
    return jnp.where(none, 0, jnp.where(all_, 1, 2)).astype(jnp.int32)


def kernel(x, positions, norm1, w_in, q_a_norm, w_uq, kv_a_norm, w_ukv, q_norm, k_norm,
           out_norm_mla, out_norm_sb, w_o, norm2, w_router, router_bias, w_gate_up, w_down,
           w_shared_gate_up, w_shared_down):
    B, S, D = x.shape
    T = B * S
    c0 = Q_LORA
    c1 = c0 + KV_LORA
    c2 = c1 + MLA_ROPE

    row2 = lambda g: g.reshape(1, -1).astype(F32)
    wcq = w_in[:, :c0].astype(BF16)
    wckv = w_in[:, c0:c1].astype(BF16)
    wkr = jnp.pad(w_in[:, c1:c2], ((0, 0), (MLA_NOPE, SLOT - MLA_QK))).astype(BF16)
    wsb = w_in[:, c2:].astype(BF16)
    wuq = _slot_cols(w_uq, MLA_QK).astype(BF16)
    w_ukv3 = w_ukv.reshape(KV_LORA, MLA_HEADS, MLA_NOPE + MLA_V)
    wuk = _slot_cols(w_ukv3[:, :, :MLA_NOPE].reshape(KV_LORA, -1), MLA_NOPE).astype(BF16)
    wuv = _slot_cols(w_ukv3[:, :, MLA_NOPE:].reshape(KV_LORA, MLA_WIDTH), MLA_V).astype(BF16)
    slot_lane = jnp.arange(SLOT)
    vone = jnp.tile((slot_lane >= MLA_V).astype(F32), MLA_HEADS).reshape(1, MLA_HEADS * SLOT)
    gq = jnp.pad(q_norm, (0, SLOT - MLA_QK)).reshape(1, SLOT)
    gk = jnp.pad(k_norm, (0, SLOT - MLA_QK)).reshape(1, SLOT)
    inv = ROPE_THETA ** (-jnp.arange(HALF_ROPE, dtype=F32) / HALF_ROPE)
    inv_slot = jnp.pad(jnp.concatenate([inv, inv]), (MLA_NOPE, SLOT - MLA_QK)).reshape(1, SLOT)

    bound = (math.sqrt(MLA_QK) * LOG2E * BOUND_MARGIN) * jnp.max(jnp.abs(q_norm)) * jnp.max(jnp.abs(k_norm))
    bounded = bound <= MLA_MAX_SHIFT
    shift = jnp.where(bounded, bound, 0.0)
    qaug = (slot_lane == MLA_QK).astype(F32).reshape(1, SLOT)
    kaug = -shift * qaug

    x2 = x.reshape(T, D)
    pos_col = positions.reshape(T, 1)
    q, k, v, sq, sk, sv = _proj_call(
        x2, pos_col,
        (row2(norm1), wcq, wckv, wkr, wsb, row2(q_a_norm), wuq, row2(kv_a_norm), wuk, wuv, vone,
         gq, gk, inv_slot, qaug, kaug), tm=512)

    pos_c3 = positions.reshape(B, S, 1)
    pos_r3 = positions.reshape(B, 1, S)
    r3 = lambda a: a.reshape(B, S, a.shape[-1])

    tq_m, tk_m = 256, 512
    mla_args = (_block_flags(positions, tq_m, tk_m, True), r3(q), r3(k), r3(v), pos_c3, pos_r3)
    o_mla = lax.cond(bounded,
                     lambda *a: _mla_call(*a, tq_m, tk_m, online=False),
                     lambda *a: _mla_call(*a, tq_m, tk_m, online=True), *mla_args)

    tq_s = tk_s = 128
    sb_flags = _block_flags(positions, tq_s, tk_s, False)
    kstart = jnp.max(jnp.where(sb_flags != 0, jnp.arange(S // tk_s, dtype=jnp.int32), -1), axis=-1)
    o_sb = _sb_call(kstart.astype(jnp.int32), r3(sq), r3(sk), r3(sv), pos_c3, pos_r3,
                    tq_s, tk_s, win=3)

    wr_t = w_router.T
    wrh = wr_t.astype(BF16)
    wrl = (wr_t - wrh.astype(F32)).astype(BF16)
    base, hn, gates_t = _merge_call(
        x2, o_mla.reshape(T, MLA_WIDTH), o_sb.reshape(T, SB_WIDTH), row2(out_norm_mla),
        row2(out_norm_sb), w_o.astype(BF16), row2(norm2), wrh, wrl,
        router_bias.reshape(N_EXPERTS, 1).astype(F32), w_shared_gate_up.astype(BF16),
        w_shared_down.astype(BF16), tm=512)

    out = _moe_call(hn, gates_t.T, base, w_gate_up.astype(BF16), w_down.astype(BF16),
                    tm=1024, eg=4)
    return out.reshape(B, S, D)
```

```python
import functools
import math

import jax
import jax.numpy as jnp
from jax import lax
from jax.experimental import pallas as pl
from jax.experimental.pallas import tpu as pltpu

D_MODEL = 1024
CHUNK = 64
MLA_HEADS = 8
MLA_NOPE = 64
MLA_ROPE = 32
MLA_QK = MLA_NOPE + MLA_ROPE
MLA_V = 64
Q_LORA = 256
KV_LORA = 128
ROPE_THETA = 10000.0
SB_HEADS = 8
SB_HEAD_DIM = 64
SB_WIDTH = SB_HEADS * SB_HEAD_DIM
MLA_WIDTH = MLA_HEADS * MLA_V
N_EXPERTS = 64
TOP_K = 8
EXPERT_FF = 256
SHARED_FF = 256
ROUTED_SCALE = 2.5
EPS = 1e-6
NEG = -1e30

LANES = 128
VMEM_LIMIT_BYTES = 56 * 1024 * 1024

SLOT = LANES
HALF_ROPE = MLA_ROPE // 2

SB_LOG_ZERO = -90.0

LOG2E = 1.4426950408889634
BOUND_MARGIN = 1.02
MLA_MAX_SHIFT = 50.0

BF16 = jnp.bfloat16
F32 = jnp.float32


def _rms(x, g):
    return x * lax.rsqrt(jnp.mean(x * x, axis=-1, keepdims=True) + EPS) * g


def _dot(a, b):
    return jnp.dot(a, b, preferred_element_type=F32)


def _dot_nt(a, b):
    return lax.dot_general(a, b, (((1,), (1,)), ((), ())), preferred_element_type=F32)


def _proj_kernel(x_ref, pos_ref, g1_ref, wcq_ref, wckv_ref, wkr_ref, wsb_ref, gqa_ref, wuq_ref,
                 gkva_ref, wuk_ref, wuv_ref, vone_ref, gq_ref, gk_ref, inv_ref, qaug_ref, kaug_ref,
                 q_ref, k_ref, v_ref, sq_ref, sk_ref, sv_ref):
    x = x_ref[...]
    xn = _rms(x, g1_ref[...]).astype(BF16)

    cq = _rms(_dot(xn, wcq_ref[...]), gqa_ref[...]).astype(BF16)
    q = _dot(cq, wuq_ref[...])
    ckv = _rms(_dot(xn, wckv_ref[...]), gkva_ref[...]).astype(BF16)
    kn = _dot(ckv, wuk_ref[...])
    v_ref[...] = (_dot(ckv, wuv_ref[...]) + vone_ref[...]).astype(BF16)
    kr = _dot(xn, wkr_ref[...])

    sb = _dot(xn, wsb_ref[...])
    sq_ref[...] = (sb[:, :SB_WIDTH] * (1.0 / math.sqrt(SB_HEAD_DIM))).astype(BF16)
    sk_ref[...] = sb[:, SB_WIDTH:2 * SB_WIDTH].astype(BF16)
    sv_ref[...] = sb[:, 2 * SB_WIDTH:].astype(BF16)

    ang = pos_ref[...].astype(F32) * inv_ref[...]
    cos = jnp.cos(ang)
    sin = jnp.sin(ang)
    lane = lax.broadcasted_iota(jnp.int32, ang.shape, 1)
    first_half = lane < MLA_NOPE + HALF_ROPE

    def rope(t):
        up = pltpu.roll(t, SLOT - HALF_ROPE, 1)
        down = pltpu.roll(t, HALF_ROPE, 1)
        return t * cos + jnp.where(first_half, -up, down) * sin

    def head_norm(t, g):
        ss = jnp.sum(t * t, axis=-1, keepdims=True) * (1.0 / MLA_QK)
        return t * lax.rsqrt(ss + EPS) * g

    scale = LOG2E / math.sqrt(MLA_QK)
    for h in range(MLA_HEADS):
        sl = slice(h * SLOT, (h + 1) * SLOT)
        qh = rope(head_norm(q[:, sl], gq_ref[...])) * scale + qaug_ref[...]
        q_ref[:, sl] = qh.astype(BF16)
        kh = rope(head_norm(kn[:, sl] + kr, gk_ref[...])) + kaug_ref[...]
        k_ref[:, sl] = kh.astype(BF16)


def _proj_call(x2, pos_col, weights, tm):
    T = x2.shape[0]
    full = lambda a: pl.BlockSpec(a.shape, lambda i: (0,) * a.ndim)
    row = lambda w: pl.BlockSpec((tm, w), lambda i: (i, 0))
    out_w = (MLA_HEADS * SLOT, MLA_HEADS * SLOT, MLA_HEADS * SLOT, SB_WIDTH, SB_WIDTH, SB_WIDTH)
    return pl.pallas_call(
        _proj_kernel,
        grid=(T // tm,),
        in_specs=[row(D_MODEL), row(1)] + [full(w) for w in weights],
        out_specs=[row(w) for w in out_w],
        out_shape=[jax.ShapeDtypeStruct((T, w), BF16) for w in out_w],
        compiler_params=pltpu.CompilerParams(dimension_semantics=("arbitrary",),
                                             vmem_limit_bytes=VMEM_LIMIT_BYTES),
        name="proj",
    )(x2, pos_col, *weights)


def _mla_kernel(flags_ref, q_ref, k_ref, v_ref, posq_ref, posk_ref, o_ref, acc_ref, m_ref,
                *, tq, tk, nk, hp, online):
    b = pl.program_id(0)
    qi = pl.program_id(2)
    qchunk = jnp.right_shift(posq_ref[0], 6)
    acc_ref[...] = jnp.zeros(acc_ref.shape, F32)
    if online:
        m_ref[...] = jnp.full(m_ref.shape, NEG, F32)

    def block(kb, masked):
        k0 = pl.multiple_of(kb * tk, tk)
        if masked:
            kchunk = jnp.right_shift(posk_ref[0, :, pl.ds(k0, tk)], 6)
            vis = kchunk <= qchunk
        for h in range(hp):
            hs = slice(h * SLOT, (h + 1) * SLOT)
            s = _dot_nt(q_ref[0, :, hs], k_ref[0, pl.ds(k0, tk), hs])
            if masked:
                s = jnp.where(vis, s, NEG)
            vh = v_ref[0, pl.ds(k0, tk), hs]
            if online:
                m_old = m_ref[h]
                m_new = jnp.maximum(m_old, jnp.max(s, axis=-1, keepdims=True))
                p = jnp.exp2(s - m_new)
                acc_ref[h] = jnp.exp2(m_old - m_new) * acc_ref[h] + _dot(p.astype(BF16), vh)
                m_ref[h] = m_new
            else:
                acc_ref[h] += _dot(jnp.exp2(s).astype(BF16), vh)

    def body(kb, carry):
        flag = flags_ref[b, qi, kb]

        @pl.when(flag == 1)
        def _():
            block(kb, False)

        @pl.when(flag == 2)
        def _():
            block(kb, True)

        return carry

    lax.fori_loop(0, nk, body, 0)
    lane = lax.broadcasted_iota(jnp.int32, (tq, LANES), 1)
    for j in range(hp // 2):
        o0 = acc_ref[2 * j] / pltpu.roll(acc_ref[2 * j], MLA_V, 1)
        o1 = acc_ref[2 * j + 1] / pltpu.roll(acc_ref[2 * j + 1], MLA_V, 1)
        o_ref[0, :, j * LANES:(j + 1) * LANES] = jnp.where(lane < MLA_V, o0, pltpu.roll(o1, MLA_V, 1))


def _mla_call(flags, q, k, v, pos_col, pos_row, tq, tk, hp, online):
    B, S, _ = q.shape
    nq, nk = S // tq, S // tk
    grid_spec = pltpu.PrefetchScalarGridSpec(
        num_scalar_prefetch=1,
        grid=(B, MLA_HEADS // hp, nq),
        in_specs=[
            pl.BlockSpec((1, tq, hp * SLOT), lambda b, p, i, f: (b, i, p)),
            pl.BlockSpec((1, S, hp * SLOT), lambda b, p, i, f: (b, 0, p)),
            pl.BlockSpec((1, S, hp * SLOT), lambda b, p, i, f: (b, 0, p)),
            pl.BlockSpec((1, tq, 1), lambda b, p, i, f: (b, i, 0)),
            pl.BlockSpec((1, 1, S), lambda b, p, i, f: (b, 0, 0)),
        ],
        out_specs=pl.BlockSpec((1, tq, hp * MLA_V), lambda b, p, i, f: (b, i, p)),
        scratch_shapes=[pltpu.VMEM((hp, tq, LANES), F32), pltpu.VMEM((hp, tq, 1), F32)],
    )
    return pl.pallas_call(
        functools.partial(_mla_kernel, tq=tq, tk=tk, nk=nk, hp=hp, online=online),
        grid_spec=grid_spec,
        out_shape=jax.ShapeDtypeStruct((B, S, MLA_WIDTH), F32),
        compiler_params=pltpu.CompilerParams(
            dimension_semantics=("arbitrary", "arbitrary", "arbitrary"),
            vmem_limit_bytes=VMEM_LIMIT_BYTES),
        name="mla_attn_online" if online else "mla_attn",
    )(flags, q, k, v, pos_col, pos_row)


def _softplus(z):
    return jnp.maximum(z, 0.0) + jnp.log(1.0 + jnp.exp(-jnp.abs(z)))


def _sb_kernel(kstart_ref, q_ref, k_ref, v_ref, posq_ref, posk_ref, o_ref, run_ref, acc_ref,
               *, tq, tk, win):
    b = pl.program_id(0)
    qi = pl.program_id(2)
    lane = lax.broadcasted_iota(jnp.int32, (tq, LANES), 1)
    row_i = lax.broadcasted_iota(jnp.int32, (tk, tk), 0)
    col_i = lax.broadcasted_iota(jnp.int32, (tk, tk), 1)
    tri = (row_i >= col_i).astype(BF16)
    kstart = kstart_ref[b, qi]
    q_pair = q_ref[0]
    zero = jnp.zeros_like(q_pair)
    q2 = jnp.concatenate([jnp.where(lane < SB_HEAD_DIM, q_pair, zero),
                          jnp.where(lane >= SB_HEAD_DIM, q_pair, zero)], axis=0)
    qpos = jnp.concatenate([posq_ref[0], posq_ref[0]], axis=0)

    def suffix_sums(l1m):
        hi = lax.bitcast_convert_type(
            lax.bitcast_convert_type(l1m, jnp.int32) & jnp.int32(-65536), F32)
        lo = l1m - hi
        both = _dot(jnp.concatenate([hi.astype(BF16), lo.astype(BF16)], axis=0), tri)
        return both[:2 * tq] + both[2 * tq:]

    run_ref[...] = jnp.zeros(run_ref.shape, F32)
    acc_ref[...] = jnp.zeros(acc_ref.shape, F32)

    @pl.when(kstart >= win - 1)
    def _():
        k0 = pl.multiple_of((kstart - (win - 1)) * tk, tk)
        causal = posk_ref[0, :, pl.ds(k0, win * tk)] < qpos
        z = _dot_nt(q2, k_ref[0, pl.ds(k0, win * tk), :])
        l1m = jnp.where(causal, -_softplus(z), 0.0)
        run = jnp.zeros((2 * tq, 1), F32)
        a_blocks = [None] * win
        for j in reversed(range(win)):
            cs = slice(j * tk, (j + 1) * tk)
            suffix = suffix_sums(l1m[:, cs])
            a = jnp.exp(z[:, cs] + suffix + run)
            a_blocks[j] = jnp.where(causal[:, cs], a, 0.0).astype(BF16)
            run = run + suffix[:, 0:1]
        acc_ref[...] = _dot(jnp.concatenate(a_blocks, axis=1), v_ref[0, pl.ds(k0, win * tk), :])
        run_ref[...] = run

    def block(kb):
        k0 = pl.multiple_of(kb * tk, tk)
        z = _dot_nt(q2, k_ref[0, pl.ds(k0, tk), :])
        causal = posk_ref[0, :, pl.ds(k0, tk)] < qpos
        l1m = jnp.where(causal, -_softplus(z), 0.0)
        suffix = suffix_sums(l1m)
        run = run_ref[...]
        a = jnp.where(causal, jnp.exp(z + suffix + run), 0.0)
        acc_ref[...] += _dot(a.astype(BF16), v_ref[0, pl.ds(k0, tk), :])
        run_new = run + suffix[:, 0:1]
        run_ref[...] = run_new
        return jnp.max(run_new)

    def cond(c):
        kb, top = c
        return (kb >= 0) & (top >= SB_LOG_ZERO)

    def body(c):
        kb, _ = c
        return kb - 1, block(kb)

    kb_first = jnp.where(kstart >= win - 1, kstart - win, kstart)
    lax.while_loop(cond, body, (kb_first, jnp.max(run_ref[...])))

    o_ref[0] = jnp.where(lane < SB_HEAD_DIM, acc_ref[:tq], acc_ref[tq:])


def _sb_call(kstart, q, k, v, pos_col, pos_row, tq, tk, win):
    B, S, _ = q.shape
    nq = S // tq
    grid_spec = pltpu.PrefetchScalarGridSpec(
        num_scalar_prefetch=1,
        grid=(B, SB_HEADS // 2, nq),
        in_specs=[
            pl.BlockSpec((1, tq, LANES), lambda b, p, i, s: (b, i, p)),
            pl.BlockSpec((1, S, LANES), lambda b, p, i, s: (b, 0, p)),
            pl.BlockSpec((1, S, LANES), lambda b, p, i, s: (b, 0, p)),
            pl.BlockSpec((1, tq, 1), lambda b, p, i, s: (b, i, 0)),
            pl.BlockSpec((1, 1, S), lambda b, p, i, s: (b, 0, 0)),
        ],
        out_specs=pl.BlockSpec((1, tq, LANES), lambda b, p, i, s: (b, i, p)),
        scratch_shapes=[pltpu.VMEM((2 * tq, 1), F32), pltpu.VMEM((2 * tq, LANES), F32)],
    )
    return pl.pallas_call(
        functools.partial(_sb_kernel, tq=tq, tk=tk, win=win),
        grid_spec=grid_spec,
        out_shape=jax.ShapeDtypeStruct((B, S, SB_WIDTH), F32),
        compiler_params=pltpu.CompilerParams(
            dimension_semantics=("arbitrary", "arbitrary", "arbitrary"),
            vmem_limit_bytes=VMEM_LIMIT_BYTES),
        name="sb_attn",
    )(kstart, q, k, v, pos_col, pos_row)


def _split_bf16(a):
    hi = a.astype(BF16)
    lo = (a - hi.astype(F32)).astype(BF16)
    return hi, lo


def _merge_kernel(x_ref, om_ref, os_ref, gm_ref, gs_ref, wo_ref, g2_ref, wrh_ref, wrl_ref,
                  bias_ref, wsgu_ref, wsd_ref, base_ref, hn_ref, gates_ref):
    mm = _rms(om_ref[...], gm_ref[...]).astype(BF16)
    ms = _rms(os_ref[...], gs_ref[...]).astype(BF16)
    h = x_ref[...] + _dot(mm, wo_ref[:MLA_WIDTH, :]) + _dot(ms, wo_ref[MLA_WIDTH:, :])
    hn = _rms(h, g2_ref[...])
    hn_hi, hn_lo = _split_bf16(hn)
    hn_ref[...] = hn_hi

    logits = (_dot_nt(wrh_ref[...], hn_hi) + _dot_nt(wrh_ref[...], hn_lo)
              + _dot_nt(wrl_ref[...], hn_hi))
    scores = jax.nn.sigmoid(logits)
    work = scores + bias_ref[...]
    eidx = lax.broadcasted_iota(jnp.int32, work.shape, 0)
    picked = jnp.zeros(work.shape, jnp.bool_)
    for _ in range(TOP_K):
        top = jnp.max(work, axis=0, keepdims=True)
        first = jnp.min(jnp.where(work == top, eidx, N_EXPERTS), axis=0, keepdims=True)
        hit = eidx == first
        picked = picked | hit
        work = jnp.where(hit, -jnp.inf, work)
    sel = jnp.where(picked, scores, 0.0)
    gates_ref[...] = sel / jnp.sum(sel, axis=0, keepdims=True) * ROUTED_SCALE

    sgu = _dot(hn_hi, wsgu_ref[...])
    act = (jax.nn.silu(sgu[:, :SHARED_FF]) * sgu[:, SHARED_FF:]).astype(BF16)
    base_ref[...] = h + _dot(act, wsd_ref[...])


def _merge_call(x2, o_mla, o_sb, gm, gs, wo, g2, wrh, wrl, bias, wsgu, wsd, tm):
    T = x2.shape[0]
    full = lambda a: pl.BlockSpec(a.shape, lambda i: (0,) * a.ndim)
    row = lambda w: pl.BlockSpec((tm, w), lambda i: (i, 0))
    weights = (gm, gs, wo, g2, wrh, wrl, bias, wsgu, wsd)
    return pl.pallas_call(
        _merge_kernel,
        grid=(T // tm,),
        in_specs=[row(D_MODEL), row(MLA_WIDTH), row(SB_WIDTH)] + [full(w) for w in weights],
        out_specs=[row(D_MODEL), row(D_MODEL), pl.BlockSpec((N_EXPERTS, tm), lambda i: (0, i))],
        out_shape=[jax.ShapeDtypeStruct((T, D_MODEL), F32),
                   jax.ShapeDtypeStruct((T, D_MODEL), BF16),
                   jax.ShapeDtypeStruct((N_EXPERTS, T), F32)],
        compiler_params=pltpu.CompilerParams(dimension_semantics=("arbitrary",),
                                             vmem_limit_bytes=VMEM_LIMIT_BYTES),
        name="merge",
    )(x2, o_mla, o_sb, *weights)


def _moe_kernel(hn_ref, gates_ref, base_ref, wgu_ref, wd_ref, o_ref, *, eg):
    g = pl.program_id(1)

    @pl.when(g == 0)
    def _():
        o_ref[...] = base_ref[...]

    hn = hn_ref[...]
    gates = gates_ref[...]
    lane = lax.broadcasted_iota(jnp.int32, gates.shape, 1)
    acc = jnp.zeros(o_ref.shape, F32)
    for j in range(eg):
        e = g * eg + j
        gate = jnp.sum(jnp.where(lane == e, gates, 0.0), axis=1, keepdims=True)
        gu = _dot(hn, wgu_ref[j])
        act = jax.nn.silu(gu[:, :EXPERT_FF]) * gu[:, EXPERT_FF:] * gate
        acc = acc + _dot(act.astype(BF16), wd_ref[j])
    o_ref[...] += acc


def _moe_call(hn, gates, base, wgu, wd, tm, eg):
    T = hn.shape[0]
    return pl.pallas_call(
        functools.partial(_moe_kernel, eg=eg),
        grid=(T // tm, N_EXPERTS // eg),
        in_specs=[
            pl.BlockSpec((tm, D_MODEL), lambda i, g: (i, 0)),
            pl.BlockSpec((tm, N_EXPERTS), lambda i, g: (i, 0)),
            pl.BlockSpec((tm, D_MODEL), lambda i, g: (i, 0)),
            pl.BlockSpec((eg, D_MODEL, 2 * EXPERT_FF), lambda i, g: (g, 0, 0)),
            pl.BlockSpec((eg, EXPERT_FF, D_MODEL), lambda i, g: (g, 0, 0)),
        ],
        out_specs=pl.BlockSpec((tm, D_MODEL), lambda i, g: (i, 0)),
        out_shape=jax.ShapeDtypeStruct((T, D_MODEL), F32),
        compiler_params=pltpu.CompilerParams(dimension_semantics=("arbitrary", "arbitrary"),
                                             vmem_limit_bytes=VMEM_LIMIT_BYTES),
        name="moe",
    )(hn, gates, base, wgu, wd)


def _slot_cols(w, width):
    k = w.shape[0]
    w = w.reshape(k, MLA_HEADS, width)
    return jnp.pad(w, ((0, 0), (0, 0), (0, SLOT - width))).reshape(k, MLA_HEADS * SLOT)


def _block_flags(pos, tq, tk, chunked):
    B, S = pos.shape
    p = jnp.right_shift(pos, 6) if chunked else pos
    qmin = p.reshape(B, S // tq, tq).min(-1)[:, :, None]
    qmax = p.reshape(B, S // tq, tq).max(-1)[:, :, None]
    kmin = p.reshape(B, S // tk, tk).min(-1)[:, None, :]
    kmax = p.reshape(B, S // tk, tk).max(-1)[:, None, :]
    if chunked:
        none, all_ = kmin > qmax, kmax <= qmin
    else:
        none, all_ = kmin >= qmax, kmax < qmin
    return jnp.where(none, 0, jnp.where(all_, 1, 2)).astype(jnp.int32)


def kernel(x, positions, norm1, w_in, q_a_norm, w_uq, kv_a_norm, w_ukv, q_norm, k_norm,
           out_norm_mla, out_norm_sb, w_o, norm2, w_router, router_bias, w_gate_up, w_down,
           w_shared_gate_up, w_shared_down):
    B, S, D = x.shape
    T = B * S
    c0 = Q_LORA
    c1 = c0 + KV_LORA
    c2 = c1 + MLA_ROPE

    row2 = lambda g: g.reshape(1, -1).astype(F32)
    wcq = w_in[:, :c0].astype(BF16)
    wckv = w_in[:, c0:c1].astype(BF16)
    wkr = jnp.pad(w_in[:, c1:c2], ((0, 0), (MLA_NOPE, SLOT - MLA_QK))).astype(BF16)
    wsb = w_in[:, c2:].astype(BF16)
    wuq = _slot_cols(w_uq, MLA_QK).astype(BF16)
    w_ukv3 = w_ukv.reshape(KV_LORA, MLA_HEADS, MLA_NOPE + MLA_V)
    wuk = _slot_cols(w_ukv3[:, :, :MLA_NOPE].reshape(KV_LORA, -1), MLA_NOPE).astype(BF16)
    wuv = _slot_cols(w_ukv3[:, :, MLA_NOPE:].reshape(KV_LORA, MLA_WIDTH), MLA_V).astype(BF16)
    slot_lane = jnp.arange(SLOT)
    vone = jnp.tile((slot_lane >= MLA_V).astype(F32), MLA_HEADS).reshape(1, MLA_HEADS * SLOT)
    gq = jnp.pad(q_norm, (0, SLOT - MLA_QK)).reshape(1, SLOT)
    gk = jnp.pad(k_norm, (0, SLOT - MLA_QK)).reshape(1, SLOT)
    inv = ROPE_THETA ** (-jnp.arange(HALF_ROPE, dtype=F32) / HALF_ROPE)
    inv_slot = jnp.pad(jnp.concatenate([inv, inv]), (MLA_NOPE, SLOT - MLA_QK)).reshape(1, SLOT)

    bound = (math.sqrt(MLA_QK) * LOG2E * BOUND_MARGIN) * jnp.max(jnp.abs(q_norm)) * jnp.max(jnp.abs(k_norm))
    bounded = bound <= MLA_MAX_SHIFT
    shift = jnp.where(bounded, bound, 0.0)
    qaug = (slot_lane == MLA_QK).astype(F32).reshape(1, SLOT)
    kaug = -shift * qaug

    x2 = x.reshape(T, D)
    pos_col = positions.reshape(T, 1)
    q, k, v, sq, sk, sv = _proj_call(
        x2, pos_col,
        (row2(norm1), wcq, wckv, wkr, wsb, row2(q_a_norm), wuq, row2(kv_a_norm), wuk, wuv, vone,
         gq, gk, inv_slot, qaug, kaug), tm=512)

    pos_c3 = positions.reshape(B, S, 1)
    pos_r3 = positions.reshape(B, 1, S)
    r3 = lambda a: a.reshape(B, S, a.shape[-1])

    tq_m, tk_m = 512, 512
    mla_args = (_block_flags(positions, tq_m, tk_m, True), r3(q), r3(k), r3(v), pos_c3, pos_r3)
    o_mla = lax.cond(bounded,
                     lambda *a: _mla_call(*a, tq_m, tk_m, hp=4, online=False),
                     lambda *a: _mla_call(*a, tq_m, tk_m, hp=2, online=True), *mla_args)

    tq_s, tk_s = 256, 128
    sb_flags = _block_flags(positions, tq_s, tk_s, False)
    kstart = jnp.max(jnp.where(sb_flags != 0, jnp.arange(S // tk_s, dtype=jnp.int32), -1), axis=-1)
    o_sb = _sb_call(kstart.astype(jnp.int32), r3(sq), r3(sk), r3(sv), pos_c3, pos_r3,
                    tq_s, tk_s, win=4)

    wr_t = w_router.T
    wrh = wr_t.astype(BF16)
    wrl = (wr_t - wrh.astype(F32)).astype(BF16)
    base, hn, gates_t = _merge_call(
        x2, o_mla.reshape(T, MLA_WIDTH), o_sb.reshape(T, SB_WIDTH), row2(out_norm_mla),
        row2(out_norm_sb), w_o.astype(BF16), row2(norm2), wrh, wrl,
        router_bias.reshape(N_EXPERTS, 1).astype(F32), w_shared_gate_up.astype(BF16),
        w_shared_down.astype(BF16), tm=512)

    out = _moe_call(hn, gates_t.T, base, w_gate_up.astype(BF16), w_down.astype(BF16),
                    tm=1024, eg=4)
    return out.reshape(B, S, D)
```

```python
import functools
import math

import jax
import jax.numpy as jnp
from jax import lax
from jax.experimental import pallas as pl
from jax.experimental.pallas import tpu as pltpu

D_MODEL = 1024
CHUNK = 64
MLA_HEADS = 8
MLA_NOPE = 64
MLA_ROPE = 32
MLA_QK = MLA_NOPE + MLA_ROPE
MLA_V = 64
Q_LORA = 256
KV_LORA = 128
ROPE_THETA = 10000.0
SB_HEADS = 8
SB_HEAD_DIM = 64
SB_WIDTH = SB_HEADS * SB_HEAD_DIM
MLA_WIDTH = MLA_HEADS * MLA_V
N_EXPERTS = 64
TOP_K = 8
EXPERT_FF = 256
SHARED_FF = 256
ROUTED_SCALE = 2.5
EPS = 1e-6
NEG = -1e30

LANES = 128
VMEM_LIMIT_BYTES = 56 * 1024 * 1024

SLOT = LANES
HALF_ROPE = MLA_ROPE // 2

SB_LOG_ZERO = -90.0

LOG2E = 1.4426950408889634
BOUND_MARGIN = 1.02
MLA_MAX_SHIFT = 50.0

BF16 = jnp.bfloat16
F32 = jnp.float32


def _rms(x, g):
    return x * lax.rsqrt(jnp.mean(x * x, axis=-1, keepdims=True) + EPS) * g


def _dot(a, b):
    return jnp.dot(a, b, preferred_element_type=F32)


def _dot_nt(a, b):
    return lax.dot_general(a, b, (((1,), (1,)), ((), ())), preferred_element_type=F32)


def _proj_kernel(x_ref, pos_ref, g1_ref, wcq_ref, wckv_ref, wkr_ref, wsb_ref, gqa_ref, wuq_ref,
                 gkva_ref, wuk_ref, wuv_ref, vone_ref, gq_ref, gk_ref, inv_ref, qaug_ref, kaug_ref,
                 q_ref, k_ref, v_ref, sq_ref, sk_ref, sv_ref):
    x = x_ref[...]
    xn = _rms(x, g1_ref[...]).astype(BF16)

    cq = _rms(_dot(xn, wcq_ref[...]), gqa_ref[...]).astype(BF16)
    q = _dot(cq, wuq_ref[...])
    ckv = _rms(_dot(xn, wckv_ref[...]), gkva_ref[...]).astype(BF16)
    kn = _dot(ckv, wuk_ref[...])
    v_ref[...] = (_dot(ckv, wuv_ref[...]) + vone_ref[...]).astype(BF16)
    kr = _dot(xn, wkr_ref[...])

    sb = _dot(xn, wsb_ref[...])
    sq_ref[...] = (sb[:, :SB_WIDTH] * (1.0 / math.sqrt(SB_HEAD_DIM))).astype(BF16)
    sk_ref[...] = sb[:, SB_WIDTH:2 * SB_WIDTH].astype(BF16)
    sv_ref[...] = sb[:, 2 * SB_WIDTH:].astype(BF16)

    ang = pos_ref[...].astype(F32) * inv_ref[...]
    cos = jnp.cos(ang)
    sin = jnp.sin(ang)
    lane = lax.broadcasted_iota(jnp.int32, ang.shape, 1)
    first_half = lane < MLA_NOPE + HALF_ROPE

    def rope(t):
        up = pltpu.roll(t, SLOT - HALF_ROPE, 1)
        down = pltpu.roll(t, HALF_ROPE, 1)
        return t * cos + jnp.where(first_half, -up, down) * sin

    def head_norm(t, g):
        ss = jnp.sum(t * t, axis=-1, keepdims=True) * (1.0 / MLA_QK)
        return t * lax.rsqrt(ss + EPS) * g

    scale = LOG2E / math.sqrt(MLA_QK)
    for h in range(MLA_HEADS):
        sl = slice(h * SLOT, (h + 1) * SLOT)
        qh = rope(head_norm(q[:, sl], gq_ref[...])) * scale + qaug_ref[...]
        q_ref[:, sl] = qh.astype(BF16)
        kh = rope(head_norm(kn[:, sl] + kr, gk_ref[...])) + kaug_ref[...]
        k_ref[:, sl] = kh.astype(BF16)


def _proj_call(x2, pos_col, weights, tm):
    T = x2.shape[0]
    full = lambda a: pl.BlockSpec(a.shape, lambda i: (0,) * a.ndim)
    row = lambda w: pl.BlockSpec((tm, w), lambda i: (i, 0))
    out_w = (MLA_HEADS * SLOT, MLA_HEADS * SLOT, MLA_HEADS * SLOT, SB_WIDTH, SB_WIDTH, SB_WIDTH)
    return pl.pallas_call(
        _proj_kernel,
        grid=(T // tm,),
        in_specs=[row(D_MODEL), row(1)] + [full(w) for w in weights],
        out_specs=[row(w) for w in out_w],
        out_shape=[jax.ShapeDtypeStruct((T, w), BF16) for w in out_w],
        compiler_params=pltpu.CompilerParams(dimension_semantics=("arbitrary",),
                                             vmem_limit_bytes=VMEM_LIMIT_BYTES),
        name="proj",
    )(x2, pos_col, *weights)


def _mla_kernel(flags_ref, q_ref, k_ref, v_ref, posq_ref, posk_ref, o_ref, acc_ref, m_ref,
                *, tq, tk, nk, hp, online):
    b = pl.program_id(0)
    qi = pl.program_id(2)
    qchunk = jnp.right_shift(posq_ref[0], 6)
    acc_ref[...] = jnp.zeros(acc_ref.shape, F32)
    if online:
        m_ref[...] = jnp.full(m_ref.shape, NEG, F32)

    def block(kb, masked):
        k0 = pl.multiple_of(kb * tk, tk)
        if masked:
            kchunk = jnp.right_shift(posk_ref[0, :, pl.ds(k0, tk)], 6)
            vis = kchunk <= qchunk
        for h in range(hp):
            hs = slice(h * SLOT, (h + 1) * SLOT)
            s = _dot_nt(q_ref[0, :, hs], k_ref[0, pl.ds(k0, tk), hs])
            if masked:
                s = jnp.where(vis, s, NEG)
            vh = v_ref[0, pl.ds(k0, tk), hs]
            if online:
                m_old = m_ref[h]
                m_new = jnp.maximum(m_old, jnp.max(s, axis=-1, keepdims=True))
                p = jnp.exp2(s - m_new)
                acc_ref[h] = jnp.exp2(m_old - m_new) * acc_ref[h] + _dot(p.astype(BF16), vh)
                m_ref[h] = m_new
            else:
                acc_ref[h] += _dot(jnp.exp2(s).astype(BF16), vh)

    def body(kb, carry):
        flag = flags_ref[b, qi, kb]

        @pl.when(flag == 1)
        def _():
            block(kb, False)

        @pl.when(flag == 2)
        def _():
            block(kb, True)

        return carry

    lax.fori_loop(0, nk, body, 0)
    lane = lax.broadcasted_iota(jnp.int32, (tq, LANES), 1)
    for j in range(hp // 2):
        o0 = acc_ref[2 * j] / pltpu.roll(acc_ref[2 * j], MLA_V, 1)
        o1 = acc_ref[2 * j + 1] / pltpu.roll(acc_ref[2 * j + 1], MLA_V, 1)
        o_ref[0, :, j * LANES:(j + 1) * LANES] = jnp.where(lane < MLA_V, o0, pltpu.roll(o1, MLA_V, 1))


def _mla_call(flags, q, k, v, pos_col, pos_row, tq, tk, hp, online):
    B, S, _ = q.shape
    nq, nk = S // tq, S // tk
    grid_spec = pltpu.PrefetchScalarGridSpec(
        num_scalar_prefetch=1,
        grid=(B, MLA_HEADS // hp, nq),
        in_specs=[
            pl.BlockSpec((1, tq, hp * SLOT), lambda b, p, i, f: (b, i, p)),
            pl.BlockSpec((1, S, hp * SLOT), lambda b, p, i, f: (b, 0, p)),
            pl.BlockSpec((1, S, hp * SLOT), lambda b, p, i, f: (b, 0, p)),
            pl.BlockSpec((1, tq, 1), lambda b, p, i, f: (b, i, 0)),
            pl.BlockSpec((1, 1, S), lambda b, p, i, f: (b, 0, 0)),
        ],
        out_specs=pl.BlockSpec((1, tq, hp * MLA_V), lambda b, p, i, f: (b, i, p)),
        scratch_shapes=[pltpu.VMEM((hp, tq, LANES), F32), pltpu.VMEM((hp, tq, 1), F32)],
    )
    return pl.pallas_call(
        functools.partial(_mla_kernel, tq=tq, tk=tk, nk=nk, hp=hp, online=online),
        grid_spec=grid_spec,
        out_shape=jax.ShapeDtypeStruct((B, S, MLA_WIDTH), F32),
        compiler_params=pltpu.CompilerParams(
            dimension_semantics=("arbitrary", "arbitrary", "arbitrary"),
            vmem_limit_bytes=VMEM_LIMIT_BYTES),
        name="mla_attn_online" if online else "mla_attn",
    )(flags, q, k, v, pos_col, pos_row)


def _softplus(z):
    return jnp.maximum(z, 0.0) + jnp.log(1.0 + jnp.exp(-jnp.abs(z)))


def _sb_kernel(kstart_ref, q_ref, k_ref, v_ref, posq_ref, posk_ref, o_ref, run_ref, acc_ref,
               *, tq, tk, win):
    b = pl.program_id(0)
    qi = pl.program_id(2)
    lane = lax.broadcasted_iota(jnp.int32, (tq, LANES), 1)
    row_i = lax.broadcasted_iota(jnp.int32, (tk, tk), 0)
    col_i = lax.broadcasted_iota(jnp.int32, (tk, tk), 1)
    tri = (row_i >= col_i).astype(BF16)
    kstart = kstart_ref[b, qi]
    q_pair = q_ref[0]
    zero = jnp.zeros_like(q_pair)
    q2 = jnp.concatenate([jnp.where(lane < SB_HEAD_DIM, q_pair, zero),
                          jnp.where(lane >= SB_HEAD_DIM, q_pair, zero)], axis=0)
    qpos = jnp.concatenate([posq_ref[0], posq_ref[0]], axis=0)

    def suffix_sums(l1m):
        hi = lax.bitcast_convert_type(
            lax.bitcast_convert_type(l1m, jnp.int32) & jnp.int32(-65536), F32)
        lo = l1m - hi
        both = _dot(jnp.concatenate([hi.astype(BF16), lo.astype(BF16)], axis=0), tri)
        return both[:2 * tq] + both[2 * tq:]

    run_ref[...] = jnp.zeros(run_ref.shape, F32)
    acc_ref[...] = jnp.zeros(acc_ref.shape, F32)

    @pl.when(kstart >= win - 1)
    def _():
        k0 = pl.multiple_of((kstart - (win - 1)) * tk, tk)
        causal = posk_ref[0, :, pl.ds(k0, win * tk)] < qpos
        z = _dot_nt(q2, k_ref[0, pl.ds(k0, win * tk), :])
        l1m = jnp.where(causal, -_softplus(z), 0.0)
        run = jnp.zeros((2 * tq, 1), F32)
        a_blocks = [None] * win
        for j in reversed(range(win)):
            cs = slice(j * tk, (j + 1) * tk)
            suffix = suffix_sums(l1m[:, cs])
            a = jnp.exp(z[:, cs] + suffix + run)
            a_blocks[j] = jnp.where(causal[:, cs], a, 0.0).astype(BF16)
            run = run + suffix[:, 0:1]
        acc_ref[...] = _dot(jnp.concatenate(a_blocks, axis=1), v_ref[0, pl.ds(k0, win * tk), :])
        run_ref[...] = run

    def block(kb):
        k0 = pl.multiple_of(kb * tk, tk)
        z = _dot_nt(q2, k_ref[0, pl.ds(k0, tk), :])
        causal = posk_ref[0, :, pl.ds(k0, tk)] < qpos
        l1m = jnp.where(causal, -_softplus(z), 0.0)
        suffix = suffix_sums(l1m)
        run = run_ref[...]
        a = jnp.where(causal, jnp.exp(z + suffix + run), 0.0)
        acc_ref[...] += _dot(a.astype(BF16), v_ref[0, pl.ds(k0, tk), :])
        run_new = run + suffix[:, 0:1]
        run_ref[...] = run_new
        return jnp.max(run_new)

    def cond(c):
        kb, top = c
        return (kb >= 0) & (top >= SB_LOG_ZERO)

    def body(c):
        kb, _ = c
        return kb - 1, block(kb)

    kb_first = jnp.where(kstart >= win - 1, kstart - win, kstart)
    lax.while_loop(cond, body, (kb_first, jnp.max(run_ref[...])))

    o_ref[0] = jnp.where(lane < SB_HEAD_DIM, acc_ref[:tq], acc_ref[tq:])


def _sb_call(kstart, q, k, v, pos_col, pos_row, tq, tk, win):
    B, S, _ = q.shape
    nq = S // tq
    grid_spec = pltpu.PrefetchScalarGridSpec(
        num_scalar_prefetch=1,
        grid=(B, SB_HEADS // 2, nq),
        in_specs=[
            pl.BlockSpec((1, tq, LANES), lambda b, p, i, s: (b, i, p)),
            pl.BlockSpec((1, S, LANES), lambda b, p, i, s: (b, 0, p)),
            pl.BlockSpec((1, S, LANES), lambda b, p, i, s: (b, 0, p)),
            pl.BlockSpec((1, tq, 1), lambda b, p, i, s: (b, i, 0)),
            pl.BlockSpec((1, 1, S), lambda b, p, i, s: (b, 0, 0)),
        ],
        out_specs=pl.BlockSpec((1, tq, LANES), lambda b, p, i, s: (b, i, p)),
        scratch_shapes=[pltpu.VMEM((2 * tq, 1), F32), pltpu.VMEM((2 * tq, LANES), F32)],
    )
    return pl.pallas_call(
        functools.partial(_sb_kernel, tq=tq, tk=tk, win=win),
        grid_spec=grid_spec,
        out_shape=jax.ShapeDtypeStruct((B, S, SB_WIDTH), F32),
        compiler_params=pltpu.CompilerParams(
            dimension_semantics=("arbitrary", "arbitrary", "arbitrary"),
            vmem_limit_bytes=VMEM_LIMIT_BYTES),
        name="sb_attn",
    )(kstart, q, k, v, pos_col, pos_row)


def _split_bf16(a):
    hi = a.astype(BF16)
    lo = (a - hi.astype(F32)).astype(BF16)
    return hi, lo


def _merge_kernel(x_ref, om_ref, os_ref, gm_ref, gs_ref, wo_ref, g2_ref, wrh_ref, wrl_ref,
                  bias_ref, wsgu_ref, wsd_ref, base_ref, hn_ref, eidx_ref, gate_ref):
    mm = _rms(om_ref[...], gm_ref[...]).astype(BF16)
    ms = _rms(os_ref[...], gs_ref[...]).astype(BF16)
    h = x_ref[...] + _dot(mm, wo_ref[:MLA_WIDTH, :]) + _dot(ms, wo_ref[MLA_WIDTH:, :])
    hn = _rms(h, g2_ref[...])
    hn_hi, hn_lo = _split_bf16(hn)
    hn_ref[...] = hn_hi

    logits = (_dot_nt(wrh_ref[...], hn_hi) + _dot_nt(wrh_ref[...], hn_lo)
              + _dot_nt(wrl_ref[...], hn_hi))
    scores = jax.nn.sigmoid(logits)
    work = scores + bias_ref[...]
    eidx = lax.broadcasted_iota(jnp.int32, work.shape, 0)
    chosen, chosen_score = [], []
    for _ in range(TOP_K):
        top = jnp.max(work, axis=0, keepdims=True)
        first = jnp.min(jnp.where(work == top, eidx, N_EXPERTS), axis=0, keepdims=True)
        hit = eidx == first
        chosen.append(first)
        chosen_score.append(jnp.sum(jnp.where(hit, scores, 0.0), axis=0, keepdims=True))
        work = jnp.where(hit, -jnp.inf, work)
    sel = jnp.concatenate(chosen_score, axis=0)
    eidx_ref[...] = jnp.concatenate(chosen, axis=0)
    gate_ref[...] = sel / jnp.sum(sel, axis=0, keepdims=True) * ROUTED_SCALE

    sgu = _dot(hn_hi, wsgu_ref[...])
    act = (jax.nn.silu(sgu[:, :SHARED_FF]) * sgu[:, SHARED_FF:]).astype(BF16)
    base_ref[...] = h + _dot(act, wsd_ref[...])


def _merge_call(x2, o_mla, o_sb, gm, gs, wo, g2, wrh, wrl, bias, wsgu, wsd, tm):
    T = x2.shape[0]
    full = lambda a: pl.BlockSpec(a.shape, lambda i: (0,) * a.ndim)
    row = lambda w: pl.BlockSpec((tm, w), lambda i: (i, 0))
    weights = (gm, gs, wo, g2, wrh, wrl, bias, wsgu, wsd)
    return pl.pallas_call(
        _merge_kernel,
        grid=(T // tm,),
        in_specs=[row(D_MODEL), row(MLA_WIDTH), row(SB_WIDTH)] + [full(w) for w in weights],
        out_specs=[row(D_MODEL), row(D_MODEL), pl.BlockSpec((TOP_K, tm), lambda i: (0, i)),
                   pl.BlockSpec((TOP_K, tm), lambda i: (0, i))],
        out_shape=[jax.ShapeDtypeStruct((T, D_MODEL), F32),
                   jax.ShapeDtypeStruct((T, D_MODEL), BF16),
                   jax.ShapeDtypeStruct((TOP_K, T), jnp.int32),
                   jax.ShapeDtypeStruct((TOP_K, T), F32)],
        compiler_params=pltpu.CompilerParams(dimension_semantics=("arbitrary",),
                                             vmem_limit_bytes=VMEM_LIMIT_BYTES),
        name="merge",
    )(x2, o_mla, o_sb, *weights)


MOE_TB = 1024
MOE_RT = 64
MOE_MAX_TILES = 4
MOE_ROWS = 12288
MOE_PACK = D_MODEL // 2
MOE_SUB = MOE_PACK // LANES
MOE_UNROLL = 8
HI_MASK = -65536


def _unpack_lo(w):
    return lax.bitcast_convert_type(w << 16, F32)


def _unpack_hi(w):
    return lax.bitcast_convert_type(w & jnp.int32(HI_MASK), F32)


def _moe_kernel(off_ref, nt_ref, pos_ref, gate_ref, xp_ref, base_ref, wgu_ref, wd_ref, o_ref, xy_ref):
    i = pl.program_id(0)
    e = pl.program_id(1)

    @pl.when((i == 0) & (e == 0))
    def _():
        xy_ref[...] = jnp.zeros(xy_ref.shape, jnp.int32)

    @pl.when(e == 0)
    def _():
        def step(tt, c):
            for u in range(MOE_UNROLL):
                t = tt * MOE_UNROLL + u
                row = xp_ref[pl.ds(pl.multiple_of(MOE_SUB * t, MOE_SUB), MOE_SUB), :]
                for k in range(TOP_K):
                    p = pl.multiple_of(pos_ref[TOP_K * t + k], MOE_SUB)
                    xy_ref[pl.ds(p, MOE_SUB), :] = row
            return c

        lax.fori_loop(0, MOE_TB // MOE_UNROLL, step, 0)

    def ffn(r0, m):
        base_row = pl.multiple_of(MOE_SUB * r0, 8)
        words = jnp.concatenate(
            [xy_ref[pl.ds(base_row + j, m, stride=MOE_SUB), :] for j in range(MOE_SUB)], axis=1)
        gu = (_dot(_unpack_lo(words).astype(BF16), wgu_ref[0, :MOE_PACK, :])
              + _dot(_unpack_hi(words).astype(BF16), wgu_ref[0, MOE_PACK:, :]))
        act = (jax.nn.silu(gu[:, :EXPERT_FF]) * gu[:, EXPERT_FF:]).astype(BF16)
        y = _dot(act, wd_ref[0])
        ya = lax.bitcast_convert_type(y[:, :MOE_PACK].astype(BF16).astype(F32), jnp.int32)
        yb = lax.bitcast_convert_type(y[:, MOE_PACK:].astype(BF16).astype(F32), jnp.int32)
        packed = lax.shift_right_logical(ya, 16) | yb
        for j in range(MOE_SUB):
            xy_ref[pl.ds(base_row + j, m, stride=MOE_SUB), :] = packed[:, j * LANES:(j + 1) * LANES]

    off = off_ref[i * N_EXPERTS + e]
    nt = nt_ref[i * N_EXPERTS + e]

    def chunk(c, carry):
        ffn(off + c * (MOE_MAX_TILES * MOE_RT), MOE_MAX_TILES * MOE_RT)
        return carry

    full = nt // MOE_MAX_TILES
    lax.fori_loop(0, full, chunk, 0)
    rest = nt - full * MOE_MAX_TILES
    for tiles in range(1, MOE_MAX_TILES):
        @pl.when(rest == tiles)
        def _():
            ffn(off + full * (MOE_MAX_TILES * MOE_RT), tiles * MOE_RT)

    @pl.when(e == N_EXPERTS - 1)
    def _():
        def step(tt, c):
            for u in range(MOE_UNROLL):
                t = tt * MOE_UNROLL + u
                lo = jnp.zeros((MOE_SUB, LANES), F32)
                hi = jnp.zeros((MOE_SUB, LANES), F32)
                for k in range(TOP_K):
                    p = pl.multiple_of(pos_ref[TOP_K * t + k], MOE_SUB)
                    g = gate_ref[TOP_K * t + k]
                    w = xy_ref[pl.ds(p, MOE_SUB), :]
                    lo = lo + g * _unpack_lo(w)
                    hi = hi + g * _unpack_hi(w)
                routed = jnp.concatenate([lo[j:j + 1] for j in range(MOE_SUB)]
                                         + [hi[j:j + 1] for j in range(MOE_SUB)], axis=1)
                o_ref[pl.ds(t, 1), :] = base_ref[pl.ds(t, 1), :] + routed
            return c

        lax.fori_loop(0, MOE_TB // MOE_UNROLL, step, 0)


def _moe_call(off, nt, pos, gates, xp, base, wgu, wd):
    T = base.shape[0]
    slots = MOE_TB * TOP_K
    grid_spec = pltpu.PrefetchScalarGridSpec(
        num_scalar_prefetch=2,
        grid=(T // MOE_TB, N_EXPERTS),
        in_specs=[
            pl.BlockSpec((slots,), lambda i, e, o, n: (i,), memory_space=pltpu.SMEM),
            pl.BlockSpec((slots,), lambda i, e, o, n: (i,), memory_space=pltpu.SMEM),
            pl.BlockSpec((MOE_TB * MOE_SUB, LANES), lambda i, e, o, n: (i, 0)),
            pl.BlockSpec((MOE_TB, D_MODEL), lambda i, e, o, n: (i, 0)),
            pl.BlockSpec((1, D_MODEL, 2 * EXPERT_FF), lambda i, e, o, n: (e, 0, 0)),
            pl.BlockSpec((1, EXPERT_FF, D_MODEL), lambda i, e, o, n: (e, 0, 0)),
        ],
        out_specs=pl.BlockSpec((MOE_TB, D_MODEL), lambda i, e, o, n: (i, 0)),
        scratch_shapes=[pltpu.VMEM((MOE_ROWS * MOE_SUB, LANES), jnp.int32)],
    )
    return pl.pallas_call(
        _moe_kernel,
        grid_spec=grid_spec,
        out_shape=jax.ShapeDtypeStruct((T, D_MODEL), F32),
        compiler_params=pltpu.CompilerParams(dimension_semantics=("arbitrary", "arbitrary"),
                                             vmem_limit_bytes=VMEM_LIMIT_BYTES),
        name="moe",
    )(off, nt, pos, gates, xp, base, wgu, wd)


def _route_tables(eidx, gate):
    T = eidx.shape[1]
    nblk = T // MOE_TB
    e_tk = eidx.T.reshape(nblk, MOE_TB, TOP_K)
    member = (e_tk[..., None] == jnp.arange(N_EXPERTS, dtype=jnp.int32)).any(axis=2).astype(jnp.int32)
    rank = jnp.cumsum(member, axis=1) - member
    tiles = (member.sum(axis=1) + MOE_RT - 1) // MOE_RT
    off = jnp.cumsum(tiles, axis=1) * MOE_RT - tiles * MOE_RT
    pos = jnp.take_along_axis(off[:, None, :] + rank, e_tk, axis=2)
    return (off.reshape(-1).astype(jnp.int32), tiles.reshape(-1).astype(jnp.int32),
            (pos * MOE_SUB).reshape(-1).astype(jnp.int32), gate.T.reshape(-1))


def _pack_rows(a):
    u = lax.bitcast_convert_type(a, jnp.uint16).astype(jnp.uint32)
    words = u[:, :MOE_PACK] | (u[:, MOE_PACK:] << 16)
    return lax.bitcast_convert_type(words, jnp.int32).reshape(-1, LANES)


def _slot_cols(w, width):
    k = w.shape[0]
    w = w.reshape(k, MLA_HEADS, width)
    return jnp.pad(w, ((0, 0), (0, 0), (0, SLOT - width))).reshape(k, MLA_HEADS * SLOT)


def _block_flags(pos, tq, tk, chunked):
    B, S = pos.shape
    p = jnp.right_shift(pos, 6) if chunked else pos
    qmin = p.reshape(B, S // tq, tq).min(-1)[:, :, None]
    qmax = p.reshape(B, S // tq, tq).max(-1)[:, :, None]
    kmin = p.reshape(B, S // tk, tk).min(-1)[:, None, :]
    kmax = p.reshape(B, S // tk, tk).max(-1)[:, None, :]
    if chunked:
        none, all_ = kmin > qmax, kmax <= qmin
    else:
        none, all_ = kmin >= qmax, kmax < qmin
    return jnp.where(none, 0, jnp.where(all_, 1, 2)).astype(jnp.int32)


def kernel(x, positions, norm1, w_in, q_a_norm, w_uq, kv_a_norm, w_ukv, q_norm, k_norm,
           out_norm_mla, out_norm_sb, w_o, norm2, w_router, router_bias, w_gate_up, w_down,
           w_shared_gate_up, w_shared_down):
    B, S, D = x.shape
    T = B * S
    c0 = Q_LORA
    c1 = c0 + KV_LORA
    c2 = c1 + MLA_ROPE

    row2 = lambda g: g.reshape(1, -1).astype(F32)
    wcq = w_in[:, :c0].astype(BF16)
    wckv = w_in[:, c0:c1].astype(BF16)
    wkr = jnp.pad(w_in[:, c1:c2], ((0, 0), (MLA_NOPE, SLOT - MLA_QK))).astype(BF16)
    wsb = w_in[:, c2:].astype(BF16)
    wuq = _slot_cols(w_uq, MLA_QK).astype(BF16)
    w_ukv3 = w_ukv.reshape(KV_LORA, MLA_HEADS, MLA_NOPE + MLA_V)
    wuk = _slot_cols(w_ukv3[:, :, :MLA_NOPE].reshape(KV_LORA, -1), MLA_NOPE).astype(BF16)
    wuv = _slot_cols(w_ukv3[:, :, MLA_NOPE:].reshape(KV_LORA, MLA_WIDTH), MLA_V).astype(BF16)
    slot_lane = jnp.arange(SLOT)
    vone = jnp.tile((slot_lane >= MLA_V).astype(F32), MLA_HEADS).reshape(1, MLA_HEADS * SLOT)
    gq = jnp.pad(q_norm, (0, SLOT - MLA_QK)).reshape(1, SLOT)
    gk = jnp.pad(k_norm, (0, SLOT - MLA_QK)).reshape(1, SLOT)
    inv = ROPE_THETA ** (-jnp.arange(HALF_ROPE, dtype=F32) / HALF_ROPE)
    inv_slot = jnp.pad(jnp.concatenate([inv, inv]), (MLA_NOPE, SLOT - MLA_QK)).reshape(1, SLOT)

    bound = (math.sqrt(MLA_QK) * LOG2E * BOUND_MARGIN) * jnp.max(jnp.abs(q_norm)) * jnp.max(jnp.abs(k_norm))
    bounded = bound <= MLA_MAX_SHIFT
    shift = jnp.where(bounded, bound, 0.0)
    qaug = (slot_lane == MLA_QK).astype(F32).reshape(1, SLOT)
    kaug = -shift * qaug

    x2 = x.reshape(T, D)
    pos_col = positions.reshape(T, 1)
    q, k, v, sq, sk, sv = _proj_call(
        x2, pos_col,
        (row2(norm1), wcq, wckv, wkr, wsb, row2(q_a_norm), wuq, row2(kv_a_norm), wuk, wuv, vone,
         gq, gk, inv_slot, qaug, kaug), tm=512)

    pos_c3 = positions.reshape(B, S, 1)
    pos_r3 = positions.reshape(B, 1, S)
    r3 = lambda a: a.reshape(B, S, a.shape[-1])

    tq_m, tk_m = 512, 512
    mla_args = (_block_flags(positions, tq_m, tk_m, True), r3(q), r3(k), r3(v), pos_c3, pos_r3)
    o_mla = lax.cond(bounded,
                     lambda *a: _mla_call(*a, tq_m, tk_m, hp=4, online=False),
                     lambda *a: _mla_call(*a, tq_m, tk_m, hp=2, online=True), *mla_args)

    tq_s, tk_s = 256, 128
    sb_flags = _block_flags(positions, tq_s, tk_s, False)
    kstart = jnp.max(jnp.where(sb_flags != 0, jnp.arange(S // tk_s, dtype=jnp.int32), -1), axis=-1)
    o_sb = _sb_call(kstart.astype(jnp.int32), r3(sq), r3(sk), r3(sv), pos_c3, pos_r3,
                    tq_s, tk_s, win=4)

    wr_t = w_router.T
    wrh = wr_t.astype(BF16)
    wrl = (wr_t - wrh.astype(F32)).astype(BF16)
    base, hn, eidx, gate = _merge_call(
        x2, o_mla.reshape(T, MLA_WIDTH), o_sb.reshape(T, SB_WIDTH), row2(out_norm_mla),
        row2(out_norm_sb), w_o.astype(BF16), row2(norm2), wrh, wrl,
        router_bias.reshape(N_EXPERTS, 1).astype(F32), w_shared_gate_up.astype(BF16),
        w_shared_down.astype(BF16), tm=512)

    off, tiles, pos, gates = _route_tables(eidx, gate)
    out = _moe_call(off, tiles, pos, gates, _pack_rows(hn), base, w_gate_up.astype(BF16),
                    w_down.astype(BF16))
    return out.reshape(B, S, D)
```

```python
import functools
import math

import jax
import jax.numpy as jnp
from jax import lax
from jax.experimental import pallas as pl
from jax.experimental.pallas import tpu as pltpu

D_MODEL = 1024
CHUNK = 64
MLA_HEADS = 8
MLA_NOPE = 64
MLA_ROPE = 32
MLA_QK = MLA_NOPE + MLA_ROPE
MLA_V = 64
Q_LORA = 256
KV_LORA = 128
ROPE_THETA = 10000.0
SB_HEADS = 8
SB_HEAD_DIM = 64
SB_WIDTH = SB_HEADS * SB_HEAD_DIM
MLA_WIDTH = MLA_HEADS * MLA_V
N_EXPERTS = 64
TOP_K = 8
EXPERT_FF = 256
SHARED_FF = 256
ROUTED_SCALE = 2.5
EPS = 1e-6
NEG = -1e30

LANES = 128
VMEM_LIMIT_BYTES = 56 * 1024 * 1024

SLOT = LANES
HALF_ROPE = MLA_ROPE // 2

SB_LOG_ZERO = -90.0

LOG2E = 1.4426950408889634
BOUND_MARGIN = 1.02
MLA_MAX_SHIFT = 50.0

BF16 = jnp.bfloat16
F32 = jnp.float32


def _rms(x, g):
    return x * lax.rsqrt(jnp.mean(x * x, axis=-1, keepdims=True) + EPS) * g


def _dot(a, b):
    return jnp.dot(a, b, preferred_element_type=F32)


def _dot_nt(a, b):
    return lax.dot_general(a, b, (((1,), (1,)), ((), ())), preferred_element_type=F32)


def _proj_kernel(x_ref, pos_ref, g1_ref, wcq_ref, wckv_ref, wkr_ref, wsb_ref, gqa_ref, wuq_ref,
                 gkva_ref, wuk_ref, wuv_ref, vone_ref, gq_ref, gk_ref, inv_ref, qaug_ref, kaug_ref,
                 q_ref, k_ref, v_ref, sq_ref, sk_ref, sv_ref):
    x = x_ref[...]
    xn = _rms(x, g1_ref[...]).astype(BF16)

    cq = _rms(_dot(xn, wcq_ref[...]), gqa_ref[...]).astype(BF16)
    q = _dot(cq, wuq_ref[...])
    ckv = _rms(_dot(xn, wckv_ref[...]), gkva_ref[...]).astype(BF16)
    kn = _dot(ckv, wuk_ref[...])
    v_ref[...] = (_dot(ckv, wuv_ref[...]) + vone_ref[...]).astype(BF16)
    kr = _dot(xn, wkr_ref[...])

    sb = _dot(xn, wsb_ref[...])
    sq_ref[...] = (sb[:, :SB_WIDTH] * (1.0 / math.sqrt(SB_HEAD_DIM))).astype(BF16)
    sk_ref[...] = sb[:, SB_WIDTH:2 * SB_WIDTH].astype(BF16)
    sv_ref[...] = sb[:, 2 * SB_WIDTH:].astype(BF16)

    ang = pos_ref[...].astype(F32) * inv_ref[...]
    cos = jnp.cos(ang)
    sin = jnp.sin(ang)
    lane = lax.broadcasted_iota(jnp.int32, ang.shape, 1)
    first_half = lane < MLA_NOPE + HALF_ROPE

    def rope(t):
        up = pltpu.roll(t, SLOT - HALF_ROPE, 1)
        down = pltpu.roll(t, HALF_ROPE, 1)
        return t * cos + jnp.where(first_half, -up, down) * sin

    def head_norm(t, g):
        ss = jnp.sum(t * t, axis=-1, keepdims=True) * (1.0 / MLA_QK)
        return t * lax.rsqrt(ss + EPS) * g

    scale = LOG2E / math.sqrt(MLA_QK)
    for h in range(MLA_HEADS):
        sl = slice(h * SLOT, (h + 1) * SLOT)
        qh = rope(head_norm(q[:, sl], gq_ref[...])) * scale + qaug_ref[...]
        q_ref[:, sl] = qh.astype(BF16)
        kh = rope(head_norm(kn[:, sl] + kr, gk_ref[...])) + kaug_ref[...]
        k_ref[:, sl] = kh.astype(BF16)


def _proj_call(x2, pos_col, weights, tm):
    T = x2.shape[0]
    full = lambda a: pl.BlockSpec(a.shape, lambda i: (0,) * a.ndim)
    row = lambda w: pl.BlockSpec((tm, w), lambda i: (i, 0))
    out_w = (MLA_HEADS * SLOT, MLA_HEADS * SLOT, MLA_HEADS * SLOT, SB_WIDTH, SB_WIDTH, SB_WIDTH)
    return pl.pallas_call(
        _proj_kernel,
        grid=(T // tm,),
        in_specs=[row(D_MODEL), row(1)] + [full(w) for w in weights],
        out_specs=[row(w) for w in out_w],
        out_shape=[jax.ShapeDtypeStruct((T, w), BF16) for w in out_w],
        compiler_params=pltpu.CompilerParams(dimension_semantics=("arbitrary",),
                                             vmem_limit_bytes=VMEM_LIMIT_BYTES),
        name="proj",
    )(x2, pos_col, *weights)


def _mla_kernel(flags_ref, q_ref, k_ref, v_ref, posq_ref, posk_ref, o_ref, acc_ref, m_ref,
                *, tq, tk, nk, hp, online):
    b = pl.program_id(0)
    qi = pl.program_id(2)
    qchunk = jnp.right_shift(posq_ref[0], 6)
    acc_ref[...] = jnp.zeros(acc_ref.shape, F32)
    if online:
        m_ref[...] = jnp.full(m_ref.shape, NEG, F32)

    def block(kb, masked):
        k0 = pl.multiple_of(kb * tk, tk)
        if masked:
            kchunk = jnp.right_shift(posk_ref[0, :, pl.ds(k0, tk)], 6)
            vis = kchunk <= qchunk
        for h in range(hp):
            hs = slice(h * SLOT, (h + 1) * SLOT)
            s = _dot_nt(q_ref[0, :, hs], k_ref[0, pl.ds(k0, tk), hs])
            if masked:
                s = jnp.where(vis, s, NEG)
            vh = v_ref[0, pl.ds(k0, tk), hs]
            if online:
                m_old = m_ref[h]
                m_new = jnp.maximum(m_old, jnp.max(s, axis=-1, keepdims=True))
                p = jnp.exp2(s - m_new)
                acc_ref[h] = jnp.exp2(m_old - m_new) * acc_ref[h] + _dot(p.astype(BF16), vh)
                m_ref[h] = m_new
            else:
                acc_ref[h] += _dot(jnp.exp2(s).astype(BF16), vh)

    def body(kb, carry):
        flag = flags_ref[b, qi, kb]

        @pl.when(flag == 1)
        def _():
            block(kb, False)

        @pl.when(flag == 2)
        def _():
            block(kb, True)

        return carry

    lax.fori_loop(0, nk, body, 0)
    lane = lax.broadcasted_iota(jnp.int32, (tq, LANES), 1)
    for j in range(hp // 2):
        o0 = acc_ref[2 * j] / pltpu.roll(acc_ref[2 * j], MLA_V, 1)
        o1 = acc_ref[2 * j + 1] / pltpu.roll(acc_ref[2 * j + 1], MLA_V, 1)
        o_ref[0, :, j * LANES:(j + 1) * LANES] = jnp.where(lane < MLA_V, o0, pltpu.roll(o1, MLA_V, 1))


def _mla_call(flags, q, k, v, pos_col, pos_row, tq, tk, hp, online):
    B, S, _ = q.shape
    nq, nk = S // tq, S // tk
    grid_spec = pltpu.PrefetchScalarGridSpec(
        num_scalar_prefetch=1,
        grid=(B, MLA_HEADS // hp, nq),
        in_specs=[
            pl.BlockSpec((1, tq, hp * SLOT), lambda b, p, i, f: (b, i, p)),
            pl.BlockSpec((1, S, hp * SLOT), lambda b, p, i, f: (b, 0, p)),
            pl.BlockSpec((1, S, hp * SLOT), lambda b, p, i, f: (b, 0, p)),
            pl.BlockSpec((1, tq, 1), lambda b, p, i, f: (b, i, 0)),
            pl.BlockSpec((1, 1, S), lambda b, p, i, f: (b, 0, 0)),
        ],
        out_specs=pl.BlockSpec((1, tq, hp * MLA_V), lambda b, p, i, f: (b, i, p)),
        scratch_shapes=[pltpu.VMEM((hp, tq, LANES), F32), pltpu.VMEM((hp, tq, 1), F32)],
    )
    return pl.pallas_call(
        functools.partial(_mla_kernel, tq=tq, tk=tk, nk=nk, hp=hp, online=online),
        grid_spec=grid_spec,
        out_shape=jax.ShapeDtypeStruct((B, S, MLA_WIDTH), F32),
        compiler_params=pltpu.CompilerParams(
            dimension_semantics=("arbitrary", "arbitrary", "arbitrary"),
            vmem_limit_bytes=VMEM_LIMIT_BYTES),
        name="mla_attn_online" if online else "mla_attn",
    )(flags, q, k, v, pos_col, pos_row)


def _softplus(z):
    return jnp.maximum(z, 0.0) + jnp.log(1.0 + jnp.exp(-jnp.abs(z)))


def _sb_kernel(kstart_ref, q_ref, k_ref, v_ref, posq_ref, posk_ref, o_ref, run_ref, acc_ref,
               *, tq, tk, win):
    b = pl.program_id(0)
    qi = pl.program_id(2)
    lane = lax.broadcasted_iota(jnp.int32, (tq, LANES), 1)
    row_i = lax.broadcasted_iota(jnp.int32, (tk, tk), 0)
    col_i = lax.broadcasted_iota(jnp.int32, (tk, tk), 1)
    tri = (row_i >= col_i).astype(BF16)
    kstart = kstart_ref[b, qi]
    q_pair = q_ref[0]
    zero = jnp.zeros_like(q_pair)
    q2 = jnp.concatenate([jnp.where(lane < SB_HEAD_DIM, q_pair, zero),
                          jnp.where(lane >= SB_HEAD_DIM, q_pair, zero)], axis=0)
    qpos = jnp.concatenate([posq_ref[0], posq_ref[0]], axis=0)

    def suffix_sums(l1m):
        hi = lax.bitcast_convert_type(
            lax.bitcast_convert_type(l1m, jnp.int32) & jnp.int32(-65536), F32)
        lo = l1m - hi
        both = _dot(jnp.concatenate([hi.astype(BF16), lo.astype(BF16)], axis=0), tri)
        return both[:2 * tq] + both[2 * tq:]

    run_ref[...] = jnp.zeros(run_ref.shape, F32)
    acc_ref[...] = jnp.zeros(acc_ref.shape, F32)

    @pl.when(kstart >= win - 1)
    def _():
        k0 = pl.multiple_of((kstart - (win - 1)) * tk, tk)
        causal = posk_ref[0, :, pl.ds(k0, win * tk)] < qpos
        z = _dot_nt(q2, k_ref[0, pl.ds(k0, win * tk), :])
        l1m = jnp.where(causal, -_softplus(z), 0.0)
        run = jnp.zeros((2 * tq, 1), F32)
        a_blocks = [None] * win
        for j in reversed(range(win)):
            cs = slice(j * tk, (j + 1) * tk)
            suffix = suffix_sums(l1m[:, cs])
            a = jnp.exp(z[:, cs] + suffix + run)
            a_blocks[j] = jnp.where(causal[:, cs], a, 0.0).astype(BF16)
            run = run + suffix[:, 0:1]
        acc_ref[...] = _dot(jnp.concatenate(a_blocks, axis=1), v_ref[0, pl.ds(k0, win * tk), :])
        run_ref[...] = run

    def block(kb):
        k0 = pl.multiple_of(kb * tk, tk)
        z = _dot_nt(q2, k_ref[0, pl.ds(k0, tk), :])
        causal = posk_ref[0, :, pl.ds(k0, tk)] < qpos
        l1m = jnp.where(causal, -_softplus(z), 0.0)
        suffix = suffix_sums(l1m)
        run = run_ref[...]
        a = jnp.where(causal, jnp.exp(z + suffix + run), 0.0)
        acc_ref[...] += _dot(a.astype(BF16), v_ref[0, pl.ds(k0, tk), :])
        run_new = run + suffix[:, 0:1]
        run_ref[...] = run_new
        return jnp.max(run_new)

    def cond(c):
        kb, top = c
        return (kb >= 0) & (top >= SB_LOG_ZERO)

    def body(c):
        kb, _ = c
        return kb - 1, block(kb)

    kb_first = jnp.where(kstart >= win - 1, kstart - win, kstart)
    lax.while_loop(cond, body, (kb_first, jnp.max(run_ref[...])))

    o_ref[0] = jnp.where(lane < SB_HEAD_DIM, acc_ref[:tq], acc_ref[tq:])


def _sb_call(kstart, q, k, v, pos_col, pos_row, tq, tk, win):
    B, S, _ = q.shape
    nq = S // tq
    grid_spec = pltpu.PrefetchScalarGridSpec(
        num_scalar_prefetch=1,
        grid=(B, SB_HEADS // 2, nq),
        in_specs=[
            pl.BlockSpec((1, tq, LANES), lambda b, p, i, s: (b, i, p)),
            pl.BlockSpec((1, S, LANES), lambda b, p, i, s: (b, 0, p)),
            pl.BlockSpec((1, S, LANES), lambda b, p, i, s: (b, 0, p)),
            pl.BlockSpec((1, tq, 1), lambda b, p, i, s: (b, i, 0)),
            pl.BlockSpec((1, 1, S), lambda b, p, i, s: (b, 0, 0)),
        ],
        out_specs=pl.BlockSpec((1, tq, LANES), lambda b, p, i, s: (b, i, p)),
        scratch_shapes=[pltpu.VMEM((2 * tq, 1), F32), pltpu.VMEM((2 * tq, LANES), F32)],
    )
    return pl.pallas_call(
        functools.partial(_sb_kernel, tq=tq, tk=tk, win=win),
        grid_spec=grid_spec,
        out_shape=jax.ShapeDtypeStruct((B, S, SB_WIDTH), F32),
        compiler_params=pltpu.CompilerParams(
            dimension_semantics=("arbitrary", "arbitrary", "arbitrary"),
            vmem_limit_bytes=VMEM_LIMIT_BYTES),
        name="sb_attn",
    )(kstart, q, k, v, pos_col, pos_row)


def _split_bf16(a):
    hi = a.astype(BF16)
    lo = (a - hi.astype(F32)).astype(BF16)
    return hi, lo


def _merge_kernel(x_ref, om_ref, os_ref, gm_ref, gs_ref, wo_ref, g2_ref, wrh_ref, wrl_ref,
                  bias_ref, wsgu_ref, wsd_ref, base_ref, hn_ref, eidx_ref, gate_ref, rank_ref,
                  cnt_ref):
    mm = _rms(om_ref[...], gm_ref[...]).astype(BF16)
    ms = _rms(os_ref[...], gs_ref[...]).astype(BF16)
    h = x_ref[...] + _dot(mm, wo_ref[:MLA_WIDTH, :]) + _dot(ms, wo_ref[MLA_WIDTH:, :])
    hn = _rms(h, g2_ref[...])
    hn_hi, hn_lo = _split_bf16(hn)
    hn_ref[...] = hn_hi

    logits = (_dot_nt(wrh_ref[...], hn_hi) + _dot_nt(wrh_ref[...], hn_lo)
              + _dot_nt(wrl_ref[...], hn_hi))
    scores = jax.nn.sigmoid(logits)
    work = scores + bias_ref[...]
    eidx = lax.broadcasted_iota(jnp.int32, work.shape, 0)
    chosen, chosen_score, hits = [], [], []
    for _ in range(TOP_K):
        top = jnp.max(work, axis=0, keepdims=True)
        first = jnp.min(jnp.where(work == top, eidx, N_EXPERTS), axis=0, keepdims=True)
        hit = eidx == first
        hits.append(hit)
        chosen.append(first)
        chosen_score.append(jnp.sum(jnp.where(hit, scores, 0.0), axis=0, keepdims=True))
        work = jnp.where(hit, -jnp.inf, work)
    sel = jnp.concatenate(chosen_score, axis=0)
    eidx_ref[...] = jnp.concatenate(chosen, axis=0)
    gate_ref[...] = sel / jnp.sum(sel, axis=0, keepdims=True) * ROUTED_SCALE

    tm = work.shape[1]
    member = functools.reduce(jnp.logical_or, hits)
    member = jnp.where(member, 1.0, 0.0)
    tri = (lax.broadcasted_iota(jnp.int32, (tm, tm), 0)
           <= lax.broadcasted_iota(jnp.int32, (tm, tm), 1)).astype(BF16)
    upto = _dot(member.astype(BF16), tri)
    before = upto - member
    rank_ref[...] = jnp.concatenate(
        [jnp.sum(jnp.where(hit, before, 0.0), axis=0, keepdims=True) for hit in hits],
        axis=0).astype(jnp.int32)
    cnt_ref[0] = upto[:, tm - 1:tm].astype(jnp.int32)

    sgu = _dot(hn_hi, wsgu_ref[...])
    act = (jax.nn.silu(sgu[:, :SHARED_FF]) * sgu[:, SHARED_FF:]).astype(BF16)
    base_ref[...] = h + _dot(act, wsd_ref[...])


def _merge_call(x2, o_mla, o_sb, gm, gs, wo, g2, wrh, wrl, bias, wsgu, wsd, tm):
    T = x2.shape[0]
    full = lambda a: pl.BlockSpec(a.shape, lambda i: (0,) * a.ndim)
    row = lambda w: pl.BlockSpec((tm, w), lambda i: (i, 0))
    weights = (gm, gs, wo, g2, wrh, wrl, bias, wsgu, wsd)
    topk = pl.BlockSpec((TOP_K, tm), lambda i: (0, i))
    return pl.pallas_call(
        _merge_kernel,
        grid=(T // tm,),
        in_specs=[row(D_MODEL), row(MLA_WIDTH), row(SB_WIDTH)] + [full(w) for w in weights],
        out_specs=[row(D_MODEL), row(D_MODEL), topk, topk, topk,
                   pl.BlockSpec((1, N_EXPERTS, 1), lambda i: (i, 0, 0))],
        out_shape=[jax.ShapeDtypeStruct((T, D_MODEL), F32),
                   jax.ShapeDtypeStruct((T, D_MODEL), BF16),
                   jax.ShapeDtypeStruct((TOP_K, T), jnp.int32),
                   jax.ShapeDtypeStruct((TOP_K, T), F32),
                   jax.ShapeDtypeStruct((TOP_K, T), jnp.int32),
                   jax.ShapeDtypeStruct((T // tm, N_EXPERTS, 1), jnp.int32)],
        compiler_params=pltpu.CompilerParams(dimension_semantics=("arbitrary",),
                                             vmem_limit_bytes=VMEM_LIMIT_BYTES),
        name="merge",
    )(x2, o_mla, o_sb, *weights)


MOE_TB = 1024
MOE_RT = 64
MOE_MAX_TILES = 4
MOE_ROWS = 12288
MOE_PACK = D_MODEL // 2
MOE_SUB = MOE_PACK // LANES
MOE_UNROLL = 8
HI_MASK = -65536


def _unpack_lo(w):
    return lax.bitcast_convert_type(w << 16, F32)


def _unpack_hi(w):
    return lax.bitcast_convert_type(w & jnp.int32(HI_MASK), F32)


def _moe_kernel(off_ref, nt_ref, pos_ref, gate_ref, xp_ref, base_ref, wgu_ref, wd_ref, o_ref, xy_ref):
    i = pl.program_id(0)
    e = pl.program_id(1)

    @pl.when((i == 0) & (e == 0))
    def _():
        xy_ref[...] = jnp.zeros(xy_ref.shape, jnp.int32)

    @pl.when(e == 0)
    def _():
        def step(tt, c):
            for u in range(MOE_UNROLL):
                t = tt * MOE_UNROLL + u
                row = xp_ref[pl.ds(pl.multiple_of(MOE_SUB * t, MOE_SUB), MOE_SUB), :]
                for k in range(TOP_K):
                    p = pl.multiple_of(pos_ref[TOP_K * t + k], MOE_SUB)
                    xy_ref[pl.ds(p, MOE_SUB), :] = row
            return c

        lax.fori_loop(0, MOE_TB // MOE_UNROLL, step, 0)

    def ffn(r0, m):
        base_row = pl.multiple_of(MOE_SUB * r0, 8)
        words = jnp.concatenate(
            [xy_ref[pl.ds(base_row + j, m, stride=MOE_SUB), :] for j in range(MOE_SUB)], axis=1)
        gu = (_dot(_unpack_lo(words).astype(BF16), wgu_ref[0, :MOE_PACK, :])
              + _dot(_unpack_hi(words).astype(BF16), wgu_ref[0, MOE_PACK:, :]))
        act = (jax.nn.silu(gu[:, :EXPERT_FF]) * gu[:, EXPERT_FF:]).astype(BF16)
        y = _dot(act, wd_ref[0])
        ya = lax.bitcast_convert_type(y[:, :MOE_PACK].astype(BF16).astype(F32), jnp.int32)
        yb = lax.bitcast_convert_type(y[:, MOE_PACK:].astype(BF16).astype(F32), jnp.int32)
        packed = lax.shift_right_logical(ya, 16) | yb
        for j in range(MOE_SUB):
            xy_ref[pl.ds(base_row + j, m, stride=MOE_SUB), :] = packed[:, j * LANES:(j + 1) * LANES]

    off = off_ref[i * N_EXPERTS + e]
    nt = nt_ref[i * N_EXPERTS + e]

    def chunk(c, carry):
        ffn(off + c * (MOE_MAX_TILES * MOE_RT), MOE_MAX_TILES * MOE_RT)
        return carry

    full = nt // MOE_MAX_TILES
    lax.fori_loop(0, full, chunk, 0)
    rest = nt - full * MOE_MAX_TILES
    for tiles in range(1, MOE_MAX_TILES):
        @pl.when(rest == tiles)
        def _():
            ffn(off + full * (MOE_MAX_TILES * MOE_RT), tiles * MOE_RT)

    @pl.when(e == N_EXPERTS - 1)
    def _():
        def step(tt, c):
            for u in range(MOE_UNROLL):
                t = tt * MOE_UNROLL + u
                lo = jnp.zeros((MOE_SUB, LANES), F32)
                hi = jnp.zeros((MOE_SUB, LANES), F32)
                for k in range(TOP_K):
                    p = pl.multiple_of(pos_ref[TOP_K * t + k], MOE_SUB)
                    g = gate_ref[TOP_K * t + k]
                    w = xy_ref[pl.ds(p, MOE_SUB), :]
                    lo = lo + g * _unpack_lo(w)
                    hi = hi + g * _unpack_hi(w)
                routed = jnp.concatenate([lo[j:j + 1] for j in range(MOE_SUB)]
                                         + [hi[j:j + 1] for j in range(MOE_SUB)], axis=1)
                o_ref[pl.ds(t, 1), :] = base_ref[pl.ds(t, 1), :] + routed
            return c

        lax.fori_loop(0, MOE_TB // MOE_UNROLL, step, 0)


def _moe_call(off, nt, pos, gates, xp, base, wgu, wd):
    T = base.shape[0]
    slots = MOE_TB * TOP_K
    grid_spec = pltpu.PrefetchScalarGridSpec(
        num_scalar_prefetch=2,
        grid=(T // MOE_TB, N_EXPERTS),
        in_specs=[
            pl.BlockSpec((slots,), lambda i, e, o, n: (i,), memory_space=pltpu.SMEM),
            pl.BlockSpec((slots,), lambda i, e, o, n: (i,), memory_space=pltpu.SMEM),
            pl.BlockSpec((MOE_TB * MOE_SUB, LANES), lambda i, e, o, n: (i, 0)),
            pl.BlockSpec((MOE_TB, D_MODEL), lambda i, e, o, n: (i, 0)),
            pl.BlockSpec((1, D_MODEL, 2 * EXPERT_FF), lambda i, e, o, n: (e, 0, 0)),
            pl.BlockSpec((1, EXPERT_FF, D_MODEL), lambda i, e, o, n: (e, 0, 0)),
        ],
        out_specs=pl.BlockSpec((MOE_TB, D_MODEL), lambda i, e, o, n: (i, 0)),
        scratch_shapes=[pltpu.VMEM((MOE_ROWS * MOE_SUB, LANES), jnp.int32)],
    )
    return pl.pallas_call(
        _moe_kernel,
        grid_spec=grid_spec,
        out_shape=jax.ShapeDtypeStruct((T, D_MODEL), F32),
        compiler_params=pltpu.CompilerParams(dimension_semantics=("arbitrary", "arbitrary"),
                                             vmem_limit_bytes=VMEM_LIMIT_BYTES),
        name="moe",
    )(off, nt, pos, gates, xp, base, wgu, wd)


def _route_tables(eidx, gate, rank, cnt):
    T = eidx.shape[1]
    nblk = T // MOE_TB
    cnt = cnt.reshape(nblk, -1, N_EXPERTS)
    per_blk = cnt.shape[1]
    earlier = jnp.cumsum(cnt, axis=1) - cnt
    tiles = (cnt.sum(axis=1) + MOE_RT - 1) // MOE_RT
    off = jnp.cumsum(tiles, axis=1) * MOE_RT - tiles * MOE_RT
    start = (off[:, None, :] + earlier).reshape(nblk * per_blk, N_EXPERTS)
    e_tk = eidx.T.reshape(nblk * per_blk, T // (nblk * per_blk), TOP_K)
    pos = jnp.take_along_axis(start[:, None, :], e_tk, axis=2).reshape(T, TOP_K) + rank.T
    return (off.reshape(-1).astype(jnp.int32), tiles.reshape(-1).astype(jnp.int32),
            (pos * MOE_SUB).reshape(-1).astype(jnp.int32), gate.T.reshape(-1))


def _pack_rows(a):
    u = lax.bitcast_convert_type(a, jnp.uint16).astype(jnp.uint32)
    words = u[:, :MOE_PACK] | (u[:, MOE_PACK:] << 16)
    return lax.bitcast_convert_type(words, jnp.int32).reshape(-1, LANES)


def _slot_cols(w, width):
    k = w.shape[0]
    w = w.reshape(k, MLA_HEADS, width)
    return jnp.pad(w, ((0, 0), (0, 0), (0, SLOT - width))).reshape(k, MLA_HEADS * SLOT)


def _block_flags(pos, tq, tk, chunked):
    B, S = pos.shape
    p = jnp.right_shift(pos, 6) if chunked else pos
    qmin = p.reshape(B, S // tq, tq).min(-1)[:, :, None]
    qmax = p.reshape(B, S // tq, tq).max(-1)[:, :, None]
    kmin = p.reshape(B, S // tk, tk).min(-1)[:, None, :]
    kmax = p.reshape(B, S // tk, tk).max(-1)[:, None, :]
    if chunked:
        none, all_ = kmin > qmax, kmax <= qmin
    else:
        none, all_ = kmin >= qmax, kmax < qmin
    return jnp.where(none, 0, jnp.where(all_, 1, 2)).astype(jnp.int32)


def kernel(x, positions, norm1, w_in, q_a_norm, w_uq, kv_a_norm, w_ukv, q_norm, k_norm,
           out_norm_mla, out_norm_sb, w_o, norm2, w_router, router_bias, w_gate_up, w_down,
           w_shared_gate_up, w_shared_down):
    B, S, D = x.shape
    T = B * S
    c0 = Q_LORA
    c1 = c0 + KV_LORA
    c2 = c1 + MLA_ROPE

    row2 = lambda g: g.reshape(1, -1).astype(F32)
    wcq = w_in[:, :c0].astype(BF16)
    wckv = w_in[:, c0:c1].astype(BF16)
    wkr = jnp.pad(w_in[:, c1:c2], ((0, 0), (MLA_NOPE, SLOT - MLA_QK))).astype(BF16)
    wsb = w_in[:, c2:].astype(BF16)
    wuq = _slot_cols(w_uq, MLA_QK).astype(BF16)
    w_ukv3 = w_ukv.reshape(KV_LORA, MLA_HEADS, MLA_NOPE + MLA_V)
    wuk = _slot_cols(w_ukv3[:, :, :MLA_NOPE].reshape(KV_LORA, -1), MLA_NOPE).astype(BF16)
    wuv = _slot_cols(w_ukv3[:, :, MLA_NOPE:].reshape(KV_LORA, MLA_WIDTH), MLA_V).astype(BF16)
    slot_lane = jnp.arange(SLOT)
    vone = jnp.tile((slot_lane >= MLA_V).astype(F32), MLA_HEADS).reshape(1, MLA_HEADS * SLOT)
    gq = jnp.pad(q_norm, (0, SLOT - MLA_QK)).reshape(1, SLOT)
    gk = jnp.pad(k_norm, (0, SLOT - MLA_QK)).reshape(1, SLOT)
    inv = ROPE_THETA ** (-jnp.arange(HALF_ROPE, dtype=F32) / HALF_ROPE)
    inv_slot = jnp.pad(jnp.concatenate([inv, inv]), (MLA_NOPE, SLOT - MLA_QK)).reshape(1, SLOT)

    bound = (math.sqrt(MLA_QK) * LOG2E * BOUND_MARGIN) * jnp.max(jnp.abs(q_norm)) * jnp.max(jnp.abs(k_norm))
    bounded = bound <= MLA_MAX_SHIFT
    shift = jnp.where(bounded, bound, 0.0)
    qaug = (slot_lane == MLA_QK).astype(F32).reshape(1, SLOT)
    kaug = -shift * qaug

    x2 = x.reshape(T, D)
    pos_col = positions.reshape(T, 1)
    q, k, v, sq, sk, sv = _proj_call(
        x2, pos_col,
        (row2(norm1), wcq, wckv, wkr, wsb, row2(q_a_norm), wuq, row2(kv_a_norm), wuk, wuv, vone,
         gq, gk, inv_slot, qaug, kaug), tm=512)

    pos_c3 = positions.reshape(B, S, 1)
    pos_r3 = positions.reshape(B, 1, S)
    r3 = lambda a: a.reshape(B, S, a.shape[-1])

    tq_m, tk_m = 512, 512
    mla_args = (_block_flags(positions, tq_m, tk_m, True), r3(q), r3(k), r3(v), pos_c3, pos_r3)
    o_mla = lax.cond(bounded,
                     lambda *a: _mla_call(*a, tq_m, tk_m, hp=4, online=False),
                     lambda *a: _mla_call(*a, tq_m, tk_m, hp=2, online=True), *mla_args)

    tq_s, tk_s = 256, 128
    sb_flags = _block_flags(positions, tq_s, tk_s, False)
    kstart = jnp.max(jnp.where(sb_flags != 0, jnp.arange(S // tk_s, dtype=jnp.int32), -1), axis=-1)
    o_sb = _sb_call(kstart.astype(jnp.int32), r3(sq), r3(sk), r3(sv), pos_c3, pos_r3,
                    tq_s, tk_s, win=4)

    wr_t = w_router.T
    wrh = wr_t.astype(BF16)
    wrl = (wr_t - wrh.astype(F32)).astype(BF16)
    base, hn, eidx, gate, rank, cnt = _merge_call(
        x2, o_mla.reshape(T, MLA_WIDTH), o_sb.reshape(T, SB_WIDTH), row2(out_norm_mla),
        row2(out_norm_sb), w_o.astype(BF16), row2(norm2), wrh, wrl,
        router_bias.reshape(N_EXPERTS, 1).astype(F32), w_shared_gate_up.astype(BF16),
        w_shared_down.astype(BF16), tm=512)

    off, tiles, pos, gates = _route_tables(eidx, gate, rank, cnt)
    out = _moe_call(off, tiles, pos, gates, _pack_rows(hn), base, w_gate_up.astype(BF16),
                    w_down.astype(BF16))
    return out.reshape(B, S, D)
```

```python
import functools
import math

import jax
import jax.numpy as jnp
from jax import lax
from jax.experimental import pallas as pl
from jax.experimental.pallas import tpu as pltpu

D_MODEL = 1024
CHUNK = 64
MLA_HEADS = 8
MLA_NOPE = 64
MLA_ROPE = 32
MLA_QK = MLA_NOPE + MLA_ROPE
MLA_V = 64
Q_LORA = 256
KV_LORA = 128
ROPE_THETA = 10000.0
SB_HEADS = 8
SB_HEAD_DIM = 64
SB_WIDTH = SB_HEADS * SB_HEAD_DIM
MLA_WIDTH = MLA_HEADS * MLA_V
N_EXPERTS = 64
TOP_K = 8
EXPERT_FF = 256
SHARED_FF = 256
ROUTED_SCALE = 2.5
EPS = 1e-6
NEG = -1e30

LANES = 128
VMEM_LIMIT_BYTES = 56 * 1024 * 1024

SLOT = LANES
HALF_ROPE = MLA_ROPE // 2

SB_LOG_ZERO = -90.0

LOG2E = 1.4426950408889634
BOUND_MARGIN = 1.02
MLA_MAX_SHIFT = 50.0

BF16 = jnp.bfloat16
F32 = jnp.float32


def _rms(x, g):
    return x * lax.rsqrt(jnp.mean(x * x, axis=-1, keepdims=True) + EPS) * g


def _dot(a, b):
    return jnp.dot(a, b, preferred_element_type=F32)


def _dot_nt(a, b):
    return lax.dot_general(a, b, (((1,), (1,)), ((), ())), preferred_element_type=F32)


def _proj_kernel(x_ref, pos_ref, g1_ref, wcq_ref, wckv_ref, wkr_ref, wsb_ref, gqa_ref, wuq_ref,
                 gkva_ref, wuk_ref, wuv_ref, vone_ref, gq_ref, gk_ref, inv_ref, qaug_ref, kaug_ref,
                 q_ref, k_ref, v_ref, sq_ref, sk_ref, sv_ref):
    x = x_ref[...]
    xn = _rms(x, g1_ref[...]).astype(BF16)

    cq = _rms(_dot(xn, wcq_ref[...]), gqa_ref[...]).astype(BF16)
    q = _dot(cq, wuq_ref[...])
    ckv = _rms(_dot(xn, wckv_ref[...]), gkva_ref[...]).astype(BF16)
    kn = _dot(ckv, wuk_ref[...])
    v_ref[...] = (_dot(ckv, wuv_ref[...]) + vone_ref[...]).astype(BF16)
    kr = _dot(xn, wkr_ref[...])

    sb = _dot(xn, wsb_ref[...])
    sq_ref[...] = (sb[:, :SB_WIDTH] * (1.0 / math.sqrt(SB_HEAD_DIM))).astype(BF16)
    sk_ref[...] = sb[:, SB_WIDTH:2 * SB_WIDTH].astype(BF16)
    sv_ref[...] = sb[:, 2 * SB_WIDTH:].astype(BF16)

    ang = pos_ref[...].astype(F32) * inv_ref[...]
    cos = jnp.cos(ang)
    sin = jnp.sin(ang)
    lane = lax.broadcasted_iota(jnp.int32, ang.shape, 1)
    first_half = lane < MLA_NOPE + HALF_ROPE

    def rope(t):
        up = pltpu.roll(t, SLOT - HALF_ROPE, 1)
        down = pltpu.roll(t, HALF_ROPE, 1)
        return t * cos + jnp.where(first_half, -up, down) * sin

    def head_norm(t, g):
        ss = jnp.sum(t * t, axis=-1, keepdims=True) * (1.0 / MLA_QK)
        return t * lax.rsqrt(ss + EPS) * g

    scale = LOG2E / math.sqrt(MLA_QK)
    for h in range(MLA_HEADS):
        sl = slice(h * SLOT, (h + 1) * SLOT)
        qh = rope(head_norm(q[:, sl], gq_ref[...])) * scale + qaug_ref[...]
        q_ref[:, sl] = qh.astype(BF16)
        kh = rope(head_norm(kn[:, sl] + kr, gk_ref[...])) + kaug_ref[...]
        k_ref[:, sl] = kh.astype(BF16)


def _proj_call(x2, pos_col, weights, tm):
    T = x2.shape[0]
    full = lambda a: pl.BlockSpec(a.shape, lambda i: (0,) * a.ndim)
    row = lambda w: pl.BlockSpec((tm, w), lambda i: (i, 0))
    out_w = (MLA_HEADS * SLOT, MLA_HEADS * SLOT, MLA_HEADS * SLOT, SB_WIDTH, SB_WIDTH, SB_WIDTH)
    return pl.pallas_call(
        _proj_kernel,
        grid=(T // tm,),
        in_specs=[row(D_MODEL), row(1)] + [full(w) for w in weights],
        out_specs=[row(w) for w in out_w],
        out_shape=[jax.ShapeDtypeStruct((T, w), BF16) for w in out_w],
        compiler_params=pltpu.CompilerParams(dimension_semantics=("arbitrary",),
                                             vmem_limit_bytes=VMEM_LIMIT_BYTES),
        name="proj",
    )(x2, pos_col, *weights)


def _mla_kernel(flags_ref, q_ref, k_ref, v_ref, posq_ref, posk_ref, o_ref, acc_ref, m_ref,
                *, tq, tk, nk, hp, online):
    b = pl.program_id(0)
    qi = pl.program_id(2)
    qchunk = jnp.right_shift(posq_ref[0], 6)
    acc_ref[...] = jnp.zeros(acc_ref.shape, F32)
    if online:
        m_ref[...] = jnp.full(m_ref.shape, NEG, F32)

    def block(kb, masked):
        k0 = pl.multiple_of(kb * tk, tk)
        if masked:
            kchunk = jnp.right_shift(posk_ref[0, :, pl.ds(k0, tk)], 6)
            vis = kchunk <= qchunk
        for h in range(hp):
            hs = slice(h * SLOT, (h + 1) * SLOT)
            s = _dot_nt(q_ref[0, :, hs], k_ref[0, pl.ds(k0, tk), hs])
            if masked:
                s = jnp.where(vis, s, NEG)
            vh = v_ref[0, pl.ds(k0, tk), hs]
            if online:
                m_old = m_ref[h]
                m_new = jnp.maximum(m_old, jnp.max(s, axis=-1, keepdims=True))
                p = jnp.exp2(s - m_new)
                acc_ref[h] = jnp.exp2(m_old - m_new) * acc_ref[h] + _dot(p.astype(BF16), vh)
                m_ref[h] = m_new
            else:
                acc_ref[h] += _dot(jnp.exp2(s).astype(BF16), vh)

    def body(kb, carry):
        flag = flags_ref[b, qi, kb]

        @pl.when(flag == 1)
        def _():
            block(kb, False)

        @pl.when(flag == 2)
        def _():
            block(kb, True)

        return carry

    lax.fori_loop(0, nk, body, 0)
    lane = lax.broadcasted_iota(jnp.int32, (tq, LANES), 1)
    for j in range(hp // 2):
        o0 = acc_ref[2 * j] / pltpu.roll(acc_ref[2 * j], MLA_V, 1)
        o1 = acc_ref[2 * j + 1] / pltpu.roll(acc_ref[2 * j + 1], MLA_V, 1)
        o_ref[0, :, j * LANES:(j + 1) * LANES] = jnp.where(lane < MLA_V, o0, pltpu.roll(o1, MLA_V, 1))


def _mla_call(flags, q, k, v, pos_col, pos_row, tq, tk, hp, online):
    B, S, _ = q.shape
    nq, nk = S // tq, S // tk
    grid_spec = pltpu.PrefetchScalarGridSpec(
        num_scalar_prefetch=1,
        grid=(B, MLA_HEADS // hp, nq),
        in_specs=[
            pl.BlockSpec((1, tq, hp * SLOT), lambda b, p, i, f: (b, i, p)),
            pl.BlockSpec((1, S, hp * SLOT), lambda b, p, i, f: (b, 0, p)),
            pl.BlockSpec((1, S, hp * SLOT), lambda b, p, i, f: (b, 0, p)),
            pl.BlockSpec((1, tq, 1), lambda b, p, i, f: (b, i, 0)),
            pl.BlockSpec((1, 1, S), lambda b, p, i, f: (b, 0, 0)),
        ],
        out_specs=pl.BlockSpec((1, tq, hp * MLA_V), lambda b, p, i, f: (b, i, p)),
        scratch_shapes=[pltpu.VMEM((hp, tq, LANES), F32), pltpu.VMEM((hp, tq, 1), F32)],
    )
    return pl.pallas_call(
        functools.partial(_mla_kernel, tq=tq, tk=tk, nk=nk, hp=hp, online=online),
        grid_spec=grid_spec,
        out_shape=jax.ShapeDtypeStruct((B, S, MLA_WIDTH), F32),
        compiler_params=pltpu.CompilerParams(
            dimension_semantics=("arbitrary", "arbitrary", "arbitrary"),
            vmem_limit_bytes=VMEM_LIMIT_BYTES),
        name="mla_attn_online" if online else "mla_attn",
    )(flags, q, k, v, pos_col, pos_row)


def _softplus(z):
    return jnp.maximum(z, 0.0) + jnp.log(1.0 + jnp.exp(-jnp.abs(z)))


def _sb_kernel(kstart_ref, q_ref, k_ref, v_ref, posq_ref, posk_ref, o_ref, run_ref, acc_ref,
               *, tq, tk, win):
    b = pl.program_id(0)
    qi = pl.program_id(2)
    lane = lax.broadcasted_iota(jnp.int32, (tq, LANES), 1)
    row_i = lax.broadcasted_iota(jnp.int32, (tk, tk), 0)
    col_i = lax.broadcasted_iota(jnp.int32, (tk, tk), 1)
    tri = (row_i >= col_i).astype(BF16)
    kstart = kstart_ref[b, qi]
    q_pair = q_ref[0]
    zero = jnp.zeros_like(q_pair)
    q2 = jnp.concatenate([jnp.where(lane < SB_HEAD_DIM, q_pair, zero),
                          jnp.where(lane >= SB_HEAD_DIM, q_pair, zero)], axis=0)
    qpos = jnp.concatenate([posq_ref[0], posq_ref[0]], axis=0)

    def suffix_sums(l1m):
        hi = lax.bitcast_convert_type(
            lax.bitcast_convert_type(l1m, jnp.int32) & jnp.int32(-65536), F32)
        lo = l1m - hi
        both = _dot(jnp.concatenate([hi.astype(BF16), lo.astype(BF16)], axis=0), tri)
        return both[:2 * tq] + both[2 * tq:]

    run_ref[...] = jnp.zeros(run_ref.shape, F32)
    acc_ref[...] = jnp.zeros(acc_ref.shape, F32)

    @pl.when(kstart >= win - 1)
    def _():
        k0 = pl.multiple_of((kstart - (win - 1)) * tk, tk)
        causal = posk_ref[0, :, pl.ds(k0, win * tk)] < qpos
        z = _dot_nt(q2, k_ref[0, pl.ds(k0, win * tk), :])
        l1m = jnp.where(causal, -_softplus(z), 0.0)
        run = jnp.zeros((2 * tq, 1), F32)
        a_blocks = [None] * win
        for j in reversed(range(win)):
            cs = slice(j * tk, (j + 1) * tk)
            suffix = suffix_sums(l1m[:, cs])
            a = jnp.exp(z[:, cs] + suffix + run)
            a_blocks[j] = jnp.where(causal[:, cs], a, 0.0).astype(BF16)
            run = run + suffix[:, 0:1]
        acc_ref[...] = _dot(jnp.concatenate(a_blocks, axis=1), v_ref[0, pl.ds(k0, win * tk), :])
        run_ref[...] = run

    def block(kb):
        k0 = pl.multiple_of(kb * tk, tk)
        z = _dot_nt(q2, k_ref[0, pl.ds(k0, tk), :])
        causal = posk_ref[0, :, pl.ds(k0, tk)] < qpos
        l1m = jnp.where(causal, -_softplus(z), 0.0)
        suffix = suffix_sums(l1m)
        run = run_ref[...]
        a = jnp.where(causal, jnp.exp(z + suffix + run), 0.0)
        acc_ref[...] += _dot(a.astype(BF16), v_ref[0, pl.ds(k0, tk), :])
        run_new = run + suffix[:, 0:1]
        run_ref[...] = run_new
        return jnp.max(run_new)

    def cond(c):
        kb, top = c
        return (kb >= 0) & (top >= SB_LOG_ZERO)

    def body(c):
        kb, _ = c
        return kb - 1, block(kb)

    kb_first = jnp.where(kstart >= win - 1, kstart - win, kstart)
    lax.while_loop(cond, body, (kb_first, jnp.max(run_ref[...])))

    o_ref[0] = jnp.where(lane < SB_HEAD_DIM, acc_ref[:tq], acc_ref[tq:])


def _sb_call(kstart, q, k, v, pos_col, pos_row, tq, tk, win):
    B, S, _ = q.shape
    nq = S // tq
    grid_spec = pltpu.PrefetchScalarGridSpec(
        num_scalar_prefetch=1,
        grid=(B, SB_HEADS // 2, nq),
        in_specs=[
            pl.BlockSpec((1, tq, LANES), lambda b, p, i, s: (b, i, p)),
            pl.BlockSpec((1, S, LANES), lambda b, p, i, s: (b, 0, p)),
            pl.BlockSpec((1, S, LANES), lambda b, p, i, s: (b, 0, p)),
            pl.BlockSpec((1, tq, 1), lambda b, p, i, s: (b, i, 0)),
            pl.BlockSpec((1, 1, S), lambda b, p, i, s: (b, 0, 0)),
        ],
        out_specs=pl.BlockSpec((1, tq, LANES), lambda b, p, i, s: (b, i, p)),
        scratch_shapes=[pltpu.VMEM((2 * tq, 1), F32), pltpu.VMEM((2 * tq, LANES), F32)],
    )
    return pl.pallas_call(
        functools.partial(_sb_kernel, tq=tq, tk=tk, win=win),
        grid_spec=grid_spec,
        out_shape=jax.ShapeDtypeStruct((B, S, SB_WIDTH), F32),
        compiler_params=pltpu.CompilerParams(
            dimension_semantics=("arbitrary", "arbitrary", "arbitrary"),
            vmem_limit_bytes=VMEM_LIMIT_BYTES),
        name="sb_attn",
    )(kstart, q, k, v, pos_col, pos_row)


def _split_bf16(a):
    hi = a.astype(BF16)
    lo = (a - hi.astype(F32)).astype(BF16)
    return hi, lo


def _merge_kernel(x_ref, om_ref, os_ref, gm_ref, gs_ref, wo_ref, g2_ref, wrh_ref, wrl_ref,
                  bias_ref, wsgu_ref, wsd_ref, base_ref, hn_ref, eidx_ref, gate_ref, rank_ref,
                  cnt_ref):
    mm = _rms(om_ref[...], gm_ref[...]).astype(BF16)
    ms = _rms(os_ref[...], gs_ref[...]).astype(BF16)
    h = x_ref[...] + _dot(mm, wo_ref[:MLA_WIDTH, :]) + _dot(ms, wo_ref[MLA_WIDTH:, :])
    hn = _rms(h, g2_ref[...])
    hn_hi, hn_lo = _split_bf16(hn)
    hn_ref[...] = hn_hi

    logits = (_dot_nt(wrh_ref[...], hn_hi) + _dot_nt(wrh_ref[...], hn_lo)
              + _dot_nt(wrl_ref[...], hn_hi))
    scores = jax.nn.sigmoid(logits)
    work = scores + bias_ref[...]
    eidx = lax.broadcasted_iota(jnp.int32, work.shape, 0)
    chosen, chosen_score, hits = [], [], []
    for _ in range(TOP_K):
        top = jnp.max(work, axis=0, keepdims=True)
        first = jnp.min(jnp.where(work == top, eidx, N_EXPERTS), axis=0, keepdims=True)
        hit = eidx == first
        hits.append(hit)
        chosen.append(first)
        chosen_score.append(jnp.sum(jnp.where(hit, scores, 0.0), axis=0, keepdims=True))
        work = jnp.where(hit, -jnp.inf, work)
    sel = jnp.concatenate(chosen_score, axis=0)
    eidx_ref[...] = jnp.concatenate(chosen, axis=0)
    gate_ref[...] = sel / jnp.sum(sel, axis=0, keepdims=True) * ROUTED_SCALE

    tm = work.shape[1]
    member = functools.reduce(jnp.logical_or, hits)
    member = jnp.where(member, 1.0, 0.0)
    tri = (lax.broadcasted_iota(jnp.int32, (tm, tm), 0)
           <= lax.broadcasted_iota(jnp.int32, (tm, tm), 1)).astype(BF16)
    upto = _dot(member.astype(BF16), tri)
    before = upto - member
    rank_ref[...] = jnp.concatenate(
        [jnp.sum(jnp.where(hit, before, 0.0), axis=0, keepdims=True) for hit in hits],
        axis=0).astype(jnp.int32)
    cnt_ref[0] = upto[:, tm - 1:tm].astype(jnp.int32)

    sgu = _dot(hn_hi, wsgu_ref[...])
    act = (jax.nn.silu(sgu[:, :SHARED_FF]) * sgu[:, SHARED_FF:]).astype(BF16)
    base_ref[...] = h + _dot(act, wsd_ref[...])


def _merge_call(x2, o_mla, o_sb, gm, gs, wo, g2, wrh, wrl, bias, wsgu, wsd, tm):
    T = x2.shape[0]
    full = lambda a: pl.BlockSpec(a.shape, lambda i: (0,) * a.ndim)
    row = lambda w: pl.BlockSpec((tm, w), lambda i: (i, 0))
    weights = (gm, gs, wo, g2, wrh, wrl, bias, wsgu, wsd)
    topk = pl.BlockSpec((TOP_K, tm), lambda i: (0, i))
    return pl.pallas_call(
        _merge_kernel,
        grid=(T // tm,),
        in_specs=[row(D_MODEL), row(MLA_WIDTH), row(SB_WIDTH)] + [full(w) for w in weights],
        out_specs=[row(D_MODEL), row(D_MODEL), topk, topk, topk,
                   pl.BlockSpec((1, N_EXPERTS, 1), lambda i: (i, 0, 0))],
        out_shape=[jax.ShapeDtypeStruct((T, D_MODEL), F32),
                   jax.ShapeDtypeStruct((T, D_MODEL), BF16),
                   jax.ShapeDtypeStruct((TOP_K, T), jnp.int32),
                   jax.ShapeDtypeStruct((TOP_K, T), F32),
                   jax.ShapeDtypeStruct((TOP_K, T), jnp.int32),
                   jax.ShapeDtypeStruct((T // tm, N_EXPERTS, 1), jnp.int32)],
        compiler_params=pltpu.CompilerParams(dimension_semantics=("arbitrary",),
                                             vmem_limit_bytes=VMEM_LIMIT_BYTES),
        name="merge",
    )(x2, o_mla, o_sb, *weights)


MOE_TB = 1024
MOE_RT = 64
MOE_MAX_TILES = 4
MOE_ROWS = 12288
MOE_PACK = D_MODEL // 2
MOE_SUB = MOE_PACK // LANES
MOE_UNROLL = 8
HI_MASK = -65536


def _unpack_lo(w):
    return lax.bitcast_convert_type(w << 16, F32)


def _unpack_hi(w):
    return lax.bitcast_convert_type(w & jnp.int32(HI_MASK), F32)


def _moe_kernel(off_ref, nt_ref, pos_ref, gate_ref, xp_ref, base_ref, wgu_ref, wd_ref, o_ref, xy_ref):
    i = pl.program_id(0)
    e = pl.program_id(1)

    @pl.when((i == 0) & (e == 0))
    def _():
        xy_ref[...] = jnp.zeros(xy_ref.shape, jnp.int32)

    @pl.when(e == 0)
    def _():
        def step(tt, c):
            for u in range(MOE_UNROLL):
                t = tt * MOE_UNROLL + u
                row = xp_ref[pl.ds(pl.multiple_of(MOE_SUB * t, MOE_SUB), MOE_SUB), :]
                for k in range(TOP_K):
                    p = pl.multiple_of(pos_ref[TOP_K * t + k], MOE_SUB)
                    xy_ref[pl.ds(p, MOE_SUB), :] = row
            return c

        lax.fori_loop(0, MOE_TB // MOE_UNROLL, step, 0)

    def ffn(r0, m):
        base_row = pl.multiple_of(MOE_SUB * r0, 8)
        words = jnp.concatenate(
            [xy_ref[pl.ds(base_row + j, m, stride=MOE_SUB), :] for j in range(MOE_SUB)], axis=1)
        gu = (_dot(_unpack_lo(words).astype(BF16), wgu_ref[0, :MOE_PACK, :])
              + _dot(_unpack_hi(words).astype(BF16), wgu_ref[0, MOE_PACK:, :]))
        act = (jax.nn.silu(gu[:, :EXPERT_FF]) * gu[:, EXPERT_FF:]).astype(BF16)
        y = _dot(act, wd_ref[0])
        ya = lax.bitcast_convert_type(y[:, :MOE_PACK].astype(BF16).astype(F32), jnp.int32)
        yb = lax.bitcast_convert_type(y[:, MOE_PACK:].astype(BF16).astype(F32), jnp.int32)
        packed = lax.shift_right_logical(ya, 16) | yb
        for j in range(MOE_SUB):
            xy_ref[pl.ds(base_row + j, m, stride=MOE_SUB), :] = packed[:, j * LANES:(j + 1) * LANES]

    off = off_ref[i * N_EXPERTS + e]
    nt = nt_ref[i * N_EXPERTS + e]

    def chunk(c, carry):
        ffn(off + c * (MOE_MAX_TILES * MOE_RT), MOE_MAX_TILES * MOE_RT)
        return carry

    full = nt // MOE_MAX_TILES
    lax.fori_loop(0, full, chunk, 0)
    rest = nt - full * MOE_MAX_TILES
    for tiles in range(1, MOE_MAX_TILES):
        @pl.when(rest == tiles)
        def _():
            ffn(off + full * (MOE_MAX_TILES * MOE_RT), tiles * MOE_RT)

    @pl.when(e == N_EXPERTS - 1)
    def _():
        def step(tt, c):
            for u in range(MOE_UNROLL):
                t = tt * MOE_UNROLL + u
                lo = jnp.zeros((MOE_SUB, LANES), F32)
                hi = jnp.zeros((MOE_SUB, LANES), F32)
                for k in range(TOP_K):
                    p = pl.multiple_of(pos_ref[TOP_K * t + k], MOE_SUB)
                    g = gate_ref[TOP_K * t + k]
                    w = xy_ref[pl.ds(p, MOE_SUB), :]
                    lo = lo + g * _unpack_lo(w)
                    hi = hi + g * _unpack_hi(w)
                routed = jnp.concatenate([lo[j:j + 1] for j in range(MOE_SUB)]
                                         + [hi[j:j + 1] for j in range(MOE_SUB)], axis=1)
                o_ref[pl.ds(t, 1), :] = base_ref[pl.ds(t, 1), :] + routed
            return c

        lax.fori_loop(0, MOE_TB // MOE_UNROLL, step, 0)


def _moe_call(off, nt, pos, gates, xp, base, wgu, wd):
    T = base.shape[0]
    slots = MOE_TB * TOP_K
    grid_spec = pltpu.PrefetchScalarGridSpec(
        num_scalar_prefetch=2,
        grid=(T // MOE_TB, N_EXPERTS),
        in_specs=[
            pl.BlockSpec((slots,), lambda i, e, o, n: (i,), memory_space=pltpu.SMEM),
            pl.BlockSpec((slots,), lambda i, e, o, n: (i,), memory_space=pltpu.SMEM),
            pl.BlockSpec((MOE_TB * MOE_SUB, LANES), lambda i, e, o, n: (i, 0)),
            pl.BlockSpec((MOE_TB, D_MODEL), lambda i, e, o, n: (i, 0)),
            pl.BlockSpec((1, D_MODEL, 2 * EXPERT_FF), lambda i, e, o, n: (e, 0, 0)),
            pl.BlockSpec((1, EXPERT_FF, D_MODEL), lambda i, e, o, n: (e, 0, 0)),
        ],
        out_specs=pl.BlockSpec((MOE_TB, D_MODEL), lambda i, e, o, n: (i, 0)),
        scratch_shapes=[pltpu.VMEM((MOE_ROWS * MOE_SUB, LANES), jnp.int32)],
    )
    return pl.pallas_call(
        _moe_kernel,
        grid_spec=grid_spec,
        out_shape=jax.ShapeDtypeStruct((T, D_MODEL), F32),
        compiler_params=pltpu.CompilerParams(dimension_semantics=("arbitrary", "arbitrary"),
                                             vmem_limit_bytes=VMEM_LIMIT_BYTES),
        name="moe",
    )(off, nt, pos, gates, xp, base, wgu, wd)


def _route_tables(eidx, gate, rank, cnt):
    T = eidx.shape[1]
    nblk = T // MOE_TB
    cnt = cnt.reshape(nblk, -1, N_EXPERTS)
    per_blk = cnt.shape[1]
    earlier = jnp.cumsum(cnt, axis=1) - cnt
    tiles = (cnt.sum(axis=1) + MOE_RT - 1) // MOE_RT
    off = jnp.cumsum(tiles, axis=1) * MOE_RT - tiles * MOE_RT
    start = (off[:, None, :] + earlier).reshape(nblk * per_blk, N_EXPERTS)
    e_tk = eidx.T.reshape(nblk * per_blk, T // (nblk * per_blk), TOP_K)
    chosen = e_tk[..., None] == jnp.arange(N_EXPERTS, dtype=jnp.int32)
    pos = jnp.where(chosen, start[:, None, None, :], 0).sum(axis=-1).reshape(T, TOP_K) + rank.T
    return (off.reshape(-1).astype(jnp.int32), tiles.reshape(-1).astype(jnp.int32),
            (pos * MOE_SUB).reshape(-1).astype(jnp.int32), gate.T.reshape(-1))


def _pack_rows(a):
    u = lax.bitcast_convert_type(a, jnp.uint16).astype(jnp.uint32)
    words = u[:, :MOE_PACK] | (u[:, MOE_PACK:] << 16)
    return lax.bitcast_convert_type(words, jnp.int32).reshape(-1, LANES)


def _slot_cols(w, width):
    k = w.shape[0]
    w = w.reshape(k, MLA_HEADS, width)
    return jnp.pad(w, ((0, 0), (0, 0), (0, SLOT - width))).reshape(k, MLA_HEADS * SLOT)


def _block_flags(pos, tq, tk, chunked):
    B, S = pos.shape
    p = jnp.right_shift(pos, 6) if chunked else pos
    qmin = p.reshape(B, S // tq, tq).min(-1)[:, :, None]
    qmax = p.reshape(B, S // tq, tq).max(-1)[:, :, None]
    kmin = p.reshape(B, S // tk, tk).min(-1)[:, None, :]
    kmax = p.reshape(B, S // tk, tk).max(-1)[:, None, :]
    if chunked:
        none, all_ = kmin > qmax, kmax <= qmin
    else:
        none, all_ = kmin >= qmax, kmax < qmin
    return jnp.where(none, 0, jnp.where(all_, 1, 2)).astype(jnp.int32)


def kernel(x, positions, norm1, w_in, q_a_norm, w_uq, kv_a_norm, w_ukv, q_norm, k_norm,
           out_norm_mla, out_norm_sb, w_o, norm2, w_router, router_bias, w_gate_up, w_down,
           w_shared_gate_up, w_shared_down):
    B, S, D = x.shape
    T = B * S
    c0 = Q_LORA
    c1 = c0 + KV_LORA
    c2 = c1 + MLA_ROPE

    row2 = lambda g: g.reshape(1, -1).astype(F32)
    wcq = w_in[:, :c0].astype(BF16)
    wckv = w_in[:, c0:c1].astype(BF16)
    wkr = jnp.pad(w_in[:, c1:c2], ((0, 0), (MLA_NOPE, SLOT - MLA_QK))).astype(BF16)
    wsb = w_in[:, c2:].astype(BF16)
    wuq = _slot_cols(w_uq, MLA_QK).astype(BF16)
    w_ukv3 = w_ukv.reshape(KV_LORA, MLA_HEADS, MLA_NOPE + MLA_V)
    wuk = _slot_cols(w_ukv3[:, :, :MLA_NOPE].reshape(KV_LORA, -1), MLA_NOPE).astype(BF16)
    wuv = _slot_cols(w_ukv3[:, :, MLA_NOPE:].reshape(KV_LORA, MLA_WIDTH), MLA_V).astype(BF16)
    slot_lane = jnp.arange(SLOT)
    vone = jnp.tile((slot_lane >= MLA_V).astype(F32), MLA_HEADS).reshape(1, MLA_HEADS * SLOT)
    gq = jnp.pad(q_norm, (0, SLOT - MLA_QK)).reshape(1, SLOT)
    gk = jnp.pad(k_norm, (0, SLOT - MLA_QK)).reshape(1, SLOT)
    inv = ROPE_THETA ** (-jnp.arange(HALF_ROPE, dtype=F32) / HALF_ROPE)
    inv_slot = jnp.pad(jnp.concatenate([inv, inv]), (MLA_NOPE, SLOT - MLA_QK)).reshape(1, SLOT)

    bound = (math.sqrt(MLA_QK) * LOG2E * BOUND_MARGIN) * jnp.max(jnp.abs(q_norm)) * jnp.max(jnp.abs(k_norm))
    bounded = bound <= MLA_MAX_SHIFT
    shift = jnp.where(bounded, bound, 0.0)
    qaug = (slot_lane == MLA_QK).astype(F32).reshape(1, SLOT)
    kaug = -shift * qaug

    x2 = x.reshape(T, D)
    pos_col = positions.reshape(T, 1)
    q, k, v, sq, sk, sv = _proj_call(
        x2, pos_col,
        (row2(norm1), wcq, wckv, wkr, wsb, row2(q_a_norm), wuq, row2(kv_a_norm), wuk, wuv, vone,
         gq, gk, inv_slot, qaug, kaug), tm=512)

    pos_c3 = positions.reshape(B, S, 1)
    pos_r3 = positions.reshape(B, 1, S)
    r3 = lambda a: a.reshape(B, S, a.shape[-1])

    tq_m, tk_m = 512, 512
    mla_args = (_block_flags(positions, tq_m, tk_m, True), r3(q), r3(k), r3(v), pos_c3, pos_r3)
    o_mla = lax.cond(bounded,
                     lambda *a: _mla_call(*a, tq_m, tk_m, hp=4, online=False),
                     lambda *a: _mla_call(*a, tq_m, tk_m, hp=2, online=True), *mla_args)

    tq_s, tk_s = 256, 128
    sb_flags = _block_flags(positions, tq_s, tk_s, False)
    kstart = jnp.max(jnp.where(sb_flags != 0, jnp.arange(S // tk_s, dtype=jnp.int32), -1), axis=-1)
    o_sb = _sb_call(kstart.astype(jnp.int32), r3(sq), r3(sk), r3(sv), pos_c3, pos_r3,
                    tq_s, tk_s, win=4)

    wr_t = w_router.T
    wrh = wr_t.astype(BF16)
    wrl = (wr_t - wrh.astype(F32)).astype(BF16)
    base, hn, eidx, gate, rank, cnt = _merge_call(
        x2, o_mla.reshape(T, MLA_WIDTH), o_sb.reshape(T, SB_WIDTH), row2(out_norm_mla),
        row2(out_norm_sb), w_o.astype(BF16), row2(norm2), wrh, wrl,
        router_bias.reshape(N_EXPERTS, 1).astype(F32), w_shared_gate_up.astype(BF16),
        w_shared_down.astype(BF16), tm=512)

    off, tiles, pos, gates = _route_tables(eidx, gate, rank, cnt)
    out = _moe_call(off, tiles, pos, gates, _pack_rows(hn), base, w_gate_up.astype(BF16),
                    w_down.astype(BF16))
    return out.reshape(B, S, D)
```

```python
import functools
import math

import jax
import jax.numpy as jnp
from jax import lax
from jax.experimental import pallas as pl
from jax.experimental.pallas import tpu as pltpu

D_MODEL = 1024
CHUNK = 64
MLA_HEADS = 8
MLA_NOPE = 64
MLA_ROPE = 32
MLA_QK = MLA_NOPE + MLA_ROPE
MLA_V = 64
Q_LORA = 256
KV_LORA = 128
ROPE_THETA = 10000.0
SB_HEADS = 8
SB_HEAD_DIM = 64
SB_WIDTH = SB_HEADS * SB_HEAD_DIM
MLA_WIDTH = MLA_HEADS * MLA_V
N_EXPERTS = 64
TOP_K = 8
EXPERT_FF = 256
SHARED_FF = 256
ROUTED_SCALE = 2.5
EPS = 1e-6
NEG = -1e30

LANES = 128
VMEM_LIMIT_BYTES = 56 * 1024 * 1024

SLOT = LANES
HALF_ROPE = MLA_ROPE // 2

SB_LOG_ZERO = -90.0

LOG2E = 1.4426950408889634
BOUND_MARGIN = 1.02
MLA_MAX_SHIFT = 50.0

BF16 = jnp.bfloat16
F32 = jnp.float32


def _rms(x, g):
    return x * lax.rsqrt(jnp.mean(x * x, axis=-1, keepdims=True) + EPS) * g


def _dot(a, b):
    return jnp.dot(a, b, preferred_element_type=F32)


def _dot_nt(a, b):
    return lax.dot_general(a, b, (((1,), (1,)), ((), ())), preferred_element_type=F32)


def _proj_kernel(x_ref, pos_ref, g1_ref, wcq_ref, wckv_ref, wkr_ref, wsb_ref, gqa_ref, wuq_ref,
                 gkva_ref, wuk_ref, wuv_ref, vone_ref, gq_ref, gk_ref, inv_ref, qaug_ref, kaug_ref,
                 q_ref, k_ref, v_ref, sq_ref, sk_ref, sv_ref):
    x = x_ref[...]
    xn = _rms(x, g1_ref[...]).astype(BF16)

    cq = _rms(_dot(xn, wcq_ref[...]), gqa_ref[...]).astype(BF16)
    q = _dot(cq, wuq_ref[...])
    ckv = _rms(_dot(xn, wckv_ref[...]), gkva_ref[...]).astype(BF16)
    kn = _dot(ckv, wuk_ref[...])
    v_ref[...] = (_dot(ckv, wuv_ref[...]) + vone_ref[...]).astype(BF16)
    kr = _dot(xn, wkr_ref[...])

    sb = _dot(xn, wsb_ref[...])
    sq_ref[...] = (sb[:, :SB_WIDTH] * (1.0 / math.sqrt(SB_HEAD_DIM))).astype(BF16)
    sk_ref[...] = sb[:, SB_WIDTH:2 * SB_WIDTH].astype(BF16)
    sv_ref[...] = sb[:, 2 * SB_WIDTH:].astype(BF16)

    ang = pos_ref[...].astype(F32) * inv_ref[...]
    cos = jnp.cos(ang)
    sin = jnp.sin(ang)
    lane = lax.broadcasted_iota(jnp.int32, ang.shape, 1)
    first_half = lane < MLA_NOPE + HALF_ROPE

    def rope(t):
        up = pltpu.roll(t, SLOT - HALF_ROPE, 1)
        down = pltpu.roll(t, HALF_ROPE, 1)
        return t * cos + jnp.where(first_half, -up, down) * sin

    def head_norm(t, g):
        ss = jnp.sum(t * t, axis=-1, keepdims=True) * (1.0 / MLA_QK)
        return t * lax.rsqrt(ss + EPS) * g

    scale = LOG2E / math.sqrt(MLA_QK)
    for h in range(MLA_HEADS):
        sl = slice(h * SLOT, (h + 1) * SLOT)
        qh = rope(head_norm(q[:, sl], gq_ref[...])) * scale + qaug_ref[...]
        q_ref[:, sl] = qh.astype(BF16)
        kh = rope(head_norm(kn[:, sl] + kr, gk_ref[...])) + kaug_ref[...]
        k_ref[:, sl] = kh.astype(BF16)


def _proj_call(x2, pos_col, weights, tm):
    T = x2.shape[0]
    full = lambda a: pl.BlockSpec(a.shape, lambda i: (0,) * a.ndim)
    row = lambda w: pl.BlockSpec((tm, w), lambda i: (i, 0))
    out_w = (MLA_HEADS * SLOT, MLA_HEADS * SLOT, MLA_HEADS * SLOT, SB_WIDTH, SB_WIDTH, SB_WIDTH)
    return pl.pallas_call(
        _proj_kernel,
        grid=(T // tm,),
        in_specs=[row(D_MODEL), row(1)] + [full(w) for w in weights],
        out_specs=[row(w) for w in out_w],
        out_shape=[jax.ShapeDtypeStruct((T, w), BF16) for w in out_w],
        compiler_params=pltpu.CompilerParams(dimension_semantics=("arbitrary",),
                                             vmem_limit_bytes=VMEM_LIMIT_BYTES),
        name="proj",
    )(x2, pos_col, *weights)


def _mla_kernel(flags_ref, q_ref, k_ref, v_ref, posq_ref, posk_ref, o_ref, acc_ref, m_ref,
                *, tq, tk, nk, hp, online):
    b = pl.program_id(0)
    qi = pl.program_id(2)
    qchunk = jnp.right_shift(posq_ref[0], 6)
    acc_ref[...] = jnp.zeros(acc_ref.shape, F32)
    if online:
        m_ref[...] = jnp.full(m_ref.shape, NEG, F32)

    def block(kb, masked):
        k0 = pl.multiple_of(kb * tk, tk)
        if masked:
            kchunk = jnp.right_shift(posk_ref[0, :, pl.ds(k0, tk)], 6)
            vis = kchunk <= qchunk
        for h in range(hp):
            hs = slice(h * SLOT, (h + 1) * SLOT)
            s = _dot_nt(q_ref[0, :, hs], k_ref[0, pl.ds(k0, tk), hs])
            if masked:
                s = jnp.where(vis, s, NEG)
            vh = v_ref[0, pl.ds(k0, tk), hs]
            if online:
                m_old = m_ref[h]
                m_new = jnp.maximum(m_old, jnp.max(s, axis=-1, keepdims=True))
                p = jnp.exp2(s - m_new)
                acc_ref[h] = jnp.exp2(m_old - m_new) * acc_ref[h] + _dot(p.astype(BF16), vh)
                m_ref[h] = m_new
            else:
                acc_ref[h] += _dot(jnp.exp2(s).astype(BF16), vh)

    def body(kb, carry):
        flag = flags_ref[b, qi, kb]

        @pl.when(flag == 1)
        def _():
            block(kb, False)

        @pl.when(flag == 2)
        def _():
            block(kb, True)

        return carry

    lax.fori_loop(0, nk, body, 0)
    lane = lax.broadcasted_iota(jnp.int32, (tq, LANES), 1)
    for j in range(hp // 2):
        o0 = acc_ref[2 * j] / pltpu.roll(acc_ref[2 * j], MLA_V, 1)
        o1 = acc_ref[2 * j + 1] / pltpu.roll(acc_ref[2 * j + 1], MLA_V, 1)
        o_ref[0, :, j * LANES:(j + 1) * LANES] = jnp.where(lane < MLA_V, o0, pltpu.roll(o1, MLA_V, 1))


def _mla_call(flags, q, k, v, pos_col, pos_row, tq, tk, hp, online):
    B, S, _ = q.shape
    nq, nk = S // tq, S // tk
    grid_spec = pltpu.PrefetchScalarGridSpec(
        num_scalar_prefetch=1,
        grid=(B, MLA_HEADS // hp, nq),
        in_specs=[
            pl.BlockSpec((1, tq, hp * SLOT), lambda b, p, i, f: (b, i, p)),
            pl.BlockSpec((1, S, hp * SLOT), lambda b, p, i, f: (b, 0, p)),
            pl.BlockSpec((1, S, hp * SLOT), lambda b, p, i, f: (b, 0, p)),
            pl.BlockSpec((1, tq, 1), lambda b, p, i, f: (b, i, 0)),
            pl.BlockSpec((1, 1, S), lambda b, p, i, f: (b, 0, 0)),
        ],
        out_specs=pl.BlockSpec((1, tq, hp * MLA_V), lambda b, p, i, f: (b, i, p)),
        scratch_shapes=[pltpu.VMEM((hp, tq, LANES), F32), pltpu.VMEM((hp, tq, 1), F32)],
    )
    return pl.pallas_call(
        functools.partial(_mla_kernel, tq=tq, tk=tk, nk=nk, hp=hp, online=online),
        grid_spec=grid_spec,
        out_shape=jax.ShapeDtypeStruct((B, S, MLA_WIDTH), F32),
        compiler_params=pltpu.CompilerParams(
            dimension_semantics=("arbitrary", "arbitrary", "arbitrary"),
            vmem_limit_bytes=VMEM_LIMIT_BYTES),
        name="mla_attn_online" if online else "mla_attn",
    )(flags, q, k, v, pos_col, pos_row)


def _softplus(z):
    return jnp.maximum(z, 0.0) + jnp.log(1.0 + jnp.exp(-jnp.abs(z)))


def _sb_kernel(kstart_ref, q_ref, k_ref, v_ref, posq_ref, posk_ref, o_ref, run_ref, acc_ref,
               *, tq, tk, win):
    b = pl.program_id(0)
    qi = pl.program_id(2)
    lane = lax.broadcasted_iota(jnp.int32, (tq, LANES), 1)
    row_i = lax.broadcasted_iota(jnp.int32, (tk, tk), 0)
    col_i = lax.broadcasted_iota(jnp.int32, (tk, tk), 1)
    tri = (row_i >= col_i).astype(BF16)
    kstart = kstart_ref[b, qi]
    q_pair = q_ref[0]
    zero = jnp.zeros_like(q_pair)
    q2 = jnp.concatenate([jnp.where(lane < SB_HEAD_DIM, q_pair, zero),
                          jnp.where(lane >= SB_HEAD_DIM, q_pair, zero)], axis=0)
    qpos = jnp.concatenate([posq_ref[0], posq_ref[0]], axis=0)

    def suffix_sums(l1m):
        hi = lax.bitcast_convert_type(
            lax.bitcast_convert_type(l1m, jnp.int32) & jnp.int32(-65536), F32)
        lo = l1m - hi
        both = _dot(jnp.concatenate([hi.astype(BF16), lo.astype(BF16)], axis=0), tri)
        return both[:2 * tq] + both[2 * tq:]

    run_ref[...] = jnp.zeros(run_ref.shape, F32)
    acc_ref[...] = jnp.zeros(acc_ref.shape, F32)

    @pl.when(kstart >= win - 1)
    def _():
        k0 = pl.multiple_of((kstart - (win - 1)) * tk, tk)
        causal = posk_ref[0, :, pl.ds(k0, win * tk)] < qpos
        z = _dot_nt(q2, k_ref[0, pl.ds(k0, win * tk), :])
        l1m = jnp.where(causal, -_softplus(z), 0.0)
        run = jnp.zeros((2 * tq, 1), F32)
        a_blocks = [None] * win
        for j in reversed(range(win)):
            cs = slice(j * tk, (j + 1) * tk)
            suffix = suffix_sums(l1m[:, cs])
            a = jnp.exp(z[:, cs] + suffix + run)
            a_blocks[j] = jnp.where(causal[:, cs], a, 0.0).astype(BF16)
            run = run + suffix[:, 0:1]
        acc_ref[...] = _dot(jnp.concatenate(a_blocks, axis=1), v_ref[0, pl.ds(k0, win * tk), :])
        run_ref[...] = run

    def block(kb):
        k0 = pl.multiple_of(kb * tk, tk)
        z = _dot_nt(q2, k_ref[0, pl.ds(k0, tk), :])
        causal = posk_ref[0, :, pl.ds(k0, tk)] < qpos
        l1m = jnp.where(causal, -_softplus(z), 0.0)
        suffix = suffix_sums(l1m)
        run = run_ref[...]
        a = jnp.where(causal, jnp.exp(z + suffix + run), 0.0)
        acc_ref[...] += _dot(a.astype(BF16), v_ref[0, pl.ds(k0, tk), :])
        run_new = run + suffix[:, 0:1]
        run_ref[...] = run_new
        return jnp.max(run_new)

    def cond(c):
        kb, top = c
        return (kb >= 0) & (top >= SB_LOG_ZERO)

    def body(c):
        kb, _ = c
        return kb - 1, block(kb)

    kb_first = jnp.where(kstart >= win - 1, kstart - win, kstart)
    lax.while_loop(cond, body, (kb_first, jnp.max(run_ref[...])))

    o_ref[0] = jnp.where(lane < SB_HEAD_DIM, acc_ref[:tq], acc_ref[tq:])


def _sb_call(kstart, q, k, v, pos_col, pos_row, tq, tk, win):
    B, S, _ = q.shape
    nq = S // tq
    grid_spec = pltpu.PrefetchScalarGridSpec(
        num_scalar_prefetch=1,
        grid=(B, SB_HEADS // 2, nq),
        in_specs=[
            pl.BlockSpec((1, tq, LANES), lambda b, p, i, s: (b, i, p)),
            pl.BlockSpec((1, S, LANES), lambda b, p, i, s: (b, 0, p)),
            pl.BlockSpec((1, S, LANES), lambda b, p, i, s: (b, 0, p)),
            pl.BlockSpec((1, tq, 1), lambda b, p, i, s: (b, i, 0)),
            pl.BlockSpec((1, 1, S), lambda b, p, i, s: (b, 0, 0)),
        ],
        out_specs=pl.BlockSpec((1, tq, LANES), lambda b, p, i, s: (b, i, p)),
        scratch_shapes=[pltpu.VMEM((2 * tq, 1), F32), pltpu.VMEM((2 * tq, LANES), F32)],
    )
    return pl.pallas_call(
        functools.partial(_sb_kernel, tq=tq, tk=tk, win=win),
        grid_spec=grid_spec,
        out_shape=jax.ShapeDtypeStruct((B, S, SB_WIDTH), F32),
        compiler_params=pltpu.CompilerParams(
            dimension_semantics=("arbitrary", "arbitrary", "arbitrary"),
            vmem_limit_bytes=VMEM_LIMIT_BYTES),
        name="sb_attn",
    )(kstart, q, k, v, pos_col, pos_row)


def _split_bf16(a):
    hi = a.astype(BF16)
    lo = (a - hi.astype(F32)).astype(BF16)
    return hi, lo


def _merge_kernel(x_ref, om_ref, os_ref, gm_ref, gs_ref, wo_ref, g2_ref, wrh_ref, wrl_ref,
                  bias_ref, wsgu_ref, wsd_ref, base_ref, hn_ref, eidx_ref, gate_ref, rank_ref,
                  cnt_ref):
    mm = _rms(om_ref[...], gm_ref[...]).astype(BF16)
    ms = _rms(os_ref[...], gs_ref[...]).astype(BF16)
    h = x_ref[...] + _dot(mm, wo_ref[:MLA_WIDTH, :]) + _dot(ms, wo_ref[MLA_WIDTH:, :])
    hn = _rms(h, g2_ref[...])
    hn_hi, hn_lo = _split_bf16(hn)
    hn_ref[...] = hn_hi

    logits = (_dot_nt(wrh_ref[...], hn_hi) + _dot_nt(wrh_ref[...], hn_lo)
              + _dot_nt(wrl_ref[...], hn_hi))
    scores = jax.nn.sigmoid(logits)
    work = scores + bias_ref[...]
    eidx = lax.broadcasted_iota(jnp.int32, work.shape, 0)
    chosen, chosen_score, hits = [], [], []
    for _ in range(TOP_K):
        top = jnp.max(work, axis=0, keepdims=True)
        first = jnp.min(jnp.where(work == top, eidx, N_EXPERTS), axis=0, keepdims=True)
        hit = eidx == first
        hits.append(hit)
        chosen.append(first)
        chosen_score.append(jnp.sum(jnp.where(hit, scores, 0.0), axis=0, keepdims=True))
        work = jnp.where(hit, -jnp.inf, work)
    sel = jnp.concatenate(chosen_score, axis=0)
    eidx_ref[...] = jnp.concatenate(chosen, axis=0)
    gate_ref[...] = sel / jnp.sum(sel, axis=0, keepdims=True) * ROUTED_SCALE

    tm = work.shape[1]
    member = functools.reduce(jnp.logical_or, hits)
    member = jnp.where(member, 1.0, 0.0)
    tri = (lax.broadcasted_iota(jnp.int32, (tm, tm), 0)
           <= lax.broadcasted_iota(jnp.int32, (tm, tm), 1)).astype(BF16)
    upto = _dot(member.astype(BF16), tri)
    before = upto - member
    rank_ref[...] = jnp.concatenate(
        [jnp.sum(jnp.where(hit, before, 0.0), axis=0, keepdims=True) for hit in hits],
        axis=0).astype(jnp.int32)
    cnt_ref[0] = upto[:, tm - 1:tm].astype(jnp.int32)

    sgu = _dot(hn_hi, wsgu_ref[...])
    act = (jax.nn.silu(sgu[:, :SHARED_FF]) * sgu[:, SHARED_FF:]).astype(BF16)
    base_ref[...] = h + _dot(act, wsd_ref[...])


def _merge_call(x2, o_mla, o_sb, gm, gs, wo, g2, wrh, wrl, bias, wsgu, wsd, tm):
    T = x2.shape[0]
    full = lambda a: pl.BlockSpec(a.shape, lambda i: (0,) * a.ndim)
    row = lambda w: pl.BlockSpec((tm, w), lambda i: (i, 0))
    weights = (gm, gs, wo, g2, wrh, wrl, bias, wsgu, wsd)
    topk = pl.BlockSpec((TOP_K, tm), lambda i: (0, i))
    return pl.pallas_call(
        _merge_kernel,
        grid=(T // tm,),
        in_specs=[row(D_MODEL), row(MLA_WIDTH), row(SB_WIDTH)] + [full(w) for w in weights],
        out_specs=[row(D_MODEL), row(D_MODEL), topk, topk, topk,
                   pl.BlockSpec((1, N_EXPERTS, 1), lambda i: (i, 0, 0))],
        out_shape=[jax.ShapeDtypeStruct((T, D_MODEL), F32),
                   jax.ShapeDtypeStruct((T, D_MODEL), BF16),
                   jax.ShapeDtypeStruct((TOP_K, T), jnp.int32),
                   jax.ShapeDtypeStruct((TOP_K, T), F32),
                   jax.ShapeDtypeStruct((TOP_K, T), jnp.int32),
                   jax.ShapeDtypeStruct((T // tm, N_EXPERTS, 1), jnp.int32)],
        compiler_params=pltpu.CompilerParams(dimension_semantics=("arbitrary",),
                                             vmem_limit_bytes=VMEM_LIMIT_BYTES),
        name="merge",
    )(x2, o_mla, o_sb, *weights)


MOE_TB = 2048
MOE_RT = 64
MOE_MAX_TILES = 6
MOE_ROWS = MOE_TB * TOP_K + N_EXPERTS * (MOE_RT - 1)
MOE_PACK = D_MODEL // 2
MOE_SUB = MOE_PACK // LANES
MOE_UNROLL = 8
MOE_XCH = 256
MOE_OCH = 128
HI_MASK = -65536


def _unpack_lo(w):
    return lax.bitcast_convert_type(w << 16, F32)


def _unpack_hi(w):
    return lax.bitcast_convert_type(w & jnp.int32(HI_MASK), F32)


def _moe_kernel(off_ref, nt_ref, pos_ref, gate_ref, xp_hbm, base_hbm, wgu_ref, wd_ref, o_hbm,
                xy_ref, xbuf, bbuf, obuf, xsem, bsem, osem):
    i = pl.program_id(0)
    e = pl.program_id(1)

    @pl.when((i == 0) & (e == 0))
    def _():
        def clear(r, c):
            xy_ref[pl.ds(pl.multiple_of(r * 1024, 1024), 1024), :] = jnp.zeros((1024, LANES), jnp.int32)
            return c

        lax.fori_loop(0, MOE_ROWS * MOE_SUB // 1024, clear, 0)
        tail = MOE_ROWS * MOE_SUB % 1024
        if tail:
            xy_ref[pl.ds(MOE_ROWS * MOE_SUB - tail, tail), :] = jnp.zeros((tail, LANES), jnp.int32)

    def x_copy(c, slot):
        rows = MOE_XCH * MOE_SUB
        return pltpu.make_async_copy(
            xp_hbm.at[pl.ds(i * (MOE_TB * MOE_SUB) + c * rows, rows), :], xbuf.at[slot], xsem.at[slot])

    def base_copy(c, slot):
        return pltpu.make_async_copy(
            base_hbm.at[pl.ds(i * MOE_TB + c * MOE_OCH, MOE_OCH), :], bbuf.at[slot], bsem.at[slot])

    def out_copy(c, slot):
        return pltpu.make_async_copy(
            obuf.at[slot], o_hbm.at[pl.ds(i * MOE_TB + c * MOE_OCH, MOE_OCH), :], osem.at[slot])

    @pl.when(e == 0)
    def _():
        n_chunks = MOE_TB // MOE_XCH
        x_copy(0, 0).start()
        for c in range(n_chunks):
            slot = c % 2
            if c + 1 < n_chunks:
                x_copy(c + 1, 1 - slot).start()
            x_copy(c, slot).wait()

            def step(tt, carry, c=c, slot=slot):
                for u in range(MOE_UNROLL):
                    t = tt * MOE_UNROLL + u
                    row = xbuf[slot, pl.ds(pl.multiple_of(MOE_SUB * t, MOE_SUB), MOE_SUB), :]
                    for k in range(TOP_K):
                        p = pl.multiple_of(pos_ref[TOP_K * (c * MOE_XCH + t) + k], MOE_SUB)
                        xy_ref[pl.ds(p, MOE_SUB), :] = row
                return carry

            lax.fori_loop(0, MOE_XCH // MOE_UNROLL, step, 0)

    def ffn(r0, m):
        base_row = pl.multiple_of(MOE_SUB * r0, 8)
        words = jnp.concatenate(
            [xy_ref[pl.ds(base_row + j, m, stride=MOE_SUB), :] for j in range(MOE_SUB)], axis=1)
        gu = (_dot(_unpack_lo(words).astype(BF16), wgu_ref[0, :MOE_PACK, :])
              + _dot(_unpack_hi(words).astype(BF16), wgu_ref[0, MOE_PACK:, :]))
        act = (jax.nn.silu(gu[:, :EXPERT_FF]) * gu[:, EXPERT_FF:]).astype(BF16)
        y = _dot(act, wd_ref[0])
        ya = lax.bitcast_convert_type(y[:, :MOE_PACK].astype(BF16).astype(F32), jnp.int32)
        yb = lax.bitcast_convert_type(y[:, MOE_PACK:].astype(BF16).astype(F32), jnp.int32)
        packed = lax.shift_right_logical(ya, 16) | yb
        for j in range(MOE_SUB):
            xy_ref[pl.ds(base_row + j, m, stride=MOE_SUB), :] = packed[:, j * LANES:(j + 1) * LANES]

    off = off_ref[i * N_EXPERTS + e]
    nt = nt_ref[i * N_EXPERTS + e]

    def chunk(c, carry):
        ffn(off + c * (MOE_MAX_TILES * MOE_RT), MOE_MAX_TILES * MOE_RT)
        return carry

    full = nt // MOE_MAX_TILES
    lax.fori_loop(0, full, chunk, 0)
    rest = nt - full * MOE_MAX_TILES
    for tiles in range(1, MOE_MAX_TILES):
        @pl.when(rest == tiles)
        def _():
            ffn(off + full * (MOE_MAX_TILES * MOE_RT), tiles * MOE_RT)

    @pl.when(e == N_EXPERTS - 1)
    def _():
        n_chunks = MOE_TB // MOE_OCH
        base_copy(0, 0).start()
        for c in range(n_chunks):
            slot = c % 2
            if c + 1 < n_chunks:
                base_copy(c + 1, 1 - slot).start()
            base_copy(c, slot).wait()
            if c >= 2:
                out_copy(c - 2, slot).wait()

            def step(tt, carry, c=c, slot=slot):
                for u in range(MOE_UNROLL):
                    t = tt * MOE_UNROLL + u
                    lo = jnp.zeros((MOE_SUB, LANES), F32)
                    hi = jnp.zeros((MOE_SUB, LANES), F32)
                    for k in range(TOP_K):
                        s = TOP_K * (c * MOE_OCH + t) + k
                        w = xy_ref[pl.ds(pl.multiple_of(pos_ref[s], MOE_SUB), MOE_SUB), :]
                        g = gate_ref[s]
                        lo = lo + g * _unpack_lo(w)
                        hi = hi + g * _unpack_hi(w)
                    routed = jnp.concatenate([lo[j:j + 1] for j in range(MOE_SUB)]
                                             + [hi[j:j + 1] for j in range(MOE_SUB)], axis=1)
                    obuf[slot, pl.ds(t, 1), :] = bbuf[slot, pl.ds(t, 1), :] + routed
                return carry

            lax.fori_loop(0, MOE_OCH // MOE_UNROLL, step, 0)
            out_copy(c, slot).start()
        out_copy(n_chunks - 2, n_chunks % 2).wait()
        out_copy(n_chunks - 1, (n_chunks - 1) % 2).wait()


def _moe_call(off, nt, pos, gates, xp, base, wgu, wd):
    T = base.shape[0]
    slots = MOE_TB * TOP_K
    grid_spec = pltpu.PrefetchScalarGridSpec(
        num_scalar_prefetch=2,
        grid=(T // MOE_TB, N_EXPERTS),
        in_specs=[
            pl.BlockSpec((slots,), lambda i, e, o, n: (i,), memory_space=pltpu.SMEM),
            pl.BlockSpec((slots,), lambda i, e, o, n: (i,), memory_space=pltpu.SMEM),
            pl.BlockSpec(memory_space=pl.ANY),
            pl.BlockSpec(memory_space=pl.ANY),
            pl.BlockSpec((1, D_MODEL, 2 * EXPERT_FF), lambda i, e, o, n: (e, 0, 0)),
            pl.BlockSpec((1, EXPERT_FF, D_MODEL), lambda i, e, o, n: (e, 0, 0)),
        ],
        out_specs=pl.BlockSpec(memory_space=pl.ANY),
        scratch_shapes=[
            pltpu.VMEM((MOE_ROWS * MOE_SUB, LANES), jnp.int32),
            pltpu.VMEM((2, MOE_XCH * MOE_SUB, LANES), jnp.int32),
            pltpu.VMEM((2, MOE_OCH, D_MODEL), F32),
            pltpu.VMEM((2, MOE_OCH, D_MODEL), F32),
            pltpu.SemaphoreType.DMA((2,)),
            pltpu.SemaphoreType.DMA((2,)),
            pltpu.SemaphoreType.DMA((2,)),
        ],
    )
    return pl.pallas_call(
        _moe_kernel,
        grid_spec=grid_spec,
        out_shape=jax.ShapeDtypeStruct((T, D_MODEL), F32),
        compiler_params=pltpu.CompilerParams(dimension_semantics=("arbitrary", "arbitrary"),
                                             vmem_limit_bytes=VMEM_LIMIT_BYTES),
        name="moe",
    )(off, nt, pos, gates, xp, base, wgu, wd)


def _route_tables(eidx, gate, rank, cnt):
    T = eidx.shape[1]
    nblk = T // MOE_TB
    cnt = cnt.reshape(nblk, -1, N_EXPERTS)
    per_blk = cnt.shape[1]
    earlier = jnp.cumsum(cnt, axis=1) - cnt
    tiles = (cnt.sum(axis=1) + MOE_RT - 1) // MOE_RT
    off = jnp.cumsum(tiles, axis=1) * MOE_RT - tiles * MOE_RT
    start = (off[:, None, :] + earlier).reshape(nblk * per_blk, N_EXPERTS)
    e_tk = eidx.T.reshape(nblk * per_blk, T // (nblk * per_blk), TOP_K)
    chosen = e_tk[..., None] == jnp.arange(N_EXPERTS, dtype=jnp.int32)
    pos = jnp.where(chosen, start[:, None, None, :], 0).sum(axis=-1).reshape(T, TOP_K) + rank.T
    return (off.reshape(-1).astype(jnp.int32), tiles.reshape(-1).astype(jnp.int32),
            (pos * MOE_SUB).reshape(-1).astype(jnp.int32), gate.T.reshape(-1))


def _pack_rows(a):
    u = lax.bitcast_convert_type(a, jnp.uint16).astype(jnp.uint32)
    words = u[:, :MOE_PACK] | (u[:, MOE_PACK:] << 16)
    return lax.bitcast_convert_type(words, jnp.int32).reshape(-1, LANES)


def _slot_cols(w, width):
    k = w.shape[0]
    w = w.reshape(k, MLA_HEADS, width)
    return jnp.pad(w, ((0, 0), (0, 0), (0, SLOT - width))).reshape(k, MLA_HEADS * SLOT)


def _block_flags(pos, tq, tk, chunked):
    B, S = pos.shape
    p = jnp.right_shift(pos, 6) if chunked else pos
    qmin = p.reshape(B, S // tq, tq).min(-1)[:, :, None]
    qmax = p.reshape(B, S // tq, tq).max(-1)[:, :, None]
    kmin = p.reshape(B, S // tk, tk).min(-1)[:, None, :]
    kmax = p.reshape(B, S // tk, tk).max(-1)[:, None, :]
    if chunked:
        none, all_ = kmin > qmax, kmax <= qmin
    else:
        none, all_ = kmin >= qmax, kmax < qmin
    return jnp.where(none, 0, jnp.where(all_, 1, 2)).astype(jnp.int32)


def kernel(x, positions, norm1, w_in, q_a_norm, w_uq, kv_a_norm, w_ukv, q_norm, k_norm,
           out_norm_mla, out_norm_sb, w_o, norm2, w_router, router_bias, w_gate_up, w_down,
           w_shared_gate_up, w_shared_down):
    B, S, D = x.shape
    T = B * S
    c0 = Q_LORA
    c1 = c0 + KV_LORA
    c2 = c1 + MLA_ROPE

    row2 = lambda g: g.reshape(1, -1).astype(F32)
    wcq = w_in[:, :c0].astype(BF16)
    wckv = w_in[:, c0:c1].astype(BF16)
    wkr = jnp.pad(w_in[:, c1:c2], ((0, 0), (MLA_NOPE, SLOT - MLA_QK))).astype(BF16)
    wsb = w_in[:, c2:].astype(BF16)
    wuq = _slot_cols(w_uq, MLA_QK).astype(BF16)
    w_ukv3 = w_ukv.reshape(KV_LORA, MLA_HEADS, MLA_NOPE + MLA_V)
    wuk = _slot_cols(w_ukv3[:, :, :MLA_NOPE].reshape(KV_LORA, -1), MLA_NOPE).astype(BF16)
    wuv = _slot_cols(w_ukv3[:, :, MLA_NOPE:].reshape(KV_LORA, MLA_WIDTH), MLA_V).astype(BF16)
    slot_lane = jnp.arange(SLOT)
    vone = jnp.tile((slot_lane >= MLA_V).astype(F32), MLA_HEADS).reshape(1, MLA_HEADS * SLOT)
    gq = jnp.pad(q_norm, (0, SLOT - MLA_QK)).reshape(1, SLOT)
    gk = jnp.pad(k_norm, (0, SLOT - MLA_QK)).reshape(1, SLOT)
    inv = ROPE_THETA ** (-jnp.arange(HALF_ROPE, dtype=F32) / HALF_ROPE)
    inv_slot = jnp.pad(jnp.concatenate([inv, inv]), (MLA_NOPE, SLOT - MLA_QK)).reshape(1, SLOT)

    bound = (math.sqrt(MLA_QK) * LOG2E * BOUND_MARGIN) * jnp.max(jnp.abs(q_norm)) * jnp.max(jnp.abs(k_norm))
    bounded = bound <= MLA_MAX_SHIFT
    shift = jnp.where(bounded, bound, 0.0)
    qaug = (slot_lane == MLA_QK).astype(F32).reshape(1, SLOT)
    kaug = -shift * qaug

    x2 = x.reshape(T, D)
    pos_col = positions.reshape(T, 1)
    q, k, v, sq, sk, sv = _proj_call(
        x2, pos_col,
        (row2(norm1), wcq, wckv, wkr, wsb, row2(q_a_norm), wuq, row2(kv_a_norm), wuk, wuv, vone,
         gq, gk, inv_slot, qaug, kaug), tm=512)

    pos_c3 = positions.reshape(B, S, 1)
    pos_r3 = positions.reshape(B, 1, S)
    r3 = lambda a: a.reshape(B, S, a.shape[-1])

    tq_m, tk_m = 512, 512
    mla_args = (_block_flags(positions, tq_m, tk_m, True), r3(q), r3(k), r3(v), pos_c3, pos_r3)
    o_mla = lax.cond(bounded,
                     lambda *a: _mla_call(*a, tq_m, tk_m, hp=4, online=False),
                     lambda *a: _mla_call(*a, tq_m, tk_m, hp=2, online=True), *mla_args)

    tq_s, tk_s = 256, 128
    sb_flags = _block_flags(positions, tq_s, tk_s, False)
    kstart = jnp.max(jnp.where(sb_flags != 0, jnp.arange(S // tk_s, dtype=jnp.int32), -1), axis=-1)
    o_sb = _sb_call(kstart.astype(jnp.int32), r3(sq), r3(sk), r3(sv), pos_c3, pos_r3,
                    tq_s, tk_s, win=4)

    wr_t = w_router.T
    wrh = wr_t.astype(BF16)
    wrl = (wr_t - wrh.astype(F32)).astype(BF16)
    base, hn, eidx, gate, rank, cnt = _merge_call(
        x2, o_mla.reshape(T, MLA_WIDTH), o_sb.reshape(T, SB_WIDTH), row2(out_norm_mla),
        row2(out_norm_sb), w_o.astype(BF16), row2(norm2), wrh, wrl,
        router_bias.reshape(N_EXPERTS, 1).astype(F32), w_shared_gate_up.astype(BF16),
        w_shared_down.astype(BF16), tm=512)

    off, tiles, pos, gates = _route_tables(eidx, gate, rank, cnt)
    out = _moe_call(off, tiles, pos, gates, _pack_rows(hn), base, w_gate_up.astype(BF16),
                    w_down.astype(BF16))
    return out.reshape(B, S, D)
```

```python
import functools
import math

import jax
import jax.numpy as jnp
from jax import lax
from jax.experimental import pallas as pl
from jax.experimental.pallas import tpu as pltpu

D_MODEL = 1024
CHUNK = 64
MLA_HEADS = 8
MLA_NOPE = 64
MLA_ROPE = 32
MLA_QK = MLA_NOPE + MLA_ROPE
MLA_V = 64
Q_LORA = 256
KV_LORA = 128
ROPE_THETA = 10000.0
SB_HEADS = 8
SB_HEAD_DIM = 64
SB_WIDTH = SB_HEADS * SB_HEAD_DIM
MLA_WIDTH = MLA_HEADS * MLA_V
N_EXPERTS = 64
TOP_K = 8
EXPERT_FF = 256
SHARED_FF = 256
ROUTED_SCALE = 2.5
EPS = 1e-6
NEG = -1e30

LANES = 128
VMEM_LIMIT_BYTES = 56 * 1024 * 1024

SLOT = LANES
HALF_ROPE = MLA_ROPE // 2

SB_LOG_ZERO = -90.0

LOG2E = 1.4426950408889634
BOUND_MARGIN = 1.02
MLA_MAX_SHIFT = 50.0

BF16 = jnp.bfloat16
F32 = jnp.float32


def _rms(x, g):
    return x * lax.rsqrt(jnp.mean(x * x, axis=-1, keepdims=True) + EPS) * g


def _dot(a, b):
    return jnp.dot(a, b, preferred_element_type=F32)


def _dot_nt(a, b):
    return lax.dot_general(a, b, (((1,), (1,)), ((), ())), preferred_element_type=F32)


def _proj_kernel(x_ref, pos_ref, g1_ref, wcq_ref, wckv_ref, wkr_ref, wsb_ref, gqa_ref, wuq_ref,
                 gkva_ref, wuk_ref, wuv_ref, vone_ref, gq_ref, gk_ref, inv_ref, qaug_ref, kaug_ref,
                 q_ref, k_ref, v_ref, sq_ref, sk_ref, sv_ref):
    x = x_ref[...]
    xn = _rms(x, g1_ref[...]).astype(BF16)

    cq = _rms(_dot(xn, wcq_ref[...]), gqa_ref[...]).astype(BF16)
    q = _dot(cq, wuq_ref[...])
    ckv = _rms(_dot(xn, wckv_ref[...]), gkva_ref[...]).astype(BF16)
    kn = _dot(ckv, wuk_ref[...])
    v_ref[...] = (_dot(ckv, wuv_ref[...]) + vone_ref[...]).astype(BF16)
    kr = _dot(xn, wkr_ref[...])

    sb = _dot(xn, wsb_ref[...])
    sq_ref[...] = (sb[:, :SB_WIDTH] * (1.0 / math.sqrt(SB_HEAD_DIM))).astype(BF16)
    sk_ref[...] = sb[:, SB_WIDTH:2 * SB_WIDTH].astype(BF16)
    sv_ref[...] = sb[:, 2 * SB_WIDTH:].astype(BF16)

    ang = pos_ref[...].astype(F32) * inv_ref[...]
    cos = jnp.cos(ang)
    sin = jnp.sin(ang)
    lane = lax.broadcasted_iota(jnp.int32, ang.shape, 1)
    first_half = lane < MLA_NOPE + HALF_ROPE

    def rope(t):
        up = pltpu.roll(t, SLOT - HALF_ROPE, 1)
        down = pltpu.roll(t, HALF_ROPE, 1)
        return t * cos + jnp.where(first_half, -up, down) * sin

    def head_norm(t, g):
        ss = jnp.sum(t * t, axis=-1, keepdims=True) * (1.0 / MLA_QK)
        return t * lax.rsqrt(ss + EPS) * g

    scale = LOG2E / math.sqrt(MLA_QK)
    for h in range(MLA_HEADS):
        sl = slice(h * SLOT, (h + 1) * SLOT)
        qh = rope(head_norm(q[:, sl], gq_ref[...])) * scale + qaug_ref[...]
        q_ref[:, sl] = qh.astype(BF16)
        kh = rope(head_norm(kn[:, sl] + kr, gk_ref[...])) + kaug_ref[...]
        k_ref[:, sl] = kh.astype(BF16)


def _proj_call(x2, pos_col, weights, tm):
    T = x2.shape[0]
    full = lambda a: pl.BlockSpec(a.shape, lambda i: (0,) * a.ndim)
    row = lambda w: pl.BlockSpec((tm, w), lambda i: (i, 0))
    out_w = (MLA_HEADS * SLOT, MLA_HEADS * SLOT, MLA_HEADS * SLOT, SB_WIDTH, SB_WIDTH, SB_WIDTH)
    return pl.pallas_call(
        _proj_kernel,
        grid=(T // tm,),
        in_specs=[row(D_MODEL), row(1)] + [full(w) for w in weights],
        out_specs=[row(w) for w in out_w],
        out_shape=[jax.ShapeDtypeStruct((T, w), BF16) for w in out_w],
        compiler_params=pltpu.CompilerParams(dimension_semantics=("arbitrary",),
                                             vmem_limit_bytes=VMEM_LIMIT_BYTES),
        name="proj",
    )(x2, pos_col, *weights)


def _mla_kernel(flags_ref, q_ref, k_ref, v_ref, posq_ref, posk_ref, o_ref, acc_ref, m_ref,
                *, tq, tk, nk, hp, online):
    b = pl.program_id(0)
    qi = pl.program_id(2)
    qchunk = jnp.right_shift(posq_ref[0], 6)
    acc_ref[...] = jnp.zeros(acc_ref.shape, F32)
    if online:
        m_ref[...] = jnp.full(m_ref.shape, NEG, F32)

    def block(kb, masked):
        k0 = pl.multiple_of(kb * tk, tk)
        if masked:
            kchunk = jnp.right_shift(posk_ref[0, :, pl.ds(k0, tk)], 6)
            vis = kchunk <= qchunk
        for h in range(hp):
            hs = slice(h * SLOT, (h + 1) * SLOT)
            s = _dot_nt(q_ref[0, :, hs], k_ref[0, pl.ds(k0, tk), hs])
            if masked:
                s = jnp.where(vis, s, NEG)
            vh = v_ref[0, pl.ds(k0, tk), hs]
            if online:
                m_old = m_ref[h]
                m_new = jnp.maximum(m_old, jnp.max(s, axis=-1, keepdims=True))
                p = jnp.exp2(s - m_new)
                acc_ref[h] = jnp.exp2(m_old - m_new) * acc_ref[h] + _dot(p.astype(BF16), vh)
                m_ref[h] = m_new
            else:
                acc_ref[h] += _dot(jnp.exp2(s).astype(BF16), vh)

    def body(kb, carry):
        flag = flags_ref[b, qi, kb]

        @pl.when(flag == 1)
        def _():
            block(kb, False)

        @pl.when(flag == 2)
        def _():
            block(kb, True)

        return carry

    lax.fori_loop(0, nk, body, 0)
    lane = lax.broadcasted_iota(jnp.int32, (tq, LANES), 1)
    for j in range(hp // 2):
        o0 = acc_ref[2 * j] / pltpu.roll(acc_ref[2 * j], MLA_V, 1)
        o1 = acc_ref[2 * j + 1] / pltpu.roll(acc_ref[2 * j + 1], MLA_V, 1)
        o_ref[0, :, j * LANES:(j + 1) * LANES] = jnp.where(lane < MLA_V, o0, pltpu.roll(o1, MLA_V, 1))


def _mla_call(flags, q, k, v, pos_col, pos_row, tq, tk, hp, online):
    B, S, _ = q.shape
    nq, nk = S // tq, S // tk
    grid_spec = pltpu.PrefetchScalarGridSpec(
        num_scalar_prefetch=1,
        grid=(B, MLA_HEADS // hp, nq),
        in_specs=[
            pl.BlockSpec((1, tq, hp * SLOT), lambda b, p, i, f: (b, i, p)),
            pl.BlockSpec((1, S, hp * SLOT), lambda b, p, i, f: (b, 0, p)),
            pl.BlockSpec((1, S, hp * SLOT), lambda b, p, i, f: (b, 0, p)),
            pl.BlockSpec((1, tq, 1), lambda b, p, i, f: (b, i, 0)),
            pl.BlockSpec((1, 1, S), lambda b, p, i, f: (b, 0, 0)),
        ],
        out_specs=pl.BlockSpec((1, tq, hp * MLA_V), lambda b, p, i, f: (b, i, p)),
        scratch_shapes=[pltpu.VMEM((hp, tq, LANES), F32), pltpu.VMEM((hp, tq, 1), F32)],
    )
    return pl.pallas_call(
        functools.partial(_mla_kernel, tq=tq, tk=tk, nk=nk, hp=hp, online=online),
        grid_spec=grid_spec,
        out_shape=jax.ShapeDtypeStruct((B, S, MLA_WIDTH), F32),
        compiler_params=pltpu.CompilerParams(
            dimension_semantics=("arbitrary", "arbitrary", "arbitrary"),
            vmem_limit_bytes=VMEM_LIMIT_BYTES),
        name="mla_attn_online" if online else "mla_attn",
    )(flags, q, k, v, pos_col, pos_row)


def _softplus(z):
    return jnp.maximum(z, 0.0) + jnp.log(1.0 + jnp.exp(-jnp.abs(z)))


def _sb_kernel(kstart_ref, q_ref, k_ref, v_ref, posq_ref, posk_ref, o_ref, run_ref, acc_ref,
               *, tq, tk, win):
    b = pl.program_id(0)
    qi = pl.program_id(2)
    lane = lax.broadcasted_iota(jnp.int32, (tq, LANES), 1)
    row_i = lax.broadcasted_iota(jnp.int32, (tk, tk), 0)
    col_i = lax.broadcasted_iota(jnp.int32, (tk, tk), 1)
    tri = (row_i >= col_i).astype(BF16)
    kstart = kstart_ref[b, qi]
    q_pair = q_ref[0]
    zero = jnp.zeros_like(q_pair)
    q2 = jnp.concatenate([jnp.where(lane < SB_HEAD_DIM, q_pair, zero),
                          jnp.where(lane >= SB_HEAD_DIM, q_pair, zero)], axis=0)
    qpos = jnp.concatenate([posq_ref[0], posq_ref[0]], axis=0)

    def suffix_sums(l1m):
        hi = lax.bitcast_convert_type(
            lax.bitcast_convert_type(l1m, jnp.int32) & jnp.int32(-65536), F32)
        lo = l1m - hi
        both = _dot(jnp.concatenate([hi.astype(BF16), lo.astype(BF16)], axis=0), tri)
        return both[:2 * tq] + both[2 * tq:]

    run_ref[...] = jnp.zeros(run_ref.shape, F32)
    acc_ref[...] = jnp.zeros(acc_ref.shape, F32)

    @pl.when(kstart >= win - 1)
    def _():
        k0 = pl.multiple_of((kstart - (win - 1)) * tk, tk)
        causal = posk_ref[0, :, pl.ds(k0, win * tk)] < qpos
        z = _dot_nt(q2, k_ref[0, pl.ds(k0, win * tk), :])
        l1m = jnp.where(causal, -_softplus(z), 0.0)
        run = jnp.zeros((2 * tq, 1), F32)
        a_blocks = [None] * win
        for j in reversed(range(win)):
            cs = slice(j * tk, (j + 1) * tk)
            suffix = suffix_sums(l1m[:, cs])
            a = jnp.exp(z[:, cs] + suffix + run)
            a_blocks[j] = jnp.where(causal[:, cs], a, 0.0).astype(BF16)
            run = run + suffix[:, 0:1]
        acc_ref[...] = _dot(jnp.concatenate(a_blocks, axis=1), v_ref[0, pl.ds(k0, win * tk), :])
        run_ref[...] = run

    def block(kb):
        k0 = pl.multiple_of(kb * tk, tk)
        z = _dot_nt(q2, k_ref[0, pl.ds(k0, tk), :])
        causal = posk_ref[0, :, pl.ds(k0, tk)] < qpos
        l1m = jnp.where(causal, -_softplus(z), 0.0)
        suffix = suffix_sums(l1m)
        run = run_ref[...]
        a = jnp.where(causal, jnp.exp(z + suffix + run), 0.0)
        acc_ref[...] += _dot(a.astype(BF16), v_ref[0, pl.ds(k0, tk), :])
        run_new = run + suffix[:, 0:1]
        run_ref[...] = run_new
        return jnp.max(run_new)

    def cond(c):
        kb, top = c
        return (kb >= 0) & (top >= SB_LOG_ZERO)

    def body(c):
        kb, _ = c
        return kb - 1, block(kb)

    kb_first = jnp.where(kstart >= win - 1, kstart - win, kstart)
    lax.while_loop(cond, body, (kb_first, jnp.max(run_ref[...])))

    o_ref[0] = jnp.where(lane < SB_HEAD_DIM, acc_ref[:tq], acc_ref[tq:])


def _sb_call(kstart, q, k, v, pos_col, pos_row, tq, tk, win):
    B, S, _ = q.shape
    nq = S // tq
    grid_spec = pltpu.PrefetchScalarGridSpec(
        num_scalar_prefetch=1,
        grid=(B, SB_HEADS // 2, nq),
        in_specs=[
            pl.BlockSpec((1, tq, LANES), lambda b, p, i, s: (b, i, p)),
            pl.BlockSpec((1, S, LANES), lambda b, p, i, s: (b, 0, p)),
            pl.BlockSpec((1, S, LANES), lambda b, p, i, s: (b, 0, p)),
            pl.BlockSpec((1, tq, 1), lambda b, p, i, s: (b, i, 0)),
            pl.BlockSpec((1, 1, S), lambda b, p, i, s: (b, 0, 0)),
        ],
        out_specs=pl.BlockSpec((1, tq, LANES), lambda b, p, i, s: (b, i, p)),
        scratch_shapes=[pltpu.VMEM((2 * tq, 1), F32), pltpu.VMEM((2 * tq, LANES), F32)],
    )
    return pl.pallas_call(
        functools.partial(_sb_kernel, tq=tq, tk=tk, win=win),
        grid_spec=grid_spec,
        out_shape=jax.ShapeDtypeStruct((B, S, SB_WIDTH), F32),
        compiler_params=pltpu.CompilerParams(
            dimension_semantics=("arbitrary", "arbitrary", "arbitrary"),
            vmem_limit_bytes=VMEM_LIMIT_BYTES),
        name="sb_attn",
    )(kstart, q, k, v, pos_col, pos_row)


def _split_bf16(a):
    hi = a.astype(BF16)
    lo = (a - hi.astype(F32)).astype(BF16)
    return hi, lo


def _merge_kernel(x_ref, om_ref, os_ref, gm_ref, gs_ref, wo_ref, g2_ref, wrh_ref, wrl_ref,
                  bias_ref, wsgu_ref, wsd_ref, base_ref, hn_ref, eidx_ref, gate_ref, rank_ref,
                  cnt_ref):
    mm = _rms(om_ref[...], gm_ref[...]).astype(BF16)
    ms = _rms(os_ref[...], gs_ref[...]).astype(BF16)
    h = x_ref[...] + _dot(mm, wo_ref[:MLA_WIDTH, :]) + _dot(ms, wo_ref[MLA_WIDTH:, :])
    hn = _rms(h, g2_ref[...])
    hn_hi, hn_lo = _split_bf16(hn)
    bits = lax.bitcast_convert_type(hn_hi.astype(F32), jnp.int32)
    hn_ref[...] = (lax.shift_right_logical(bits[:, :D_MODEL // 2], 16)
                   | (bits[:, D_MODEL // 2:] & jnp.int32(-65536)))

    logits = (_dot_nt(wrh_ref[...], hn_hi) + _dot_nt(wrh_ref[...], hn_lo)
              + _dot_nt(wrl_ref[...], hn_hi))
    scores = jax.nn.sigmoid(logits)
    work = scores + bias_ref[...]
    eidx = lax.broadcasted_iota(jnp.int32, work.shape, 0)
    chosen, chosen_score, hits = [], [], []
    for _ in range(TOP_K):
        top = jnp.max(work, axis=0, keepdims=True)
        first = jnp.min(jnp.where(work == top, eidx, N_EXPERTS), axis=0, keepdims=True)
        hit = eidx == first
        hits.append(hit)
        chosen.append(first)
        chosen_score.append(jnp.sum(jnp.where(hit, scores, 0.0), axis=0, keepdims=True))
        work = jnp.where(hit, -jnp.inf, work)
    sel = jnp.concatenate(chosen_score, axis=0)
    eidx_ref[...] = jnp.concatenate(chosen, axis=0)
    gate_ref[...] = sel / jnp.sum(sel, axis=0, keepdims=True) * ROUTED_SCALE

    tm = work.shape[1]
    member = functools.reduce(jnp.logical_or, hits)
    member = jnp.where(member, 1.0, 0.0)
    tri = (lax.broadcasted_iota(jnp.int32, (tm, tm), 0)
           <= lax.broadcasted_iota(jnp.int32, (tm, tm), 1)).astype(BF16)
    upto = _dot(member.astype(BF16), tri)
    before = upto - member
    rank_ref[...] = jnp.concatenate(
        [jnp.sum(jnp.where(hit, before, 0.0), axis=0, keepdims=True) for hit in hits],
        axis=0).astype(jnp.int32)
    cnt_ref[0] = upto[:, tm - 1:tm].astype(jnp.int32)

    sgu = _dot(hn_hi, wsgu_ref[...])
    act = (jax.nn.silu(sgu[:, :SHARED_FF]) * sgu[:, SHARED_FF:]).astype(BF16)
    base_ref[...] = h + _dot(act, wsd_ref[...])


def _merge_call(x2, o_mla, o_sb, gm, gs, wo, g2, wrh, wrl, bias, wsgu, wsd, tm):
    T = x2.shape[0]
    full = lambda a: pl.BlockSpec(a.shape, lambda i: (0,) * a.ndim)
    row = lambda w: pl.BlockSpec((tm, w), lambda i: (i, 0))
    weights = (gm, gs, wo, g2, wrh, wrl, bias, wsgu, wsd)
    topk = pl.BlockSpec((TOP_K, tm), lambda i: (0, i))
    return pl.pallas_call(
        _merge_kernel,
        grid=(T // tm,),
        in_specs=[row(D_MODEL), row(MLA_WIDTH), row(SB_WIDTH)] + [full(w) for w in weights],
        out_specs=[row(D_MODEL), row(D_MODEL // 2), topk, topk, topk,
                   pl.BlockSpec((1, N_EXPERTS, 1), lambda i: (i, 0, 0))],
        out_shape=[jax.ShapeDtypeStruct((T, D_MODEL), F32),
                   jax.ShapeDtypeStruct((T, D_MODEL // 2), jnp.int32),
                   jax.ShapeDtypeStruct((TOP_K, T), jnp.int32),
                   jax.ShapeDtypeStruct((TOP_K, T), F32),
                   jax.ShapeDtypeStruct((TOP_K, T), jnp.int32),
                   jax.ShapeDtypeStruct((T // tm, N_EXPERTS, 1), jnp.int32)],
        compiler_params=pltpu.CompilerParams(dimension_semantics=("arbitrary",),
                                             vmem_limit_bytes=VMEM_LIMIT_BYTES),
        name="merge",
    )(x2, o_mla, o_sb, *weights)


MOE_TB = 2048
MOE_RT = 64
MOE_MAX_TILES = 6
MOE_ROWS = MOE_TB * TOP_K + N_EXPERTS * (MOE_RT - 1)
MOE_PACK = D_MODEL // 2
MOE_SUB = MOE_PACK // LANES
MOE_UNROLL = 8
MOE_XCH = 256
MOE_OCH = 128
HI_MASK = -65536


def _unpack_lo(w):
    return lax.bitcast_convert_type(w << 16, F32)


def _unpack_hi(w):
    return lax.bitcast_convert_type(w & jnp.int32(HI_MASK), F32)


def _moe_kernel(off_ref, nt_ref, pos_ref, gate_ref, xp_hbm, base_hbm, wgu_ref, wd_ref, o_hbm,
                xy_ref, xbuf, bbuf, obuf, xsem, bsem, osem):
    i = pl.program_id(0)
    e = pl.program_id(1)

    @pl.when((i == 0) & (e == 0))
    def _():
        def clear(r, c):
            xy_ref[pl.ds(pl.multiple_of(r * 1024, 1024), 1024), :] = jnp.zeros((1024, LANES), jnp.int32)
            return c

        lax.fori_loop(0, MOE_ROWS * MOE_SUB // 1024, clear, 0)
        tail = MOE_ROWS * MOE_SUB % 1024
        if tail:
            xy_ref[pl.ds(MOE_ROWS * MOE_SUB - tail, tail), :] = jnp.zeros((tail, LANES), jnp.int32)

    def x_copy(c, slot):
        return pltpu.make_async_copy(
            xp_hbm.at[pl.ds(i * MOE_TB + c * MOE_XCH, MOE_XCH), :], xbuf.at[slot], xsem.at[slot])

    def base_copy(c, slot):
        return pltpu.make_async_copy(
            base_hbm.at[pl.ds(i * MOE_TB + c * MOE_OCH, MOE_OCH), :], bbuf.at[slot], bsem.at[slot])

    def out_copy(c, slot):
        return pltpu.make_async_copy(
            obuf.at[slot], o_hbm.at[pl.ds(i * MOE_TB + c * MOE_OCH, MOE_OCH), :], osem.at[slot])

    @pl.when(e == 0)
    def _():
        n_chunks = MOE_TB // MOE_XCH
        x_copy(0, 0).start()
        for c in range(n_chunks):
            slot = c % 2
            if c + 1 < n_chunks:
                x_copy(c + 1, 1 - slot).start()
            x_copy(c, slot).wait()

            def step(tt, carry, c=c, slot=slot):
                slab = xbuf[slot, pl.ds(pl.multiple_of(tt * MOE_UNROLL, MOE_UNROLL), MOE_UNROLL), :]
                for u in range(MOE_UNROLL):
                    t = tt * MOE_UNROLL + u
                    row = jnp.concatenate([slab[u:u + 1, j * LANES:(j + 1) * LANES]
                                           for j in range(MOE_SUB)], axis=0)
                    for k in range(TOP_K):
                        p = pl.multiple_of(pos_ref[TOP_K * (c * MOE_XCH + t) + k], MOE_SUB)
                        xy_ref[pl.ds(p, MOE_SUB), :] = row
                return carry

            lax.fori_loop(0, MOE_XCH // MOE_UNROLL, step, 0)

    def ffn(r0, m):
        base_row = pl.multiple_of(MOE_SUB * r0, 8)
        words = jnp.concatenate(
            [xy_ref[pl.ds(base_row + j, m, stride=MOE_SUB), :] for j in range(MOE_SUB)], axis=1)
        gu = (_dot(_unpack_lo(words).astype(BF16), wgu_ref[0, :MOE_PACK, :])
              + _dot(_unpack_hi(words).astype(BF16), wgu_ref[0, MOE_PACK:, :]))
        act = (jax.nn.silu(gu[:, :EXPERT_FF]) * gu[:, EXPERT_FF:]).astype(BF16)
        y = _dot(act, wd_ref[0])
        ya = lax.bitcast_convert_type(y[:, :MOE_PACK].astype(BF16).astype(F32), jnp.int32)
        yb = lax.bitcast_convert_type(y[:, MOE_PACK:].astype(BF16).astype(F32), jnp.int32)
        packed = lax.shift_right_logical(ya, 16) | yb
        for j in range(MOE_SUB):
            xy_ref[pl.ds(base_row + j, m, stride=MOE_SUB), :] = packed[:, j * LANES:(j + 1) * LANES]

    off = off_ref[i * N_EXPERTS + e]
    nt = nt_ref[i * N_EXPERTS + e]

    def chunk(c, carry):
        ffn(off + c * (MOE_MAX_TILES * MOE_RT), MOE_MAX_TILES * MOE_RT)
        return carry

    full = nt // MOE_MAX_TILES
    lax.fori_loop(0, full, chunk, 0)
    rest = nt - full * MOE_MAX_TILES
    for tiles in range(1, MOE_MAX_TILES):
        @pl.when(rest == tiles)
        def _():
            ffn(off + full * (MOE_MAX_TILES * MOE_RT), tiles * MOE_RT)

    @pl.when(e == N_EXPERTS - 1)
    def _():
        n_chunks = MOE_TB // MOE_OCH
        base_copy(0, 0).start()
        for c in range(n_chunks):
            slot = c % 2
            if c + 1 < n_chunks:
                base_copy(c + 1, 1 - slot).start()
            base_copy(c, slot).wait()
            if c >= 2:
                out_copy(c - 2, slot).wait()

            def step(tt, carry, c=c, slot=slot):
                for u in range(MOE_UNROLL):
                    t = tt * MOE_UNROLL + u
                    lo = jnp.zeros((MOE_SUB, LANES), F32)
                    hi = jnp.zeros((MOE_SUB, LANES), F32)
                    for k in range(TOP_K):
                        s = TOP_K * (c * MOE_OCH + t) + k
                        w = xy_ref[pl.ds(pl.multiple_of(pos_ref[s], MOE_SUB), MOE_SUB), :]
                        g = gate_ref[s]
                        lo = lo + g * _unpack_lo(w)
                        hi = hi + g * _unpack_hi(w)
                    routed = jnp.concatenate([lo[j:j + 1] for j in range(MOE_SUB)]
                                             + [hi[j:j + 1] for j in range(MOE_SUB)], axis=1)
                    obuf[slot, pl.ds(t, 1), :] = bbuf[slot, pl.ds(t, 1), :] + routed
                return carry

            lax.fori_loop(0, MOE_OCH // MOE_UNROLL, step, 0)
            out_copy(c, slot).start()
        out_copy(n_chunks - 2, n_chunks % 2).wait()
        out_copy(n_chunks - 1, (n_chunks - 1) % 2).wait()


def _moe_call(off, nt, pos, gates, xp, base, wgu, wd):
    T = base.shape[0]
    slots = MOE_TB * TOP_K
    grid_spec = pltpu.PrefetchScalarGridSpec(
        num_scalar_prefetch=2,
        grid=(T // MOE_TB, N_EXPERTS),
        in_specs=[
            pl.BlockSpec((slots,), lambda i, e, o, n: (i,), memory_space=pltpu.SMEM),
            pl.BlockSpec((slots,), lambda i, e, o, n: (i,), memory_space=pltpu.SMEM),
            pl.BlockSpec(memory_space=pl.ANY),
            pl.BlockSpec(memory_space=pl.ANY),
            pl.BlockSpec((1, D_MODEL, 2 * EXPERT_FF), lambda i, e, o, n: (e, 0, 0)),
            pl.BlockSpec((1, EXPERT_FF, D_MODEL), lambda i, e, o, n: (e, 0, 0)),
        ],
        out_specs=pl.BlockSpec(memory_space=pl.ANY),
        scratch_shapes=[
            pltpu.VMEM((MOE_ROWS * MOE_SUB, LANES), jnp.int32),
            pltpu.VMEM((2, MOE_XCH, MOE_PACK), jnp.int32),
            pltpu.VMEM((2, MOE_OCH, D_MODEL), F32),
            pltpu.VMEM((2, MOE_OCH, D_MODEL), F32),
            pltpu.SemaphoreType.DMA((2,)),
            pltpu.SemaphoreType.DMA((2,)),
            pltpu.SemaphoreType.DMA((2,)),
        ],
    )
    return pl.pallas_call(
        _moe_kernel,
        grid_spec=grid_spec,
        out_shape=jax.ShapeDtypeStruct((T, D_MODEL), F32),
        compiler_params=pltpu.CompilerParams(dimension_semantics=("arbitrary", "arbitrary"),
                                             vmem_limit_bytes=VMEM_LIMIT_BYTES),
        name="moe",
    )(off, nt, pos, gates, xp, base, wgu, wd)


def _route_tables(eidx, gate, rank, cnt):
    T = eidx.shape[1]
    nblk = T // MOE_TB
    cnt = cnt.reshape(nblk, -1, N_EXPERTS)
    per_blk = cnt.shape[1]
    earlier = jnp.cumsum(cnt, axis=1) - cnt
    tiles = (cnt.sum(axis=1) + MOE_RT - 1) // MOE_RT
    off = jnp.cumsum(tiles, axis=1) * MOE_RT - tiles * MOE_RT
    start = (off[:, None, :] + earlier).reshape(nblk * per_blk, N_EXPERTS)
    e_tk = eidx.T.reshape(nblk * per_blk, T // (nblk * per_blk), TOP_K)
    chosen = e_tk[..., None] == jnp.arange(N_EXPERTS, dtype=jnp.int32)
    pos = jnp.where(chosen, start[:, None, None, :], 0).sum(axis=-1).reshape(T, TOP_K) + rank.T
    return (off.reshape(-1).astype(jnp.int32), tiles.reshape(-1).astype(jnp.int32),
            (pos * MOE_SUB).reshape(-1).astype(jnp.int32), gate.T.reshape(-1))


def _slot_cols(w, width):
    k = w.shape[0]
    w = w.reshape(k, MLA_HEADS, width)
    return jnp.pad(w, ((0, 0), (0, 0), (0, SLOT - width))).reshape(k, MLA_HEADS * SLOT)


def _block_flags(pos, tq, tk, chunked):
    B, S = pos.shape
    p = jnp.right_shift(pos, 6) if chunked else pos
    qmin = p.reshape(B, S // tq, tq).min(-1)[:, :, None]
    qmax = p.reshape(B, S // tq, tq).max(-1)[:, :, None]
    kmin = p.reshape(B, S // tk, tk).min(-1)[:, None, :]
    kmax = p.reshape(B, S // tk, tk).max(-1)[:, None, :]
    if chunked:
        none, all_ = kmin > qmax, kmax <= qmin
    else:
        none, all_ = kmin >= qmax, kmax < qmin
    return jnp.where(none, 0, jnp.where(all_, 1, 2)).astype(jnp.int32)


def kernel(x, positions, norm1, w_in, q_a_norm, w_uq, kv_a_norm, w_ukv, q_norm, k_norm,
           out_norm_mla, out_norm_sb, w_o, norm2, w_router, router_bias, w_gate_up, w_down,
           w_shared_gate_up, w_shared_down):
    B, S, D = x.shape
    T = B * S
    c0 = Q_LORA
    c1 = c0 + KV_LORA
    c2 = c1 + MLA_ROPE

    row2 = lambda g: g.reshape(1, -1).astype(F32)
    wcq = w_in[:, :c0].astype(BF16)
    wckv = w_in[:, c0:c1].astype(BF16)
    wkr = jnp.pad(w_in[:, c1:c2], ((0, 0), (MLA_NOPE, SLOT - MLA_QK))).astype(BF16)
    wsb = w_in[:, c2:].astype(BF16)
    wuq = _slot_cols(w_uq, MLA_QK).astype(BF16)
    w_ukv3 = w_ukv.reshape(KV_LORA, MLA_HEADS, MLA_NOPE + MLA_V)
    wuk = _slot_cols(w_ukv3[:, :, :MLA_NOPE].reshape(KV_LORA, -1), MLA_NOPE).astype(BF16)
    wuv = _slot_cols(w_ukv3[:, :, MLA_NOPE:].reshape(KV_LORA, MLA_WIDTH), MLA_V).astype(BF16)
    slot_lane = jnp.arange(SLOT)
    vone = jnp.tile((slot_lane >= MLA_V).astype(F32), MLA_HEADS).reshape(1, MLA_HEADS * SLOT)
    gq = jnp.pad(q_norm, (0, SLOT - MLA_QK)).reshape(1, SLOT)
    gk = jnp.pad(k_norm, (0, SLOT - MLA_QK)).reshape(1, SLOT)
    inv = ROPE_THETA ** (-jnp.arange(HALF_ROPE, dtype=F32) / HALF_ROPE)
    inv_slot = jnp.pad(jnp.concatenate([inv, inv]), (MLA_NOPE, SLOT - MLA_QK)).reshape(1, SLOT)

    bound = (math.sqrt(MLA_QK) * LOG2E * BOUND_MARGIN) * jnp.max(jnp.abs(q_norm)) * jnp.max(jnp.abs(k_norm))
    bounded = bound <= MLA_MAX_SHIFT
    shift = jnp.where(bounded, bound, 0.0)
    qaug = (slot_lane == MLA_QK).astype(F32).reshape(1, SLOT)
    kaug = -shift * qaug

    x2 = x.reshape(T, D)
    pos_col = positions.reshape(T, 1)
    q, k, v, sq, sk, sv = _proj_call(
        x2, pos_col,
        (row2(norm1), wcq, wckv, wkr, wsb, row2(q_a_norm), wuq, row2(kv_a_norm), wuk, wuv, vone,
         gq, gk, inv_slot, qaug, kaug), tm=512)

    pos_c3 = positions.reshape(B, S, 1)
    pos_r3 = positions.reshape(B, 1, S)
    r3 = lambda a: a.reshape(B, S, a.shape[-1])

    tq_m, tk_m = 512, 512
    mla_args = (_block_flags(positions, tq_m, tk_m, True), r3(q), r3(k), r3(v), pos_c3, pos_r3)
    o_mla = lax.cond(bounded,
                     lambda *a: _mla_call(*a, tq_m, tk_m, hp=4, online=False),
                     lambda *a: _mla_call(*a, tq_m, tk_m, hp=2, online=True), *mla_args)

    tq_s, tk_s = 256, 128
    sb_flags = _block_flags(positions, tq_s, tk_s, False)
    kstart = jnp.max(jnp.where(sb_flags != 0, jnp.arange(S // tk_s, dtype=jnp.int32), -1), axis=-1)
    o_sb = _sb_call(kstart.astype(jnp.int32), r3(sq), r3(sk), r3(sv), pos_c3, pos_r3,
                    tq_s, tk_s, win=4)

    wr_t = w_router.T
    wrh = wr_t.astype(BF16)
    wrl = (wr_t - wrh.astype(F32)).astype(BF16)
    base, hn, eidx, gate, rank, cnt = _merge_call(
        x2, o_mla.reshape(T, MLA_WIDTH), o_sb.reshape(T, SB_WIDTH), row2(out_norm_mla),
        row2(out_norm_sb), w_o.astype(BF16), row2(norm2), wrh, wrl,
        router_bias.reshape(N_EXPERTS, 1).astype(F32), w_shared_gate_up.astype(BF16),
        w_shared_down.astype(BF16), tm=512)

    off, tiles, pos, gates = _route_tables(eidx, gate, rank, cnt)
    out = _moe_call(off, tiles, pos, gates, hn, base, w_gate_up.astype(BF16),
                    w_down.astype(BF16))
    return out.reshape(B, S, D)
```

```python
import functools
import math

import jax
import jax.numpy as jnp
from jax import lax
from jax.experimental import pallas as pl
from jax.experimental.pallas import tpu as pltpu

D_MODEL = 1024
CHUNK = 64
MLA_HEADS = 8
MLA_NOPE = 64
MLA_ROPE = 32
MLA_QK = MLA_NOPE + MLA_ROPE
MLA_V = 64
Q_LORA = 256
KV_LORA = 128
ROPE_THETA = 10000.0
SB_HEADS = 8
SB_HEAD_DIM = 64
SB_WIDTH = SB_HEADS * SB_HEAD_DIM
MLA_WIDTH = MLA_HEADS * MLA_V
N_EXPERTS = 64
TOP_K = 8
EXPERT_FF = 256
SHARED_FF = 256
ROUTED_SCALE = 2.5
EPS = 1e-6
NEG = -1e30

LANES = 128
VMEM_LIMIT_BYTES = 56 * 1024 * 1024

SLOT = LANES
HALF_ROPE = MLA_ROPE // 2

SB_LOG_ZERO = -90.0

LOG2E = 1.4426950408889634
BOUND_MARGIN = 1.02
MLA_MAX_SHIFT = 50.0

BF16 = jnp.bfloat16
F32 = jnp.float32


def _rms(x, g):
    return x * lax.rsqrt(jnp.mean(x * x, axis=-1, keepdims=True) + EPS) * g


def _dot(a, b):
    return jnp.dot(a, b, preferred_element_type=F32)


def _dot_nt(a, b):
    return lax.dot_general(a, b, (((1,), (1,)), ((), ())), preferred_element_type=F32)


def _proj_kernel(x_ref, pos_ref, g1_ref, wcq_ref, wckv_ref, wkr_ref, wsb_ref, gqa_ref, wuq_ref,
                 gkva_ref, wuk_ref, wuv_ref, vone_ref, gq_ref, gk_ref, inv_ref, qaug_ref, kaug_ref,
                 q_ref, k_ref, v_ref, sq_ref, sk_ref, sv_ref):
    x = x_ref[...]
    xn = _rms(x, g1_ref[...]).astype(BF16)

    cq = _rms(_dot(xn, wcq_ref[...]), gqa_ref[...]).astype(BF16)
    q = _dot(cq, wuq_ref[...])
    ckv = _rms(_dot(xn, wckv_ref[...]), gkva_ref[...]).astype(BF16)
    kn = _dot(ckv, wuk_ref[...])
    v_ref[...] = (_dot(ckv, wuv_ref[...]) + vone_ref[...]).astype(BF16)
    kr = _dot(xn, wkr_ref[...])

    sb = _dot(xn, wsb_ref[...])
    sq_ref[...] = (sb[:, :SB_WIDTH] * (1.0 / math.sqrt(SB_HEAD_DIM))).astype(BF16)
    sk_ref[...] = sb[:, SB_WIDTH:2 * SB_WIDTH].astype(BF16)
    sv_ref[...] = sb[:, 2 * SB_WIDTH:].astype(BF16)

    ang = pos_ref[...].astype(F32) * inv_ref[...]
    cos = jnp.cos(ang)
    sin = jnp.sin(ang)
    lane = lax.broadcasted_iota(jnp.int32, ang.shape, 1)
    first_half = lane < MLA_NOPE + HALF_ROPE

    def rope(t):
        up = pltpu.roll(t, SLOT - HALF_ROPE, 1)
        down = pltpu.roll(t, HALF_ROPE, 1)
        return t * cos + jnp.where(first_half, -up, down) * sin

    def head_norm(t, g):
        ss = jnp.sum(t * t, axis=-1, keepdims=True) * (1.0 / MLA_QK)
        return t * lax.rsqrt(ss + EPS) * g

    scale = LOG2E / math.sqrt(MLA_QK)
    for h in range(MLA_HEADS):
        sl = slice(h * SLOT, (h + 1) * SLOT)
        qh = rope(head_norm(q[:, sl], gq_ref[...])) * scale + qaug_ref[...]
        q_ref[:, sl] = qh.astype(BF16)
        kh = rope(head_norm(kn[:, sl] + kr, gk_ref[...])) + kaug_ref[...]
        k_ref[:, sl] = kh.astype(BF16)


def _proj_call(x2, pos_col, weights, tm):
    T = x2.shape[0]
    full = lambda a: pl.BlockSpec(a.shape, lambda i: (0,) * a.ndim)
    row = lambda w: pl.BlockSpec((tm, w), lambda i: (i, 0))
    out_w = (MLA_HEADS * SLOT, MLA_HEADS * SLOT, MLA_HEADS * SLOT, SB_WIDTH, SB_WIDTH, SB_WIDTH)
    return pl.pallas_call(
        _proj_kernel,
        grid=(T // tm,),
        in_specs=[row(D_MODEL), row(1)] + [full(w) for w in weights],
        out_specs=[row(w) for w in out_w],
        out_shape=[jax.ShapeDtypeStruct((T, w), BF16) for w in out_w],
        compiler_params=pltpu.CompilerParams(dimension_semantics=("arbitrary",),
                                             vmem_limit_bytes=VMEM_LIMIT_BYTES),
        name="proj",
    )(x2, pos_col, *weights)


def _mla_kernel(flags_ref, q_ref, k_ref, v_ref, posq_ref, posk_ref, o_ref, acc_ref, m_ref,
                *, tq, tk, nk, hp, online):
    b = pl.program_id(0)
    qi = pl.program_id(2)
    qchunk = jnp.right_shift(posq_ref[0], 6)
    acc_ref[...] = jnp.zeros(acc_ref.shape, F32)
    if online:
        m_ref[...] = jnp.full(m_ref.shape, NEG, F32)

    def block(kb, masked):
        k0 = pl.multiple_of(kb * tk, tk)
        if masked:
            kchunk = jnp.right_shift(posk_ref[0, :, pl.ds(k0, tk)], 6)
            vis = kchunk <= qchunk
        for h in range(hp):
            hs = slice(h * SLOT, (h + 1) * SLOT)
            s = _dot_nt(q_ref[0, :, hs], k_ref[0, pl.ds(k0, tk), hs])
            if masked:
                s = jnp.where(vis, s, NEG)
            vh = v_ref[0, pl.ds(k0, tk), hs]
            if online:
                m_old = m_ref[h]
                m_new = jnp.maximum(m_old, jnp.max(s, axis=-1, keepdims=True))
                p = jnp.exp2(s - m_new)
                acc_ref[h] = jnp.exp2(m_old - m_new) * acc_ref[h] + _dot(p.astype(BF16), vh)
                m_ref[h] = m_new
            else:
                acc_ref[h] += _dot(jnp.exp2(s).astype(BF16), vh)

    def body(kb, carry):
        flag = flags_ref[b, qi, kb]

        @pl.when(flag == 1)
        def _():
            block(kb, False)

        @pl.when(flag == 2)
        def _():
            block(kb, True)

        return carry

    lax.fori_loop(0, nk, body, 0)
    lane = lax.broadcasted_iota(jnp.int32, (tq, LANES), 1)
    for j in range(hp // 2):
        o0 = acc_ref[2 * j] / pltpu.roll(acc_ref[2 * j], MLA_V, 1)
        o1 = acc_ref[2 * j + 1] / pltpu.roll(acc_ref[2 * j + 1], MLA_V, 1)
        o_ref[0, :, j * LANES:(j + 1) * LANES] = jnp.where(lane < MLA_V, o0, pltpu.roll(o1, MLA_V, 1))


def _mla_call(flags, q, k, v, pos_col, pos_row, tq, tk, hp, online):
    B, S, _ = q.shape
    nq, nk = S // tq, S // tk
    grid_spec = pltpu.PrefetchScalarGridSpec(
        num_scalar_prefetch=1,
        grid=(B, MLA_HEADS // hp, nq),
        in_specs=[
            pl.BlockSpec((1, tq, hp * SLOT), lambda b, p, i, f: (b, i, p)),
            pl.BlockSpec((1, S, hp * SLOT), lambda b, p, i, f: (b, 0, p)),
            pl.BlockSpec((1, S, hp * SLOT), lambda b, p, i, f: (b, 0, p)),
            pl.BlockSpec((1, tq, 1), lambda b, p, i, f: (b, i, 0)),
            pl.BlockSpec((1, 1, S), lambda b, p, i, f: (b, 0, 0)),
        ],
        out_specs=pl.BlockSpec((1, tq, hp * MLA_V), lambda b, p, i, f: (b, i, p)),
        scratch_shapes=[pltpu.VMEM((hp, tq, LANES), F32), pltpu.VMEM((hp, tq, 1), F32)],
    )
    return pl.pallas_call(
        functools.partial(_mla_kernel, tq=tq, tk=tk, nk=nk, hp=hp, online=online),
        grid_spec=grid_spec,
        out_shape=jax.ShapeDtypeStruct((B, S, MLA_WIDTH), F32),
        compiler_params=pltpu.CompilerParams(
            dimension_semantics=("arbitrary", "arbitrary", "arbitrary"),
            vmem_limit_bytes=VMEM_LIMIT_BYTES),
        name="mla_attn_online" if online else "mla_attn",
    )(flags, q, k, v, pos_col, pos_row)


def _softplus(z):
    return jnp.maximum(z, 0.0) + jnp.log(1.0 + jnp.exp(-jnp.abs(z)))


def _sb_kernel(kstart_ref, q_ref, k_ref, v_ref, posq_ref, posk_ref, o_ref, run_ref, acc_ref,
               *, tq, tk, win):
    b = pl.program_id(0)
    qi = pl.program_id(2)
    lane = lax.broadcasted_iota(jnp.int32, (tq, LANES), 1)
    row_i = lax.broadcasted_iota(jnp.int32, (tk, tk), 0)
    col_i = lax.broadcasted_iota(jnp.int32, (tk, tk), 1)
    tri = (row_i >= col_i).astype(BF16)
    kstart = kstart_ref[b, qi]
    q_pair = q_ref[0]
    zero = jnp.zeros_like(q_pair)
    q2 = jnp.concatenate([jnp.where(lane < SB_HEAD_DIM, q_pair, zero),
                          jnp.where(lane >= SB_HEAD_DIM, q_pair, zero)], axis=0)
    qpos = jnp.concatenate([posq_ref[0], posq_ref[0]], axis=0)

    def suffix_sums(l1m):
        hi = lax.bitcast_convert_type(
            lax.bitcast_convert_type(l1m, jnp.int32) & jnp.int32(-65536), F32)
        lo = l1m - hi
        both = _dot(jnp.concatenate([hi.astype(BF16), lo.astype(BF16)], axis=0), tri)
        return both[:2 * tq] + both[2 * tq:]

    run_ref[...] = jnp.zeros(run_ref.shape, F32)
    acc_ref[...] = jnp.zeros(acc_ref.shape, F32)

    @pl.when(kstart >= win - 1)
    def _():
        k0 = pl.multiple_of((kstart - (win - 1)) * tk, tk)
        causal = posk_ref[0, :, pl.ds(k0, win * tk)] < qpos
        z = _dot_nt(q2, k_ref[0, pl.ds(k0, win * tk), :])
        l1m = jnp.where(causal, -_softplus(z), 0.0)
        run = jnp.zeros((2 * tq, 1), F32)
        a_blocks = [None] * win
        for j in reversed(range(win)):
            cs = slice(j * tk, (j + 1) * tk)
            suffix = suffix_sums(l1m[:, cs])
            a = jnp.exp(z[:, cs] + suffix + run)
            a_blocks[j] = jnp.where(causal[:, cs], a, 0.0).astype(BF16)
            run = run + suffix[:, 0:1]
        acc_ref[...] = _dot(jnp.concatenate(a_blocks, axis=1), v_ref[0, pl.ds(k0, win * tk), :])
        run_ref[...] = run

    def block(kb):
        k0 = pl.multiple_of(kb * tk, tk)
        z = _dot_nt(q2, k_ref[0, pl.ds(k0, tk), :])
        causal = posk_ref[0, :, pl.ds(k0, tk)] < qpos
        l1m = jnp.where(causal, -_softplus(z), 0.0)
        suffix = suffix_sums(l1m)
        run = run_ref[...]
        a = jnp.where(causal, jnp.exp(z + suffix + run), 0.0)
        acc_ref[...] += _dot(a.astype(BF16), v_ref[0, pl.ds(k0, tk), :])
        run_new = run + suffix[:, 0:1]
        run_ref[...] = run_new
        return jnp.max(run_new)

    def cond(c):
        kb, top = c
        return (kb >= 0) & (top >= SB_LOG_ZERO)

    def body(c):
        kb, _ = c
        return kb - 1, block(kb)

    kb_first = jnp.where(kstart >= win - 1, kstart - win, kstart)
    lax.while_loop(cond, body, (kb_first, jnp.max(run_ref[...])))

    o_ref[0] = jnp.where(lane < SB_HEAD_DIM, acc_ref[:tq], acc_ref[tq:])


def _sb_call(kstart, q, k, v, pos_col, pos_row, tq, tk, win):
    B, S, _ = q.shape
    nq = S // tq
    grid_spec = pltpu.PrefetchScalarGridSpec(
        num_scalar_prefetch=1,
        grid=(B, SB_HEADS // 2, nq),
        in_specs=[
            pl.BlockSpec((1, tq, LANES), lambda b, p, i, s: (b, i, p)),
            pl.BlockSpec((1, S, LANES), lambda b, p, i, s: (b, 0, p)),
            pl.BlockSpec((1, S, LANES), lambda b, p, i, s: (b, 0, p)),
            pl.BlockSpec((1, tq, 1), lambda b, p, i, s: (b, i, 0)),
            pl.BlockSpec((1, 1, S), lambda b, p, i, s: (b, 0, 0)),
        ],
        out_specs=pl.BlockSpec((1, tq, LANES), lambda b, p, i, s: (b, i, p)),
        scratch_shapes=[pltpu.VMEM((2 * tq, 1), F32), pltpu.VMEM((2 * tq, LANES), F32)],
    )
    return pl.pallas_call(
        functools.partial(_sb_kernel, tq=tq, tk=tk, win=win),
        grid_spec=grid_spec,
        out_shape=jax.ShapeDtypeStruct((B, S, SB_WIDTH), F32),
        compiler_params=pltpu.CompilerParams(
            dimension_semantics=("arbitrary", "arbitrary", "arbitrary"),
            vmem_limit_bytes=VMEM_LIMIT_BYTES),
        name="sb_attn",
    )(kstart, q, k, v, pos_col, pos_row)


def _split_bf16(a):
    hi = a.astype(BF16)
    lo = (a - hi.astype(F32)).astype(BF16)
    return hi, lo


def _merge_kernel(x_ref, om_ref, os_ref, gm_ref, gs_ref, wo_ref, g2_ref, wrh_ref, wrl_ref,
                  bias_ref, wsgu_ref, wsd_ref, base_ref, hn_ref, eidx_ref, gate_ref, rank_ref,
                  cnt_ref):
    mm = _rms(om_ref[...], gm_ref[...]).astype(BF16)
    ms = _rms(os_ref[...], gs_ref[...]).astype(BF16)
    h = x_ref[...] + _dot(mm, wo_ref[:MLA_WIDTH, :]) + _dot(ms, wo_ref[MLA_WIDTH:, :])
    hn = _rms(h, g2_ref[...])
    hn_hi, hn_lo = _split_bf16(hn)
    bits = lax.bitcast_convert_type(hn_hi.astype(F32), jnp.int32)
    hn_ref[...] = (lax.shift_right_logical(bits[:, :D_MODEL // 2], 16)
                   | (bits[:, D_MODEL // 2:] & jnp.int32(-65536)))

    logits = (_dot_nt(wrh_ref[...], hn_hi) + _dot_nt(wrh_ref[...], hn_lo)
              + _dot_nt(wrl_ref[...], hn_hi))
    scores = jax.nn.sigmoid(logits)
    work = scores + bias_ref[...]
    eidx = lax.broadcasted_iota(jnp.int32, work.shape, 0)
    chosen, chosen_score, hits = [], [], []
    for _ in range(TOP_K):
        top = jnp.max(work, axis=0, keepdims=True)
        first = jnp.min(jnp.where(work == top, eidx, N_EXPERTS), axis=0, keepdims=True)
        hit = eidx == first
        hits.append(hit)
        chosen.append(first)
        chosen_score.append(jnp.sum(jnp.where(hit, scores, 0.0), axis=0, keepdims=True))
        work = jnp.where(hit, -jnp.inf, work)
    sel = jnp.concatenate(chosen_score, axis=0)
    eidx_ref[...] = jnp.concatenate(chosen, axis=0)
    gate_ref[...] = sel / jnp.sum(sel, axis=0, keepdims=True) * ROUTED_SCALE

    tm = work.shape[1]
    member = functools.reduce(jnp.logical_or, hits)
    member = jnp.where(member, 1.0, 0.0)
    tri = (lax.broadcasted_iota(jnp.int32, (tm, tm), 0)
           <= lax.broadcasted_iota(jnp.int32, (tm, tm), 1)).astype(BF16)
    upto = _dot(member.astype(BF16), tri)
    before = upto - member
    rank_ref[...] = jnp.concatenate(
        [jnp.sum(jnp.where(hit, before, 0.0), axis=0, keepdims=True) for hit in hits],
        axis=0).astype(jnp.int32)
    cnt_ref[0] = upto[:, tm - 1:tm].astype(jnp.int32)

    sgu = _dot(hn_hi, wsgu_ref[...])
    act = (jax.nn.silu(sgu[:, :SHARED_FF]) * sgu[:, SHARED_FF:]).astype(BF16)
    base_ref[...] = h + _dot(act, wsd_ref[...])


def _merge_call(x2, o_mla, o_sb, gm, gs, wo, g2, wrh, wrl, bias, wsgu, wsd, tm):
    T = x2.shape[0]
    full = lambda a: pl.BlockSpec(a.shape, lambda i: (0,) * a.ndim)
    row = lambda w: pl.BlockSpec((tm, w), lambda i: (i, 0))
    weights = (gm, gs, wo, g2, wrh, wrl, bias, wsgu, wsd)
    topk = pl.BlockSpec((TOP_K, tm), lambda i: (0, i))
    return pl.pallas_call(
        _merge_kernel,
        grid=(T // tm,),
        in_specs=[row(D_MODEL), row(MLA_WIDTH), row(SB_WIDTH)] + [full(w) for w in weights],
        out_specs=[row(D_MODEL), row(D_MODEL // 2), topk, topk, topk,
                   pl.BlockSpec((1, N_EXPERTS, 1), lambda i: (i, 0, 0))],
        out_shape=[jax.ShapeDtypeStruct((T, D_MODEL), F32),
                   jax.ShapeDtypeStruct((T, D_MODEL // 2), jnp.int32),
                   jax.ShapeDtypeStruct((TOP_K, T), jnp.int32),
                   jax.ShapeDtypeStruct((TOP_K, T), F32),
                   jax.ShapeDtypeStruct((TOP_K, T), jnp.int32),
                   jax.ShapeDtypeStruct((T // tm, N_EXPERTS, 1), jnp.int32)],
        compiler_params=pltpu.CompilerParams(dimension_semantics=("arbitrary",),
                                             vmem_limit_bytes=VMEM_LIMIT_BYTES),
        name="merge",
    )(x2, o_mla, o_sb, *weights)


MOE_TB = 2048
MOE_RT = 64
MOE_MAX_TILES = 6
MOE_ROWS = MOE_TB * TOP_K + N_EXPERTS * (MOE_RT - 1)
MOE_PACK = D_MODEL // 2
MOE_SUB = MOE_PACK // LANES
MOE_UNROLL = 8
MOE_XCH = 256
MOE_OCH = 128
MOE_WBUF = 3
HI_MASK = -65536


def _unpack_lo(w):
    return lax.bitcast_convert_type(w << 16, F32)


def _unpack_hi(w):
    return lax.bitcast_convert_type(w & jnp.int32(HI_MASK), F32)


def _moe_kernel(off_ref, nt_ref, pos_ref, gate_ref, xp_hbm, base_hbm, wgu_hbm, wd_hbm, o_hbm,
                xy_ref, xbuf, bbuf, obuf, wgu_buf, wd_buf, xsem, bsem, osem, wsem):
    i = pl.program_id(0)
    e = pl.program_id(1)

    step_id = i * N_EXPERTS + e
    n_steps = pl.num_programs(0) * N_EXPERTS

    def w_copies(step):
        slot = step % MOE_WBUF
        ex = step % N_EXPERTS
        return (pltpu.make_async_copy(wgu_hbm.at[ex], wgu_buf.at[slot], wsem.at[0, slot]),
                pltpu.make_async_copy(wd_hbm.at[ex], wd_buf.at[slot], wsem.at[1, slot]))

    @pl.when(step_id == 0)
    def _():
        for ahead in range(MOE_WBUF - 1):
            for cp in w_copies(ahead):
                cp.start()

    @pl.when(step_id + (MOE_WBUF - 1) < n_steps)
    def _():
        for cp in w_copies(step_id + (MOE_WBUF - 1)):
            cp.start()

    @pl.when((i == 0) & (e == 0))
    def _():
        def clear(r, c):
            xy_ref[pl.ds(pl.multiple_of(r * 1024, 1024), 1024), :] = jnp.zeros((1024, LANES), jnp.int32)
            return c

        lax.fori_loop(0, MOE_ROWS * MOE_SUB // 1024, clear, 0)
        tail = MOE_ROWS * MOE_SUB % 1024
        if tail:
            xy_ref[pl.ds(MOE_ROWS * MOE_SUB - tail, tail), :] = jnp.zeros((tail, LANES), jnp.int32)

    def x_copy(c, slot):
        return pltpu.make_async_copy(
            xp_hbm.at[pl.ds(i * MOE_TB + c * MOE_XCH, MOE_XCH), :], xbuf.at[slot], xsem.at[slot])

    def base_copy(c, slot):
        return pltpu.make_async_copy(
            base_hbm.at[pl.ds(i * MOE_TB + c * MOE_OCH, MOE_OCH), :], bbuf.at[slot], bsem.at[slot])

    def out_copy(c, slot):
        return pltpu.make_async_copy(
            obuf.at[slot], o_hbm.at[pl.ds(i * MOE_TB + c * MOE_OCH, MOE_OCH), :], osem.at[slot])

    @pl.when(e == 0)
    def _():
        n_chunks = MOE_TB // MOE_XCH
        x_copy(0, 0).start()
        for c in range(n_chunks):
            slot = c % 2
            if c + 1 < n_chunks:
                x_copy(c + 1, 1 - slot).start()
            x_copy(c, slot).wait()

            def step(tt, carry, c=c, slot=slot):
                slab = xbuf[slot, pl.ds(pl.multiple_of(tt * MOE_UNROLL, MOE_UNROLL), MOE_UNROLL), :]
                for u in range(MOE_UNROLL):
                    t = tt * MOE_UNROLL + u
                    row = jnp.concatenate([slab[u:u + 1, j * LANES:(j + 1) * LANES]
                                           for j in range(MOE_SUB)], axis=0)
                    for k in range(TOP_K):
                        p = pl.multiple_of(pos_ref[TOP_K * (c * MOE_XCH + t) + k], MOE_SUB)
                        xy_ref[pl.ds(p, MOE_SUB), :] = row
                return carry

            lax.fori_loop(0, MOE_XCH // MOE_UNROLL, step, 0)

    for cp in w_copies(step_id):
        cp.wait()
    wslot = step_id % MOE_WBUF

    def ffn(r0, m):
        base_row = pl.multiple_of(MOE_SUB * r0, 8)
        words = jnp.concatenate(
            [xy_ref[pl.ds(base_row + j, m, stride=MOE_SUB), :] for j in range(MOE_SUB)], axis=1)
        gu = (_dot(_unpack_lo(words).astype(BF16), wgu_buf[wslot, :MOE_PACK, :])
              + _dot(_unpack_hi(words).astype(BF16), wgu_buf[wslot, MOE_PACK:, :]))
        act = (jax.nn.silu(gu[:, :EXPERT_FF]) * gu[:, EXPERT_FF:]).astype(BF16)
        y = _dot(act, wd_buf[wslot])
        ya = lax.bitcast_convert_type(y[:, :MOE_PACK].astype(BF16).astype(F32), jnp.int32)
        yb = lax.bitcast_convert_type(y[:, MOE_PACK:].astype(BF16).astype(F32), jnp.int32)
        packed = lax.shift_right_logical(ya, 16) | yb
        for j in range(MOE_SUB):
            xy_ref[pl.ds(base_row + j, m, stride=MOE_SUB), :] = packed[:, j * LANES:(j + 1) * LANES]

    off = off_ref[i * N_EXPERTS + e]
    nt = nt_ref[i * N_EXPERTS + e]

    def chunk(c, carry):
        ffn(off + c * (MOE_MAX_TILES * MOE_RT), MOE_MAX_TILES * MOE_RT)
        return carry

    full = nt // MOE_MAX_TILES
    lax.fori_loop(0, full, chunk, 0)
    rest = nt - full * MOE_MAX_TILES
    for tiles in range(1, MOE_MAX_TILES):
        @pl.when(rest == tiles)
        def _():
            ffn(off + full * (MOE_MAX_TILES * MOE_RT), tiles * MOE_RT)

    @pl.when(e == N_EXPERTS - 1)
    def _():
        n_chunks = MOE_TB // MOE_OCH
        base_copy(0, 0).start()
        for c in range(n_chunks):
            slot = c % 2
            if c + 1 < n_chunks:
                base_copy(c + 1, 1 - slot).start()
            base_copy(c, slot).wait()
            if c >= 2:
                out_copy(c - 2, slot).wait()

            def step(tt, carry, c=c, slot=slot):
                for u in range(MOE_UNROLL):
                    t = tt * MOE_UNROLL + u
                    lo = jnp.zeros((MOE_SUB, LANES), F32)
                    hi = jnp.zeros((MOE_SUB, LANES), F32)
                    for k in range(TOP_K):
                        s = TOP_K * (c * MOE_OCH + t) + k
                        w = xy_ref[pl.ds(pl.multiple_of(pos_ref[s], MOE_SUB), MOE_SUB), :]
                        g = gate_ref[s]
                        lo = lo + g * _unpack_lo(w)
                        hi = hi + g * _unpack_hi(w)
                    routed = jnp.concatenate([lo[j:j + 1] for j in range(MOE_SUB)]
                                             + [hi[j:j + 1] for j in range(MOE_SUB)], axis=1)
                    obuf[slot, pl.ds(t, 1), :] = bbuf[slot, pl.ds(t, 1), :] + routed
                return carry

            lax.fori_loop(0, MOE_OCH // MOE_UNROLL, step, 0)
            out_copy(c, slot).start()
        out_copy(n_chunks - 2, n_chunks % 2).wait()
        out_copy(n_chunks - 1, (n_chunks - 1) % 2).wait()


def _moe_call(off, nt, pos, gates, xp, base, wgu, wd):
    T = base.shape[0]
    slots = MOE_TB * TOP_K
    grid_spec = pltpu.PrefetchScalarGridSpec(
        num_scalar_prefetch=2,
        grid=(T // MOE_TB, N_EXPERTS),
        in_specs=[
            pl.BlockSpec((slots,), lambda i, e, o, n: (i,), memory_space=pltpu.SMEM),
            pl.BlockSpec((slots,), lambda i, e, o, n: (i,), memory_space=pltpu.SMEM),
            pl.BlockSpec(memory_space=pl.ANY),
            pl.BlockSpec(memory_space=pl.ANY),
            pl.BlockSpec(memory_space=pl.ANY),
            pl.BlockSpec(memory_space=pl.ANY),
        ],
        out_specs=pl.BlockSpec(memory_space=pl.ANY),
        scratch_shapes=[
            pltpu.VMEM((MOE_ROWS * MOE_SUB, LANES), jnp.int32),
            pltpu.VMEM((2, MOE_XCH, MOE_PACK), jnp.int32),
            pltpu.VMEM((2, MOE_OCH, D_MODEL), F32),
            pltpu.VMEM((2, MOE_OCH, D_MODEL), F32),
            pltpu.VMEM((MOE_WBUF, D_MODEL, 2 * EXPERT_FF), BF16),
            pltpu.VMEM((MOE_WBUF, EXPERT_FF, D_MODEL), BF16),
            pltpu.SemaphoreType.DMA((2,)),
            pltpu.SemaphoreType.DMA((2,)),
            pltpu.SemaphoreType.DMA((2,)),
            pltpu.SemaphoreType.DMA((2, MOE_WBUF)),
        ],
    )
    return pl.pallas_call(
        _moe_kernel,
        grid_spec=grid_spec,
        out_shape=jax.ShapeDtypeStruct((T, D_MODEL), F32),
        compiler_params=pltpu.CompilerParams(dimension_semantics=("arbitrary", "arbitrary"),
                                             vmem_limit_bytes=VMEM_LIMIT_BYTES),
        name="moe",
    )(off, nt, pos, gates, xp, base, wgu, wd)


def _route_tables(eidx, gate, rank, cnt):
    T = eidx.shape[1]
    nblk = T // MOE_TB
    cnt = cnt.reshape(nblk, -1, N_EXPERTS)
    per_blk = cnt.shape[1]
    earlier = jnp.cumsum(cnt, axis=1) - cnt
    tiles = (cnt.sum(axis=1) + MOE_RT - 1) // MOE_RT
    off = jnp.cumsum(tiles, axis=1) * MOE_RT - tiles * MOE_RT
    start = (off[:, None, :] + earlier).reshape(nblk * per_blk, N_EXPERTS)
    e_tk = eidx.T.reshape(nblk * per_blk, T // (nblk * per_blk), TOP_K)
    chosen = e_tk[..., None] == jnp.arange(N_EXPERTS, dtype=jnp.int32)
    pos = jnp.where(chosen, start[:, None, None, :], 0).sum(axis=-1).reshape(T, TOP_K) + rank.T
    return (off.reshape(-1).astype(jnp.int32), tiles.reshape(-1).astype(jnp.int32),
            (pos * MOE_SUB).reshape(-1).astype(jnp.int32), gate.T.reshape(-1))


def _slot_cols(w, width):
    k = w.shape[0]
    w = w.reshape(k, MLA_HEADS, width)
    return jnp.pad(w, ((0, 0), (0, 0), (0, SLOT - width))).reshape(k, MLA_HEADS * SLOT)


def _block_flags(pos, tq, tk, chunked):
    B, S = pos.shape
    p = jnp.right_shift(pos, 6) if chunked else pos
    qmin = p.reshape(B, S // tq, tq).min(-1)[:, :, None]
    qmax = p.reshape(B, S // tq, tq).max(-1)[:, :, None]
    kmin = p.reshape(B, S // tk, tk).min(-1)[:, None, :]
    kmax = p.reshape(B, S // tk, tk).max(-1)[:, None, :]
    if chunked:
        none, all_ = kmin > qmax, kmax <= qmin
    else:
        none, all_ = kmin >= qmax, kmax < qmin
    return jnp.where(none, 0, jnp.where(all_, 1, 2)).astype(jnp.int32)


def kernel(x, positions, norm1, w_in, q_a_norm, w_uq, kv_a_norm, w_ukv, q_norm, k_norm,
           out_norm_mla, out_norm_sb, w_o, norm2, w_router, router_bias, w_gate_up, w_down,
           w_shared_gate_up, w_shared_down):
    B, S, D = x.shape
    T = B * S
    c0 = Q_LORA
    c1 = c0 + KV_LORA
    c2 = c1 + MLA_ROPE

    row2 = lambda g: g.reshape(1, -1).astype(F32)
    wcq = w_in[:, :c0].astype(BF16)
    wckv = w_in[:, c0:c1].astype(BF16)
    wkr = jnp.pad(w_in[:, c1:c2], ((0, 0), (MLA_NOPE, SLOT - MLA_QK))).astype(BF16)
    wsb = w_in[:, c2:].astype(BF16)
    wuq = _slot_cols(w_uq, MLA_QK).astype(BF16)
    w_ukv3 = w_ukv.reshape(KV_LORA, MLA_HEADS, MLA_NOPE + MLA_V)
    wuk = _slot_cols(w_ukv3[:, :, :MLA_NOPE].reshape(KV_LORA, -1), MLA_NOPE).astype(BF16)
    wuv = _slot_cols(w_ukv3[:, :, MLA_NOPE:].reshape(KV_LORA, MLA_WIDTH), MLA_V).astype(BF16)
    slot_lane = jnp.arange(SLOT)
    vone = jnp.tile((slot_lane >= MLA_V).astype(F32), MLA_HEADS).reshape(1, MLA_HEADS * SLOT)
    gq = jnp.pad(q_norm, (0, SLOT - MLA_QK)).reshape(1, SLOT)
    gk = jnp.pad(k_norm, (0, SLOT - MLA_QK)).reshape(1, SLOT)
    inv = ROPE_THETA ** (-jnp.arange(HALF_ROPE, dtype=F32) / HALF_ROPE)
    inv_slot = jnp.pad(jnp.concatenate([inv, inv]), (MLA_NOPE, SLOT - MLA_QK)).reshape(1, SLOT)

    bound = (math.sqrt(MLA_QK) * LOG2E * BOUND_MARGIN) * jnp.max(jnp.abs(q_norm)) * jnp.max(jnp.abs(k_norm))
    bounded = bound <= MLA_MAX_SHIFT
    shift = jnp.where(bounded, bound, 0.0)
    qaug = (slot_lane == MLA_QK).astype(F32).reshape(1, SLOT)
    kaug = -shift * qaug

    x2 = x.reshape(T, D)
    pos_col = positions.reshape(T, 1)
    q, k, v, sq, sk, sv = _proj_call(
        x2, pos_col,
        (row2(norm1), wcq, wckv, wkr, wsb, row2(q_a_norm), wuq, row2(kv_a_norm), wuk, wuv, vone,
         gq, gk, inv_slot, qaug, kaug), tm=512)

    pos_c3 = positions.reshape(B, S, 1)
    pos_r3 = positions.reshape(B, 1, S)
    r3 = lambda a: a.reshape(B, S, a.shape[-1])

    tq_m, tk_m = 512, 512
    mla_args = (_block_flags(positions, tq_m, tk_m, True), r3(q), r3(k), r3(v), pos_c3, pos_r3)
    o_mla = lax.cond(bounded,
                     lambda *a: _mla_call(*a, tq_m, tk_m, hp=4, online=False),
                     lambda *a: _mla_call(*a, tq_m, tk_m, hp=2, online=True), *mla_args)

    tq_s, tk_s = 256, 128
    sb_flags = _block_flags(positions, tq_s, tk_s, False)
    kstart = jnp.max(jnp.where(sb_flags != 0, jnp.arange(S // tk_s, dtype=jnp.int32), -1), axis=-1)
    o_sb = _sb_call(kstart.astype(jnp.int32), r3(sq), r3(sk), r3(sv), pos_c3, pos_r3,
                    tq_s, tk_s, win=4)

    wr_t = w_router.T
    wrh = wr_t.astype(BF16)
    wrl = (wr_t - wrh.astype(F32)).astype(BF16)
    base, hn, eidx, gate, rank, cnt = _merge_call(
        x2, o_mla.reshape(T, MLA_WIDTH), o_sb.reshape(T, SB_WIDTH), row2(out_norm_mla),
        row2(out_norm_sb), w_o.astype(BF16), row2(norm2), wrh, wrl,
        router_bias.reshape(N_EXPERTS, 1).astype(F32), w_shared_gate_up.astype(BF16),
        w_shared_down.astype(BF16), tm=512)

    off, tiles, pos, gates = _route_tables(eidx, gate, rank, cnt)
    out = _moe_call(off, tiles, pos, gates, hn, base, w_gate_up.astype(BF16),
                    w_down.astype(BF16))
    return out.reshape(B, S, D)
```

```python
import functools
import math

import jax
import jax.numpy as jnp
from jax import lax
from jax.experimental import pallas as pl
from jax.experimental.pallas import tpu as pltpu

D_MODEL = 1024
CHUNK = 64
MLA_HEADS = 8
MLA_NOPE = 64
MLA_ROPE = 32
MLA_QK = MLA_NOPE + MLA_ROPE
MLA_V = 64
Q_LORA = 256
KV_LORA = 128
ROPE_THETA = 10000.0
SB_HEADS = 8
SB_HEAD_DIM = 64
SB_WIDTH = SB_HEADS * SB_HEAD_DIM
MLA_WIDTH = MLA_HEADS * MLA_V
N_EXPERTS = 64
TOP_K = 8
EXPERT_FF = 256
SHARED_FF = 256
ROUTED_SCALE = 2.5
EPS = 1e-6
NEG = -1e30

LANES = 128
VMEM_LIMIT_BYTES = 56 * 1024 * 1024

SLOT = LANES
HALF_ROPE = MLA_ROPE // 2

SB_LOG_ZERO = -90.0

LOG2E = 1.4426950408889634
BOUND_MARGIN = 1.02
MLA_MAX_SHIFT = 50.0

BF16 = jnp.bfloat16
F32 = jnp.float32
HI_MASK = -65536
CHUNK_SHIFT = CHUNK.bit_length() - 1
assert 1 << CHUNK_SHIFT == CHUNK

PROJ_TM = 512
MERGE_TM = 1024
MLA_TQ, MLA_TK = 512, 512
MLA_HEADS_PER_STEP = 4
SB_TQ, SB_TK, SB_WIN = 256, 128, 4


def _rms(x, g):
    return x * lax.rsqrt(jnp.mean(x * x, axis=-1, keepdims=True) + EPS) * g


def _dot(a, b):
    return jnp.dot(a, b, preferred_element_type=F32)


def _dot_nt(a, b):
    return lax.dot_general(a, b, (((1,), (1,)), ((), ())), preferred_element_type=F32)


def _proj_kernel(x_ref, pos_ref, g1_ref, wcq_ref, wckv_ref, wkr_ref, wsb_ref, gqa_ref, wuq_ref,
                 gkva_ref, wuk_ref, wuv_ref, vone_ref, gq_ref, gk_ref, inv_ref, qaug_ref, kaug_ref,
                 q_ref, k_ref, v_ref, sq_ref, sk_ref, sv_ref):
    x = x_ref[...]
    xn = _rms(x, g1_ref[...]).astype(BF16)

    cq = _rms(_dot(xn, wcq_ref[...]), gqa_ref[...]).astype(BF16)
    q = _dot(cq, wuq_ref[...])
    ckv = _rms(_dot(xn, wckv_ref[...]), gkva_ref[...]).astype(BF16)
    kn = _dot(ckv, wuk_ref[...])
    v_ref[...] = (_dot(ckv, wuv_ref[...]) + vone_ref[...]).astype(BF16)
    kr = _dot(xn, wkr_ref[...])

    sb = _dot(xn, wsb_ref[...])
    sq_ref[...] = (sb[:, :SB_WIDTH] * (1.0 / math.sqrt(SB_HEAD_DIM))).astype(BF16)
    sk_ref[...] = sb[:, SB_WIDTH:2 * SB_WIDTH].astype(BF16)
    sv_ref[...] = sb[:, 2 * SB_WIDTH:].astype(BF16)

    ang = pos_ref[...].astype(F32) * inv_ref[...]
    cos = jnp.cos(ang)
    sin = jnp.sin(ang)
    lane = lax.broadcasted_iota(jnp.int32, ang.shape, 1)
    first_half = lane < MLA_NOPE + HALF_ROPE

    def rope(t):
        up = pltpu.roll(t, SLOT - HALF_ROPE, 1)
        down = pltpu.roll(t, HALF_ROPE, 1)
        return t * cos + jnp.where(first_half, -up, down) * sin

    def head_norm(t, g):
        ss = jnp.sum(t * t, axis=-1, keepdims=True) * (1.0 / MLA_QK)
        return t * lax.rsqrt(ss + EPS) * g

    scale = LOG2E / math.sqrt(MLA_QK)
    for h in range(MLA_HEADS):
        sl = slice(h * SLOT, (h + 1) * SLOT)
        qh = rope(head_norm(q[:, sl], gq_ref[...])) * scale + qaug_ref[...]
        q_ref[:, sl] = qh.astype(BF16)
        kh = rope(head_norm(kn[:, sl] + kr, gk_ref[...])) + kaug_ref[...]
        k_ref[:, sl] = kh.astype(BF16)


def _proj_call(x2, pos_col, weights, tm):
    T = x2.shape[0]
    full = lambda a: pl.BlockSpec(a.shape, lambda i: (0,) * a.ndim)
    row = lambda w: pl.BlockSpec((tm, w), lambda i: (i, 0))
    out_w = (MLA_HEADS * SLOT, MLA_HEADS * SLOT, MLA_HEADS * SLOT, SB_WIDTH, SB_WIDTH, SB_WIDTH)
    return pl.pallas_call(
        _proj_kernel,
        grid=(T // tm,),
        in_specs=[row(D_MODEL), row(1)] + [full(w) for w in weights],
        out_specs=[row(w) for w in out_w],
        out_shape=[jax.ShapeDtypeStruct((T, w), BF16) for w in out_w],
        compiler_params=pltpu.CompilerParams(dimension_semantics=("arbitrary",),
                                             vmem_limit_bytes=VMEM_LIMIT_BYTES),
        name="proj",
    )(x2, pos_col, *weights)


def _mla_kernel(flags_ref, q_ref, k_ref, v_ref, posq_ref, posk_ref, o_ref, acc_ref, m_ref,
                *, tq, tk, nk, hp, online):
    b = pl.program_id(0)
    qi = pl.program_id(2)
    qchunk = jnp.right_shift(posq_ref[0], CHUNK_SHIFT)
    acc_ref[...] = jnp.zeros(acc_ref.shape, F32)
    if online:
        m_ref[...] = jnp.full(m_ref.shape, NEG, F32)

    def block(kb, masked):
        k0 = pl.multiple_of(kb * tk, tk)
        if masked:
            kchunk = jnp.right_shift(posk_ref[0, :, pl.ds(k0, tk)], CHUNK_SHIFT)
            vis = kchunk <= qchunk
        for h in range(hp):
            hs = slice(h * SLOT, (h + 1) * SLOT)
            s = _dot_nt(q_ref[0, :, hs], k_ref[0, pl.ds(k0, tk), hs])
            if masked:
                s = jnp.where(vis, s, NEG)
            vh = v_ref[0, pl.ds(k0, tk), hs]
            if online:
                m_old = m_ref[h]
                m_new = jnp.maximum(m_old, jnp.max(s, axis=-1, keepdims=True))
                p = jnp.exp2(s - m_new)
                acc_ref[h] = jnp.exp2(m_old - m_new) * acc_ref[h] + _dot(p.astype(BF16), vh)
                m_ref[h] = m_new
            else:
                acc_ref[h] += _dot(jnp.exp2(s).astype(BF16), vh)

    def body(kb, carry):
        flag = flags_ref[b, qi, kb]

        @pl.when(flag == 1)
        def _():
            block(kb, False)

        @pl.when(flag == 2)
        def _():
            block(kb, True)

        return carry

    lax.fori_loop(0, nk, body, 0)
    lane = lax.broadcasted_iota(jnp.int32, (tq, LANES), 1)
    for j in range(hp // 2):
        o0 = acc_ref[2 * j] / pltpu.roll(acc_ref[2 * j], MLA_V, 1)
        o1 = acc_ref[2 * j + 1] / pltpu.roll(acc_ref[2 * j + 1], MLA_V, 1)
        o_ref[0, :, j * LANES:(j + 1) * LANES] = jnp.where(lane < MLA_V, o0, pltpu.roll(o1, MLA_V, 1))


def _mla_call(flags, q, k, v, pos_col, pos_row, tq, tk, hp, online):
    B, S, _ = q.shape
    nq, nk = S // tq, S // tk
    grid_spec = pltpu.PrefetchScalarGridSpec(
        num_scalar_prefetch=1,
        grid=(B, MLA_HEADS // hp, nq),
        in_specs=[
            pl.BlockSpec((1, tq, hp * SLOT), lambda b, p, i, f: (b, i, p)),
            pl.BlockSpec((1, S, hp * SLOT), lambda b, p, i, f: (b, 0, p)),
            pl.BlockSpec((1, S, hp * SLOT), lambda b, p, i, f: (b, 0, p)),
            pl.BlockSpec((1, tq, 1), lambda b, p, i, f: (b, i, 0)),
            pl.BlockSpec((1, 1, S), lambda b, p, i, f: (b, 0, 0)),
        ],
        out_specs=pl.BlockSpec((1, tq, hp * MLA_V), lambda b, p, i, f: (b, i, p)),
        scratch_shapes=[pltpu.VMEM((hp, tq, LANES), F32), pltpu.VMEM((hp, tq, 1), F32)],
    )
    return pl.pallas_call(
        functools.partial(_mla_kernel, tq=tq, tk=tk, nk=nk, hp=hp, online=online),
        grid_spec=grid_spec,
        out_shape=jax.ShapeDtypeStruct((B, S, MLA_WIDTH), F32),
        compiler_params=pltpu.CompilerParams(
            dimension_semantics=("arbitrary", "arbitrary", "arbitrary"),
            vmem_limit_bytes=VMEM_LIMIT_BYTES),
        name="mla_attn_online" if online else "mla_attn",
    )(flags, q, k, v, pos_col, pos_row)


def _softplus(z):
    return jnp.maximum(z, 0.0) + jnp.log(1.0 + jnp.exp(-jnp.abs(z)))


def _sb_kernel(kstart_ref, q_ref, k_ref, v_ref, posq_ref, posk_ref, o_ref, run_ref, acc_ref,
               *, tq, tk, win):
    b = pl.program_id(0)
    qi = pl.program_id(2)
    lane = lax.broadcasted_iota(jnp.int32, (tq, LANES), 1)
    row_i = lax.broadcasted_iota(jnp.int32, (tk, tk), 0)
    col_i = lax.broadcasted_iota(jnp.int32, (tk, tk), 1)
    tri = (row_i >= col_i).astype(BF16)
    kstart = kstart_ref[b, qi]
    q_pair = q_ref[0]
    zero = jnp.zeros_like(q_pair)
    q2 = jnp.concatenate([jnp.where(lane < SB_HEAD_DIM, q_pair, zero),
                          jnp.where(lane >= SB_HEAD_DIM, q_pair, zero)], axis=0)
    qpos = jnp.concatenate([posq_ref[0], posq_ref[0]], axis=0)

    def suffix_sums(l1m):
        hi = lax.bitcast_convert_type(
            lax.bitcast_convert_type(l1m, jnp.int32) & jnp.int32(HI_MASK), F32)
        lo = l1m - hi
        both = _dot(jnp.concatenate([hi.astype(BF16), lo.astype(BF16)], axis=0), tri)
        return both[:2 * tq] + both[2 * tq:]

    run_ref[...] = jnp.zeros(run_ref.shape, F32)
    acc_ref[...] = jnp.zeros(acc_ref.shape, F32)

    @pl.when(kstart >= win - 1)
    def _():
        k0 = pl.multiple_of((kstart - (win - 1)) * tk, tk)
        causal = posk_ref[0, :, pl.ds(k0, win * tk)] < qpos
        z = _dot_nt(q2, k_ref[0, pl.ds(k0, win * tk), :])
        l1m = jnp.where(causal, -_softplus(z), 0.0)
        run = jnp.zeros((2 * tq, 1), F32)
        a_blocks = [None] * win
        for j in reversed(range(win)):
            cs = slice(j * tk, (j + 1) * tk)
            suffix = suffix_sums(l1m[:, cs])
            a = jnp.exp(z[:, cs] + suffix + run)
            a_blocks[j] = jnp.where(causal[:, cs], a, 0.0).astype(BF16)
            run = run + suffix[:, 0:1]
        acc_ref[...] = _dot(jnp.concatenate(a_blocks, axis=1), v_ref[0, pl.ds(k0, win * tk), :])
        run_ref[...] = run

    def block(kb):
        k0 = pl.multiple_of(kb * tk, tk)
        z = _dot_nt(q2, k_ref[0, pl.ds(k0, tk), :])
        causal = posk_ref[0, :, pl.ds(k0, tk)] < qpos
        l1m = jnp.where(causal, -_softplus(z), 0.0)
        suffix = suffix_sums(l1m)
        run = run_ref[...]
        a = jnp.where(causal, jnp.exp(z + suffix + run), 0.0)
        acc_ref[...] += _dot(a.astype(BF16), v_ref[0, pl.ds(k0, tk), :])
        run_new = run + suffix[:, 0:1]
        run_ref[...] = run_new
        return jnp.max(run_new)

    def cond(c):
        kb, top = c
        return (kb >= 0) & (top >= SB_LOG_ZERO)

    def body(c):
        kb, _ = c
        return kb - 1, block(kb)

    kb_first = jnp.where(kstart >= win - 1, kstart - win, kstart)
    lax.while_loop(cond, body, (kb_first, jnp.max(run_ref[...])))

    o_ref[0] = jnp.where(lane < SB_HEAD_DIM, acc_ref[:tq], acc_ref[tq:])


def _sb_call(kstart, q, k, v, pos_col, pos_row, tq, tk, win):
    B, S, _ = q.shape
    nq = S // tq
    grid_spec = pltpu.PrefetchScalarGridSpec(
        num_scalar_prefetch=1,
        grid=(B, SB_HEADS // 2, nq),
        in_specs=[
            pl.BlockSpec((1, tq, LANES), lambda b, p, i, s: (b, i, p)),
            pl.BlockSpec((1, S, LANES), lambda b, p, i, s: (b, 0, p)),
            pl.BlockSpec((1, S, LANES), lambda b, p, i, s: (b, 0, p)),
            pl.BlockSpec((1, tq, 1), lambda b, p, i, s: (b, i, 0)),
            pl.BlockSpec((1, 1, S), lambda b, p, i, s: (b, 0, 0)),
        ],
        out_specs=pl.BlockSpec((1, tq, LANES), lambda b, p, i, s: (b, i, p)),
        scratch_shapes=[pltpu.VMEM((2 * tq, 1), F32), pltpu.VMEM((2 * tq, LANES), F32)],
    )
    return pl.pallas_call(
        functools.partial(_sb_kernel, tq=tq, tk=tk, win=win),
        grid_spec=grid_spec,
        out_shape=jax.ShapeDtypeStruct((B, S, SB_WIDTH), F32),
        compiler_params=pltpu.CompilerParams(
            dimension_semantics=("arbitrary", "arbitrary", "arbitrary"),
            vmem_limit_bytes=VMEM_LIMIT_BYTES),
        name="sb_attn",
    )(kstart, q, k, v, pos_col, pos_row)


def _split_bf16(a):
    hi = a.astype(BF16)
    lo = (a - hi.astype(F32)).astype(BF16)
    return hi, lo


def _merge_kernel(x_ref, om_ref, os_ref, gm_ref, gs_ref, wo_ref, g2_ref, wrh_ref, wrl_ref,
                  bias_ref, wsgu_ref, wsd_ref, base_ref, hn_ref, eidx_ref, gate_ref, rank_ref,
                  cnt_ref):
    mm = _rms(om_ref[...], gm_ref[...]).astype(BF16)
    ms = _rms(os_ref[...], gs_ref[...]).astype(BF16)
    h = x_ref[...] + _dot(mm, wo_ref[:MLA_WIDTH, :]) + _dot(ms, wo_ref[MLA_WIDTH:, :])
    hn = _rms(h, g2_ref[...])
    hn_hi, hn_lo = _split_bf16(hn)
    bits = lax.bitcast_convert_type(hn_hi.astype(F32), jnp.int32)
    hn_ref[...] = (lax.shift_right_logical(bits[:, :D_MODEL // 2], 16)
                   | (bits[:, D_MODEL // 2:] & jnp.int32(HI_MASK)))

    logits = (_dot_nt(wrh_ref[...], hn_hi) + _dot_nt(wrh_ref[...], hn_lo)
              + _dot_nt(wrl_ref[...], hn_hi))
    scores = jax.nn.sigmoid(logits)
    work = scores + bias_ref[...]
    eidx = lax.broadcasted_iota(jnp.int32, work.shape, 0)
    chosen, chosen_score, hits = [], [], []
    for _ in range(TOP_K):
        top = jnp.max(work, axis=0, keepdims=True)
        first = jnp.min(jnp.where(work == top, eidx, N_EXPERTS), axis=0, keepdims=True)
        hit = eidx == first
        hits.append(hit)
        chosen.append(first)
        chosen_score.append(jnp.sum(jnp.where(hit, scores, 0.0), axis=0, keepdims=True))
        work = jnp.where(hit, -jnp.inf, work)
    sel = jnp.concatenate(chosen_score, axis=0)
    eidx_ref[...] = jnp.concatenate(chosen, axis=0)
    gate_ref[...] = sel / jnp.sum(sel, axis=0, keepdims=True) * ROUTED_SCALE

    tm = work.shape[1]
    member = functools.reduce(jnp.logical_or, hits)
    member = jnp.where(member, 1.0, 0.0)
    tri = (lax.broadcasted_iota(jnp.int32, (tm, tm), 0)
           <= lax.broadcasted_iota(jnp.int32, (tm, tm), 1)).astype(BF16)
    upto = _dot(member.astype(BF16), tri)
    before = upto - member
    rank_ref[...] = jnp.concatenate(
        [jnp.sum(jnp.where(hit, before, 0.0), axis=0, keepdims=True) for hit in hits],
        axis=0).astype(jnp.int32)
    cnt_ref[0] = upto[:, tm - 1:tm].astype(jnp.int32)

    sgu = _dot(hn_hi, wsgu_ref[...])
    act = (jax.nn.silu(sgu[:, :SHARED_FF]) * sgu[:, SHARED_FF:]).astype(BF16)
    base_ref[...] = h + _dot(act, wsd_ref[...])


def _merge_call(x2, o_mla, o_sb, gm, gs, wo, g2, wrh, wrl, bias, wsgu, wsd, tm):
    T = x2.shape[0]
    full = lambda a: pl.BlockSpec(a.shape, lambda i: (0,) * a.ndim)
    row = lambda w: pl.BlockSpec((tm, w), lambda i: (i, 0))
    weights = (gm, gs, wo, g2, wrh, wrl, bias, wsgu, wsd)
    topk = pl.BlockSpec((TOP_K, tm), lambda i: (0, i))
    return pl.pallas_call(
        _merge_kernel,
        grid=(T // tm,),
        in_specs=[row(D_MODEL), row(MLA_WIDTH), row(SB_WIDTH)] + [full(w) for w in weights],
        out_specs=[row(D_MODEL), row(D_MODEL // 2), topk, topk, topk,
                   pl.BlockSpec((1, N_EXPERTS, 1), lambda i: (i, 0, 0))],
        out_shape=[jax.ShapeDtypeStruct((T, D_MODEL), F32),
                   jax.ShapeDtypeStruct((T, D_MODEL // 2), jnp.int32),
                   jax.ShapeDtypeStruct((TOP_K, T), jnp.int32),
                   jax.ShapeDtypeStruct((TOP_K, T), F32),
                   jax.ShapeDtypeStruct((TOP_K, T), jnp.int32),
                   jax.ShapeDtypeStruct((T // tm, N_EXPERTS, 1), jnp.int32)],
        compiler_params=pltpu.CompilerParams(dimension_semantics=("arbitrary",),
                                             vmem_limit_bytes=VMEM_LIMIT_BYTES),
        name="merge",
    )(x2, o_mla, o_sb, *weights)


MOE_TB = 2048
MOE_RT = 64
MOE_MAX_TILES = 6
MOE_ROWS = MOE_TB * TOP_K + N_EXPERTS * (MOE_RT - 1)
MOE_PACK = D_MODEL // 2
MOE_SUB = MOE_PACK // LANES
MOE_UNROLL = 8
MOE_XCH = 256
MOE_OCH = 128
MOE_WBUF = 3
MOE_CLEAR_ROWS = 1024


def _unpack_lo(w):
    return lax.bitcast_convert_type(w << 16, F32)


def _unpack_hi(w):
    return lax.bitcast_convert_type(w & jnp.int32(HI_MASK), F32)


def _moe_kernel(off_ref, nt_ref, pos_ref, gate_ref, xp_hbm, base_hbm, wgu_hbm, wd_hbm, o_hbm,
                xy_ref, xbuf, bbuf, obuf, wgu_buf, wd_buf, xsem, bsem, osem, wsem):
    i = pl.program_id(0)
    e = pl.program_id(1)

    step_id = i * N_EXPERTS + e
    n_steps = pl.num_programs(0) * N_EXPERTS

    def w_copies(step):
        slot = step % MOE_WBUF
        ex = step % N_EXPERTS
        return (pltpu.make_async_copy(wgu_hbm.at[ex], wgu_buf.at[slot], wsem.at[0, slot]),
                pltpu.make_async_copy(wd_hbm.at[ex], wd_buf.at[slot], wsem.at[1, slot]))

    @pl.when(step_id == 0)
    def _():
        for ahead in range(MOE_WBUF - 1):
            for cp in w_copies(ahead):
                cp.start()

    @pl.when(step_id + (MOE_WBUF - 1) < n_steps)
    def _():
        for cp in w_copies(step_id + (MOE_WBUF - 1)):
            cp.start()

    @pl.when((i == 0) & (e == 0))
    def _():
        def clear(r, c):
            xy_ref[pl.ds(pl.multiple_of(r * MOE_CLEAR_ROWS, MOE_CLEAR_ROWS), MOE_CLEAR_ROWS), :] = (
                jnp.zeros((MOE_CLEAR_ROWS, LANES), jnp.int32))
            return c

        lax.fori_loop(0, MOE_ROWS * MOE_SUB // MOE_CLEAR_ROWS, clear, 0)
        tail = MOE_ROWS * MOE_SUB % MOE_CLEAR_ROWS
        if tail:
            xy_ref[pl.ds(MOE_ROWS * MOE_SUB - tail, tail), :] = jnp.zeros((tail, LANES), jnp.int32)

    def x_copy(c, slot):
        return pltpu.make_async_copy(
            xp_hbm.at[pl.ds(i * MOE_TB + c * MOE_XCH, MOE_XCH), :], xbuf.at[slot], xsem.at[slot])

    def base_copy(c, slot):
        return pltpu.make_async_copy(
            base_hbm.at[pl.ds(i * MOE_TB + c * MOE_OCH, MOE_OCH), :], bbuf.at[slot], bsem.at[slot])

    def out_copy(c, slot):
        return pltpu.make_async_copy(
            obuf.at[slot], o_hbm.at[pl.ds(i * MOE_TB + c * MOE_OCH, MOE_OCH), :], osem.at[slot])

    @pl.when(e == 0)
    def _():
        n_chunks = MOE_TB // MOE_XCH
        x_copy(0, 0).start()
        for c in range(n_chunks):
            slot = c % 2
            if c + 1 < n_chunks:
                x_copy(c + 1, 1 - slot).start()
            x_copy(c, slot).wait()

            def step(tt, carry, c=c, slot=slot):
                slab = xbuf[slot, pl.ds(pl.multiple_of(tt * MOE_UNROLL, MOE_UNROLL), MOE_UNROLL), :]
                for u in range(MOE_UNROLL):
                    t = tt * MOE_UNROLL + u
                    row = jnp.concatenate([slab[u:u + 1, j * LANES:(j + 1) * LANES]
                                           for j in range(MOE_SUB)], axis=0)
                    for k in range(TOP_K):
                        p = pl.multiple_of(pos_ref[TOP_K * (c * MOE_XCH + t) + k], MOE_SUB)
                        xy_ref[pl.ds(p, MOE_SUB), :] = row
                return carry

            lax.fori_loop(0, MOE_XCH // MOE_UNROLL, step, 0)

    for cp in w_copies(step_id):
        cp.wait()
    wslot = step_id % MOE_WBUF

    def ffn(r0, m):
        base_row = pl.multiple_of(MOE_SUB * r0, 8)
        words = jnp.concatenate(
            [xy_ref[pl.ds(base_row + j, m, stride=MOE_SUB), :] for j in range(MOE_SUB)], axis=1)
        gu = (_dot(_unpack_lo(words).astype(BF16), wgu_buf[wslot, :MOE_PACK, :])
              + _dot(_unpack_hi(words).astype(BF16), wgu_buf[wslot, MOE_PACK:, :]))
        act = (jax.nn.silu(gu[:, :EXPERT_FF]) * gu[:, EXPERT_FF:]).astype(BF16)
        y = _dot(act, wd_buf[wslot])
        ya = lax.bitcast_convert_type(y[:, :MOE_PACK].astype(BF16).astype(F32), jnp.int32)
        yb = lax.bitcast_convert_type(y[:, MOE_PACK:].astype(BF16).astype(F32), jnp.int32)
        packed = lax.shift_right_logical(ya, 16) | yb
        for j in range(MOE_SUB):
            xy_ref[pl.ds(base_row + j, m, stride=MOE_SUB), :] = packed[:, j * LANES:(j + 1) * LANES]

    off = off_ref[i * N_EXPERTS + e]
    nt = nt_ref[i * N_EXPERTS + e]

    def chunk(c, carry):
        ffn(off + c * (MOE_MAX_TILES * MOE_RT), MOE_MAX_TILES * MOE_RT)
        return carry

    full = nt // MOE_MAX_TILES
    lax.fori_loop(0, full, chunk, 0)
    rest = nt - full * MOE_MAX_TILES
    for tiles in range(1, MOE_MAX_TILES):
        @pl.when(rest == tiles)
        def _():
            ffn(off + full * (MOE_MAX_TILES * MOE_RT), tiles * MOE_RT)

    @pl.when(e == N_EXPERTS - 1)
    def _():
        n_chunks = MOE_TB // MOE_OCH
        base_copy(0, 0).start()
        for c in range(n_chunks):
            slot = c % 2
            if c + 1 < n_chunks:
                base_copy(c + 1, 1 - slot).start()
            base_copy(c, slot).wait()
            if c >= 2:
                out_copy(c - 2, slot).wait()

            def step(tt, carry, c=c, slot=slot):
                for u in range(MOE_UNROLL):
                    t = tt * MOE_UNROLL + u
                    lo = jnp.zeros((MOE_SUB, LANES), F32)
                    hi = jnp.zeros((MOE_SUB, LANES), F32)
                    for k in range(TOP_K):
                        s = TOP_K * (c * MOE_OCH + t) + k
                        w = xy_ref[pl.ds(pl.multiple_of(pos_ref[s], MOE_SUB), MOE_SUB), :]
                        g = gate_ref[s]
                        lo = lo + g * _unpack_lo(w)
                        hi = hi + g * _unpack_hi(w)
                    routed = jnp.concatenate([lo[j:j + 1] for j in range(MOE_SUB)]
                                             + [hi[j:j + 1] for j in range(MOE_SUB)], axis=1)
                    obuf[slot, pl.ds(t, 1), :] = bbuf[slot, pl.ds(t, 1), :] + routed
                return carry

            lax.fori_loop(0, MOE_OCH // MOE_UNROLL, step, 0)
            out_copy(c, slot).start()
        out_copy(n_chunks - 2, n_chunks % 2).wait()
        out_copy(n_chunks - 1, (n_chunks - 1) % 2).wait()


def _moe_call(off, nt, pos, gates, xp, base, wgu, wd):
    T = base.shape[0]
    slots = MOE_TB * TOP_K
    grid_spec = pltpu.PrefetchScalarGridSpec(
        num_scalar_prefetch=2,
        grid=(T // MOE_TB, N_EXPERTS),
        in_specs=[
            pl.BlockSpec((slots,), lambda i, e, o, n: (i,), memory_space=pltpu.SMEM),
            pl.BlockSpec((slots,), lambda i, e, o, n: (i,), memory_space=pltpu.SMEM),
            pl.BlockSpec(memory_space=pl.ANY),
            pl.BlockSpec(memory_space=pl.ANY),
            pl.BlockSpec(memory_space=pl.ANY),
            pl.BlockSpec(memory_space=pl.ANY),
        ],
        out_specs=pl.BlockSpec(memory_space=pl.ANY),
        scratch_shapes=[
            pltpu.VMEM((MOE_ROWS * MOE_SUB, LANES), jnp.int32),
            pltpu.VMEM((2, MOE_XCH, MOE_PACK), jnp.int32),
            pltpu.VMEM((2, MOE_OCH, D_MODEL), F32),
            pltpu.VMEM((2, MOE_OCH, D_MODEL), F32),
            pltpu.VMEM((MOE_WBUF, D_MODEL, 2 * EXPERT_FF), BF16),
            pltpu.VMEM((MOE_WBUF, EXPERT_FF, D_MODEL), BF16),
            pltpu.SemaphoreType.DMA((2,)),
            pltpu.SemaphoreType.DMA((2,)),
            pltpu.SemaphoreType.DMA((2,)),
            pltpu.SemaphoreType.DMA((2, MOE_WBUF)),
        ],
    )
    return pl.pallas_call(
        _moe_kernel,
        grid_spec=grid_spec,
        out_shape=jax.ShapeDtypeStruct((T, D_MODEL), F32),
        compiler_params=pltpu.CompilerParams(dimension_semantics=("arbitrary", "arbitrary"),
                                             vmem_limit_bytes=VMEM_LIMIT_BYTES),
        name="moe",
    )(off, nt, pos, gates, xp, base, wgu, wd)


def _route_tables(eidx, gate, rank, cnt):
    T = eidx.shape[1]
    nblk = T // MOE_TB
    cnt = cnt.reshape(nblk, -1, N_EXPERTS)
    per_blk = cnt.shape[1]
    earlier = jnp.cumsum(cnt, axis=1) - cnt
    tiles = (cnt.sum(axis=1) + MOE_RT - 1) // MOE_RT
    off = jnp.cumsum(tiles, axis=1) * MOE_RT - tiles * MOE_RT
    start = (off[:, None, :] + earlier).reshape(nblk * per_blk, N_EXPERTS)
    e_tk = eidx.T.reshape(nblk * per_blk, T // (nblk * per_blk), TOP_K)
    chosen = e_tk[..., None] == jnp.arange(N_EXPERTS, dtype=jnp.int32)
    pos = jnp.where(chosen, start[:, None, None, :], 0).sum(axis=-1).reshape(T, TOP_K) + rank.T
    return (off.reshape(-1).astype(jnp.int32), tiles.reshape(-1).astype(jnp.int32),
            (pos * MOE_SUB).reshape(-1).astype(jnp.int32), gate.T.reshape(-1))


def _slot_cols(w, width):
    k = w.shape[0]
    w = w.reshape(k, MLA_HEADS, width)
    return jnp.pad(w, ((0, 0), (0, 0), (0, SLOT - width))).reshape(k, MLA_HEADS * SLOT)


def _block_flags(pos, tq, tk, chunked):
    B, S = pos.shape
    p = jnp.right_shift(pos, CHUNK_SHIFT) if chunked else pos
    qmin = p.reshape(B, S // tq, tq).min(-1)[:, :, None]
    qmax = p.reshape(B, S // tq, tq).max(-1)[:, :, None]
    kmin = p.reshape(B, S // tk, tk).min(-1)[:, None, :]
    kmax = p.reshape(B, S // tk, tk).max(-1)[:, None, :]
    if chunked:
        none, all_ = kmin > qmax, kmax <= qmin
    else:
        none, all_ = kmin >= qmax, kmax < qmin
    return jnp.where(none, 0, jnp.where(all_, 1, 2)).astype(jnp.int32)


def kernel(x, positions, norm1, w_in, q_a_norm, w_uq, kv_a_norm, w_ukv, q_norm, k_norm,
           out_norm_mla, out_norm_sb, w_o, norm2, w_router, router_bias, w_gate_up, w_down,
           w_shared_gate_up, w_shared_down):
    B, S, D = x.shape
    T = B * S
    c0 = Q_LORA
    c1 = c0 + KV_LORA
    c2 = c1 + MLA_ROPE

    row2 = lambda g: g.reshape(1, -1).astype(F32)
    wcq = w_in[:, :c0].astype(BF16)
    wckv = w_in[:, c0:c1].astype(BF16)
    wkr = jnp.pad(w_in[:, c1:c2], ((0, 0), (MLA_NOPE, SLOT - MLA_QK))).astype(BF16)
    wsb = w_in[:, c2:].astype(BF16)
    wuq = _slot_cols(w_uq, MLA_QK).astype(BF16)
    w_ukv3 = w_ukv.reshape(KV_LORA, MLA_HEADS, MLA_NOPE + MLA_V)
    wuk = _slot_cols(w_ukv3[:, :, :MLA_NOPE].reshape(KV_LORA, -1), MLA_NOPE).astype(BF16)
    wuv = _slot_cols(w_ukv3[:, :, MLA_NOPE:].reshape(KV_LORA, MLA_WIDTH), MLA_V).astype(BF16)
    slot_lane = jnp.arange(SLOT)
    vone = jnp.tile((slot_lane >= MLA_V).astype(F32), MLA_HEADS).reshape(1, MLA_HEADS * SLOT)
    gq = jnp.pad(q_norm, (0, SLOT - MLA_QK)).reshape(1, SLOT)
    gk = jnp.pad(k_norm, (0, SLOT - MLA_QK)).reshape(1, SLOT)
    inv = ROPE_THETA ** (-jnp.arange(HALF_ROPE, dtype=F32) / HALF_ROPE)
    inv_slot = jnp.pad(jnp.concatenate([inv, inv]), (MLA_NOPE, SLOT - MLA_QK)).reshape(1, SLOT)

    bound = (math.sqrt(MLA_QK) * LOG2E * BOUND_MARGIN) * jnp.max(jnp.abs(q_norm)) * jnp.max(jnp.abs(k_norm))
    bounded = bound <= MLA_MAX_SHIFT
    shift = jnp.where(bounded, bound, 0.0)
    qaug = (slot_lane == MLA_QK).astype(F32).reshape(1, SLOT)
    kaug = -shift * qaug

    x2 = x.reshape(T, D)
    pos_col = positions.reshape(T, 1)
    q, k, v, sq, sk, sv = _proj_call(
        x2, pos_col,
        (row2(norm1), wcq, wckv, wkr, wsb, row2(q_a_norm), wuq, row2(kv_a_norm), wuk, wuv, vone,
         gq, gk, inv_slot, qaug, kaug), tm=PROJ_TM)

    pos_c3 = positions.reshape(B, S, 1)
    pos_r3 = positions.reshape(B, 1, S)
    r3 = lambda a: a.reshape(B, S, a.shape[-1])

    mla_args = (_block_flags(positions, MLA_TQ, MLA_TK, True), r3(q), r3(k), r3(v), pos_c3, pos_r3)
    o_mla = lax.cond(
        bounded,
        lambda *a: _mla_call(*a, MLA_TQ, MLA_TK, hp=MLA_HEADS_PER_STEP, online=False),
        lambda *a: _mla_call(*a, MLA_TQ, MLA_TK, hp=2, online=True), *mla_args)

    sb_flags = _block_flags(positions, SB_TQ, SB_TK, False)
    kstart = jnp.max(jnp.where(sb_flags != 0, jnp.arange(S // SB_TK, dtype=jnp.int32), -1), axis=-1)
    o_sb = _sb_call(kstart.astype(jnp.int32), r3(sq), r3(sk), r3(sv), pos_c3, pos_r3,
                    SB_TQ, SB_TK, win=SB_WIN)

    wr_t = w_router.T
    wrh = wr_t.astype(BF16)
    wrl = (wr_t - wrh.astype(F32)).astype(BF16)
    base, hn, eidx, gate, rank, cnt = _merge_call(
        x2, o_mla.reshape(T, MLA_WIDTH), o_sb.reshape(T, SB_WIDTH), row2(out_norm_mla),
        row2(out_norm_sb), w_o.astype(BF16), row2(norm2), wrh, wrl,
        router_bias.reshape(N_EXPERTS, 1).astype(F32), w_shared_gate_up.astype(BF16),
        w_shared_down.astype(BF16), tm=MERGE_TM)

    off, tiles, pos, gates = _route_tables(eidx, gate, rank, cnt)
    out = _moe_call(off, tiles, pos, gates, hn, base, w_gate_up.astype(BF16),
                    w_down.astype(BF16))
    return out.reshape(B, S, D)
```

```python
import functools
import math

import jax
import jax.numpy as jnp
from jax import lax
from jax.experimental import pallas as pl
from jax.experimental.pallas import tpu as pltpu

D_MODEL = 1024
CHUNK = 64
MLA_HEADS = 8
MLA_NOPE = 64
MLA_ROPE = 32
MLA_QK = MLA_NOPE + MLA_ROPE
MLA_V = 64
Q_LORA = 256
KV_LORA = 128
ROPE_THETA = 10000.0
SB_HEADS = 8
SB_HEAD_DIM = 64
SB_WIDTH = SB_HEADS * SB_HEAD_DIM
MLA_WIDTH = MLA_HEADS * MLA_V
N_EXPERTS = 64
TOP_K = 8
EXPERT_FF = 256
SHARED_FF = 256
ROUTED_SCALE = 2.5
EPS = 1e-6
NEG = -1e30

LANES = 128
VMEM_LIMIT_BYTES = 56 * 1024 * 1024

SLOT = LANES
HALF_ROPE = MLA_ROPE // 2

SB_LOG_ZERO = -90.0

LOG2E = 1.4426950408889634
BOUND_MARGIN = 1.02
MLA_MAX_SHIFT = 50.0

BF16 = jnp.bfloat16
F32 = jnp.float32
HI_MASK = -65536
CHUNK_SHIFT = CHUNK.bit_length() - 1
assert 1 << CHUNK_SHIFT == CHUNK

PROJ_TM = 512
MERGE_TM = 1024
MLA_TQ, MLA_TK = 512, 512
MLA_HEADS_PER_STEP = 4
SB_TQ, SB_TK, SB_WIN = 256, 128, 4


def _rms(x, g):
    return x * lax.rsqrt(jnp.mean(x * x, axis=-1, keepdims=True) + EPS) * g


def _dot(a, b):
    return jnp.dot(a, b, preferred_element_type=F32)


def _dot_nt(a, b):
    return lax.dot_general(a, b, (((1,), (1,)), ((), ())), preferred_element_type=F32)


def _proj_kernel(x_ref, pos_ref, g1_ref, wcq_ref, wckv_ref, wkr_ref, wsb_ref, gqa_ref, wuq_ref,
                 gkva_ref, wuk_ref, wuv_ref, vone_ref, gq_ref, gk_ref, inv_ref, qaug_ref, kaug_ref,
                 q_ref, k_ref, v_ref, sq_ref, sk_ref, sv_ref):
    x = x_ref[...]
    xn = _rms(x, g1_ref[...]).astype(BF16)

    cq = _rms(_dot(xn, wcq_ref[...]), gqa_ref[...]).astype(BF16)
    q = _dot(cq, wuq_ref[...])
    ckv = _rms(_dot(xn, wckv_ref[...]), gkva_ref[...]).astype(BF16)
    kn = _dot(ckv, wuk_ref[...])
    v_ref[...] = (_dot(ckv, wuv_ref[...]) + vone_ref[...]).astype(BF16)
    kr = _dot(xn, wkr_ref[...])

    sb = _dot(xn, wsb_ref[...])
    sq_ref[...] = (sb[:, :SB_WIDTH] * (1.0 / math.sqrt(SB_HEAD_DIM))).astype(BF16)
    sk_ref[...] = sb[:, SB_WIDTH:2 * SB_WIDTH].astype(BF16)
    sv_ref[...] = sb[:, 2 * SB_WIDTH:].astype(BF16)

    ang = pos_ref[...].astype(F32) * inv_ref[...]
    cos = jnp.cos(ang)
    sin = jnp.sin(ang)
    lane = lax.broadcasted_iota(jnp.int32, ang.shape, 1)
    first_half = lane < MLA_NOPE + HALF_ROPE

    def rope(t):
        up = pltpu.roll(t, SLOT - HALF_ROPE, 1)
        down = pltpu.roll(t, HALF_ROPE, 1)
        return t * cos + jnp.where(first_half, -up, down) * sin

    def head_norm(t, g):
        ss = jnp.sum(t * t, axis=-1, keepdims=True) * (1.0 / MLA_QK)
        return t * lax.rsqrt(ss + EPS) * g

    scale = LOG2E / math.sqrt(MLA_QK)
    for h in range(MLA_HEADS):
        sl = slice(h * SLOT, (h + 1) * SLOT)
        qh = rope(head_norm(q[:, sl], gq_ref[...])) * scale + qaug_ref[...]
        q_ref[:, sl] = qh.astype(BF16)
        kh = rope(head_norm(kn[:, sl] + kr, gk_ref[...])) + kaug_ref[...]
        k_ref[:, sl] = kh.astype(BF16)


def _proj_call(x2, pos_col, weights, tm):
    T = x2.shape[0]
    full = lambda a: pl.BlockSpec(a.shape, lambda i: (0,) * a.ndim)
    row = lambda w: pl.BlockSpec((tm, w), lambda i: (i, 0))
    out_w = (MLA_HEADS * SLOT, MLA_HEADS * SLOT, MLA_HEADS * SLOT, SB_WIDTH, SB_WIDTH, SB_WIDTH)
    return pl.pallas_call(
        _proj_kernel,
        grid=(T // tm,),
        in_specs=[row(D_MODEL), row(1)] + [full(w) for w in weights],
        out_specs=[row(w) for w in out_w],
        out_shape=[jax.ShapeDtypeStruct((T, w), BF16) for w in out_w],
        compiler_params=pltpu.CompilerParams(dimension_semantics=("arbitrary",),
                                             vmem_limit_bytes=VMEM_LIMIT_BYTES),
        name="proj",
    )(x2, pos_col, *weights)


def _mla_kernel(flags_ref, q_ref, k_ref, v_ref, posq_ref, posk_ref, o_ref, acc_ref, m_ref,
                *, tq, tk, nk, hp, online):
    b = pl.program_id(0)
    qi = pl.program_id(2)
    qchunk = jnp.right_shift(posq_ref[0], CHUNK_SHIFT)
    acc_ref[...] = jnp.zeros(acc_ref.shape, F32)
    if online:
        m_ref[...] = jnp.full(m_ref.shape, NEG, F32)

    def block(kb, masked):
        k0 = pl.multiple_of(kb * tk, tk)
        if masked:
            kchunk = jnp.right_shift(posk_ref[0, :, pl.ds(k0, tk)], CHUNK_SHIFT)
            vis = kchunk <= qchunk
        for h in range(hp):
            hs = slice(h * SLOT, (h + 1) * SLOT)
            s = _dot_nt(q_ref[0, :, hs], k_ref[0, pl.ds(k0, tk), hs])
            if masked:
                s = jnp.where(vis, s, NEG)
            vh = v_ref[0, pl.ds(k0, tk), hs]
            if online:
                m_old = m_ref[h]
                m_new = jnp.maximum(m_old, jnp.max(s, axis=-1, keepdims=True))
                p = jnp.exp2(s - m_new)
                acc_ref[h] = jnp.exp2(m_old - m_new) * acc_ref[h] + _dot(p.astype(BF16), vh)
                m_ref[h] = m_new
            else:
                acc_ref[h] += _dot(jnp.exp2(s).astype(BF16), vh)

    def body(kb, carry):
        flag = flags_ref[b, qi, kb]

        @pl.when(flag == 1)
        def _():
            block(kb, False)

        @pl.when(flag == 2)
        def _():
            block(kb, True)

        return carry

    lax.fori_loop(0, nk, body, 0)
    lane = lax.broadcasted_iota(jnp.int32, (tq, LANES), 1)
    for j in range(hp // 2):
        o0 = acc_ref[2 * j] / pltpu.roll(acc_ref[2 * j], MLA_V, 1)
        o1 = acc_ref[2 * j + 1] / pltpu.roll(acc_ref[2 * j + 1], MLA_V, 1)
        o_ref[0, :, j * LANES:(j + 1) * LANES] = jnp.where(lane < MLA_V, o0, pltpu.roll(o1, MLA_V, 1))


def _mla_call(flags, q, k, v, pos_col, pos_row, tq, tk, hp, online):
    B, S, _ = q.shape
    nq, nk = S // tq, S // tk
    grid_spec = pltpu.PrefetchScalarGridSpec(
        num_scalar_prefetch=1,
        grid=(B, MLA_HEADS // hp, nq),
        in_specs=[
            pl.BlockSpec((1, tq, hp * SLOT), lambda b, p, i, f: (b, i, p)),
            pl.BlockSpec((1, S, hp * SLOT), lambda b, p, i, f: (b, 0, p)),
            pl.BlockSpec((1, S, hp * SLOT), lambda b, p, i, f: (b, 0, p)),
            pl.BlockSpec((1, tq, 1), lambda b, p, i, f: (b, i, 0)),
            pl.BlockSpec((1, 1, S), lambda b, p, i, f: (b, 0, 0)),
        ],
        out_specs=pl.BlockSpec((1, tq, hp * MLA_V), lambda b, p, i, f: (b, i, p)),
        scratch_shapes=[pltpu.VMEM((hp, tq, LANES), F32), pltpu.VMEM((hp, tq, 1), F32)],
    )
    return pl.pallas_call(
        functools.partial(_mla_kernel, tq=tq, tk=tk, nk=nk, hp=hp, online=online),
        grid_spec=grid_spec,
        out_shape=jax.ShapeDtypeStruct((B, S, MLA_WIDTH), F32),
        compiler_params=pltpu.CompilerParams(
            dimension_semantics=("arbitrary", "arbitrary", "arbitrary"),
            vmem_limit_bytes=VMEM_LIMIT_BYTES),
        name="mla_attn_online" if online else "mla_attn",
    )(flags, q, k, v, pos_col, pos_row)


def _softplus(z):
    return jnp.maximum(z, 0.0) + jnp.log(1.0 + jnp.exp(-jnp.abs(z)))


def _sb_kernel(kstart_ref, q_ref, k_ref, v_ref, posq_ref, posk_ref, o_ref, run_ref, acc_ref,
               *, tq, tk, win):
    b = pl.program_id(0)
    qi = pl.program_id(2)
    lane = lax.broadcasted_iota(jnp.int32, (tq, LANES), 1)
    row_i = lax.broadcasted_iota(jnp.int32, (tk, tk), 0)
    col_i = lax.broadcasted_iota(jnp.int32, (tk, tk), 1)
    tri = (row_i >= col_i).astype(BF16)
    kstart = kstart_ref[b, qi]
    q_pair = q_ref[0]
    zero = jnp.zeros_like(q_pair)
    q2 = jnp.concatenate([jnp.where(lane < SB_HEAD_DIM, q_pair, zero),
                          jnp.where(lane >= SB_HEAD_DIM, q_pair, zero)], axis=0)
    qpos = jnp.concatenate([posq_ref[0], posq_ref[0]], axis=0)

    def suffix_sums(l1m):
        hi = lax.bitcast_convert_type(
            lax.bitcast_convert_type(l1m, jnp.int32) & jnp.int32(HI_MASK), F32)
        lo = l1m - hi
        both = _dot(jnp.concatenate([hi.astype(BF16), lo.astype(BF16)], axis=0), tri)
        return both[:2 * tq] + both[2 * tq:]

    run_ref[...] = jnp.zeros(run_ref.shape, F32)
    acc_ref[...] = jnp.zeros(acc_ref.shape, F32)

    @pl.when(kstart >= win - 1)
    def _():
        k0 = pl.multiple_of((kstart - (win - 1)) * tk, tk)
        causal = posk_ref[0, :, pl.ds(k0, win * tk)] < qpos
        z = _dot_nt(q2, k_ref[0, pl.ds(k0, win * tk), :])
        l1m = jnp.where(causal, -_softplus(z), 0.0)
        run = jnp.zeros((2 * tq, 1), F32)
        a_blocks = [None] * win
        for j in reversed(range(win)):
            cs = slice(j * tk, (j + 1) * tk)
            suffix = suffix_sums(l1m[:, cs])
            a = jnp.exp(z[:, cs] + suffix + run)
            a_blocks[j] = jnp.where(causal[:, cs], a, 0.0).astype(BF16)
            run = run + suffix[:, 0:1]
        acc_ref[...] = _dot(jnp.concatenate(a_blocks, axis=1), v_ref[0, pl.ds(k0, win * tk), :])
        run_ref[...] = run

    def block(kb):
        k0 = pl.multiple_of(kb * tk, tk)
        z = _dot_nt(q2, k_ref[0, pl.ds(k0, tk), :])
        causal = posk_ref[0, :, pl.ds(k0, tk)] < qpos
        l1m = jnp.where(causal, -_softplus(z), 0.0)
        suffix = suffix_sums(l1m)
        run = run_ref[...]
        a = jnp.where(causal, jnp.exp(z + suffix + run), 0.0)
        acc_ref[...] += _dot(a.astype(BF16), v_ref[0, pl.ds(k0, tk), :])
        run_new = run + suffix[:, 0:1]
        run_ref[...] = run_new
        return jnp.max(run_new)

    def cond(c):
        kb, top = c
        return (kb >= 0) & (top >= SB_LOG_ZERO)

    def body(c):
        kb, _ = c
        return kb - 1, block(kb)

    kb_first = jnp.where(kstart >= win - 1, kstart - win, kstart)
    lax.while_loop(cond, body, (kb_first, jnp.max(run_ref[...])))

    o_ref[0] = jnp.where(lane < SB_HEAD_DIM, acc_ref[:tq], acc_ref[tq:])


def _sb_call(kstart, q, k, v, pos_col, pos_row, tq, tk, win):
    B, S, _ = q.shape
    nq = S // tq
    grid_spec = pltpu.PrefetchScalarGridSpec(
        num_scalar_prefetch=1,
        grid=(B, SB_HEADS // 2, nq),
        in_specs=[
            pl.BlockSpec((1, tq, LANES), lambda b, p, i, s: (b, i, p)),
            pl.BlockSpec((1, S, LANES), lambda b, p, i, s: (b, 0, p)),
            pl.BlockSpec((1, S, LANES), lambda b, p, i, s: (b, 0, p)),
            pl.BlockSpec((1, tq, 1), lambda b, p, i, s: (b, i, 0)),
            pl.BlockSpec((1, 1, S), lambda b, p, i, s: (b, 0, 0)),
        ],
        out_specs=pl.BlockSpec((1, tq, LANES), lambda b, p, i, s: (b, i, p)),
        scratch_shapes=[pltpu.VMEM((2 * tq, 1), F32), pltpu.VMEM((2 * tq, LANES), F32)],
    )
    return pl.pallas_call(
        functools.partial(_sb_kernel, tq=tq, tk=tk, win=win),
        grid_spec=grid_spec,
        out_shape=jax.ShapeDtypeStruct((B, S, SB_WIDTH), F32),
        compiler_params=pltpu.CompilerParams(
            dimension_semantics=("arbitrary", "arbitrary", "arbitrary"),
            vmem_limit_bytes=VMEM_LIMIT_BYTES),
        name="sb_attn",
    )(kstart, q, k, v, pos_col, pos_row)


def _round_robin(*gens):
    gens = list(gens)
    while gens:
        for g in list(gens):
            if next(g, StopIteration) is StopIteration:
                gens.remove(g)


def _attn_kernel(flags_ref, kstart_ref, q_ref, k_ref, v_ref, sq_ref, sk_ref, sv_ref, posq_ref,
                 posk_ref, om_ref, os_ref, acc_ref, run_ref, sacc_ref, *, tq, tk, nk, hp):
    b = pl.program_id(0)
    qi = pl.program_id(2)
    stq, stk, win = SB_TQ, SB_TK, SB_WIN
    halves = tq // stq
    units = halves * (hp // 2)
    qchunk = jnp.right_shift(posq_ref[0], CHUNK_SHIFT)
    acc_ref[...] = jnp.zeros(acc_ref.shape, F32)
    lane = lax.broadcasted_iota(jnp.int32, (stq, LANES), 1)
    tri = (lax.broadcasted_iota(jnp.int32, (stk, stk), 0)
           >= lax.broadcasted_iota(jnp.int32, (stk, stk), 1)).astype(BF16)

    def mla_heads(kb, masked):
        k0 = pl.multiple_of(kb * tk, tk)
        if masked:
            kchunk = jnp.right_shift(posk_ref[0, :, pl.ds(k0, tk)], CHUNK_SHIFT)
            vis = kchunk <= qchunk
        for h in range(hp):
            hs = slice(h * SLOT, (h + 1) * SLOT)
            s = _dot_nt(q_ref[0, :, hs], k_ref[0, pl.ds(k0, tk), hs])
            if masked:
                s = jnp.where(vis, s, NEG)
            acc_ref[h] += _dot(jnp.exp2(s).astype(BF16), v_ref[0, pl.ds(k0, tk), hs])
            yield

    def sb_operands(u):
        rows = pl.ds(pl.multiple_of((u % halves) * stq, stq), stq)
        cols = pl.ds(pl.multiple_of((u // halves) * LANES, LANES), LANES)
        q_pair = sq_ref[0, rows, cols]
        zero = jnp.zeros_like(q_pair)
        q2 = jnp.concatenate([jnp.where(lane < SB_HEAD_DIM, q_pair, zero),
                              jnp.where(lane >= SB_HEAD_DIM, q_pair, zero)], axis=0)
        qpos = jnp.concatenate([posq_ref[0, rows], posq_ref[0, rows]], axis=0)
        return q2, qpos, rows, cols

    def suffix_sums(l1m):
        hi = lax.bitcast_convert_type(
            lax.bitcast_convert_type(l1m, jnp.int32) & jnp.int32(HI_MASK), F32)
        lo = l1m - hi
        both = _dot(jnp.concatenate([hi.astype(BF16), lo.astype(BF16)], axis=0), tri)
        return both[:2 * stq] + both[2 * stq:]

    def sb_fast(u, kstart):
        q2, qpos, _, cols = sb_operands(u)
        k0 = pl.multiple_of((kstart - (win - 1)) * stk, stk)
        causal = posk_ref[0, :, pl.ds(k0, win * stk)] < qpos
        z = _dot_nt(q2, sk_ref[0, pl.ds(k0, win * stk), cols])
        l1m = jnp.where(causal, -_softplus(z), 0.0)
        yield
        run = jnp.zeros((2 * stq, 1), F32)
        a_blocks = [None] * win
        for j in reversed(range(win)):
            cs = slice(j * stk, (j + 1) * stk)
            suffix = suffix_sums(l1m[:, cs])
            a = jnp.exp(z[:, cs] + suffix + run)
            a_blocks[j] = jnp.where(causal[:, cs], a, 0.0).astype(BF16)
            run = run + suffix[:, 0:1]
            if j % 2 == 0:
                yield
        sacc_ref[...] = _dot(jnp.concatenate(a_blocks, axis=1), sv_ref[0, pl.ds(k0, win * stk), cols])
        run_ref[...] = run
        yield

    def sb_finish(u, kstart, fast):
        q2, qpos, rows, cols = sb_operands(u)

        def block(kb):
            k0 = pl.multiple_of(kb * stk, stk)
            z = _dot_nt(q2, sk_ref[0, pl.ds(k0, stk), cols])
            causal = posk_ref[0, :, pl.ds(k0, stk)] < qpos
            l1m = jnp.where(causal, -_softplus(z), 0.0)
            suffix = suffix_sums(l1m)
            run = run_ref[...]
            a = jnp.where(causal, jnp.exp(z + suffix + run), 0.0)
            sacc_ref[...] += _dot(a.astype(BF16), sv_ref[0, pl.ds(k0, stk), cols])
            run_new = run + suffix[:, 0:1]
            run_ref[...] = run_new
            return jnp.max(run_new)

        def cond(c):
            kb, top = c
            return (kb >= 0) & (top >= SB_LOG_ZERO)

        def body(c):
            kb, _ = c
            return kb - 1, block(kb)

        kb_first = jnp.where(fast, kstart - win, kstart)
        lax.while_loop(cond, body, (kb_first, jnp.max(run_ref[...])))
        os_ref[0, rows, cols] = jnp.where(lane < SB_HEAD_DIM, sacc_ref[:stq], sacc_ref[stq:])

    def with_unit(kb, carry):
        flag = flags_ref[b, qi, kb]
        kstart = kstart_ref[b, qi * halves + kb % halves]
        fast = kstart >= win - 1
        run_ref[...] = jnp.zeros(run_ref.shape, F32)
        sacc_ref[...] = jnp.zeros(sacc_ref.shape, F32)

        @pl.when((flag == 1) & fast)
        def _():
            _round_robin(sb_fast(kb, kstart), mla_heads(kb, False))

        @pl.when((flag == 2) & fast)
        def _():
            _round_robin(sb_fast(kb, kstart), mla_heads(kb, True))

        @pl.when((flag == 0) & fast)
        def _():
            _round_robin(sb_fast(kb, kstart))

        @pl.when((flag == 1) & jnp.logical_not(fast))
        def _():
            _round_robin(mla_heads(kb, False))

        @pl.when((flag == 2) & jnp.logical_not(fast))
        def _():
            _round_robin(mla_heads(kb, True))

        sb_finish(kb, kstart, fast)
        return carry

    def mla_only(kb, carry):
        flag = flags_ref[b, qi, kb]

        @pl.when(flag == 1)
        def _():
            _round_robin(mla_heads(kb, False))

        @pl.when(flag == 2)
        def _():
            _round_robin(mla_heads(kb, True))

        return carry

    lax.fori_loop(0, units, with_unit, 0)
    lax.fori_loop(units, nk, mla_only, 0)
    lane_q = lax.broadcasted_iota(jnp.int32, (tq, LANES), 1)
    for j in range(hp // 2):
        o0 = acc_ref[2 * j] / pltpu.roll(acc_ref[2 * j], MLA_V, 1)
        o1 = acc_ref[2 * j + 1] / pltpu.roll(acc_ref[2 * j + 1], MLA_V, 1)
        om_ref[0, :, j * LANES:(j + 1) * LANES] = jnp.where(lane_q < MLA_V, o0, pltpu.roll(o1, MLA_V, 1))


def _attn_call(flags, kstart, q, k, v, sq, sk, sv, pos_col, pos_row, tq, tk, hp):
    B, S, _ = q.shape
    nq, nk = S // tq, S // tk
    assert (tq // SB_TQ) * (hp // 2) <= nk and SB_HEADS == MLA_HEADS
    sbw = (hp // 2) * LANES
    grid_spec = pltpu.PrefetchScalarGridSpec(
        num_scalar_prefetch=2,
        grid=(B, MLA_HEADS // hp, nq),
        in_specs=[
            pl.BlockSpec((1, tq, hp * SLOT), lambda b, p, i, f, s: (b, i, p)),
            pl.BlockSpec((1, S, hp * SLOT), lambda b, p, i, f, s: (b, 0, p)),
            pl.BlockSpec((1, S, hp * SLOT), lambda b, p, i, f, s: (b, 0, p)),
            pl.BlockSpec((1, tq, sbw), lambda b, p, i, f, s: (b, i, p)),
            pl.BlockSpec((1, S, sbw), lambda b, p, i, f, s: (b, 0, p)),
            pl.BlockSpec((1, S, sbw), lambda b, p, i, f, s: (b, 0, p)),
            pl.BlockSpec((1, tq, 1), lambda b, p, i, f, s: (b, i, 0)),
            pl.BlockSpec((1, 1, S), lambda b, p, i, f, s: (b, 0, 0)),
        ],
        out_specs=[pl.BlockSpec((1, tq, hp * MLA_V), lambda b, p, i, f, s: (b, i, p)),
                   pl.BlockSpec((1, tq, sbw), lambda b, p, i, f, s: (b, i, p))],
        scratch_shapes=[pltpu.VMEM((hp, tq, LANES), F32), pltpu.VMEM((2 * SB_TQ, 1), F32),
                        pltpu.VMEM((2 * SB_TQ, LANES), F32)],
    )
    return pl.pallas_call(
        functools.partial(_attn_kernel, tq=tq, tk=tk, nk=nk, hp=hp),
        grid_spec=grid_spec,
        out_shape=[jax.ShapeDtypeStruct((B, S, MLA_WIDTH), F32),
                   jax.ShapeDtypeStruct((B, S, SB_WIDTH), F32)],
        compiler_params=pltpu.CompilerParams(
            dimension_semantics=("arbitrary", "arbitrary", "arbitrary"),
            vmem_limit_bytes=VMEM_LIMIT_BYTES),
        name="attn",
    )(flags, kstart, q, k, v, sq, sk, sv, pos_col, pos_row)


def _split_bf16(a):
    hi = a.astype(BF16)
    lo = (a - hi.astype(F32)).astype(BF16)
    return hi, lo


def _merge_kernel(x_ref, om_ref, os_ref, gm_ref, gs_ref, wo_ref, g2_ref, wrh_ref, wrl_ref,
                  bias_ref, wsgu_ref, wsd_ref, base_ref, hn_ref, eidx_ref, gate_ref, rank_ref,
                  cnt_ref):
    mm = _rms(om_ref[...], gm_ref[...]).astype(BF16)
    ms = _rms(os_ref[...], gs_ref[...]).astype(BF16)
    h = x_ref[...] + _dot(mm, wo_ref[:MLA_WIDTH, :]) + _dot(ms, wo_ref[MLA_WIDTH:, :])
    hn = _rms(h, g2_ref[...])
    hn_hi, hn_lo = _split_bf16(hn)
    bits = lax.bitcast_convert_type(hn_hi.astype(F32), jnp.int32)
    hn_ref[...] = (lax.shift_right_logical(bits[:, :D_MODEL // 2], 16)
                   | (bits[:, D_MODEL // 2:] & jnp.int32(HI_MASK)))

    logits = (_dot_nt(wrh_ref[...], hn_hi) + _dot_nt(wrh_ref[...], hn_lo)
              + _dot_nt(wrl_ref[...], hn_hi))
    scores = jax.nn.sigmoid(logits)
    work = scores + bias_ref[...]
    eidx = lax.broadcasted_iota(jnp.int32, work.shape, 0)
    chosen, chosen_score, hits = [], [], []
    for _ in range(TOP_K):
        top = jnp.max(work, axis=0, keepdims=True)
        first = jnp.min(jnp.where(work == top, eidx, N_EXPERTS), axis=0, keepdims=True)
        hit = eidx == first
        hits.append(hit)
        chosen.append(first)
        chosen_score.append(jnp.sum(jnp.where(hit, scores, 0.0), axis=0, keepdims=True))
        work = jnp.where(hit, -jnp.inf, work)
    sel = jnp.concatenate(chosen_score, axis=0)
    eidx_ref[...] = jnp.concatenate(chosen, axis=0)
    gate_ref[...] = sel / jnp.sum(sel, axis=0, keepdims=True) * ROUTED_SCALE

    tm = work.shape[1]
    member = functools.reduce(jnp.logical_or, hits)
    member = jnp.where(member, 1.0, 0.0)
    tri = (lax.broadcasted_iota(jnp.int32, (tm, tm), 0)
           <= lax.broadcasted_iota(jnp.int32, (tm, tm), 1)).astype(BF16)
    upto = _dot(member.astype(BF16), tri)
    before = upto - member
    rank_ref[...] = jnp.concatenate(
        [jnp.sum(jnp.where(hit, before, 0.0), axis=0, keepdims=True) for hit in hits],
        axis=0).astype(jnp.int32)
    cnt_ref[0] = upto[:, tm - 1:tm].astype(jnp.int32)

    sgu = _dot(hn_hi, wsgu_ref[...])
    act = (jax.nn.silu(sgu[:, :SHARED_FF]) * sgu[:, SHARED_FF:]).astype(BF16)
    base_ref[...] = h + _dot(act, wsd_ref[...])


def _merge_call(x2, o_mla, o_sb, gm, gs, wo, g2, wrh, wrl, bias, wsgu, wsd, tm):
    T = x2.shape[0]
    full = lambda a: pl.BlockSpec(a.shape, lambda i: (0,) * a.ndim)
    row = lambda w: pl.BlockSpec((tm, w), lambda i: (i, 0))
    weights = (gm, gs, wo, g2, wrh, wrl, bias, wsgu, wsd)
    topk = pl.BlockSpec((TOP_K, tm), lambda i: (0, i))
    return pl.pallas_call(
        _merge_kernel,
        grid=(T // tm,),
        in_specs=[row(D_MODEL), row(MLA_WIDTH), row(SB_WIDTH)] + [full(w) for w in weights],
        out_specs=[row(D_MODEL), row(D_MODEL // 2), topk, topk, topk,
                   pl.BlockSpec((1, N_EXPERTS, 1), lambda i: (i, 0, 0))],
        out_shape=[jax.ShapeDtypeStruct((T, D_MODEL), F32),
                   jax.ShapeDtypeStruct((T, D_MODEL // 2), jnp.int32),
                   jax.ShapeDtypeStruct((TOP_K, T), jnp.int32),
                   jax.ShapeDtypeStruct((TOP_K, T), F32),
                   jax.ShapeDtypeStruct((TOP_K, T), jnp.int32),
                   jax.ShapeDtypeStruct((T // tm, N_EXPERTS, 1), jnp.int32)],
        compiler_params=pltpu.CompilerParams(dimension_semantics=("arbitrary",),
                                             vmem_limit_bytes=VMEM_LIMIT_BYTES),
        name="merge",
    )(x2, o_mla, o_sb, *weights)


MOE_TB = 2048
MOE_RT = 64
MOE_MAX_TILES = 6
MOE_ROWS = MOE_TB * TOP_K + N_EXPERTS * (MOE_RT - 1)
MOE_PACK = D_MODEL // 2
MOE_SUB = MOE_PACK // LANES
MOE_UNROLL = 8
MOE_XCH = 256
MOE_OCH = 128
MOE_WBUF = 3
MOE_CLEAR_ROWS = 1024


def _unpack_lo(w):
    return lax.bitcast_convert_type(w << 16, F32)


def _unpack_hi(w):
    return lax.bitcast_convert_type(w & jnp.int32(HI_MASK), F32)


def _moe_kernel(off_ref, nt_ref, pos_ref, gate_ref, xp_hbm, base_hbm, wgu_hbm, wd_hbm, o_hbm,
                xy_ref, xbuf, bbuf, obuf, wgu_buf, wd_buf, xsem, bsem, osem, wsem):
    i = pl.program_id(0)
    e = pl.program_id(1)

    step_id = i * N_EXPERTS + e
    n_steps = pl.num_programs(0) * N_EXPERTS

    def w_copies(step):
        slot = step % MOE_WBUF
        ex = step % N_EXPERTS
        return (pltpu.make_async_copy(wgu_hbm.at[ex], wgu_buf.at[slot], wsem.at[0, slot]),
                pltpu.make_async_copy(wd_hbm.at[ex], wd_buf.at[slot], wsem.at[1, slot]))

    @pl.when(step_id == 0)
    def _():
        for ahead in range(MOE_WBUF - 1):
            for cp in w_copies(ahead):
                cp.start()

    @pl.when(step_id + (MOE_WBUF - 1) < n_steps)
    def _():
        for cp in w_copies(step_id + (MOE_WBUF - 1)):
            cp.start()

    @pl.when((i == 0) & (e == 0))
    def _():
        def clear(r, c):
            xy_ref[pl.ds(pl.multiple_of(r * MOE_CLEAR_ROWS, MOE_CLEAR_ROWS), MOE_CLEAR_ROWS), :] = (
                jnp.zeros((MOE_CLEAR_ROWS, LANES), jnp.int32))
            return c

        lax.fori_loop(0, MOE_ROWS * MOE_SUB // MOE_CLEAR_ROWS, clear, 0)
        tail = MOE_ROWS * MOE_SUB % MOE_CLEAR_ROWS
        if tail:
            xy_ref[pl.ds(MOE_ROWS * MOE_SUB - tail, tail), :] = jnp.zeros((tail, LANES), jnp.int32)

    def x_copy(c, slot):
        return pltpu.make_async_copy(
            xp_hbm.at[pl.ds(i * MOE_TB + c * MOE_XCH, MOE_XCH), :], xbuf.at[slot], xsem.at[slot])

    def base_copy(c, slot):
        return pltpu.make_async_copy(
            base_hbm.at[pl.ds(i * MOE_TB + c * MOE_OCH, MOE_OCH), :], bbuf.at[slot], bsem.at[slot])

    def out_copy(c, slot):
        return pltpu.make_async_copy(
            obuf.at[slot], o_hbm.at[pl.ds(i * MOE_TB + c * MOE_OCH, MOE_OCH), :], osem.at[slot])

    @pl.when(e == 0)
    def _():
        n_chunks = MOE_TB // MOE_XCH
        x_copy(0, 0).start()
        for c in range(n_chunks):
            slot = c % 2
            if c + 1 < n_chunks:
                x_copy(c + 1, 1 - slot).start()
            x_copy(c, slot).wait()

            def step(tt, carry, c=c, slot=slot):
                slab = xbuf[slot, pl.ds(pl.multiple_of(tt * MOE_UNROLL, MOE_UNROLL), MOE_UNROLL), :]
                for u in range(MOE_UNROLL):
                    t = tt * MOE_UNROLL + u
                    row = jnp.concatenate([slab[u:u + 1, j * LANES:(j + 1) * LANES]
                                           for j in range(MOE_SUB)], axis=0)
                    for k in range(TOP_K):
                        p = pl.multiple_of(pos_ref[TOP_K * (c * MOE_XCH + t) + k], MOE_SUB)
                        xy_ref[pl.ds(p, MOE_SUB), :] = row
                return carry

            lax.fori_loop(0, MOE_XCH // MOE_UNROLL, step, 0)

    for cp in w_copies(step_id):
        cp.wait()
    wslot = step_id % MOE_WBUF

    def ffn(r0, m):
        base_row = pl.multiple_of(MOE_SUB * r0, 8)
        words = jnp.concatenate(
            [xy_ref[pl.ds(base_row + j, m, stride=MOE_SUB), :] for j in range(MOE_SUB)], axis=1)
        gu = (_dot(_unpack_lo(words).astype(BF16), wgu_buf[wslot, :MOE_PACK, :])
              + _dot(_unpack_hi(words).astype(BF16), wgu_buf[wslot, MOE_PACK:, :]))
        act = (jax.nn.silu(gu[:, :EXPERT_FF]) * gu[:, EXPERT_FF:]).astype(BF16)
        y = _dot(act, wd_buf[wslot])
        ya = lax.bitcast_convert_type(y[:, :MOE_PACK].astype(BF16).astype(F32), jnp.int32)
        yb = lax.bitcast_convert_type(y[:, MOE_PACK:].astype(BF16).astype(F32), jnp.int32)
        packed = lax.shift_right_logical(ya, 16) | yb
        for j in range(MOE_SUB):
            xy_ref[pl.ds(base_row + j, m, stride=MOE_SUB), :] = packed[:, j * LANES:(j + 1) * LANES]

    off = off_ref[i * N_EXPERTS + e]
    nt = nt_ref[i * N_EXPERTS + e]

    def chunk(c, carry):
        ffn(off + c * (MOE_MAX_TILES * MOE_RT), MOE_MAX_TILES * MOE_RT)
        return carry

    full = nt // MOE_MAX_TILES
    lax.fori_loop(0, full, chunk, 0)
    rest = nt - full * MOE_MAX_TILES
    for tiles in range(1, MOE_MAX_TILES):
        @pl.when(rest == tiles)
        def _():
            ffn(off + full * (MOE_MAX_TILES * MOE_RT), tiles * MOE_RT)

    @pl.when(e == N_EXPERTS - 1)
    def _():
        n_chunks = MOE_TB // MOE_OCH
        base_copy(0, 0).start()
        for c in range(n_chunks):
            slot = c % 2
            if c + 1 < n_chunks:
                base_copy(c + 1, 1 - slot).start()
            base_copy(c, slot).wait()
            if c >= 2:
                out_copy(c - 2, slot).wait()

            def step(tt, carry, c=c, slot=slot):
                for u in range(MOE_UNROLL):
                    t = tt * MOE_UNROLL + u
                    lo = jnp.zeros((MOE_SUB, LANES), F32)
                    hi = jnp.zeros((MOE_SUB, LANES), F32)
                    for k in range(TOP_K):
                        s = TOP_K * (c * MOE_OCH + t) + k
                        w = xy_ref[pl.ds(pl.multiple_of(pos_ref[s], MOE_SUB), MOE_SUB), :]
                        g = gate_ref[s]
                        lo = lo + g * _unpack_lo(w)
                        hi = hi + g * _unpack_hi(w)
                    routed = jnp.concatenate([lo[j:j + 1] for j in range(MOE_SUB)]
                                             + [hi[j:j + 1] for j in range(MOE_SUB)], axis=1)
                    obuf[slot, pl.ds(t, 1), :] = bbuf[slot, pl.ds(t, 1), :] + routed
                return carry

            lax.fori_loop(0, MOE_OCH // MOE_UNROLL, step, 0)
            out_copy(c, slot).start()
        out_copy(n_chunks - 2, n_chunks % 2).wait()
        out_copy(n_chunks - 1, (n_chunks - 1) % 2).wait()


def _moe_call(off, nt, pos, gates, xp, base, wgu, wd):
    T = base.shape[0]
    slots = MOE_TB * TOP_K
    grid_spec = pltpu.PrefetchScalarGridSpec(
        num_scalar_prefetch=2,
        grid=(T // MOE_TB, N_EXPERTS),
        in_specs=[
            pl.BlockSpec((slots,), lambda i, e, o, n: (i,), memory_space=pltpu.SMEM),
            pl.BlockSpec((slots,), lambda i, e, o, n: (i,), memory_space=pltpu.SMEM),
            pl.BlockSpec(memory_space=pl.ANY),
            pl.BlockSpec(memory_space=pl.ANY),
            pl.BlockSpec(memory_space=pl.ANY),
            pl.BlockSpec(memory_space=pl.ANY),
        ],
        out_specs=pl.BlockSpec(memory_space=pl.ANY),
        scratch_shapes=[
            pltpu.VMEM((MOE_ROWS * MOE_SUB, LANES), jnp.int32),
            pltpu.VMEM((2, MOE_XCH, MOE_PACK), jnp.int32),
            pltpu.VMEM((2, MOE_OCH, D_MODEL), F32),
            pltpu.VMEM((2, MOE_OCH, D_MODEL), F32),
            pltpu.VMEM((MOE_WBUF, D_MODEL, 2 * EXPERT_FF), BF16),
            pltpu.VMEM((MOE_WBUF, EXPERT_FF, D_MODEL), BF16),
            pltpu.SemaphoreType.DMA((2,)),
            pltpu.SemaphoreType.DMA((2,)),
            pltpu.SemaphoreType.DMA((2,)),
            pltpu.SemaphoreType.DMA((2, MOE_WBUF)),
        ],
    )
    return pl.pallas_call(
        _moe_kernel,
        grid_spec=grid_spec,
        out_shape=jax.ShapeDtypeStruct((T, D_MODEL), F32),
        compiler_params=pltpu.CompilerParams(dimension_semantics=("arbitrary", "arbitrary"),
                                             vmem_limit_bytes=VMEM_LIMIT_BYTES),
        name="moe",
    )(off, nt, pos, gates, xp, base, wgu, wd)


def _route_tables(eidx, gate, rank, cnt):
    T = eidx.shape[1]
    nblk = T // MOE_TB
    cnt = cnt.reshape(nblk, -1, N_EXPERTS)
    per_blk = cnt.shape[1]
    earlier = jnp.cumsum(cnt, axis=1) - cnt
    tiles = (cnt.sum(axis=1) + MOE_RT - 1) // MOE_RT
    off = jnp.cumsum(tiles, axis=1) * MOE_RT - tiles * MOE_RT
    start = (off[:, None, :] + earlier).reshape(nblk * per_blk, N_EXPERTS)
    e_tk = eidx.T.reshape(nblk * per_blk, T // (nblk * per_blk), TOP_K)
    chosen = e_tk[..., None] == jnp.arange(N_EXPERTS, dtype=jnp.int32)
    pos = jnp.where(chosen, start[:, None, None, :], 0).sum(axis=-1).reshape(T, TOP_K) + rank.T
    return (off.reshape(-1).astype(jnp.int32), tiles.reshape(-1).astype(jnp.int32),
            (pos * MOE_SUB).reshape(-1).astype(jnp.int32), gate.T.reshape(-1))


def _slot_cols(w, width):
    k = w.shape[0]
    w = w.reshape(k, MLA_HEADS, width)
    return jnp.pad(w, ((0, 0), (0, 0), (0, SLOT - width))).reshape(k, MLA_HEADS * SLOT)


def _block_flags(pos, tq, tk, chunked):
    B, S = pos.shape
    p = jnp.right_shift(pos, CHUNK_SHIFT) if chunked else pos
    qmin = p.reshape(B, S // tq, tq).min(-1)[:, :, None]
    qmax = p.reshape(B, S // tq, tq).max(-1)[:, :, None]
    kmin = p.reshape(B, S // tk, tk).min(-1)[:, None, :]
    kmax = p.reshape(B, S // tk, tk).max(-1)[:, None, :]
    if chunked:
        none, all_ = kmin > qmax, kmax <= qmin
    else:
        none, all_ = kmin >= qmax, kmax < qmin
    return jnp.where(none, 0, jnp.where(all_, 1, 2)).astype(jnp.int32)


def kernel(x, positions, norm1, w_in, q_a_norm, w_uq, kv_a_norm, w_ukv, q_norm, k_norm,
           out_norm_mla, out_norm_sb, w_o, norm2, w_router, router_bias, w_gate_up, w_down,
           w_shared_gate_up, w_shared_down):
    B, S, D = x.shape
    T = B * S
    c0 = Q_LORA
    c1 = c0 + KV_LORA
    c2 = c1 + MLA_ROPE

    row2 = lambda g: g.reshape(1, -1).astype(F32)
    wcq = w_in[:, :c0].astype(BF16)
    wckv = w_in[:, c0:c1].astype(BF16)
    wkr = jnp.pad(w_in[:, c1:c2], ((0, 0), (MLA_NOPE, SLOT - MLA_QK))).astype(BF16)
    wsb = w_in[:, c2:].astype(BF16)
    wuq = _slot_cols(w_uq, MLA_QK).astype(BF16)
    w_ukv3 = w_ukv.reshape(KV_LORA, MLA_HEADS, MLA_NOPE + MLA_V)
    wuk = _slot_cols(w_ukv3[:, :, :MLA_NOPE].reshape(KV_LORA, -1), MLA_NOPE).astype(BF16)
    wuv = _slot_cols(w_ukv3[:, :, MLA_NOPE:].reshape(KV_LORA, MLA_WIDTH), MLA_V).astype(BF16)
    slot_lane = jnp.arange(SLOT)
    vone = jnp.tile((slot_lane >= MLA_V).astype(F32), MLA_HEADS).reshape(1, MLA_HEADS * SLOT)
    gq = jnp.pad(q_norm, (0, SLOT - MLA_QK)).reshape(1, SLOT)
    gk = jnp.pad(k_norm, (0, SLOT - MLA_QK)).reshape(1, SLOT)
    inv = ROPE_THETA ** (-jnp.arange(HALF_ROPE, dtype=F32) / HALF_ROPE)
    inv_slot = jnp.pad(jnp.concatenate([inv, inv]), (MLA_NOPE, SLOT - MLA_QK)).reshape(1, SLOT)

    bound = (math.sqrt(MLA_QK) * LOG2E * BOUND_MARGIN) * jnp.max(jnp.abs(q_norm)) * jnp.max(jnp.abs(k_norm))
    bounded = bound <= MLA_MAX_SHIFT
    shift = jnp.where(bounded, bound, 0.0)
    qaug = (slot_lane == MLA_QK).astype(F32).reshape(1, SLOT)
    kaug = -shift * qaug

    x2 = x.reshape(T, D)
    pos_col = positions.reshape(T, 1)
    q, k, v, sq, sk, sv = _proj_call(
        x2, pos_col,
        (row2(norm1), wcq, wckv, wkr, wsb, row2(q_a_norm), wuq, row2(kv_a_norm), wuk, wuv, vone,
         gq, gk, inv_slot, qaug, kaug), tm=PROJ_TM)

    pos_c3 = positions.reshape(B, S, 1)
    pos_r3 = positions.reshape(B, 1, S)
    r3 = lambda a: a.reshape(B, S, a.shape[-1])

    mla_flags = _block_flags(positions, MLA_TQ, MLA_TK, True)
    sb_flags = _block_flags(positions, SB_TQ, SB_TK, False)
    kstart = jnp.max(jnp.where(sb_flags != 0, jnp.arange(S // SB_TK, dtype=jnp.int32), -1),
                     axis=-1).astype(jnp.int32)
    attn_args = (mla_flags, kstart, r3(q), r3(k), r3(v), r3(sq), r3(sk), r3(sv), pos_c3, pos_r3)

    def separate(mla_flags, kstart, q, k, v, sq, sk, sv, pos_c3, pos_r3):
        o_mla = _mla_call(mla_flags, q, k, v, pos_c3, pos_r3, MLA_TQ, MLA_TK, hp=2, online=True)
        o_sb = _sb_call(kstart, sq, sk, sv, pos_c3, pos_r3, SB_TQ, SB_TK, win=SB_WIN)
        return o_mla, o_sb

    o_mla, o_sb = lax.cond(
        bounded,
        lambda *a: tuple(_attn_call(*a, MLA_TQ, MLA_TK, hp=MLA_HEADS_PER_STEP)),
        separate, *attn_args)

    wr_t = w_router.T
    wrh = wr_t.astype(BF16)
    wrl = (wr_t - wrh.astype(F32)).astype(BF16)
    base, hn, eidx, gate, rank, cnt = _merge_call(
        x2, o_mla.reshape(T, MLA_WIDTH), o_sb.reshape(T, SB_WIDTH), row2(out_norm_mla),
        row2(out_norm_sb), w_o.astype(BF16), row2(norm2), wrh, wrl,
        router_bias.reshape(N_EXPERTS, 1).astype(F32), w_shared_gate_up.astype(BF16),
        w_shared_down.astype(BF16), tm=MERGE_TM)

    off, tiles, pos, gates = _route_tables(eidx, gate, rank, cnt)
    out = _moe_call(off, tiles, pos, gates, hn, base, w_gate_up.astype(BF16),
                    w_down.astype(BF16))
    return out.reshape(B, S, D)
```

```python
import functools
import math

import jax
import jax.numpy as jnp
from jax import lax
from jax.experimental import pallas as pl
from jax.experimental.pallas import tpu as pltpu

D_MODEL = 1024
CHUNK = 64
MLA_HEADS = 8
MLA_NOPE = 64
MLA_ROPE = 32
MLA_QK = MLA_NOPE + MLA_ROPE
MLA_V = 64
Q_LORA = 256
KV_LORA = 128
ROPE_THETA = 10000.0
SB_HEADS = 8
SB_HEAD_DIM = 64
SB_WIDTH = SB_HEADS * SB_HEAD_DIM
MLA_WIDTH = MLA_HEADS * MLA_V
N_EXPERTS = 64
TOP_K = 8
EXPERT_FF = 256
SHARED_FF = 256
ROUTED_SCALE = 2.5
EPS = 1e-6
NEG = -1e30

LANES = 128
VMEM_LIMIT_BYTES = 56 * 1024 * 1024

SLOT = LANES
HALF_ROPE = MLA_ROPE // 2

SB_LOG_ZERO = -90.0

LOG2E = 1.4426950408889634
BOUND_MARGIN = 1.02
MLA_MAX_SHIFT = 50.0

BF16 = jnp.bfloat16
F32 = jnp.float32
HI_MASK = -65536
CHUNK_SHIFT = CHUNK.bit_length() - 1
assert 1 << CHUNK_SHIFT == CHUNK

PROJ_TM = 512
MERGE_TM = 1024
MLA_TQ, MLA_TK = 512, 512
MLA_HEADS_PER_STEP = 4
SB_TQ, SB_TK, SB_WIN = 256, 128, 4


def _rms(x, g):
    return x * lax.rsqrt(jnp.mean(x * x, axis=-1, keepdims=True) + EPS) * g


def _dot(a, b):
    return jnp.dot(a, b, preferred_element_type=F32)


def _dot_nt(a, b):
    return lax.dot_general(a, b, (((1,), (1,)), ((), ())), preferred_element_type=F32)


def _proj_kernel(x_ref, pos_ref, g1_ref, wcq_ref, wckv_ref, wkr_ref, wsb_ref, gqa_ref, wuq_ref,
                 gkva_ref, wuk_ref, wuv_ref, vone_ref, gq_ref, gk_ref, inv_ref, qaug_ref, kaug_ref,
                 q_ref, k_ref, v_ref, sq_ref, sk_ref, sv_ref):
    x = x_ref[...]
    xn = _rms(x, g1_ref[...]).astype(BF16)

    cq = _rms(_dot(xn, wcq_ref[...]), gqa_ref[...]).astype(BF16)
    q = _dot(cq, wuq_ref[...])
    ckv = _rms(_dot(xn, wckv_ref[...]), gkva_ref[...]).astype(BF16)
    kn = _dot(ckv, wuk_ref[...])
    v_ref[...] = (_dot(ckv, wuv_ref[...]) + vone_ref[...]).astype(BF16)
    kr = _dot(xn, wkr_ref[...])

    sb = _dot(xn, wsb_ref[...])
    sq_ref[...] = (sb[:, :SB_WIDTH] * (1.0 / math.sqrt(SB_HEAD_DIM))).astype(BF16)
    sk_ref[...] = sb[:, SB_WIDTH:2 * SB_WIDTH].astype(BF16)
    sv_ref[...] = sb[:, 2 * SB_WIDTH:].astype(BF16)

    ang = pos_ref[...].astype(F32) * inv_ref[...]
    cos = jnp.cos(ang)
    sin = jnp.sin(ang)
    lane = lax.broadcasted_iota(jnp.int32, ang.shape, 1)
    first_half = lane < MLA_NOPE + HALF_ROPE

    def rope(t):
        up = pltpu.roll(t, SLOT - HALF_ROPE, 1)
        down = pltpu.roll(t, HALF_ROPE, 1)
        return t * cos + jnp.where(first_half, -up, down) * sin

    def head_norm(t, g):
        ss = jnp.sum(t * t, axis=-1, keepdims=True) * (1.0 / MLA_QK)
        return t * lax.rsqrt(ss + EPS) * g

    scale = LOG2E / math.sqrt(MLA_QK)
    for h in range(MLA_HEADS):
        sl = slice(h * SLOT, (h + 1) * SLOT)
        qh = rope(head_norm(q[:, sl], gq_ref[...])) * scale + qaug_ref[...]
        q_ref[:, sl] = qh.astype(BF16)
        kh = rope(head_norm(kn[:, sl] + kr, gk_ref[...])) + kaug_ref[...]
        k_ref[:, sl] = kh.astype(BF16)


def _proj_call(x2, pos_col, weights, tm):
    T = x2.shape[0]
    full = lambda a: pl.BlockSpec(a.shape, lambda i: (0,) * a.ndim)
    row = lambda w: pl.BlockSpec((tm, w), lambda i: (i, 0))
    out_w = (MLA_HEADS * SLOT, MLA_HEADS * SLOT, MLA_HEADS * SLOT, SB_WIDTH, SB_WIDTH, SB_WIDTH)
    return pl.pallas_call(
        _proj_kernel,
        grid=(T // tm,),
        in_specs=[row(D_MODEL), row(1)] + [full(w) for w in weights],
        out_specs=[row(w) for w in out_w],
        out_shape=[jax.ShapeDtypeStruct((T, w), BF16) for w in out_w],
        compiler_params=pltpu.CompilerParams(dimension_semantics=("arbitrary",),
                                             vmem_limit_bytes=VMEM_LIMIT_BYTES),
        name="proj",
    )(x2, pos_col, *weights)


def _mla_kernel(flags_ref, q_ref, k_ref, v_ref, posq_ref, posk_ref, o_ref, acc_ref, m_ref,
                *, tq, tk, nk, hp, online):
    b = pl.program_id(0)
    qi = pl.program_id(2)
    qchunk = jnp.right_shift(posq_ref[0], CHUNK_SHIFT)
    acc_ref[...] = jnp.zeros(acc_ref.shape, F32)
    if online:
        m_ref[...] = jnp.full(m_ref.shape, NEG, F32)

    def block(kb, masked):
        k0 = pl.multiple_of(kb * tk, tk)
        if masked:
            kchunk = jnp.right_shift(posk_ref[0, :, pl.ds(k0, tk)], CHUNK_SHIFT)
            vis = kchunk <= qchunk
        for h in range(hp):
            hs = slice(h * SLOT, (h + 1) * SLOT)
            s = _dot_nt(q_ref[0, :, hs], k_ref[0, pl.ds(k0, tk), hs])
            if masked:
                s = jnp.where(vis, s, NEG)
            vh = v_ref[0, pl.ds(k0, tk), hs]
            if online:
                m_old = m_ref[h]
                m_new = jnp.maximum(m_old, jnp.max(s, axis=-1, keepdims=True))
                p = jnp.exp2(s - m_new)
                acc_ref[h] = jnp.exp2(m_old - m_new) * acc_ref[h] + _dot(p.astype(BF16), vh)
                m_ref[h] = m_new
            else:
                acc_ref[h] += _dot(jnp.exp2(s).astype(BF16), vh)

    def body(kb, carry):
        flag = flags_ref[b, qi, kb]

        @pl.when(flag == 1)
        def _():
            block(kb, False)

        @pl.when(flag == 2)
        def _():
            block(kb, True)

        return carry

    lax.fori_loop(0, nk, body, 0)
    lane = lax.broadcasted_iota(jnp.int32, (tq, LANES), 1)
    for j in range(hp // 2):
        o0 = acc_ref[2 * j] / pltpu.roll(acc_ref[2 * j], MLA_V, 1)
        o1 = acc_ref[2 * j + 1] / pltpu.roll(acc_ref[2 * j + 1], MLA_V, 1)
        o_ref[0, :, j * LANES:(j + 1) * LANES] = jnp.where(lane < MLA_V, o0, pltpu.roll(o1, MLA_V, 1))


def _mla_call(flags, q, k, v, pos_col, pos_row, tq, tk, hp, online):
    B, S, _ = q.shape
    nq, nk = S // tq, S // tk
    grid_spec = pltpu.PrefetchScalarGridSpec(
        num_scalar_prefetch=1,
        grid=(B, MLA_HEADS // hp, nq),
        in_specs=[
            pl.BlockSpec((1, tq, hp * SLOT), lambda b, p, i, f: (b, i, p)),
            pl.BlockSpec((1, S, hp * SLOT), lambda b, p, i, f: (b, 0, p)),
            pl.BlockSpec((1, S, hp * SLOT), lambda b, p, i, f: (b, 0, p)),
            pl.BlockSpec((1, tq, 1), lambda b, p, i, f: (b, i, 0)),
            pl.BlockSpec((1, 1, S), lambda b, p, i, f: (b, 0, 0)),
        ],
        out_specs=pl.BlockSpec((1, tq, hp * MLA_V), lambda b, p, i, f: (b, i, p)),
        scratch_shapes=[pltpu.VMEM((hp, tq, LANES), F32), pltpu.VMEM((hp, tq, 1), F32)],
    )
    return pl.pallas_call(
        functools.partial(_mla_kernel, tq=tq, tk=tk, nk=nk, hp=hp, online=online),
        grid_spec=grid_spec,
        out_shape=jax.ShapeDtypeStruct((B, S, MLA_WIDTH), F32),
        compiler_params=pltpu.CompilerParams(
            dimension_semantics=("arbitrary", "arbitrary", "arbitrary"),
            vmem_limit_bytes=VMEM_LIMIT_BYTES),
        name="mla_attn_online" if online else "mla_attn",
    )(flags, q, k, v, pos_col, pos_row)


def _softplus(z):
    return jnp.maximum(z, 0.0) + jnp.log(1.0 + jnp.exp(-jnp.abs(z)))


def _sb_kernel(kstart_ref, q_ref, k_ref, v_ref, posq_ref, posk_ref, o_ref, run_ref, acc_ref,
               *, tq, tk, win):
    b = pl.program_id(0)
    qi = pl.program_id(2)
    lane = lax.broadcasted_iota(jnp.int32, (tq, LANES), 1)
    row_i = lax.broadcasted_iota(jnp.int32, (tk, tk), 0)
    col_i = lax.broadcasted_iota(jnp.int32, (tk, tk), 1)
    tri = (row_i >= col_i).astype(BF16)
    kstart = kstart_ref[b, qi]
    q_pair = q_ref[0]
    zero = jnp.zeros_like(q_pair)
    q2 = jnp.concatenate([jnp.where(lane < SB_HEAD_DIM, q_pair, zero),
                          jnp.where(lane >= SB_HEAD_DIM, q_pair, zero)], axis=0)
    qpos = jnp.concatenate([posq_ref[0], posq_ref[0]], axis=0)

    def suffix_sums(l1m):
        hi = lax.bitcast_convert_type(
            lax.bitcast_convert_type(l1m, jnp.int32) & jnp.int32(HI_MASK), F32)
        lo = l1m - hi
        both = _dot(jnp.concatenate([hi.astype(BF16), lo.astype(BF16)], axis=0), tri)
        return both[:2 * tq] + both[2 * tq:]

    run_ref[...] = jnp.zeros(run_ref.shape, F32)
    acc_ref[...] = jnp.zeros(acc_ref.shape, F32)

    @pl.when(kstart >= win - 1)
    def _():
        k0 = pl.multiple_of((kstart - (win - 1)) * tk, tk)
        causal = posk_ref[0, :, pl.ds(k0, win * tk)] < qpos
        z = _dot_nt(q2, k_ref[0, pl.ds(k0, win * tk), :])
        l1m = jnp.where(causal, -_softplus(z), 0.0)
        run = jnp.zeros((2 * tq, 1), F32)
        a_blocks = [None] * win
        for j in reversed(range(win)):
            cs = slice(j * tk, (j + 1) * tk)
            suffix = suffix_sums(l1m[:, cs])
            a = jnp.exp(z[:, cs] + suffix + run)
            a_blocks[j] = jnp.where(causal[:, cs], a, 0.0).astype(BF16)
            run = run + suffix[:, 0:1]
        acc_ref[...] = _dot(jnp.concatenate(a_blocks, axis=1), v_ref[0, pl.ds(k0, win * tk), :])
        run_ref[...] = run

    def block(kb):
        k0 = pl.multiple_of(kb * tk, tk)
        z = _dot_nt(q2, k_ref[0, pl.ds(k0, tk), :])
        causal = posk_ref[0, :, pl.ds(k0, tk)] < qpos
        l1m = jnp.where(causal, -_softplus(z), 0.0)
        suffix = suffix_sums(l1m)
        run = run_ref[...]
        a = jnp.where(causal, jnp.exp(z + suffix + run), 0.0)
        acc_ref[...] += _dot(a.astype(BF16), v_ref[0, pl.ds(k0, tk), :])
        run_new = run + suffix[:, 0:1]
        run_ref[...] = run_new
        return jnp.max(run_new)

    def cond(c):
        kb, top = c
        return (kb >= 0) & (top >= SB_LOG_ZERO)

    def body(c):
        kb, _ = c
        return kb - 1, block(kb)

    kb_first = jnp.where(kstart >= win - 1, kstart - win, kstart)
    lax.while_loop(cond, body, (kb_first, jnp.max(run_ref[...])))

    o_ref[0] = jnp.where(lane < SB_HEAD_DIM, acc_ref[:tq], acc_ref[tq:])


def _sb_call(kstart, q, k, v, pos_col, pos_row, tq, tk, win):
    B, S, _ = q.shape
    nq = S // tq
    grid_spec = pltpu.PrefetchScalarGridSpec(
        num_scalar_prefetch=1,
        grid=(B, SB_HEADS // 2, nq),
        in_specs=[
            pl.BlockSpec((1, tq, LANES), lambda b, p, i, s: (b, i, p)),
            pl.BlockSpec((1, S, LANES), lambda b, p, i, s: (b, 0, p)),
            pl.BlockSpec((1, S, LANES), lambda b, p, i, s: (b, 0, p)),
            pl.BlockSpec((1, tq, 1), lambda b, p, i, s: (b, i, 0)),
            pl.BlockSpec((1, 1, S), lambda b, p, i, s: (b, 0, 0)),
        ],
        out_specs=pl.BlockSpec((1, tq, LANES), lambda b, p, i, s: (b, i, p)),
        scratch_shapes=[pltpu.VMEM((2 * tq, 1), F32), pltpu.VMEM((2 * tq, LANES), F32)],
    )
    return pl.pallas_call(
        functools.partial(_sb_kernel, tq=tq, tk=tk, win=win),
        grid_spec=grid_spec,
        out_shape=jax.ShapeDtypeStruct((B, S, SB_WIDTH), F32),
        compiler_params=pltpu.CompilerParams(
            dimension_semantics=("arbitrary", "arbitrary", "arbitrary"),
            vmem_limit_bytes=VMEM_LIMIT_BYTES),
        name="sb_attn",
    )(kstart, q, k, v, pos_col, pos_row)


def _round_robin(*gens):
    gens = list(gens)
    while gens:
        for g in list(gens):
            if next(g, StopIteration) is StopIteration:
                gens.remove(g)


def _attn_kernel(flags_ref, kstart_ref, q_ref, k_ref, v_ref, sq_ref, sk_ref, sv_ref, posq_ref,
                 posk_ref, om_ref, os_ref, acc_ref, run_ref, sacc_ref, *, tq, tk, nk, hp):
    b = pl.program_id(0)
    qi = pl.program_id(2)
    stq, stk, win = SB_TQ, SB_TK, SB_WIN
    halves = tq // stq
    units = halves * (hp // 2)
    qchunk = jnp.right_shift(posq_ref[0], CHUNK_SHIFT)
    acc_ref[...] = jnp.zeros(acc_ref.shape, F32)
    lane = lax.broadcasted_iota(jnp.int32, (stq, LANES), 1)
    tri2 = (lax.broadcasted_iota(jnp.int32, (2 * stk, 2 * stk), 0)
            >= lax.broadcasted_iota(jnp.int32, (2 * stk, 2 * stk), 1)).astype(BF16)
    tri = tri2[:stk, :stk]

    def mla_heads(kb, masked):
        k0 = pl.multiple_of(kb * tk, tk)
        if masked:
            kchunk = jnp.right_shift(posk_ref[0, :, pl.ds(k0, tk)], CHUNK_SHIFT)
            vis = kchunk <= qchunk
        for h in range(hp):
            hs = slice(h * SLOT, (h + 1) * SLOT)
            s = _dot_nt(q_ref[0, :, hs], k_ref[0, pl.ds(k0, tk), hs])
            if masked:
                s = jnp.where(vis, s, NEG)
            acc_ref[h] += _dot(jnp.exp2(s).astype(BF16), v_ref[0, pl.ds(k0, tk), hs])
            yield

    def sb_operands(u):
        rows = pl.ds(pl.multiple_of((u % halves) * stq, stq), stq)
        cols = pl.ds(pl.multiple_of((u // halves) * LANES, LANES), LANES)
        q_pair = sq_ref[0, rows, cols]
        zero = jnp.zeros_like(q_pair)
        q2 = jnp.concatenate([jnp.where(lane < SB_HEAD_DIM, q_pair, zero),
                              jnp.where(lane >= SB_HEAD_DIM, q_pair, zero)], axis=0)
        qpos = jnp.concatenate([posq_ref[0, rows], posq_ref[0, rows]], axis=0)
        return q2, qpos, rows, cols

    def suffix_sums(l1m, triangle):
        hi = lax.bitcast_convert_type(
            lax.bitcast_convert_type(l1m, jnp.int32) & jnp.int32(HI_MASK), F32)
        lo = l1m - hi
        both = _dot(jnp.concatenate([hi.astype(BF16), lo.astype(BF16)], axis=0), triangle)
        return both[:2 * stq] + both[2 * stq:]

    def sb_fast(u, kstart):
        q2, qpos, _, cols = sb_operands(u)
        k0 = pl.multiple_of((kstart - (win - 1)) * stk, stk)
        causal = posk_ref[0, :, pl.ds(k0, win * stk)] < qpos
        z = _dot_nt(q2, sk_ref[0, pl.ds(k0, win * stk), cols])
        l1m = jnp.where(causal, -_softplus(z), 0.0)
        yield
        run = jnp.zeros((2 * stq, 1), F32)
        a_blocks = [None] * (win // 2)
        for j in reversed(range(win // 2)):
            cs = slice(j * 2 * stk, (j + 1) * 2 * stk)
            suffix = suffix_sums(l1m[:, cs], tri2)
            a = jnp.exp(z[:, cs] + suffix + run)
            a_blocks[j] = jnp.where(causal[:, cs], a, 0.0).astype(BF16)
            run = run + suffix[:, 0:1]
            yield
        sacc_ref[...] = _dot(jnp.concatenate(a_blocks, axis=1), sv_ref[0, pl.ds(k0, win * stk), cols])
        run_ref[...] = run
        yield

    def sb_finish(u, kstart, fast):
        q2, qpos, rows, cols = sb_operands(u)

        def block(kb):
            k0 = pl.multiple_of(kb * stk, stk)
            z = _dot_nt(q2, sk_ref[0, pl.ds(k0, stk), cols])
            causal = posk_ref[0, :, pl.ds(k0, stk)] < qpos
            l1m = jnp.where(causal, -_softplus(z), 0.0)
            suffix = suffix_sums(l1m, tri)
            run = run_ref[...]
            a = jnp.where(causal, jnp.exp(z + suffix + run), 0.0)
            sacc_ref[...] += _dot(a.astype(BF16), sv_ref[0, pl.ds(k0, stk), cols])
            run_new = run + suffix[:, 0:1]
            run_ref[...] = run_new
            return jnp.max(run_new)

        def cond(c):
            kb, top = c
            return (kb >= 0) & (top >= SB_LOG_ZERO)

        def body(c):
            kb, _ = c
            return kb - 1, block(kb)

        kb_first = jnp.where(fast, kstart - win, kstart)
        lax.while_loop(cond, body, (kb_first, jnp.max(run_ref[...])))
        os_ref[0, rows, cols] = jnp.where(lane < SB_HEAD_DIM, sacc_ref[:stq], sacc_ref[stq:])

    def with_unit(kb, carry):
        flag = flags_ref[b, qi, kb]
        kstart = kstart_ref[b, qi * halves + kb % halves]
        fast = kstart >= win - 1
        run_ref[...] = jnp.zeros(run_ref.shape, F32)
        sacc_ref[...] = jnp.zeros(sacc_ref.shape, F32)

        @pl.when((flag == 1) & fast)
        def _():
            _round_robin(sb_fast(kb, kstart), mla_heads(kb, False))

        @pl.when((flag == 2) & fast)
        def _():
            _round_robin(sb_fast(kb, kstart), mla_heads(kb, True))

        @pl.when((flag == 0) & fast)
        def _():
            _round_robin(sb_fast(kb, kstart))

        @pl.when((flag == 1) & jnp.logical_not(fast))
        def _():
            _round_robin(mla_heads(kb, False))

        @pl.when((flag == 2) & jnp.logical_not(fast))
        def _():
            _round_robin(mla_heads(kb, True))

        sb_finish(kb, kstart, fast)
        return carry

    def mla_only(kb, carry):
        flag = flags_ref[b, qi, kb]

        @pl.when(flag == 1)
        def _():
            _round_robin(mla_heads(kb, False))

        @pl.when(flag == 2)
        def _():
            _round_robin(mla_heads(kb, True))

        return carry

    lax.fori_loop(0, units, with_unit, 0)
    lax.fori_loop(units, nk, mla_only, 0)
    lane_q = lax.broadcasted_iota(jnp.int32, (tq, LANES), 1)
    for j in range(hp // 2):
        o0 = acc_ref[2 * j] / pltpu.roll(acc_ref[2 * j], MLA_V, 1)
        o1 = acc_ref[2 * j + 1] / pltpu.roll(acc_ref[2 * j + 1], MLA_V, 1)
        om_ref[0, :, j * LANES:(j + 1) * LANES] = jnp.where(lane_q < MLA_V, o0, pltpu.roll(o1, MLA_V, 1))


def _attn_call(flags, kstart, q, k, v, sq, sk, sv, pos_col, pos_row, tq, tk, hp):
    B, S, _ = q.shape
    nq, nk = S // tq, S // tk
    assert (tq // SB_TQ) * (hp // 2) <= nk and SB_HEADS == MLA_HEADS and SB_WIN % 2 == 0
    sbw = (hp // 2) * LANES
    grid_spec = pltpu.PrefetchScalarGridSpec(
        num_scalar_prefetch=2,
        grid=(B, MLA_HEADS // hp, nq),
        in_specs=[
            pl.BlockSpec((1, tq, hp * SLOT), lambda b, p, i, f, s: (b, i, p)),
            pl.BlockSpec((1, S, hp * SLOT), lambda b, p, i, f, s: (b, 0, p)),
            pl.BlockSpec((1, S, hp * SLOT), lambda b, p, i, f, s: (b, 0, p)),
            pl.BlockSpec((1, tq, sbw), lambda b, p, i, f, s: (b, i, p)),
            pl.BlockSpec((1, S, sbw), lambda b, p, i, f, s: (b, 0, p)),
            pl.BlockSpec((1, S, sbw), lambda b, p, i, f, s: (b, 0, p)),
            pl.BlockSpec((1, tq, 1), lambda b, p, i, f, s: (b, i, 0)),
            pl.BlockSpec((1, 1, S), lambda b, p, i, f, s: (b, 0, 0)),
        ],
        out_specs=[pl.BlockSpec((1, tq, hp * MLA_V), lambda b, p, i, f, s: (b, i, p)),
                   pl.BlockSpec((1, tq, sbw), lambda b, p, i, f, s: (b, i, p))],
        scratch_shapes=[pltpu.VMEM((hp, tq, LANES), F32), pltpu.VMEM((2 * SB_TQ, 1), F32),
                        pltpu.VMEM((2 * SB_TQ, LANES), F32)],
    )
    return pl.pallas_call(
        functools.partial(_attn_kernel, tq=tq, tk=tk, nk=nk, hp=hp),
        grid_spec=grid_spec,
        out_shape=[jax.ShapeDtypeStruct((B, S, MLA_WIDTH), F32),
                   jax.ShapeDtypeStruct((B, S, SB_WIDTH), F32)],
        compiler_params=pltpu.CompilerParams(
            dimension_semantics=("arbitrary", "arbitrary", "arbitrary"),
            vmem_limit_bytes=VMEM_LIMIT_BYTES),
        name="attn",
    )(flags, kstart, q, k, v, sq, sk, sv, pos_col, pos_row)


def _split_bf16(a):
    hi = a.astype(BF16)
    lo = (a - hi.astype(F32)).astype(BF16)
    return hi, lo


def _merge_kernel(x_ref, om_ref, os_ref, gm_ref, gs_ref, wo_ref, g2_ref, wrh_ref, wrl_ref,
                  bias_ref, wsgu_ref, wsd_ref, base_ref, hn_ref, eidx_ref, gate_ref, rank_ref,
                  cnt_ref):
    mm = _rms(om_ref[...], gm_ref[...]).astype(BF16)
    ms = _rms(os_ref[...], gs_ref[...]).astype(BF16)
    h = x_ref[...] + _dot(mm, wo_ref[:MLA_WIDTH, :]) + _dot(ms, wo_ref[MLA_WIDTH:, :])
    hn = _rms(h, g2_ref[...])
    hn_hi, hn_lo = _split_bf16(hn)
    bits = lax.bitcast_convert_type(hn_hi.astype(F32), jnp.int32)
    hn_ref[...] = (lax.shift_right_logical(bits[:, :D_MODEL // 2], 16)
                   | (bits[:, D_MODEL // 2:] & jnp.int32(HI_MASK)))

    logits = (_dot_nt(wrh_ref[...], hn_hi) + _dot_nt(wrh_ref[...], hn_lo)
              + _dot_nt(wrl_ref[...], hn_hi))
    scores = jax.nn.sigmoid(logits)
    work = scores + bias_ref[...]
    eidx = lax.broadcasted_iota(jnp.int32, work.shape, 0)
    chosen, chosen_score, hits = [], [], []
    for _ in range(TOP_K):
        top = jnp.max(work, axis=0, keepdims=True)
        first = jnp.min(jnp.where(work == top, eidx, N_EXPERTS), axis=0, keepdims=True)
        hit = eidx == first
        hits.append(hit)
        chosen.append(first)
        chosen_score.append(jnp.sum(jnp.where(hit, scores, 0.0), axis=0, keepdims=True))
        work = jnp.where(hit, -jnp.inf, work)
    sel = jnp.concatenate(chosen_score, axis=0)
    eidx_ref[...] = jnp.concatenate(chosen, axis=0)
    gate_ref[...] = sel / jnp.sum(sel, axis=0, keepdims=True) * ROUTED_SCALE

    tm = work.shape[1]
    member = functools.reduce(jnp.logical_or, hits)
    member = jnp.where(member, 1.0, 0.0)
    tri = (lax.broadcasted_iota(jnp.int32, (tm, tm), 0)
           <= lax.broadcasted_iota(jnp.int32, (tm, tm), 1)).astype(BF16)
    upto = _dot(member.astype(BF16), tri)
    before = upto - member
    rank_ref[...] = jnp.concatenate(
        [jnp.sum(jnp.where(hit, before, 0.0), axis=0, keepdims=True) for hit in hits],
        axis=0).astype(jnp.int32)
    cnt_ref[0] = upto[:, tm - 1:tm].astype(jnp.int32)

    sgu = _dot(hn_hi, wsgu_ref[...])
    act = (jax.nn.silu(sgu[:, :SHARED_FF]) * sgu[:, SHARED_FF:]).astype(BF16)
    base_ref[...] = h + _dot(act, wsd_ref[...])


def _merge_call(x2, o_mla, o_sb, gm, gs, wo, g2, wrh, wrl, bias, wsgu, wsd, tm):
    T = x2.shape[0]
    full = lambda a: pl.BlockSpec(a.shape, lambda i: (0,) * a.ndim)
    row = lambda w: pl.BlockSpec((tm, w), lambda i: (i, 0))
    weights = (gm, gs, wo, g2, wrh, wrl, bias, wsgu, wsd)
    topk = pl.BlockSpec((TOP_K, tm), lambda i: (0, i))
    return pl.pallas_call(
        _merge_kernel,
        grid=(T // tm,),
        in_specs=[row(D_MODEL), row(MLA_WIDTH), row(SB_WIDTH)] + [full(w) for w in weights],
        out_specs=[row(D_MODEL), row(D_MODEL // 2), topk, topk, topk,
                   pl.BlockSpec((1, N_EXPERTS, 1), lambda i: (i, 0, 0))],
        out_shape=[jax.ShapeDtypeStruct((T, D_MODEL), F32),
                   jax.ShapeDtypeStruct((T, D_MODEL // 2), jnp.int32),
                   jax.ShapeDtypeStruct((TOP_K, T), jnp.int32),
                   jax.ShapeDtypeStruct((TOP_K, T), F32),
                   jax.ShapeDtypeStruct((TOP_K, T), jnp.int32),
                   jax.ShapeDtypeStruct((T // tm, N_EXPERTS, 1), jnp.int32)],
        compiler_params=pltpu.CompilerParams(dimension_semantics=("arbitrary",),
                                             vmem_limit_bytes=VMEM_LIMIT_BYTES),
        name="merge",
    )(x2, o_mla, o_sb, *weights)


MOE_TB = 2048
MOE_RT = 64
MOE_MAX_TILES = 6
MOE_ROWS = MOE_TB * TOP_K + N_EXPERTS * (MOE_RT - 1)
MOE_PACK = D_MODEL // 2
MOE_SUB = MOE_PACK // LANES
MOE_UNROLL = 8
MOE_XCH = 256
MOE_OCH = 128
MOE_WBUF = 3
MOE_CLEAR_ROWS = 1024


def _unpack_lo(w):
    return lax.bitcast_convert_type(w << 16, F32)


def _unpack_hi(w):
    return lax.bitcast_convert_type(w & jnp.int32(HI_MASK), F32)


def _moe_kernel(off_ref, nt_ref, pos_ref, gate_ref, xp_hbm, base_hbm, wgu_hbm, wd_hbm, o_hbm,
                xy_ref, xbuf, bbuf, obuf, wgu_buf, wd_buf, xsem, bsem, osem, wsem):
    i = pl.program_id(0)
    e = pl.program_id(1)

    step_id = i * N_EXPERTS + e
    n_steps = pl.num_programs(0) * N_EXPERTS

    def w_copies(step):
        slot = step % MOE_WBUF
        ex = step % N_EXPERTS
        return (pltpu.make_async_copy(wgu_hbm.at[ex], wgu_buf.at[slot], wsem.at[0, slot]),
                pltpu.make_async_copy(wd_hbm.at[ex], wd_buf.at[slot], wsem.at[1, slot]))

    @pl.when(step_id == 0)
    def _():
        for ahead in range(MOE_WBUF - 1):
            for cp in w_copies(ahead):
                cp.start()

    @pl.when(step_id + (MOE_WBUF - 1) < n_steps)
    def _():
        for cp in w_copies(step_id + (MOE_WBUF - 1)):
            cp.start()

    @pl.when((i == 0) & (e == 0))
    def _():
        def clear(r, c):
            xy_ref[pl.ds(pl.multiple_of(r * MOE_CLEAR_ROWS, MOE_CLEAR_ROWS), MOE_CLEAR_ROWS), :] = (
                jnp.zeros((MOE_CLEAR_ROWS, LANES), jnp.int32))
            return c

        lax.fori_loop(0, MOE_ROWS * MOE_SUB // MOE_CLEAR_ROWS, clear, 0)
        tail = MOE_ROWS * MOE_SUB % MOE_CLEAR_ROWS
        if tail:
            xy_ref[pl.ds(MOE_ROWS * MOE_SUB - tail, tail), :] = jnp.zeros((tail, LANES), jnp.int32)

    def x_copy(c, slot):
        return pltpu.make_async_copy(
            xp_hbm.at[pl.ds(i * MOE_TB + c * MOE_XCH, MOE_XCH), :], xbuf.at[slot], xsem.at[slot])

    def base_copy(c, slot):
        return pltpu.make_async_copy(
            base_hbm.at[pl.ds(i * MOE_TB + c * MOE_OCH, MOE_OCH), :], bbuf.at[slot], bsem.at[slot])

    def out_copy(c, slot):
        return pltpu.make_async_copy(
            obuf.at[slot], o_hbm.at[pl.ds(i * MOE_TB + c * MOE_OCH, MOE_OCH), :], osem.at[slot])

    @pl.when(e == 0)
    def _():
        n_chunks = MOE_TB // MOE_XCH
        x_copy(0, 0).start()
        for c in range(n_chunks):
            slot = c % 2
            if c + 1 < n_chunks:
                x_copy(c + 1, 1 - slot).start()
            x_copy(c, slot).wait()

            def step(tt, carry, c=c, slot=slot):
                slab = xbuf[slot, pl.ds(pl.multiple_of(tt * MOE_UNROLL, MOE_UNROLL), MOE_UNROLL), :]
                for u in range(MOE_UNROLL):
                    t = tt * MOE_UNROLL + u
                    row = jnp.concatenate([slab[u:u + 1, j * LANES:(j + 1) * LANES]
                                           for j in range(MOE_SUB)], axis=0)
                    for k in range(TOP_K):
                        p = pl.multiple_of(pos_ref[TOP_K * (c * MOE_XCH + t) + k], MOE_SUB)
                        xy_ref[pl.ds(p, MOE_SUB), :] = row
                return carry

            lax.fori_loop(0, MOE_XCH // MOE_UNROLL, step, 0)

    for cp in w_copies(step_id):
        cp.wait()
    wslot = step_id % MOE_WBUF

    def ffn(r0, m):
        base_row = pl.multiple_of(MOE_SUB * r0, 8)
        words = jnp.concatenate(
            [xy_ref[pl.ds(base_row + j, m, stride=MOE_SUB), :] for j in range(MOE_SUB)], axis=1)
        gu = (_dot(_unpack_lo(words).astype(BF16), wgu_buf[wslot, :MOE_PACK, :])
              + _dot(_unpack_hi(words).astype(BF16), wgu_buf[wslot, MOE_PACK:, :]))
        act = (jax.nn.silu(gu[:, :EXPERT_FF]) * gu[:, EXPERT_FF:]).astype(BF16)
        y = _dot(act, wd_buf[wslot])
        ya = lax.bitcast_convert_type(y[:, :MOE_PACK].astype(BF16).astype(F32), jnp.int32)
        yb = lax.bitcast_convert_type(y[:, MOE_PACK:].astype(BF16).astype(F32), jnp.int32)
        packed = lax.shift_right_logical(ya, 16) | yb
        for j in range(MOE_SUB):
            xy_ref[pl.ds(base_row + j, m, stride=MOE_SUB), :] = packed[:, j * LANES:(j + 1) * LANES]

    off = off_ref[i * N_EXPERTS + e]
    nt = nt_ref[i * N_EXPERTS + e]

    def chunk(c, carry):
        ffn(off + c * (MOE_MAX_TILES * MOE_RT), MOE_MAX_TILES * MOE_RT)
        return carry

    full = nt // MOE_MAX_TILES
    lax.fori_loop(0, full, chunk, 0)
    rest = nt - full * MOE_MAX_TILES
    for tiles in range(1, MOE_MAX_TILES):
        @pl.when(rest == tiles)
        def _():
            ffn(off + full * (MOE_MAX_TILES * MOE_RT), tiles * MOE_RT)

    @pl.when(e == N_EXPERTS - 1)
    def _():
        n_chunks = MOE_TB // MOE_OCH
        base_copy(0, 0).start()
        for c in range(n_chunks):
            slot = c % 2
            if c + 1 < n_chunks:
                base_copy(c + 1, 1 - slot).start()
            base_copy(c, slot).wait()
            if c >= 2:
                out_copy(c - 2, slot).wait()

            def step(tt, carry, c=c, slot=slot):
                for u in range(MOE_UNROLL):
                    t = tt * MOE_UNROLL + u
                    lo = jnp.zeros((MOE_SUB, LANES), F32)
                    hi = jnp.zeros((MOE_SUB, LANES), F32)
                    for k in range(TOP_K):
                        s = TOP_K * (c * MOE_OCH + t) + k
                        w = xy_ref[pl.ds(pl.multiple_of(pos_ref[s], MOE_SUB), MOE_SUB), :]
                        g = gate_ref[s]
                        lo = lo + g * _unpack_lo(w)
                        hi = hi + g * _unpack_hi(w)
                    routed = jnp.concatenate([lo[j:j + 1] for j in range(MOE_SUB)]
                                             + [hi[j:j + 1] for j in range(MOE_SUB)], axis=1)
                    obuf[slot, pl.ds(t, 1), :] = bbuf[slot, pl.ds(t, 1), :] + routed
                return carry

            lax.fori_loop(0, MOE_OCH // MOE_UNROLL, step, 0)
            out_copy(c, slot).start()
        out_copy(n_chunks - 2, n_chunks % 2).wait()
        out_copy(n_chunks - 1, (n_chunks - 1) % 2).wait()


def _moe_call(off, nt, pos, gates, xp, base, wgu, wd):
    T = base.shape[0]
    slots = MOE_TB * TOP_K
    grid_spec = pltpu.PrefetchScalarGridSpec(
        num_scalar_prefetch=2,
        grid=(T // MOE_TB, N_EXPERTS),
        in_specs=[
            pl.BlockSpec((slots,), lambda i, e, o, n: (i,), memory_space=pltpu.SMEM),
            pl.BlockSpec((slots,), lambda i, e, o, n: (i,), memory_space=pltpu.SMEM),
            pl.BlockSpec(memory_space=pl.ANY),
            pl.BlockSpec(memory_space=pl.ANY),
            pl.BlockSpec(memory_space=pl.ANY),
            pl.BlockSpec(memory_space=pl.ANY),
        ],
        out_specs=pl.BlockSpec(memory_space=pl.ANY),
        scratch_shapes=[
            pltpu.VMEM((MOE_ROWS * MOE_SUB, LANES), jnp.int32),
            pltpu.VMEM((2, MOE_XCH, MOE_PACK), jnp.int32),
            pltpu.VMEM((2, MOE_OCH, D_MODEL), F32),
            pltpu.VMEM((2, MOE_OCH, D_MODEL), F32),
            pltpu.VMEM((MOE_WBUF, D_MODEL, 2 * EXPERT_FF), BF16),
            pltpu.VMEM((MOE_WBUF, EXPERT_FF, D_MODEL), BF16),
            pltpu.SemaphoreType.DMA((2,)),
            pltpu.SemaphoreType.DMA((2,)),
            pltpu.SemaphoreType.DMA((2,)),
            pltpu.SemaphoreType.DMA((2, MOE_WBUF)),
        ],
    )
    return pl.pallas_call(
        _moe_kernel,
        grid_spec=grid_spec,
        out_shape=jax.ShapeDtypeStruct((T, D_MODEL), F32),
        compiler_params=pltpu.CompilerParams(dimension_semantics=("arbitrary", "arbitrary"),
                                             vmem_limit_bytes=VMEM_LIMIT_BYTES),
        name="moe",
    )(off, nt, pos, gates, xp, base, wgu, wd)


def _route_tables(eidx, gate, rank, cnt):
    T = eidx.shape[1]
    nblk = T // MOE_TB
    cnt = cnt.reshape(nblk, -1, N_EXPERTS)
    per_blk = cnt.shape[1]
    earlier = jnp.cumsum(cnt, axis=1) - cnt
    tiles = (cnt.sum(axis=1) + MOE_RT - 1) // MOE_RT
    off = jnp.cumsum(tiles, axis=1) * MOE_RT - tiles * MOE_RT
    start = (off[:, None, :] + earlier).reshape(nblk * per_blk, N_EXPERTS)
    e_tk = eidx.T.reshape(nblk * per_blk, T // (nblk * per_blk), TOP_K)
    chosen = e_tk[..., None] == jnp.arange(N_EXPERTS, dtype=jnp.int32)
    pos = jnp.where(chosen, start[:, None, None, :], 0).sum(axis=-1).reshape(T, TOP_K) + rank.T
    return (off.reshape(-1).astype(jnp.int32), tiles.reshape(-1).astype(jnp.int32),
            (pos * MOE_SUB).reshape(-1).astype(jnp.int32), gate.T.reshape(-1))


def _slot_cols(w, width):
    k = w.shape[0]
    w = w.reshape(k, MLA_HEADS, width)
    return jnp.pad(w, ((0, 0), (0, 0), (0, SLOT - width))).reshape(k, MLA_HEADS * SLOT)


def _block_flags(pos, tq, tk, chunked):
    B, S = pos.shape
    p = jnp.right_shift(pos, CHUNK_SHIFT) if chunked else pos
    qmin = p.reshape(B, S // tq, tq).min(-1)[:, :, None]
    qmax = p.reshape(B, S // tq, tq).max(-1)[:, :, None]
    kmin = p.reshape(B, S // tk, tk).min(-1)[:, None, :]
    kmax = p.reshape(B, S // tk, tk).max(-1)[:, None, :]
    if chunked:
        none, all_ = kmin > qmax, kmax <= qmin
    else:
        none, all_ = kmin >= qmax, kmax < qmin
    return jnp.where(none, 0, jnp.where(all_, 1, 2)).astype(jnp.int32)


def kernel(x, positions, norm1, w_in, q_a_norm, w_uq, kv_a_norm, w_ukv, q_norm, k_norm,
           out_norm_mla, out_norm_sb, w_o, norm2, w_router, router_bias, w_gate_up, w_down,
           w_shared_gate_up, w_shared_down):
    B, S, D = x.shape
    T = B * S
    c0 = Q_LORA
    c1 = c0 + KV_LORA
    c2 = c1 + MLA_ROPE

    row2 = lambda g: g.reshape(1, -1).astype(F32)
    wcq = w_in[:, :c0].astype(BF16)
    wckv = w_in[:, c0:c1].astype(BF16)
    wkr = jnp.pad(w_in[:, c1:c2], ((0, 0), (MLA_NOPE, SLOT - MLA_QK))).astype(BF16)
    wsb = w_in[:, c2:].astype(BF16)
    wuq = _slot_cols(w_uq, MLA_QK).astype(BF16)
    w_ukv3 = w_ukv.reshape(KV_LORA, MLA_HEADS, MLA_NOPE + MLA_V)
    wuk = _slot_cols(w_ukv3[:, :, :MLA_NOPE].reshape(KV_LORA, -1), MLA_NOPE).astype(BF16)
    wuv = _slot_cols(w_ukv3[:, :, MLA_NOPE:].reshape(KV_LORA, MLA_WIDTH), MLA_V).astype(BF16)
    slot_lane = jnp.arange(SLOT)
    vone = jnp.tile((slot_lane >= MLA_V).astype(F32), MLA_HEADS).reshape(1, MLA_HEADS * SLOT)
    gq = jnp.pad(q_norm, (0, SLOT - MLA_QK)).reshape(1, SLOT)
    gk = jnp.pad(k_norm, (0, SLOT - MLA_QK)).reshape(1, SLOT)
    inv = ROPE_THETA ** (-jnp.arange(HALF_ROPE, dtype=F32) / HALF_ROPE)
    inv_slot = jnp.pad(jnp.concatenate([inv, inv]), (MLA_NOPE, SLOT - MLA_QK)).reshape(1, SLOT)

    bound = (math.sqrt(MLA_QK) * LOG2E * BOUND_MARGIN) * jnp.max(jnp.abs(q_norm)) * jnp.max(jnp.abs(k_norm))
    bounded = bound <= MLA_MAX_SHIFT
    shift = jnp.where(bounded, bound, 0.0)
    qaug = (slot_lane == MLA_QK).astype(F32).reshape(1, SLOT)
    kaug = -shift * qaug

    x2 = x.reshape(T, D)
    pos_col = positions.reshape(T, 1)
    q, k, v, sq, sk, sv = _proj_call(
        x2, pos_col,
        (row2(norm1), wcq, wckv, wkr, wsb, row2(q_a_norm), wuq, row2(kv_a_norm), wuk, wuv, vone,
         gq, gk, inv_slot, qaug, kaug), tm=PROJ_TM)

    pos_c3 = positions.reshape(B, S, 1)
    pos_r3 = positions.reshape(B, 1, S)
    r3 = lambda a: a.reshape(B, S, a.shape[-1])

    mla_flags = _block_flags(positions, MLA_TQ, MLA_TK, True)
    sb_flags = _block_flags(positions, SB_TQ, SB_TK, False)
    kstart = jnp.max(jnp.where(sb_flags != 0, jnp.arange(S // SB_TK, dtype=jnp.int32), -1),
                     axis=-1).astype(jnp.int32)
    attn_args = (mla_flags, kstart, r3(q), r3(k), r3(v), r3(sq), r3(sk), r3(sv), pos_c3, pos_r3)

    def separate(mla_flags, kstart, q, k, v, sq, sk, sv, pos_c3, pos_r3):
        o_mla = _mla_call(mla_flags, q, k, v, pos_c3, pos_r3, MLA_TQ, MLA_TK, hp=2, online=True)
        o_sb = _sb_call(kstart, sq, sk, sv, pos_c3, pos_r3, SB_TQ, SB_TK, win=SB_WIN)
        return o_mla, o_sb

    o_mla, o_sb = lax.cond(
        bounded,
        lambda *a: tuple(_attn_call(*a, MLA_TQ, MLA_TK, hp=MLA_HEADS_PER_STEP)),
        separate, *attn_args)

    wr_t = w_router.T
    wrh = wr_t.astype(BF16)
    wrl = (wr_t - wrh.astype(F32)).astype(BF16)
    base, hn, eidx, gate, rank, cnt = _merge_call(
        x2, o_mla.reshape(T, MLA_WIDTH), o_sb.reshape(T, SB_WIDTH), row2(out_norm_mla),
        row2(out_norm_sb), w_o.astype(BF16), row2(norm2), wrh, wrl,
        router_bias.reshape(N_EXPERTS, 1).astype(F32), w_shared_gate_up.astype(BF16),
        w_shared_down.astype(BF16), tm=MERGE_TM)

    off, tiles, pos, gates = _route_tables(eidx, gate, rank, cnt)
    out = _moe_call(off, tiles, pos, gates, hn, base, w_gate_up.astype(BF16),
                    w_down.astype(BF16))
    return out.reshape(B, S, D)
```

```python
import functools
import math

import jax
import jax.numpy as jnp
from jax import lax
from jax.experimental import pallas as pl
from jax.experimental.pallas import tpu as pltpu

D_MODEL = 1024
CHUNK = 64
MLA_HEADS = 8
MLA_NOPE = 64
MLA_ROPE = 32
MLA_QK = MLA_NOPE + MLA_ROPE
MLA_V = 64
Q_LORA = 256
KV_LORA = 128
ROPE_THETA = 10000.0
SB_HEADS = 8
SB_HEAD_DIM = 64
SB_WIDTH = SB_HEADS * SB_HEAD_DIM
MLA_WIDTH = MLA_HEADS * MLA_V
N_EXPERTS = 64
TOP_K = 8
EXPERT_FF = 256
SHARED_FF = 256
ROUTED_SCALE = 2.5
EPS = 1e-6
NEG = -1e30

LANES = 128
VMEM_LIMIT_BYTES = 56 * 1024 * 1024

SLOT = LANES
HALF_ROPE = MLA_ROPE // 2

SB_LOG_ZERO = -90.0

LOG2E = 1.4426950408889634
BOUND_MARGIN = 1.02
MLA_MAX_SHIFT = 50.0

BF16 = jnp.bfloat16
F32 = jnp.float32
HI_MASK = -65536
CHUNK_SHIFT = CHUNK.bit_length() - 1
assert 1 << CHUNK_SHIFT == CHUNK

PROJ_TM = 512
MERGE_TM = 1024
MLA_TQ, MLA_TK = 512, 512
MLA_HEADS_PER_STEP = 4
SB_TQ, SB_TK, SB_WIN = 256, 128, 4


def _rms(x, g):
    return x * lax.rsqrt(jnp.mean(x * x, axis=-1, keepdims=True) + EPS) * g


def _dot(a, b):
    return jnp.dot(a, b, preferred_element_type=F32)


def _dot_nt(a, b):
    return lax.dot_general(a, b, (((1,), (1,)), ((), ())), preferred_element_type=F32)


def _proj_kernel(x_ref, pos_ref, g1_ref, wcq_ref, wckv_ref, wkr_ref, wsb_ref, gqa_ref, wuq_ref,
                 gkva_ref, wuk_ref, wuv_ref, vone_ref, gq_ref, gk_ref, inv_ref, qaug_ref, kaug_ref,
                 q_ref, k_ref, v_ref, sq_ref, sk_ref, sv_ref):
    x = x_ref[...]
    xn = _rms(x, g1_ref[...]).astype(BF16)

    cq = _rms(_dot(xn, wcq_ref[...]), gqa_ref[...]).astype(BF16)
    q = _dot(cq, wuq_ref[...])
    ckv = _rms(_dot(xn, wckv_ref[...]), gkva_ref[...]).astype(BF16)
    kn = _dot(ckv, wuk_ref[...])
    v_ref[...] = (_dot(ckv, wuv_ref[...]) + vone_ref[...]).astype(BF16)
    kr = _dot(xn, wkr_ref[...])

    sb = _dot(xn, wsb_ref[...])
    sq_ref[...] = (sb[:, :SB_WIDTH] * (1.0 / math.sqrt(SB_HEAD_DIM))).astype(BF16)
    sk_ref[...] = sb[:, SB_WIDTH:2 * SB_WIDTH].astype(BF16)
    sv_ref[...] = sb[:, 2 * SB_WIDTH:].astype(BF16)

    ang = pos_ref[...].astype(F32) * inv_ref[...]
    cos = jnp.cos(ang)
    sin = jnp.sin(ang)
    lane = lax.broadcasted_iota(jnp.int32, ang.shape, 1)
    first_half = lane < MLA_NOPE + HALF_ROPE

    def rope(t):
        up = pltpu.roll(t, SLOT - HALF_ROPE, 1)
        down = pltpu.roll(t, HALF_ROPE, 1)
        return t * cos + jnp.where(first_half, -up, down) * sin

    def head_norm(t, g):
        ss = jnp.sum(t * t, axis=-1, keepdims=True) * (1.0 / MLA_QK)
        return t * lax.rsqrt(ss + EPS) * g

    scale = LOG2E / math.sqrt(MLA_QK)
    for h in range(MLA_HEADS):
        sl = slice(h * SLOT, (h + 1) * SLOT)
        qh = rope(head_norm(q[:, sl], gq_ref[...])) * scale + qaug_ref[...]
        q_ref[:, sl] = qh.astype(BF16)
        kh = rope(head_norm(kn[:, sl] + kr, gk_ref[...])) + kaug_ref[...]
        k_ref[:, sl] = kh.astype(BF16)


def _proj_call(x2, pos_col, weights, tm):
    T = x2.shape[0]
    full = lambda a: pl.BlockSpec(a.shape, lambda i: (0,) * a.ndim)
    row = lambda w: pl.BlockSpec((tm, w), lambda i: (i, 0))
    out_w = (MLA_HEADS * SLOT, MLA_HEADS * SLOT, MLA_HEADS * SLOT, SB_WIDTH, SB_WIDTH, SB_WIDTH)
    return pl.pallas_call(
        _proj_kernel,
        grid=(T // tm,),
        in_specs=[row(D_MODEL), row(1)] + [full(w) for w in weights],
        out_specs=[row(w) for w in out_w],
        out_shape=[jax.ShapeDtypeStruct((T, w), BF16) for w in out_w],
        compiler_params=pltpu.CompilerParams(dimension_semantics=("arbitrary",),
                                             vmem_limit_bytes=VMEM_LIMIT_BYTES),
        name="proj",
    )(x2, pos_col, *weights)


def _mla_kernel(flags_ref, q_ref, k_ref, v_ref, posq_ref, posk_ref, o_ref, acc_ref, m_ref,
                *, tq, tk, nk, hp, online):
    b = pl.program_id(0)
    qi = pl.program_id(2)
    qchunk = jnp.right_shift(posq_ref[0], CHUNK_SHIFT)
    acc_ref[...] = jnp.zeros(acc_ref.shape, F32)
    if online:
        m_ref[...] = jnp.full(m_ref.shape, NEG, F32)

    def block(kb, masked):
        k0 = pl.multiple_of(kb * tk, tk)
        if masked:
            kchunk = jnp.right_shift(posk_ref[0, :, pl.ds(k0, tk)], CHUNK_SHIFT)
            vis = kchunk <= qchunk
        for h in range(hp):
            hs = slice(h * SLOT, (h + 1) * SLOT)
            s = _dot_nt(q_ref[0, :, hs], k_ref[0, pl.ds(k0, tk), hs])
            if masked:
                s = jnp.where(vis, s, NEG)
            vh = v_ref[0, pl.ds(k0, tk), hs]
            if online:
                m_old = m_ref[h]
                m_new = jnp.maximum(m_old, jnp.max(s, axis=-1, keepdims=True))
                p = jnp.exp2(s - m_new)
                acc_ref[h] = jnp.exp2(m_old - m_new) * acc_ref[h] + _dot(p.astype(BF16), vh)
                m_ref[h] = m_new
            else:
                acc_ref[h] += _dot(jnp.exp2(s).astype(BF16), vh)

    def body(kb, carry):
        flag = flags_ref[b, qi, kb]

        @pl.when(flag == 1)
        def _():
            block(kb, False)

        @pl.when(flag == 2)
        def _():
            block(kb, True)

        return carry

    lax.fori_loop(0, nk, body, 0)
    lane = lax.broadcasted_iota(jnp.int32, (tq, LANES), 1)
    for j in range(hp // 2):
        o0 = acc_ref[2 * j] / pltpu.roll(acc_ref[2 * j], MLA_V, 1)
        o1 = acc_ref[2 * j + 1] / pltpu.roll(acc_ref[2 * j + 1], MLA_V, 1)
        o_ref[0, :, j * LANES:(j + 1) * LANES] = jnp.where(lane < MLA_V, o0, pltpu.roll(o1, MLA_V, 1))


def _mla_call(flags, q, k, v, pos_col, pos_row, tq, tk, hp, online):
    B, S, _ = q.shape
    nq, nk = S // tq, S // tk
    grid_spec = pltpu.PrefetchScalarGridSpec(
        num_scalar_prefetch=1,
        grid=(B, MLA_HEADS // hp, nq),
        in_specs=[
            pl.BlockSpec((1, tq, hp * SLOT), lambda b, p, i, f: (b, i, p)),
            pl.BlockSpec((1, S, hp * SLOT), lambda b, p, i, f: (b, 0, p)),
            pl.BlockSpec((1, S, hp * SLOT), lambda b, p, i, f: (b, 0, p)),
            pl.BlockSpec((1, tq, 1), lambda b, p, i, f: (b, i, 0)),
            pl.BlockSpec((1, 1, S), lambda b, p, i, f: (b, 0, 0)),
        ],
        out_specs=pl.BlockSpec((1, tq, hp * MLA_V), lambda b, p, i, f: (b, i, p)),
        scratch_shapes=[pltpu.VMEM((hp, tq, LANES), F32), pltpu.VMEM((hp, tq, 1), F32)],
    )
    return pl.pallas_call(
        functools.partial(_mla_kernel, tq=tq, tk=tk, nk=nk, hp=hp, online=online),
        grid_spec=grid_spec,
        out_shape=jax.ShapeDtypeStruct((B, S, MLA_WIDTH), F32),
        compiler_params=pltpu.CompilerParams(
            dimension_semantics=("arbitrary", "arbitrary", "arbitrary"),
            vmem_limit_bytes=VMEM_LIMIT_BYTES),
        name="mla_attn_online" if online else "mla_attn",
    )(flags, q, k, v, pos_col, pos_row)


def _softplus(z):
    return jnp.maximum(z, 0.0) + jnp.log(1.0 + jnp.exp(-jnp.abs(z)))


def _sb_kernel(kstart_ref, q_ref, k_ref, v_ref, posq_ref, posk_ref, o_ref, run_ref, acc_ref,
               *, tq, tk, win):
    b = pl.program_id(0)
    qi = pl.program_id(2)
    lane = lax.broadcasted_iota(jnp.int32, (tq, LANES), 1)
    row_i = lax.broadcasted_iota(jnp.int32, (tk, tk), 0)
    col_i = lax.broadcasted_iota(jnp.int32, (tk, tk), 1)
    tri = (row_i >= col_i).astype(BF16)
    kstart = kstart_ref[b, qi]
    q_pair = q_ref[0]
    zero = jnp.zeros_like(q_pair)
    q2 = jnp.concatenate([jnp.where(lane < SB_HEAD_DIM, q_pair, zero),
                          jnp.where(lane >= SB_HEAD_DIM, q_pair, zero)], axis=0)
    qpos = jnp.concatenate([posq_ref[0], posq_ref[0]], axis=0)

    def suffix_sums(l1m):
        hi = lax.bitcast_convert_type(
            lax.bitcast_convert_type(l1m, jnp.int32) & jnp.int32(HI_MASK), F32)
        lo = l1m - hi
        both = _dot(jnp.concatenate([hi.astype(BF16), lo.astype(BF16)], axis=0), tri)
        return both[:2 * tq] + both[2 * tq:]

    run_ref[...] = jnp.zeros(run_ref.shape, F32)
    acc_ref[...] = jnp.zeros(acc_ref.shape, F32)

    @pl.when(kstart >= win - 1)
    def _():
        k0 = pl.multiple_of((kstart - (win - 1)) * tk, tk)
        causal = posk_ref[0, :, pl.ds(k0, win * tk)] < qpos
        z = _dot_nt(q2, k_ref[0, pl.ds(k0, win * tk), :])
        l1m = jnp.where(causal, -_softplus(z), 0.0)
        run = jnp.zeros((2 * tq, 1), F32)
        a_blocks = [None] * win
        for j in reversed(range(win)):
            cs = slice(j * tk, (j + 1) * tk)
            suffix = suffix_sums(l1m[:, cs])
            a = jnp.exp(z[:, cs] + suffix + run)
            a_blocks[j] = jnp.where(causal[:, cs], a, 0.0).astype(BF16)
            run = run + suffix[:, 0:1]
        acc_ref[...] = _dot(jnp.concatenate(a_blocks, axis=1), v_ref[0, pl.ds(k0, win * tk), :])
        run_ref[...] = run

    def block(kb):
        k0 = pl.multiple_of(kb * tk, tk)
        z = _dot_nt(q2, k_ref[0, pl.ds(k0, tk), :])
        causal = posk_ref[0, :, pl.ds(k0, tk)] < qpos
        l1m = jnp.where(causal, -_softplus(z), 0.0)
        suffix = suffix_sums(l1m)
        run = run_ref[...]
        a = jnp.where(causal, jnp.exp(z + suffix + run), 0.0)
        acc_ref[...] += _dot(a.astype(BF16), v_ref[0, pl.ds(k0, tk), :])
        run_new = run + suffix[:, 0:1]
        run_ref[...] = run_new
        return jnp.max(run_new)

    def cond(c):
        kb, top = c
        return (kb >= 0) & (top >= SB_LOG_ZERO)

    def body(c):
        kb, _ = c
        return kb - 1, block(kb)

    kb_first = jnp.where(kstart >= win - 1, kstart - win, kstart)
    lax.while_loop(cond, body, (kb_first, jnp.max(run_ref[...])))

    o_ref[0] = jnp.where(lane < SB_HEAD_DIM, acc_ref[:tq], acc_ref[tq:])


def _sb_call(kstart, q, k, v, pos_col, pos_row, tq, tk, win):
    B, S, _ = q.shape
    nq = S // tq
    grid_spec = pltpu.PrefetchScalarGridSpec(
        num_scalar_prefetch=1,
        grid=(B, SB_HEADS // 2, nq),
        in_specs=[
            pl.BlockSpec((1, tq, LANES), lambda b, p, i, s: (b, i, p)),
            pl.BlockSpec((1, S, LANES), lambda b, p, i, s: (b, 0, p)),
            pl.BlockSpec((1, S, LANES), lambda b, p, i, s: (b, 0, p)),
            pl.BlockSpec((1, tq, 1), lambda b, p, i, s: (b, i, 0)),
            pl.BlockSpec((1, 1, S), lambda b, p, i, s: (b, 0, 0)),
        ],
        out_specs=pl.BlockSpec((1, tq, LANES), lambda b, p, i, s: (b, i, p)),
        scratch_shapes=[pltpu.VMEM((2 * tq, 1), F32), pltpu.VMEM((2 * tq, LANES), F32)],
    )
    return pl.pallas_call(
        functools.partial(_sb_kernel, tq=tq, tk=tk, win=win),
        grid_spec=grid_spec,
        out_shape=jax.ShapeDtypeStruct((B, S, SB_WIDTH), F32),
        compiler_params=pltpu.CompilerParams(
            dimension_semantics=("arbitrary", "arbitrary", "arbitrary"),
            vmem_limit_bytes=VMEM_LIMIT_BYTES),
        name="sb_attn",
    )(kstart, q, k, v, pos_col, pos_row)


def _round_robin(*gens):
    gens = list(gens)
    while gens:
        for g in list(gens):
            if next(g, StopIteration) is StopIteration:
                gens.remove(g)


def _attn_kernel(flags_ref, kstart_ref, q_ref, k_ref, v_ref, sq_ref, sk_ref, sv_ref, posq_ref,
                 posk_ref, om_ref, os_ref, acc_ref, run_ref, sacc_ref, *, tq, tk, nk, hp):
    b = pl.program_id(0)
    qi = pl.program_id(2)
    stq, stk, win = SB_TQ, SB_TK, SB_WIN
    halves = tq // stq
    units = halves * (hp // 2)
    qchunk = jnp.right_shift(posq_ref[0], CHUNK_SHIFT)
    acc_ref[...] = jnp.zeros(acc_ref.shape, F32)
    lane = lax.broadcasted_iota(jnp.int32, (stq, LANES), 1)
    tri2 = (lax.broadcasted_iota(jnp.int32, (2 * stk, 2 * stk), 0)
            >= lax.broadcasted_iota(jnp.int32, (2 * stk, 2 * stk), 1)).astype(BF16)
    tri = tri2[:stk, :stk]

    def mla_heads(kb, masked):
        k0 = pl.multiple_of(kb * tk, tk)
        if masked:
            kchunk = jnp.right_shift(posk_ref[0, :, pl.ds(k0, tk)], CHUNK_SHIFT)
            vis = kchunk <= qchunk
        for h in range(hp):
            hs = slice(h * SLOT, (h + 1) * SLOT)
            s = _dot_nt(q_ref[0, :, hs], k_ref[0, pl.ds(k0, tk), hs])
            if masked:
                s = jnp.where(vis, s, NEG)
            acc_ref[h] += _dot(jnp.exp2(s).astype(BF16), v_ref[0, pl.ds(k0, tk), hs])
            yield

    def sb_operands(u):
        rows = pl.ds(pl.multiple_of((u % halves) * stq, stq), stq)
        cols = pl.ds(pl.multiple_of((u // halves) * LANES, LANES), LANES)
        q_pair = sq_ref[0, rows, cols]
        zero = jnp.zeros_like(q_pair)
        q2 = jnp.concatenate([jnp.where(lane < SB_HEAD_DIM, q_pair, zero),
                              jnp.where(lane >= SB_HEAD_DIM, q_pair, zero)], axis=0)
        qpos = jnp.concatenate([posq_ref[0, rows], posq_ref[0, rows]], axis=0)
        return q2, qpos, rows, cols

    def suffix_sums(l1m, triangle):
        hi = lax.bitcast_convert_type(
            lax.bitcast_convert_type(l1m, jnp.int32) & jnp.int32(HI_MASK), F32)
        lo = l1m - hi
        both = _dot(jnp.concatenate([hi.astype(BF16), lo.astype(BF16)], axis=0), triangle)
        return both[:2 * stq] + both[2 * stq:]

    def sb_fast(u, kstart):
        q2, qpos, _, cols = sb_operands(u)
        k0 = pl.multiple_of((kstart - (win - 1)) * stk, stk)
        causal = posk_ref[0, :, pl.ds(k0, win * stk)] < qpos
        z = _dot_nt(q2, sk_ref[0, pl.ds(k0, win * stk), cols])
        l1m = jnp.where(causal, -_softplus(z), 0.0)
        yield
        run = jnp.zeros((2 * stq, 1), F32)
        a_blocks = [None] * (win // 2)
        for j in reversed(range(win // 2)):
            cs = slice(j * 2 * stk, (j + 1) * 2 * stk)
            suffix = suffix_sums(l1m[:, cs], tri2)
            a = jnp.exp(z[:, cs] + suffix + run)
            a_blocks[j] = jnp.where(causal[:, cs], a, 0.0).astype(BF16)
            run = run + suffix[:, 0:1]
            yield
        sacc_ref[...] = _dot(jnp.concatenate(a_blocks, axis=1), sv_ref[0, pl.ds(k0, win * stk), cols])
        run_ref[...] = run
        yield

    def sb_finish(u, kstart, fast):
        q2, qpos, rows, cols = sb_operands(u)

        def block(kb):
            k0 = pl.multiple_of(kb * stk, stk)
            z = _dot_nt(q2, sk_ref[0, pl.ds(k0, stk), cols])
            causal = posk_ref[0, :, pl.ds(k0, stk)] < qpos
            l1m = jnp.where(causal, -_softplus(z), 0.0)
            suffix = suffix_sums(l1m, tri)
            run = run_ref[...]
            a = jnp.where(causal, jnp.exp(z + suffix + run), 0.0)
            sacc_ref[...] += _dot(a.astype(BF16), sv_ref[0, pl.ds(k0, stk), cols])
            run_new = run + suffix[:, 0:1]
            run_ref[...] = run_new
            return jnp.max(run_new)

        def cond(c):
            kb, top = c
            return (kb >= 0) & (top >= SB_LOG_ZERO)

        def body(c):
            kb, _ = c
            return kb - 1, block(kb)

        kb_first = jnp.where(fast, kstart - win, kstart)
        lax.while_loop(cond, body, (kb_first, jnp.max(run_ref[...])))
        os_ref[0, rows, cols] = jnp.where(lane < SB_HEAD_DIM, sacc_ref[:stq], sacc_ref[stq:])

    def with_unit(kb, carry):
        flag = flags_ref[b, qi, kb]
        kstart = kstart_ref[b, qi * halves + kb % halves]
        fast = kstart >= win - 1
        run_ref[...] = jnp.zeros(run_ref.shape, F32)
        sacc_ref[...] = jnp.zeros(sacc_ref.shape, F32)

        @pl.when((flag == 1) & fast)
        def _():
            _round_robin(sb_fast(kb, kstart), mla_heads(kb, False))

        @pl.when((flag == 2) & fast)
        def _():
            _round_robin(sb_fast(kb, kstart), mla_heads(kb, True))

        @pl.when((flag == 0) & fast)
        def _():
            _round_robin(sb_fast(kb, kstart))

        @pl.when((flag == 1) & jnp.logical_not(fast))
        def _():
            _round_robin(mla_heads(kb, False))

        @pl.when((flag == 2) & jnp.logical_not(fast))
        def _():
            _round_robin(mla_heads(kb, True))

        sb_finish(kb, kstart, fast)
        return carry

    def mla_only(kb, carry):
        flag = flags_ref[b, qi, kb]

        @pl.when(flag == 1)
        def _():
            _round_robin(mla_heads(kb, False))

        @pl.when(flag == 2)
        def _():
            _round_robin(mla_heads(kb, True))

        return carry

    lax.fori_loop(0, units, with_unit, 0)
    lax.fori_loop(units, nk, mla_only, 0)
    lane_q = lax.broadcasted_iota(jnp.int32, (tq, LANES), 1)
    for j in range(hp // 2):
        o0 = acc_ref[2 * j] / pltpu.roll(acc_ref[2 * j], MLA_V, 1)
        o1 = acc_ref[2 * j + 1] / pltpu.roll(acc_ref[2 * j + 1], MLA_V, 1)
        om_ref[0, :, j * LANES:(j + 1) * LANES] = jnp.where(lane_q < MLA_V, o0, pltpu.roll(o1, MLA_V, 1))


def _attn_call(flags, kstart, q, k, v, sq, sk, sv, pos_col, pos_row, tq, tk, hp):
    B, S, _ = q.shape
    nq, nk = S // tq, S // tk
    assert (tq // SB_TQ) * (hp // 2) <= nk and SB_HEADS == MLA_HEADS and SB_WIN % 2 == 0
    sbw = (hp // 2) * LANES
    grid_spec = pltpu.PrefetchScalarGridSpec(
        num_scalar_prefetch=2,
        grid=(B, MLA_HEADS // hp, nq),
        in_specs=[
            pl.BlockSpec((1, tq, hp * SLOT), lambda b, p, i, f, s: (b, i, p)),
            pl.BlockSpec((1, S, hp * SLOT), lambda b, p, i, f, s: (b, 0, p)),
            pl.BlockSpec((1, S, hp * SLOT), lambda b, p, i, f, s: (b, 0, p)),
            pl.BlockSpec((1, tq, sbw), lambda b, p, i, f, s: (b, i, p)),
            pl.BlockSpec((1, S, sbw), lambda b, p, i, f, s: (b, 0, p)),
            pl.BlockSpec((1, S, sbw), lambda b, p, i, f, s: (b, 0, p)),
            pl.BlockSpec((1, tq, 1), lambda b, p, i, f, s: (b, i, 0)),
            pl.BlockSpec((1, 1, S), lambda b, p, i, f, s: (b, 0, 0)),
        ],
        out_specs=[pl.BlockSpec((1, tq, hp * MLA_V), lambda b, p, i, f, s: (b, i, p)),
                   pl.BlockSpec((1, tq, sbw), lambda b, p, i, f, s: (b, i, p))],
        scratch_shapes=[pltpu.VMEM((hp, tq, LANES), F32), pltpu.VMEM((2 * SB_TQ, 1), F32),
                        pltpu.VMEM((2 * SB_TQ, LANES), F32)],
    )
    return pl.pallas_call(
        functools.partial(_attn_kernel, tq=tq, tk=tk, nk=nk, hp=hp),
        grid_spec=grid_spec,
        out_shape=[jax.ShapeDtypeStruct((B, S, MLA_WIDTH), F32),
                   jax.ShapeDtypeStruct((B, S, SB_WIDTH), F32)],
        compiler_params=pltpu.CompilerParams(
            dimension_semantics=("arbitrary", "arbitrary", "arbitrary"),
            vmem_limit_bytes=VMEM_LIMIT_BYTES),
        name="attn",
    )(flags, kstart, q, k, v, sq, sk, sv, pos_col, pos_row)


def _split_bf16(a):
    hi = a.astype(BF16)
    lo = (a - hi.astype(F32)).astype(BF16)
    return hi, lo


def _merge_kernel(x_ref, om_ref, os_ref, gm_ref, gs_ref, wo_ref, g2_ref, wrh_ref, wrl_ref,
                  bias_ref, wsgu_ref, wsd_ref, base_ref, hn_ref, eidx_ref, gate_ref, rank_ref,
                  cnt_ref):
    mm = _rms(om_ref[...], gm_ref[...]).astype(BF16)
    ms = _rms(os_ref[...], gs_ref[...]).astype(BF16)
    h = x_ref[...] + _dot(mm, wo_ref[:MLA_WIDTH, :]) + _dot(ms, wo_ref[MLA_WIDTH:, :])
    hn = _rms(h, g2_ref[...])
    hn_hi, hn_lo = _split_bf16(hn)
    bits = lax.bitcast_convert_type(hn_hi.astype(F32), jnp.int32)
    hn_ref[...] = (lax.shift_right_logical(bits[:, :D_MODEL // 2], 16)
                   | (bits[:, D_MODEL // 2:] & jnp.int32(HI_MASK)))

    logits = (_dot_nt(wrh_ref[...], hn_hi) + _dot_nt(wrh_ref[...], hn_lo)
              + _dot_nt(wrl_ref[...], hn_hi))
    scores = jax.nn.sigmoid(logits)
    work = scores + bias_ref[...]
    eidx = lax.broadcasted_iota(jnp.int32, work.shape, 0)
    chosen, chosen_score, hits = [], [], []
    for _ in range(TOP_K):
        top = jnp.max(work, axis=0, keepdims=True)
        first = jnp.min(jnp.where(work == top, eidx, N_EXPERTS), axis=0, keepdims=True)
        hit = eidx == first
        hits.append(hit)
        chosen.append(first)
        chosen_score.append(jnp.sum(jnp.where(hit, scores, 0.0), axis=0, keepdims=True))
        work = jnp.where(hit, -jnp.inf, work)
    sel = jnp.concatenate(chosen_score, axis=0)
    eidx_ref[...] = jnp.concatenate(chosen, axis=0)
    gate_ref[...] = sel / jnp.sum(sel, axis=0, keepdims=True) * ROUTED_SCALE

    tm = work.shape[1]
    member = functools.reduce(jnp.logical_or, hits)
    member = jnp.where(member, 1.0, 0.0)
    tri = (lax.broadcasted_iota(jnp.int32, (tm, tm), 0)
           <= lax.broadcasted_iota(jnp.int32, (tm, tm), 1)).astype(BF16)
    upto = _dot(member.astype(BF16), tri)
    before = upto - member
    rank_ref[...] = jnp.concatenate(
        [jnp.sum(jnp.where(hit, before, 0.0), axis=0, keepdims=True) for hit in hits],
        axis=0).astype(jnp.int32)
    cnt_ref[0] = upto[:, tm - 1:tm].astype(jnp.int32)

    sgu = _dot(hn_hi, wsgu_ref[...])
    act = (jax.nn.silu(sgu[:, :SHARED_FF]) * sgu[:, SHARED_FF:]).astype(BF16)
    base_ref[...] = h + _dot(act, wsd_ref[...])


def _merge_call(x2, o_mla, o_sb, gm, gs, wo, g2, wrh, wrl, bias, wsgu, wsd, tm):
    T = x2.shape[0]
    full = lambda a: pl.BlockSpec(a.shape, lambda i: (0,) * a.ndim)
    row = lambda w: pl.BlockSpec((tm, w), lambda i: (i, 0))
    weights = (gm, gs, wo, g2, wrh, wrl, bias, wsgu, wsd)
    topk = pl.BlockSpec((TOP_K, tm), lambda i: (0, i))
    return pl.pallas_call(
        _merge_kernel,
        grid=(T // tm,),
        in_specs=[row(D_MODEL), row(MLA_WIDTH), row(SB_WIDTH)] + [full(w) for w in weights],
        out_specs=[row(D_MODEL), row(D_MODEL // 2), topk, topk, topk,
                   pl.BlockSpec((1, N_EXPERTS, 1), lambda i: (i, 0, 0))],
        out_shape=[jax.ShapeDtypeStruct((T, D_MODEL), F32),
                   jax.ShapeDtypeStruct((T, D_MODEL // 2), jnp.int32),
                   jax.ShapeDtypeStruct((TOP_K, T), jnp.int32),
                   jax.ShapeDtypeStruct((TOP_K, T), F32),
                   jax.ShapeDtypeStruct((TOP_K, T), jnp.int32),
                   jax.ShapeDtypeStruct((T // tm, N_EXPERTS, 1), jnp.int32)],
        compiler_params=pltpu.CompilerParams(dimension_semantics=("arbitrary",),
                                             vmem_limit_bytes=VMEM_LIMIT_BYTES),
        name="merge",
    )(x2, o_mla, o_sb, *weights)


MOE_TB = 2048
MOE_RT = 64
MOE_MAX_TILES = 6
MOE_ROWS = MOE_TB * TOP_K + N_EXPERTS * (MOE_RT - 1)
MOE_PACK = D_MODEL // 2
MOE_SUB = MOE_PACK // LANES
MOE_UNROLL = 8
MOE_XCH = 256
MOE_OCH = 128
MOE_WBUF = 3
MOE_CLEAR_ROWS = 1024


def _unpack_lo(w):
    return lax.bitcast_convert_type(w << 16, F32)


def _unpack_hi(w):
    return lax.bitcast_convert_type(w & jnp.int32(HI_MASK), F32)


def _moe_kernel(off_ref, nt_ref, pos_ref, gate_ref, xp_hbm, base_hbm, wgu_hbm, wd_hbm, o_hbm,
                xy_ref, xbuf, bbuf, obuf, wgu_buf, wd_buf, xsem, bsem, osem, wsem):
    i = pl.program_id(0)

    n_visits = pl.num_programs(0) * N_EXPERTS

    def w_copies(step):
        slot = step % MOE_WBUF
        ex = step % N_EXPERTS
        return (pltpu.make_async_copy(wgu_hbm.at[ex], wgu_buf.at[slot], wsem.at[0, slot]),
                pltpu.make_async_copy(wd_hbm.at[ex], wd_buf.at[slot], wsem.at[1, slot]))

    @pl.when(i == 0)
    def _():
        for ahead in range(MOE_WBUF - 1):
            for cp in w_copies(ahead):
                cp.start()

        def clear(r, c):
            xy_ref[pl.ds(pl.multiple_of(r * MOE_CLEAR_ROWS, MOE_CLEAR_ROWS), MOE_CLEAR_ROWS), :] = (
                jnp.zeros((MOE_CLEAR_ROWS, LANES), jnp.int32))
            return c

        lax.fori_loop(0, MOE_ROWS * MOE_SUB // MOE_CLEAR_ROWS, clear, 0)
        tail = MOE_ROWS * MOE_SUB % MOE_CLEAR_ROWS
        if tail:
            xy_ref[pl.ds(MOE_ROWS * MOE_SUB - tail, tail), :] = jnp.zeros((tail, LANES), jnp.int32)

    def x_copy(c, slot):
        return pltpu.make_async_copy(
            xp_hbm.at[pl.ds(i * MOE_TB + c * MOE_XCH, MOE_XCH), :], xbuf.at[slot], xsem.at[slot])

    def base_copy(c, slot):
        return pltpu.make_async_copy(
            base_hbm.at[pl.ds(i * MOE_TB + c * MOE_OCH, MOE_OCH), :], bbuf.at[slot], bsem.at[slot])

    def out_copy(c, slot):
        return pltpu.make_async_copy(
            obuf.at[slot], o_hbm.at[pl.ds(i * MOE_TB + c * MOE_OCH, MOE_OCH), :], osem.at[slot])

    def dispatch():
        n_chunks = MOE_TB // MOE_XCH
        x_copy(0, 0).start()
        for c in range(n_chunks):
            slot = c % 2
            if c + 1 < n_chunks:
                x_copy(c + 1, 1 - slot).start()
            x_copy(c, slot).wait()

            def step(tt, carry, c=c, slot=slot):
                slab = xbuf[slot, pl.ds(pl.multiple_of(tt * MOE_UNROLL, MOE_UNROLL), MOE_UNROLL), :]
                for u in range(MOE_UNROLL):
                    t = tt * MOE_UNROLL + u
                    row = jnp.concatenate([slab[u:u + 1, j * LANES:(j + 1) * LANES]
                                           for j in range(MOE_SUB)], axis=0)
                    for k in range(TOP_K):
                        p = pl.multiple_of(pos_ref[TOP_K * (c * MOE_XCH + t) + k], MOE_SUB)
                        xy_ref[pl.ds(p, MOE_SUB), :] = row
                return carry

            lax.fori_loop(0, MOE_XCH // MOE_UNROLL, step, 0)

    def ffn(r0, m, wslot):
        base_row = pl.multiple_of(MOE_SUB * r0, 8)
        words = jnp.concatenate(
            [xy_ref[pl.ds(base_row + j, m, stride=MOE_SUB), :] for j in range(MOE_SUB)], axis=1)
        gu = (_dot(_unpack_lo(words).astype(BF16), wgu_buf[wslot, :MOE_PACK, :])
              + _dot(_unpack_hi(words).astype(BF16), wgu_buf[wslot, MOE_PACK:, :]))
        act = (jax.nn.silu(gu[:, :EXPERT_FF]) * gu[:, EXPERT_FF:]).astype(BF16)
        y = _dot(act, wd_buf[wslot])
        ya = lax.bitcast_convert_type(y[:, :MOE_PACK].astype(BF16).astype(F32), jnp.int32)
        yb = lax.bitcast_convert_type(y[:, MOE_PACK:].astype(BF16).astype(F32), jnp.int32)
        packed = lax.shift_right_logical(ya, 16) | yb
        for j in range(MOE_SUB):
            xy_ref[pl.ds(base_row + j, m, stride=MOE_SUB), :] = packed[:, j * LANES:(j + 1) * LANES]

    def expert(e, carry):
        visit = i * N_EXPERTS + e

        @pl.when(visit + (MOE_WBUF - 1) < n_visits)
        def _():
            for cp in w_copies(visit + (MOE_WBUF - 1)):
                cp.start()

        for cp in w_copies(visit):
            cp.wait()
        wslot = visit % MOE_WBUF
        off = off_ref[visit]
        nt = nt_ref[visit]

        def chunk(c, inner):
            ffn(off + c * (MOE_MAX_TILES * MOE_RT), MOE_MAX_TILES * MOE_RT, wslot)
            return inner

        full = nt // MOE_MAX_TILES
        lax.fori_loop(0, full, chunk, 0)
        rest = nt - full * MOE_MAX_TILES
        for tiles in range(1, MOE_MAX_TILES):
            @pl.when(rest == tiles)
            def _():
                ffn(off + full * (MOE_MAX_TILES * MOE_RT), tiles * MOE_RT, wslot)

        return carry

    def combine():
        n_chunks = MOE_TB // MOE_OCH
        base_copy(0, 0).start()
        for c in range(n_chunks):
            slot = c % 2
            if c + 1 < n_chunks:
                base_copy(c + 1, 1 - slot).start()
            base_copy(c, slot).wait()
            if c >= 2:
                out_copy(c - 2, slot).wait()

            def step(tt, carry, c=c, slot=slot):
                for u in range(MOE_UNROLL):
                    t = tt * MOE_UNROLL + u
                    lo = jnp.zeros((MOE_SUB, LANES), F32)
                    hi = jnp.zeros((MOE_SUB, LANES), F32)
                    for k in range(TOP_K):
                        s = TOP_K * (c * MOE_OCH + t) + k
                        w = xy_ref[pl.ds(pl.multiple_of(pos_ref[s], MOE_SUB), MOE_SUB), :]
                        g = gate_ref[s]
                        lo = lo + g * _unpack_lo(w)
                        hi = hi + g * _unpack_hi(w)
                    routed = jnp.concatenate([lo[j:j + 1] for j in range(MOE_SUB)]
                                             + [hi[j:j + 1] for j in range(MOE_SUB)], axis=1)
                    obuf[slot, pl.ds(t, 1), :] = bbuf[slot, pl.ds(t, 1), :] + routed
                return carry

            lax.fori_loop(0, MOE_OCH // MOE_UNROLL, step, 0)
            out_copy(c, slot).start()
        out_copy(n_chunks - 2, n_chunks % 2).wait()
        out_copy(n_chunks - 1, (n_chunks - 1) % 2).wait()

    dispatch()
    lax.fori_loop(0, N_EXPERTS, expert, 0)
    combine()


def _moe_call(off, nt, pos, gates, xp, base, wgu, wd):
    T = base.shape[0]
    slots = MOE_TB * TOP_K
    grid_spec = pltpu.PrefetchScalarGridSpec(
        num_scalar_prefetch=2,
        grid=(T // MOE_TB,),
        in_specs=[
            pl.BlockSpec((slots,), lambda i, o, n: (i,), memory_space=pltpu.SMEM),
            pl.BlockSpec((slots,), lambda i, o, n: (i,), memory_space=pltpu.SMEM),
            pl.BlockSpec(memory_space=pl.ANY),
            pl.BlockSpec(memory_space=pl.ANY),
            pl.BlockSpec(memory_space=pl.ANY),
            pl.BlockSpec(memory_space=pl.ANY),
        ],
        out_specs=pl.BlockSpec(memory_space=pl.ANY),
        scratch_shapes=[
            pltpu.VMEM((MOE_ROWS * MOE_SUB, LANES), jnp.int32),
            pltpu.VMEM((2, MOE_XCH, MOE_PACK), jnp.int32),
            pltpu.VMEM((2, MOE_OCH, D_MODEL), F32),
            pltpu.VMEM((2, MOE_OCH, D_MODEL), F32),
            pltpu.VMEM((MOE_WBUF, D_MODEL, 2 * EXPERT_FF), BF16),
            pltpu.VMEM((MOE_WBUF, EXPERT_FF, D_MODEL), BF16),
            pltpu.SemaphoreType.DMA((2,)),
            pltpu.SemaphoreType.DMA((2,)),
            pltpu.SemaphoreType.DMA((2,)),
            pltpu.SemaphoreType.DMA((2, MOE_WBUF)),
        ],
    )
    return pl.pallas_call(
        _moe_kernel,
        grid_spec=grid_spec,
        out_shape=jax.ShapeDtypeStruct((T, D_MODEL), F32),
        compiler_params=pltpu.CompilerParams(dimension_semantics=("arbitrary",),
                                             vmem_limit_bytes=VMEM_LIMIT_BYTES),
        name="moe",
    )(off, nt, pos, gates, xp, base, wgu, wd)


def _route_tables(eidx, gate, rank, cnt):
    T = eidx.shape[1]
    nblk = T // MOE_TB
    cnt = cnt.reshape(nblk, -1, N_EXPERTS)
    per_blk = cnt.shape[1]
    earlier = jnp.cumsum(cnt, axis=1) - cnt
    tiles = (cnt.sum(axis=1) + MOE_RT - 1) // MOE_RT
    off = jnp.cumsum(tiles, axis=1) * MOE_RT - tiles * MOE_RT
    start = (off[:, None, :] + earlier).reshape(nblk * per_blk, N_EXPERTS)
    e_tk = eidx.T.reshape(nblk * per_blk, T // (nblk * per_blk), TOP_K)
    chosen = e_tk[..., None] == jnp.arange(N_EXPERTS, dtype=jnp.int32)
    pos = jnp.where(chosen, start[:, None, None, :], 0).sum(axis=-1).reshape(T, TOP_K) + rank.T
    return (off.reshape(-1).astype(jnp.int32), tiles.reshape(-1).astype(jnp.int32),
            (pos * MOE_SUB).reshape(-1).astype(jnp.int32), gate.T.reshape(-1))


def _slot_cols(w, width):
    k = w.shape[0]
    w = w.reshape(k, MLA_HEADS, width)
    return jnp.pad(w, ((0, 0), (0, 0), (0, SLOT - width))).reshape(k, MLA_HEADS * SLOT)


def _block_flags(pos, tq, tk, chunked):
    B, S = pos.shape
    p = jnp.right_shift(pos, CHUNK_SHIFT) if chunked else pos
    qmin = p.reshape(B, S // tq, tq).min(-1)[:, :, None]
    qmax = p.reshape(B, S // tq, tq).max(-1)[:, :, None]
    kmin = p.reshape(B, S // tk, tk).min(-1)[:, None, :]
    kmax = p.reshape(B, S // tk, tk).max(-1)[:, None, :]
    if chunked:
        none, all_ = kmin > qmax, kmax <= qmin
    else:
        none, all_ = kmin >= qmax, kmax < qmin
    return jnp.where(none, 0, jnp.where(all_, 1, 2)).astype(jnp.int32)


def kernel(x, positions, norm1, w_in, q_a_norm, w_uq, kv_a_norm, w_ukv, q_norm, k_norm,
           out_norm_mla, out_norm_sb, w_o, norm2, w_router, router_bias, w_gate_up, w_down,
           w_shared_gate_up, w_shared_down):
    B, S, D = x.shape
    T = B * S
    c0 = Q_LORA
    c1 = c0 + KV_LORA
    c2 = c1 + MLA_ROPE

    row2 = lambda g: g.reshape(1, -1).astype(F32)
    wcq = w_in[:, :c0].astype(BF16)
    wckv = w_in[:, c0:c1].astype(BF16)
    wkr = jnp.pad(w_in[:, c1:c2], ((0, 0), (MLA_NOPE, SLOT - MLA_QK))).astype(BF16)
    wsb = w_in[:, c2:].astype(BF16)
    wuq = _slot_cols(w_uq, MLA_QK).astype(BF16)
    w_ukv3 = w_ukv.reshape(KV_LORA, MLA_HEADS, MLA_NOPE + MLA_V)
    wuk = _slot_cols(w_ukv3[:, :, :MLA_NOPE].reshape(KV_LORA, -1), MLA_NOPE).astype(BF16)
    wuv = _slot_cols(w_ukv3[:, :, MLA_NOPE:].reshape(KV_LORA, MLA_WIDTH), MLA_V).astype(BF16)
    slot_lane = jnp.arange(SLOT)
    vone = jnp.tile((slot_lane >= MLA_V).astype(F32), MLA_HEADS).reshape(1, MLA_HEADS * SLOT)
    gq = jnp.pad(q_norm, (0, SLOT - MLA_QK)).reshape(1, SLOT)
    gk = jnp.pad(k_norm, (0, SLOT - MLA_QK)).reshape(1, SLOT)
    inv = ROPE_THETA ** (-jnp.arange(HALF_ROPE, dtype=F32) / HALF_ROPE)
    inv_slot = jnp.pad(jnp.concatenate([inv, inv]), (MLA_NOPE, SLOT - MLA_QK)).reshape(1, SLOT)

    bound = (math.sqrt(MLA_QK) * LOG2E * BOUND_MARGIN) * jnp.max(jnp.abs(q_norm)) * jnp.max(jnp.abs(k_norm))
    bounded = bound <= MLA_MAX_SHIFT
    shift = jnp.where(bounded, bound, 0.0)
    qaug = (slot_lane == MLA_QK).astype(F32).reshape(1, SLOT)
    kaug = -shift * qaug

    x2 = x.reshape(T, D)
    pos_col = positions.reshape(T, 1)
    q, k, v, sq, sk, sv = _proj_call(
        x2, pos_col,
        (row2(norm1), wcq, wckv, wkr, wsb, row2(q_a_norm), wuq, row2(kv_a_norm), wuk, wuv, vone,
         gq, gk, inv_slot, qaug, kaug), tm=PROJ_TM)

    pos_c3 = positions.reshape(B, S, 1)
    pos_r3 = positions.reshape(B, 1, S)
    r3 = lambda a: a.reshape(B, S, a.shape[-1])

    mla_flags = _block_flags(positions, MLA_TQ, MLA_TK, True)
    sb_flags = _block_flags(positions, SB_TQ, SB_TK, False)
    kstart = jnp.max(jnp.where(sb_flags != 0, jnp.arange(S // SB_TK, dtype=jnp.int32), -1),
                     axis=-1).astype(jnp.int32)
    attn_args = (mla_flags, kstart, r3(q), r3(k), r3(v), r3(sq), r3(sk), r3(sv), pos_c3, pos_r3)

    def separate(mla_flags, kstart, q, k, v, sq, sk, sv, pos_c3, pos_r3):
        o_mla = _mla_call(mla_flags, q, k, v, pos_c3, pos_r3, MLA_TQ, MLA_TK, hp=2, online=True)
        o_sb = _sb_call(kstart, sq, sk, sv, pos_c3, pos_r3, SB_TQ, SB_TK, win=SB_WIN)
        return o_mla, o_sb

    o_mla, o_sb = lax.cond(
        bounded,
        lambda *a: tuple(_attn_call(*a, MLA_TQ, MLA_TK, hp=MLA_HEADS_PER_STEP)),
        separate, *attn_args)

    wr_t = w_router.T
    wrh = wr_t.astype(BF16)
    wrl = (wr_t - wrh.astype(F32)).astype(BF16)
    base, hn, eidx, gate, rank, cnt = _merge_call(
        x2, o_mla.reshape(T, MLA_WIDTH), o_sb.reshape(T, SB_WIDTH), row2(out_norm_mla),
        row2(out_norm_sb), w_o.astype(BF16), row2(norm2), wrh, wrl,
        router_bias.reshape(N_EXPERTS, 1).astype(F32), w_shared_gate_up.astype(BF16),
        w_shared_down.astype(BF16), tm=MERGE_TM)

    off, tiles, pos, gates = _route_tables(eidx, gate, rank, cnt)
    out = _moe_call(off, tiles, pos, gates, hn, base, w_gate_up.astype(BF16),
                    w_down.astype(BF16))
    return out.reshape(B, S, D)
```

```python
import functools
import math

import jax
import jax.numpy as jnp
from jax import lax
from jax.experimental import pallas as pl
from jax.experimental.pallas import tpu as pltpu

D_MODEL = 1024
CHUNK = 64
MLA_HEADS = 8
MLA_NOPE = 64
MLA_ROPE = 32
MLA_QK = MLA_NOPE + MLA_ROPE
MLA_V = 64
Q_LORA = 256
KV_LORA = 128
ROPE_THETA = 10000.0
SB_HEADS = 8
SB_HEAD_DIM = 64
SB_WIDTH = SB_HEADS * SB_HEAD_DIM
MLA_WIDTH = MLA_HEADS * MLA_V
N_EXPERTS = 64
TOP_K = 8
EXPERT_FF = 256
SHARED_FF = 256
ROUTED_SCALE = 2.5
EPS = 1e-6
NEG = -1e30

LANES = 128
VMEM_LIMIT_BYTES = 56 * 1024 * 1024

SLOT = LANES
HALF_ROPE = MLA_ROPE // 2

SB_LOG_ZERO = -90.0

LOG2E = 1.4426950408889634
BOUND_MARGIN = 1.02
MLA_MAX_SHIFT = 50.0

BF16 = jnp.bfloat16
F32 = jnp.float32
HI_MASK = -65536
CHUNK_SHIFT = CHUNK.bit_length() - 1
assert 1 << CHUNK_SHIFT == CHUNK

PROJ_TM = 512
MERGE_TM = 1024
MLA_TQ, MLA_TK = 512, 512
MLA_HEADS_PER_STEP = 4
SB_TQ, SB_TK, SB_WIN = 256, 128, 4


def _rms(x, g):
    return x * lax.rsqrt(jnp.mean(x * x, axis=-1, keepdims=True) + EPS) * g


def _dot(a, b):
    return jnp.dot(a, b, preferred_element_type=F32)


def _dot_nt(a, b):
    return lax.dot_general(a, b, (((1,), (1,)), ((), ())), preferred_element_type=F32)


def _proj_kernel(x_ref, pos_ref, g1_ref, wcq_ref, wckv_ref, wkr_ref, wsb_ref, gqa_ref, wuq_ref,
                 gkva_ref, wuk_ref, wuv_ref, vone_ref, gq_ref, gk_ref, inv_ref, qaug_ref, kaug_ref,
                 q_ref, k_ref, v_ref, sq_ref, sk_ref, sv_ref):
    x = x_ref[...]
    xn = _rms(x, g1_ref[...]).astype(BF16)

    cq = _rms(_dot(xn, wcq_ref[...]), gqa_ref[...]).astype(BF16)
    q = _dot(cq, wuq_ref[...])
    ckv = _rms(_dot(xn, wckv_ref[...]), gkva_ref[...]).astype(BF16)
    kn = _dot(ckv, wuk_ref[...])
    v_ref[...] = (_dot(ckv, wuv_ref[...]) + vone_ref[...]).astype(BF16)
    kr = _dot(xn, wkr_ref[...])

    sb = _dot(xn, wsb_ref[...])
    sq_ref[...] = (sb[:, :SB_WIDTH] * (1.0 / math.sqrt(SB_HEAD_DIM))).astype(BF16)
    sk_ref[...] = sb[:, SB_WIDTH:2 * SB_WIDTH].astype(BF16)
    sv_ref[...] = sb[:, 2 * SB_WIDTH:].astype(BF16)

    ang = pos_ref[...].astype(F32) * inv_ref[...]
    cos = jnp.cos(ang)
    sin = jnp.sin(ang)
    lane = lax.broadcasted_iota(jnp.int32, ang.shape, 1)
    first_half = lane < MLA_NOPE + HALF_ROPE

    def rope(t):
        up = pltpu.roll(t, SLOT - HALF_ROPE, 1)
        down = pltpu.roll(t, HALF_ROPE, 1)
        return t * cos + jnp.where(first_half, -up, down) * sin

    def head_norm(t, g):
        ss = jnp.sum(t * t, axis=-1, keepdims=True) * (1.0 / MLA_QK)
        return t * lax.rsqrt(ss + EPS) * g

    scale = LOG2E / math.sqrt(MLA_QK)
    for h in range(MLA_HEADS):
        sl = slice(h * SLOT, (h + 1) * SLOT)
        qh = rope(head_norm(q[:, sl], gq_ref[...])) * scale + qaug_ref[...]
        q_ref[:, sl] = qh.astype(BF16)
        kh = rope(head_norm(kn[:, sl] + kr, gk_ref[...])) + kaug_ref[...]
        k_ref[:, sl] = kh.astype(BF16)


def _proj_call(x2, pos_col, weights, tm):
    T = x2.shape[0]
    full = lambda a: pl.BlockSpec(a.shape, lambda i: (0,) * a.ndim)
    row = lambda w: pl.BlockSpec((tm, w), lambda i: (i, 0))
    out_w = (MLA_HEADS * SLOT, MLA_HEADS * SLOT, MLA_HEADS * SLOT, SB_WIDTH, SB_WIDTH, SB_WIDTH)
    return pl.pallas_call(
        _proj_kernel,
        grid=(T // tm,),
        in_specs=[row(D_MODEL), row(1)] + [full(w) for w in weights],
        out_specs=[row(w) for w in out_w],
        out_shape=[jax.ShapeDtypeStruct((T, w), BF16) for w in out_w],
        compiler_params=pltpu.CompilerParams(dimension_semantics=("arbitrary",),
                                             vmem_limit_bytes=VMEM_LIMIT_BYTES),
        name="proj",
    )(x2, pos_col, *weights)


def _mla_kernel(flags_ref, q_ref, k_ref, v_ref, posq_ref, posk_ref, o_ref, acc_ref, m_ref,
                *, tq, tk, nk, hp, online):
    b = pl.program_id(0)
    qi = pl.program_id(2)
    qchunk = jnp.right_shift(posq_ref[0], CHUNK_SHIFT)
    acc_ref[...] = jnp.zeros(acc_ref.shape, F32)
    if online:
        m_ref[...] = jnp.full(m_ref.shape, NEG, F32)

    def block(kb, masked):
        k0 = pl.multiple_of(kb * tk, tk)
        if masked:
            kchunk = jnp.right_shift(posk_ref[0, :, pl.ds(k0, tk)], CHUNK_SHIFT)
            vis = kchunk <= qchunk
        for h in range(hp):
            hs = slice(h * SLOT, (h + 1) * SLOT)
            s = _dot_nt(q_ref[0, :, hs], k_ref[0, pl.ds(k0, tk), hs])
            if masked:
                s = jnp.where(vis, s, NEG)
            vh = v_ref[0, pl.ds(k0, tk), hs]
            if online:
                m_old = m_ref[h]
                m_new = jnp.maximum(m_old, jnp.max(s, axis=-1, keepdims=True))
                p = jnp.exp2(s - m_new)
                acc_ref[h] = jnp.exp2(m_old - m_new) * acc_ref[h] + _dot(p.astype(BF16), vh)
                m_ref[h] = m_new
            else:
                acc_ref[h] += _dot(jnp.exp2(s).astype(BF16), vh)

    def body(kb, carry):
        flag = flags_ref[b, qi, kb]

        @pl.when(flag == 1)
        def _():
            block(kb, False)

        @pl.when(flag == 2)
        def _():
            block(kb, True)

        return carry

    lax.fori_loop(0, nk, body, 0)
    lane = lax.broadcasted_iota(jnp.int32, (tq, LANES), 1)
    for j in range(hp // 2):
        o0 = acc_ref[2 * j] / pltpu.roll(acc_ref[2 * j], MLA_V, 1)
        o1 = acc_ref[2 * j + 1] / pltpu.roll(acc_ref[2 * j + 1], MLA_V, 1)
        o_ref[0, :, j * LANES:(j + 1) * LANES] = jnp.where(lane < MLA_V, o0, pltpu.roll(o1, MLA_V, 1))


def _mla_call(flags, q, k, v, pos_col, pos_row, tq, tk, hp, online):
    B, S, _ = q.shape
    nq, nk = S // tq, S // tk
    grid_spec = pltpu.PrefetchScalarGridSpec(
        num_scalar_prefetch=1,
        grid=(B, MLA_HEADS // hp, nq),
        in_specs=[
            pl.BlockSpec((1, tq, hp * SLOT), lambda b, p, i, f: (b, i, p)),
            pl.BlockSpec((1, S, hp * SLOT), lambda b, p, i, f: (b, 0, p)),
            pl.BlockSpec((1, S, hp * SLOT), lambda b, p, i, f: (b, 0, p)),
            pl.BlockSpec((1, tq, 1), lambda b, p, i, f: (b, i, 0)),
            pl.BlockSpec((1, 1, S), lambda b, p, i, f: (b, 0, 0)),
        ],
        out_specs=pl.BlockSpec((1, tq, hp * MLA_V), lambda b, p, i, f: (b, i, p)),
        scratch_shapes=[pltpu.VMEM((hp, tq, LANES), F32), pltpu.VMEM((hp, tq, 1), F32)],
    )
    return pl.pallas_call(
        functools.partial(_mla_kernel, tq=tq, tk=tk, nk=nk, hp=hp, online=online),
        grid_spec=grid_spec,
        out_shape=jax.ShapeDtypeStruct((B, S, MLA_WIDTH), F32),
        compiler_params=pltpu.CompilerParams(
            dimension_semantics=("arbitrary", "arbitrary", "arbitrary"),
            vmem_limit_bytes=VMEM_LIMIT_BYTES),
        name="mla_attn_online" if online else "mla_attn",
    )(flags, q, k, v, pos_col, pos_row)


def _softplus(z):
    return jnp.maximum(z, 0.0) + jnp.log(1.0 + jnp.exp(-jnp.abs(z)))


def _sb_kernel(kstart_ref, q_ref, k_ref, v_ref, posq_ref, posk_ref, o_ref, run_ref, acc_ref,
               *, tq, tk, win):
    b = pl.program_id(0)
    qi = pl.program_id(2)
    lane = lax.broadcasted_iota(jnp.int32, (tq, LANES), 1)
    row_i = lax.broadcasted_iota(jnp.int32, (tk, tk), 0)
    col_i = lax.broadcasted_iota(jnp.int32, (tk, tk), 1)
    tri = (row_i >= col_i).astype(BF16)
    kstart = kstart_ref[b, qi]
    q_pair = q_ref[0]
    zero = jnp.zeros_like(q_pair)
    q2 = jnp.concatenate([jnp.where(lane < SB_HEAD_DIM, q_pair, zero),
                          jnp.where(lane >= SB_HEAD_DIM, q_pair, zero)], axis=0)
    qpos = jnp.concatenate([posq_ref[0], posq_ref[0]], axis=0)

    def suffix_sums(l1m):
        hi = lax.bitcast_convert_type(
            lax.bitcast_convert_type(l1m, jnp.int32) & jnp.int32(HI_MASK), F32)
        lo = l1m - hi
        both = _dot(jnp.concatenate([hi.astype(BF16), lo.astype(BF16)], axis=0), tri)
        return both[:2 * tq] + both[2 * tq:]

    run_ref[...] = jnp.zeros(run_ref.shape, F32)
    acc_ref[...] = jnp.zeros(acc_ref.shape, F32)

    @pl.when(kstart >= win - 1)
    def _():
        k0 = pl.multiple_of((kstart - (win - 1)) * tk, tk)
        causal = posk_ref[0, :, pl.ds(k0, win * tk)] < qpos
        z = _dot_nt(q2, k_ref[0, pl.ds(k0, win * tk), :])
        l1m = jnp.where(causal, -_softplus(z), 0.0)
        run = jnp.zeros((2 * tq, 1), F32)
        a_blocks = [None] * win
        for j in reversed(range(win)):
            cs = slice(j * tk, (j + 1) * tk)
            suffix = suffix_sums(l1m[:, cs])
            a = jnp.exp(z[:, cs] + suffix + run)
            a_blocks[j] = jnp.where(causal[:, cs], a, 0.0).astype(BF16)
            run = run + suffix[:, 0:1]
        acc_ref[...] = _dot(jnp.concatenate(a_blocks, axis=1), v_ref[0, pl.ds(k0, win * tk), :])
        run_ref[...] = run

    def block(kb):
        k0 = pl.multiple_of(kb * tk, tk)
        z = _dot_nt(q2, k_ref[0, pl.ds(k0, tk), :])
        causal = posk_ref[0, :, pl.ds(k0, tk)] < qpos
        l1m = jnp.where(causal, -_softplus(z), 0.0)
        suffix = suffix_sums(l1m)
        run = run_ref[...]
        a = jnp.where(causal, jnp.exp(z + suffix + run), 0.0)
        acc_ref[...] += _dot(a.astype(BF16), v_ref[0, pl.ds(k0, tk), :])
        run_new = run + suffix[:, 0:1]
        run_ref[...] = run_new
        return jnp.max(run_new)

    def cond(c):
        kb, top = c
        return (kb >= 0) & (top >= SB_LOG_ZERO)

    def body(c):
        kb, _ = c
        return kb - 1, block(kb)

    kb_first = jnp.where(kstart >= win - 1, kstart - win, kstart)
    lax.while_loop(cond, body, (kb_first, jnp.max(run_ref[...])))

    o_ref[0] = jnp.where(lane < SB_HEAD_DIM, acc_ref[:tq], acc_ref[tq:])


def _sb_call(kstart, q, k, v, pos_col, pos_row, tq, tk, win):
    B, S, _ = q.shape
    nq = S // tq
    grid_spec = pltpu.PrefetchScalarGridSpec(
        num_scalar_prefetch=1,
        grid=(B, SB_HEADS // 2, nq),
        in_specs=[
            pl.BlockSpec((1, tq, LANES), lambda b, p, i, s: (b, i, p)),
            pl.BlockSpec((1, S, LANES), lambda b, p, i, s: (b, 0, p)),
            pl.BlockSpec((1, S, LANES), lambda b, p, i, s: (b, 0, p)),
            pl.BlockSpec((1, tq, 1), lambda b, p, i, s: (b, i, 0)),
            pl.BlockSpec((1, 1, S), lambda b, p, i, s: (b, 0, 0)),
        ],
        out_specs=pl.BlockSpec((1, tq, LANES), lambda b, p, i, s: (b, i, p)),
        scratch_shapes=[pltpu.VMEM((2 * tq, 1), F32), pltpu.VMEM((2 * tq, LANES), F32)],
    )
    return pl.pallas_call(
        functools.partial(_sb_kernel, tq=tq, tk=tk, win=win),
        grid_spec=grid_spec,
        out_shape=jax.ShapeDtypeStruct((B, S, SB_WIDTH), F32),
        compiler_params=pltpu.CompilerParams(
            dimension_semantics=("arbitrary", "arbitrary", "arbitrary"),
            vmem_limit_bytes=VMEM_LIMIT_BYTES),
        name="sb_attn",
    )(kstart, q, k, v, pos_col, pos_row)


def _round_robin(*gens):
    gens = list(gens)
    while gens:
        for g in list(gens):
            if next(g, StopIteration) is StopIteration:
                gens.remove(g)


def _attn_kernel(flags_ref, kstart_ref, q_ref, k_ref, v_ref, sq_ref, sk_ref, sv_ref, posq_ref,
                 posk_ref, om_ref, os_ref, acc_ref, run_ref, sacc_ref, *, tq, tk, nk, hp):
    b = pl.program_id(0)
    qi = pl.program_id(2)
    stq, stk, win = SB_TQ, SB_TK, SB_WIN
    halves = tq // stq
    units = halves * (hp // 2)
    qchunk = jnp.right_shift(posq_ref[0], CHUNK_SHIFT)
    acc_ref[...] = jnp.zeros(acc_ref.shape, F32)
    lane = lax.broadcasted_iota(jnp.int32, (stq, LANES), 1)
    tri2 = (lax.broadcasted_iota(jnp.int32, (2 * stk, 2 * stk), 0)
            >= lax.broadcasted_iota(jnp.int32, (2 * stk, 2 * stk), 1)).astype(BF16)
    tri = tri2[:stk, :stk]

    def mla_heads(kb, masked):
        k0 = pl.multiple_of(kb * tk, tk)
        if masked:
            kchunk = jnp.right_shift(posk_ref[0, :, pl.ds(k0, tk)], CHUNK_SHIFT)
            vis = kchunk <= qchunk
        for h in range(hp):
            hs = slice(h * SLOT, (h + 1) * SLOT)
            s = _dot_nt(q_ref[0, :, hs], k_ref[0, pl.ds(k0, tk), hs])
            if masked:
                s = jnp.where(vis, s, NEG)
            acc_ref[h] += _dot(jnp.exp2(s).astype(BF16), v_ref[0, pl.ds(k0, tk), hs])
            yield

    def sb_operands(u):
        rows = pl.ds(pl.multiple_of((u % halves) * stq, stq), stq)
        cols = pl.ds(pl.multiple_of((u // halves) * LANES, LANES), LANES)
        q_pair = sq_ref[0, rows, cols]
        zero = jnp.zeros_like(q_pair)
        q2 = jnp.concatenate([jnp.where(lane < SB_HEAD_DIM, q_pair, zero),
                              jnp.where(lane >= SB_HEAD_DIM, q_pair, zero)], axis=0)
        qpos = jnp.concatenate([posq_ref[0, rows], posq_ref[0, rows]], axis=0)
        return q2, qpos, rows, cols

    def suffix_sums(l1m, triangle):
        hi = lax.bitcast_convert_type(
            lax.bitcast_convert_type(l1m, jnp.int32) & jnp.int32(HI_MASK), F32)
        lo = l1m - hi
        both = _dot(jnp.concatenate([hi.astype(BF16), lo.astype(BF16)], axis=0), triangle)
        return both[:2 * stq] + both[2 * stq:]

    def sb_fast(u, kstart):
        q2, qpos, _, cols = sb_operands(u)
        k0 = pl.multiple_of((kstart - (win - 1)) * stk, stk)
        causal = posk_ref[0, :, pl.ds(k0, win * stk)] < qpos
        z = _dot_nt(q2, sk_ref[0, pl.ds(k0, win * stk), cols])
        l1m = jnp.where(causal, -_softplus(z), 0.0)
        yield
        run = jnp.zeros((2 * stq, 1), F32)
        a_blocks = [None] * (win // 2)
        for j in reversed(range(win // 2)):
            cs = slice(j * 2 * stk, (j + 1) * 2 * stk)
            suffix = suffix_sums(l1m[:, cs], tri2)
            a = jnp.exp(z[:, cs] + suffix + run)
            a_blocks[j] = jnp.where(causal[:, cs], a, 0.0).astype(BF16)
            run = run + suffix[:, 0:1]
            yield
        sacc_ref[...] = _dot(jnp.concatenate(a_blocks, axis=1), sv_ref[0, pl.ds(k0, win * stk), cols])
        run_ref[...] = run
        yield

    def sb_finish(u, kstart, fast):
        q2, qpos, rows, cols = sb_operands(u)

        def block(kb):
            k0 = pl.multiple_of(kb * stk, stk)
            z = _dot_nt(q2, sk_ref[0, pl.ds(k0, stk), cols])
            causal = posk_ref[0, :, pl.ds(k0, stk)] < qpos
            l1m = jnp.where(causal, -_softplus(z), 0.0)
            suffix = suffix_sums(l1m, tri)
            run = run_ref[...]
            a = jnp.where(causal, jnp.exp(z + suffix + run), 0.0)
            sacc_ref[...] += _dot(a.astype(BF16), sv_ref[0, pl.ds(k0, stk), cols])
            run_new = run + suffix[:, 0:1]
            run_ref[...] = run_new
            return jnp.max(run_new)

        def cond(c):
            kb, top = c
            return (kb >= 0) & (top >= SB_LOG_ZERO)

        def body(c):
            kb, _ = c
            return kb - 1, block(kb)

        kb_first = jnp.where(fast, kstart - win, kstart)
        lax.while_loop(cond, body, (kb_first, jnp.max(run_ref[...])))
        os_ref[0, rows, cols] = jnp.where(lane < SB_HEAD_DIM, sacc_ref[:stq], sacc_ref[stq:])

    def with_unit(kb, carry):
        flag = flags_ref[b, qi, kb]
        kstart = kstart_ref[b, qi * halves + kb % halves]
        fast = kstart >= win - 1
        run_ref[...] = jnp.zeros(run_ref.shape, F32)
        sacc_ref[...] = jnp.zeros(sacc_ref.shape, F32)

        @pl.when((flag == 1) & fast)
        def _():
            _round_robin(sb_fast(kb, kstart), mla_heads(kb, False))

        @pl.when((flag == 2) & fast)
        def _():
            _round_robin(sb_fast(kb, kstart), mla_heads(kb, True))

        @pl.when((flag == 0) & fast)
        def _():
            _round_robin(sb_fast(kb, kstart))

        @pl.when((flag == 1) & jnp.logical_not(fast))
        def _():
            _round_robin(mla_heads(kb, False))

        @pl.when((flag == 2) & jnp.logical_not(fast))
        def _():
            _round_robin(mla_heads(kb, True))

        sb_finish(kb, kstart, fast)
        return carry

    def mla_only(kb, carry):
        flag = flags_ref[b, qi, kb]

        @pl.when(flag == 1)
        def _():
            _round_robin(mla_heads(kb, False))

        @pl.when(flag == 2)
        def _():
            _round_robin(mla_heads(kb, True))

        return carry

    lax.fori_loop(0, units, with_unit, 0)
    lax.fori_loop(units, nk, mla_only, 0)
    lane_q = lax.broadcasted_iota(jnp.int32, (tq, LANES), 1)
    for j in range(hp // 2):
        o0 = acc_ref[2 * j] / pltpu.roll(acc_ref[2 * j], MLA_V, 1)
        o1 = acc_ref[2 * j + 1] / pltpu.roll(acc_ref[2 * j + 1], MLA_V, 1)
        om_ref[0, :, j * LANES:(j + 1) * LANES] = jnp.where(lane_q < MLA_V, o0, pltpu.roll(o1, MLA_V, 1))


def _attn_call(flags, kstart, q, k, v, sq, sk, sv, pos_col, pos_row, tq, tk, hp):
    B, S, _ = q.shape
    nq, nk = S // tq, S // tk
    assert (tq // SB_TQ) * (hp // 2) <= nk and SB_HEADS == MLA_HEADS and SB_WIN % 2 == 0
    sbw = (hp // 2) * LANES
    grid_spec = pltpu.PrefetchScalarGridSpec(
        num_scalar_prefetch=2,
        grid=(B, MLA_HEADS // hp, nq),
        in_specs=[
            pl.BlockSpec((1, tq, hp * SLOT), lambda b, p, i, f, s: (b, i, p)),
            pl.BlockSpec((1, S, hp * SLOT), lambda b, p, i, f, s: (b, 0, p)),
            pl.BlockSpec((1, S, hp * SLOT), lambda b, p, i, f, s: (b, 0, p)),
            pl.BlockSpec((1, tq, sbw), lambda b, p, i, f, s: (b, i, p)),
            pl.BlockSpec((1, S, sbw), lambda b, p, i, f, s: (b, 0, p)),
            pl.BlockSpec((1, S, sbw), lambda b, p, i, f, s: (b, 0, p)),
            pl.BlockSpec((1, tq, 1), lambda b, p, i, f, s: (b, i, 0)),
            pl.BlockSpec((1, 1, S), lambda b, p, i, f, s: (b, 0, 0)),
        ],
        out_specs=[pl.BlockSpec((1, tq, hp * MLA_V), lambda b, p, i, f, s: (b, i, p)),
                   pl.BlockSpec((1, tq, sbw), lambda b, p, i, f, s: (b, i, p))],
        scratch_shapes=[pltpu.VMEM((hp, tq, LANES), F32), pltpu.VMEM((2 * SB_TQ, 1), F32),
                        pltpu.VMEM((2 * SB_TQ, LANES), F32)],
    )
    return pl.pallas_call(
        functools.partial(_attn_kernel, tq=tq, tk=tk, nk=nk, hp=hp),
        grid_spec=grid_spec,
        out_shape=[jax.ShapeDtypeStruct((B, S, MLA_WIDTH), F32),
                   jax.ShapeDtypeStruct((B, S, SB_WIDTH), F32)],
        compiler_params=pltpu.CompilerParams(
            dimension_semantics=("arbitrary", "arbitrary", "arbitrary"),
            vmem_limit_bytes=VMEM_LIMIT_BYTES),
        name="attn",
    )(flags, kstart, q, k, v, sq, sk, sv, pos_col, pos_row)


def _split_bf16(a):
    hi = a.astype(BF16)
    lo = (a - hi.astype(F32)).astype(BF16)
    return hi, lo


def _merge_kernel(x_ref, om_ref, os_ref, gm_ref, gs_ref, wo_ref, g2_ref, wrh_ref, wrl_ref,
                  bias_ref, wsgu_ref, wsd_ref, base_ref, hn_ref, eidx_ref, gate_ref, rank_ref,
                  cnt_ref):
    mm = _rms(om_ref[...], gm_ref[...]).astype(BF16)
    ms = _rms(os_ref[...], gs_ref[...]).astype(BF16)
    h = x_ref[...] + _dot(mm, wo_ref[:MLA_WIDTH, :]) + _dot(ms, wo_ref[MLA_WIDTH:, :])
    hn = _rms(h, g2_ref[...])
    hn_hi, hn_lo = _split_bf16(hn)
    bits = lax.bitcast_convert_type(hn_hi.astype(F32), jnp.int32)
    hn_ref[...] = (lax.shift_right_logical(bits[:, :D_MODEL // 2], 16)
                   | (bits[:, D_MODEL // 2:] & jnp.int32(HI_MASK)))

    logits = (_dot_nt(wrh_ref[...], hn_hi) + _dot_nt(wrh_ref[...], hn_lo)
              + _dot_nt(wrl_ref[...], hn_hi))
    scores = jax.nn.sigmoid(logits)
    work = scores + bias_ref[...]
    eidx = lax.broadcasted_iota(jnp.int32, work.shape, 0)
    chosen, chosen_score, hits = [], [], []
    for _ in range(TOP_K):
        top = jnp.max(work, axis=0, keepdims=True)
        first = jnp.min(jnp.where(work == top, eidx, N_EXPERTS), axis=0, keepdims=True)
        hit = eidx == first
        hits.append(hit)
        chosen.append(first)
        chosen_score.append(jnp.sum(jnp.where(hit, scores, 0.0), axis=0, keepdims=True))
        work = jnp.where(hit, -jnp.inf, work)
    sel = jnp.concatenate(chosen_score, axis=0)
    eidx_ref[...] = jnp.concatenate(chosen, axis=0)
    gate_ref[...] = sel / jnp.sum(sel, axis=0, keepdims=True) * ROUTED_SCALE

    tm = work.shape[1]
    member = functools.reduce(jnp.logical_or, hits)
    member = jnp.where(member, 1.0, 0.0)
    tri = (lax.broadcasted_iota(jnp.int32, (tm, tm), 0)
           <= lax.broadcasted_iota(jnp.int32, (tm, tm), 1)).astype(BF16)
    upto = _dot(member.astype(BF16), tri)
    before = upto - member
    rank_ref[...] = jnp.concatenate(
        [jnp.sum(jnp.where(hit, before, 0.0), axis=0, keepdims=True) for hit in hits],
        axis=0).astype(jnp.int32)
    cnt_ref[0] = upto[:, tm - 1:tm].astype(jnp.int32)

    sgu = _dot(hn_hi, wsgu_ref[...])
    act = (jax.nn.silu(sgu[:, :SHARED_FF]) * sgu[:, SHARED_FF:]).astype(BF16)
    base_ref[...] = h + _dot(act, wsd_ref[...])


def _merge_call(x2, o_mla, o_sb, gm, gs, wo, g2, wrh, wrl, bias, wsgu, wsd, tm):
    T = x2.shape[0]
    full = lambda a: pl.BlockSpec(a.shape, lambda i: (0,) * a.ndim)
    row = lambda w: pl.BlockSpec((tm, w), lambda i: (i, 0))
    weights = (gm, gs, wo, g2, wrh, wrl, bias, wsgu, wsd)
    topk = pl.BlockSpec((TOP_K, tm), lambda i: (0, i))
    return pl.pallas_call(
        _merge_kernel,
        grid=(T // tm,),
        in_specs=[row(D_MODEL), row(MLA_WIDTH), row(SB_WIDTH)] + [full(w) for w in weights],
        out_specs=[row(D_MODEL), row(D_MODEL // 2), topk, topk, topk,
                   pl.BlockSpec((1, N_EXPERTS, 1), lambda i: (i, 0, 0))],
        out_shape=[jax.ShapeDtypeStruct((T, D_MODEL), F32),
                   jax.ShapeDtypeStruct((T, D_MODEL // 2), jnp.int32),
                   jax.ShapeDtypeStruct((TOP_K, T), jnp.int32),
                   jax.ShapeDtypeStruct((TOP_K, T), F32),
                   jax.ShapeDtypeStruct((TOP_K, T), jnp.int32),
                   jax.ShapeDtypeStruct((T // tm, N_EXPERTS, 1), jnp.int32)],
        compiler_params=pltpu.CompilerParams(dimension_semantics=("arbitrary",),
                                             vmem_limit_bytes=VMEM_LIMIT_BYTES),
        name="merge",
    )(x2, o_mla, o_sb, *weights)


MOE_TB = 2048
MOE_RT = 64
MOE_MAX_TILES = 6
MOE_ROWS = MOE_TB * TOP_K + N_EXPERTS * (MOE_RT - 1)
MOE_PACK = D_MODEL // 2
MOE_SUB = MOE_PACK // LANES
MOE_UNROLL = 8
MOE_XCH = 256
MOE_OCH = 128
MOE_WBUF = 4
MOE_WSPLIT = 4
MOE_CLEAR_ROWS = 1024


def _unpack_lo(w):
    return lax.bitcast_convert_type(w << 16, F32)


def _unpack_hi(w):
    return lax.bitcast_convert_type(w & jnp.int32(HI_MASK), F32)


def _moe_kernel(off_ref, nt_ref, pos_ref, gate_ref, xp_hbm, base_hbm, wgu_hbm, wd_hbm, o_hbm,
                xy_ref, xbuf, bbuf, obuf, wgu_buf, wd_buf, xsem, bsem, osem, wsem):
    i = pl.program_id(0)

    n_visits = pl.num_programs(0) * N_EXPERTS

    def w_copies(step):
        slot = step % MOE_WBUF
        ex = step % N_EXPERTS
        copies = []
        for hbm, buf, sem, rows in ((wgu_hbm, wgu_buf, 0, D_MODEL), (wd_hbm, wd_buf, 1, EXPERT_FF)):
            part = rows // MOE_WSPLIT
            for j in range(MOE_WSPLIT):
                rs = pl.ds(j * part, part)
                copies.append(pltpu.make_async_copy(hbm.at[ex, rs], buf.at[slot, rs], wsem.at[sem, slot]))
        return copies

    @pl.when(i == 0)
    def _():
        for ahead in range(MOE_WBUF - 1):
            for cp in w_copies(ahead):
                cp.start()

        def clear(r, c):
            xy_ref[pl.ds(pl.multiple_of(r * MOE_CLEAR_ROWS, MOE_CLEAR_ROWS), MOE_CLEAR_ROWS), :] = (
                jnp.zeros((MOE_CLEAR_ROWS, LANES), jnp.int32))
            return c

        lax.fori_loop(0, MOE_ROWS * MOE_SUB // MOE_CLEAR_ROWS, clear, 0)
        tail = MOE_ROWS * MOE_SUB % MOE_CLEAR_ROWS
        if tail:
            xy_ref[pl.ds(MOE_ROWS * MOE_SUB - tail, tail), :] = jnp.zeros((tail, LANES), jnp.int32)

    def x_copy(c, slot):
        return pltpu.make_async_copy(
            xp_hbm.at[pl.ds(i * MOE_TB + c * MOE_XCH, MOE_XCH), :], xbuf.at[slot], xsem.at[slot])

    def base_copy(c, slot):
        return pltpu.make_async_copy(
            base_hbm.at[pl.ds(i * MOE_TB + c * MOE_OCH, MOE_OCH), :], bbuf.at[slot], bsem.at[slot])

    def out_copy(c, slot):
        return pltpu.make_async_copy(
            obuf.at[slot], o_hbm.at[pl.ds(i * MOE_TB + c * MOE_OCH, MOE_OCH), :], osem.at[slot])

    def dispatch():
        n_chunks = MOE_TB // MOE_XCH
        x_copy(0, 0).start()
        for c in range(n_chunks):
            slot = c % 2
            if c + 1 < n_chunks:
                x_copy(c + 1, 1 - slot).start()
            x_copy(c, slot).wait()

            def step(tt, carry, c=c, slot=slot):
                slab = xbuf[slot, pl.ds(pl.multiple_of(tt * MOE_UNROLL, MOE_UNROLL), MOE_UNROLL), :]
                for u in range(MOE_UNROLL):
                    t = tt * MOE_UNROLL + u
                    row = jnp.concatenate([slab[u:u + 1, j * LANES:(j + 1) * LANES]
                                           for j in range(MOE_SUB)], axis=0)
                    for k in range(TOP_K):
                        p = pl.multiple_of(pos_ref[TOP_K * (c * MOE_XCH + t) + k], MOE_SUB)
                        xy_ref[pl.ds(p, MOE_SUB), :] = row
                return carry

            lax.fori_loop(0, MOE_XCH // MOE_UNROLL, step, 0)

    def ffn(r0, m, wslot):
        base_row = pl.multiple_of(MOE_SUB * r0, 8)
        words = jnp.concatenate(
            [xy_ref[pl.ds(base_row + j, m, stride=MOE_SUB), :] for j in range(MOE_SUB)], axis=1)
        gu = (_dot(_unpack_lo(words).astype(BF16), wgu_buf[wslot, :MOE_PACK, :])
              + _dot(_unpack_hi(words).astype(BF16), wgu_buf[wslot, MOE_PACK:, :]))
        act = (jax.nn.silu(gu[:, :EXPERT_FF]) * gu[:, EXPERT_FF:]).astype(BF16)
        y = _dot(act, wd_buf[wslot])
        ya = lax.bitcast_convert_type(y[:, :MOE_PACK].astype(BF16).astype(F32), jnp.int32)
        yb = lax.bitcast_convert_type(y[:, MOE_PACK:].astype(BF16).astype(F32), jnp.int32)
        packed = lax.shift_right_logical(ya, 16) | yb
        for j in range(MOE_SUB):
            xy_ref[pl.ds(base_row + j, m, stride=MOE_SUB), :] = packed[:, j * LANES:(j + 1) * LANES]

    def expert(e, carry):
        visit = i * N_EXPERTS + e

        @pl.when(visit + (MOE_WBUF - 1) < n_visits)
        def _():
            for cp in w_copies(visit + (MOE_WBUF - 1)):
                cp.start()

        for cp in w_copies(visit):
            cp.wait()
        wslot = visit % MOE_WBUF
        off = off_ref[visit]
        nt = nt_ref[visit]

        def chunk(c, inner):
            ffn(off + c * (MOE_MAX_TILES * MOE_RT), MOE_MAX_TILES * MOE_RT, wslot)
            return inner

        full = nt // MOE_MAX_TILES
        lax.fori_loop(0, full, chunk, 0)
        rest = nt - full * MOE_MAX_TILES
        for tiles in range(1, MOE_MAX_TILES):
            @pl.when(rest == tiles)
            def _():
                ffn(off + full * (MOE_MAX_TILES * MOE_RT), tiles * MOE_RT, wslot)

        return carry

    def combine():
        n_chunks = MOE_TB // MOE_OCH
        base_copy(0, 0).start()
        for c in range(n_chunks):
            slot = c % 2
            if c + 1 < n_chunks:
                base_copy(c + 1, 1 - slot).start()
            base_copy(c, slot).wait()
            if c >= 2:
                out_copy(c - 2, slot).wait()

            def step(tt, carry, c=c, slot=slot):
                for u in range(MOE_UNROLL):
                    t = tt * MOE_UNROLL + u
                    lo = jnp.zeros((MOE_SUB, LANES), F32)
                    hi = jnp.zeros((MOE_SUB, LANES), F32)
                    for k in range(TOP_K):
                        s = TOP_K * (c * MOE_OCH + t) + k
                        w = xy_ref[pl.ds(pl.multiple_of(pos_ref[s], MOE_SUB), MOE_SUB), :]
                        g = gate_ref[s]
                        lo = lo + g * _unpack_lo(w)
                        hi = hi + g * _unpack_hi(w)
                    routed = jnp.concatenate([lo[j:j + 1] for j in range(MOE_SUB)]
                                             + [hi[j:j + 1] for j in range(MOE_SUB)], axis=1)
                    obuf[slot, pl.ds(t, 1), :] = bbuf[slot, pl.ds(t, 1), :] + routed
                return carry

            lax.fori_loop(0, MOE_OCH // MOE_UNROLL, step, 0)
            out_copy(c, slot).start()
        out_copy(n_chunks - 2, n_chunks % 2).wait()
        out_copy(n_chunks - 1, (n_chunks - 1) % 2).wait()

    dispatch()
    lax.fori_loop(0, N_EXPERTS, expert, 0)
    combine()


def _moe_call(off, nt, pos, gates, xp, base, wgu, wd):
    T = base.shape[0]
    slots = MOE_TB * TOP_K
    grid_spec = pltpu.PrefetchScalarGridSpec(
        num_scalar_prefetch=2,
        grid=(T // MOE_TB,),
        in_specs=[
            pl.BlockSpec((slots,), lambda i, o, n: (i,), memory_space=pltpu.SMEM),
            pl.BlockSpec((slots,), lambda i, o, n: (i,), memory_space=pltpu.SMEM),
            pl.BlockSpec(memory_space=pl.ANY),
            pl.BlockSpec(memory_space=pl.ANY),
            pl.BlockSpec(memory_space=pl.ANY),
            pl.BlockSpec(memory_space=pl.ANY),
        ],
        out_specs=pl.BlockSpec(memory_space=pl.ANY),
        scratch_shapes=[
            pltpu.VMEM((MOE_ROWS * MOE_SUB, LANES), jnp.int32),
            pltpu.VMEM((2, MOE_XCH, MOE_PACK), jnp.int32),
            pltpu.VMEM((2, MOE_OCH, D_MODEL), F32),
            pltpu.VMEM((2, MOE_OCH, D_MODEL), F32),
            pltpu.VMEM((MOE_WBUF, D_MODEL, 2 * EXPERT_FF), BF16),
            pltpu.VMEM((MOE_WBUF, EXPERT_FF, D_MODEL), BF16),
            pltpu.SemaphoreType.DMA((2,)),
            pltpu.SemaphoreType.DMA((2,)),
            pltpu.SemaphoreType.DMA((2,)),
            pltpu.SemaphoreType.DMA((2, MOE_WBUF)),
        ],
    )
    return pl.pallas_call(
        _moe_kernel,
        grid_spec=grid_spec,
        out_shape=jax.ShapeDtypeStruct((T, D_MODEL), F32),
        compiler_params=pltpu.CompilerParams(dimension_semantics=("arbitrary",),
                                             vmem_limit_bytes=VMEM_LIMIT_BYTES),
        name="moe",
    )(off, nt, pos, gates, xp, base, wgu, wd)


def _route_tables(eidx, gate, rank, cnt):
    T = eidx.shape[1]
    nblk = T // MOE_TB
    cnt = cnt.reshape(nblk, -1, N_EXPERTS)
    per_blk = cnt.shape[1]
    earlier = jnp.cumsum(cnt, axis=1) - cnt
    tiles = (cnt.sum(axis=1) + MOE_RT - 1) // MOE_RT
    off = jnp.cumsum(tiles, axis=1) * MOE_RT - tiles * MOE_RT
    start = (off[:, None, :] + earlier).reshape(nblk * per_blk, N_EXPERTS)
    e_tk = eidx.T.reshape(nblk * per_blk, T // (nblk * per_blk), TOP_K)
    chosen = e_tk[..., None] == jnp.arange(N_EXPERTS, dtype=jnp.int32)
    pos = jnp.where(chosen, start[:, None, None, :], 0).sum(axis=-1).reshape(T, TOP_K) + rank.T
    return (off.reshape(-1).astype(jnp.int32), tiles.reshape(-1).astype(jnp.int32),
            (pos * MOE_SUB).reshape(-1).astype(jnp.int32), gate.T.reshape(-1))


def _slot_cols(w, width):
    k = w.shape[0]
    w = w.reshape(k, MLA_HEADS, width)
    return jnp.pad(w, ((0, 0), (0, 0), (0, SLOT - width))).reshape(k, MLA_HEADS * SLOT)


def _block_flags(pos, tq, tk, chunked):
    B, S = pos.shape
    p = jnp.right_shift(pos, CHUNK_SHIFT) if chunked else pos
    qmin = p.reshape(B, S // tq, tq).min(-1)[:, :, None]
    qmax = p.reshape(B, S // tq, tq).max(-1)[:, :, None]
    kmin = p.reshape(B, S // tk, tk).min(-1)[:, None, :]
    kmax = p.reshape(B, S // tk, tk).max(-1)[:, None, :]
    if chunked:
        none, all_ = kmin > qmax, kmax <= qmin
    else:
        none, all_ = kmin >= qmax, kmax < qmin
    return jnp.where(none, 0, jnp.where(all_, 1, 2)).astype(jnp.int32)


def kernel(x, positions, norm1, w_in, q_a_norm, w_uq, kv_a_norm, w_ukv, q_norm, k_norm,
           out_norm_mla, out_norm_sb, w_o, norm2, w_router, router_bias, w_gate_up, w_down,
           w_shared_gate_up, w_shared_down):
    B, S, D = x.shape
    T = B * S
    c0 = Q_LORA
    c1 = c0 + KV_LORA
    c2 = c1 + MLA_ROPE

    row2 = lambda g: g.reshape(1, -1).astype(F32)
    wcq = w_in[:, :c0].astype(BF16)
    wckv = w_in[:, c0:c1].astype(BF16)
    wkr = jnp.pad(w_in[:, c1:c2], ((0, 0), (MLA_NOPE, SLOT - MLA_QK))).astype(BF16)
    wsb = w_in[:, c2:].astype(BF16)
    wuq = _slot_cols(w_uq, MLA_QK).astype(BF16)
    w_ukv3 = w_ukv.reshape(KV_LORA, MLA_HEADS, MLA_NOPE + MLA_V)
    wuk = _slot_cols(w_ukv3[:, :, :MLA_NOPE].reshape(KV_LORA, -1), MLA_NOPE).astype(BF16)
    wuv = _slot_cols(w_ukv3[:, :, MLA_NOPE:].reshape(KV_LORA, MLA_WIDTH), MLA_V).astype(BF16)
    slot_lane = jnp.arange(SLOT)
    vone = jnp.tile((slot_lane >= MLA_V).astype(F32), MLA_HEADS).reshape(1, MLA_HEADS * SLOT)
    gq = jnp.pad(q_norm, (0, SLOT - MLA_QK)).reshape(1, SLOT)
    gk = jnp.pad(k_norm, (0, SLOT - MLA_QK)).reshape(1, SLOT)
    inv = ROPE_THETA ** (-jnp.arange(HALF_ROPE, dtype=F32) / HALF_ROPE)
    inv_slot = jnp.pad(jnp.concatenate([inv, inv]), (MLA_NOPE, SLOT - MLA_QK)).reshape(1, SLOT)

    bound = (math.sqrt(MLA_QK) * LOG2E * BOUND_MARGIN) * jnp.max(jnp.abs(q_norm)) * jnp.max(jnp.abs(k_norm))
    bounded = bound <= MLA_MAX_SHIFT
    shift = jnp.where(bounded, bound, 0.0)
    qaug = (slot_lane == MLA_QK).astype(F32).reshape(1, SLOT)
    kaug = -shift * qaug

    x2 = x.reshape(T, D)
    pos_col = positions.reshape(T, 1)
    q, k, v, sq, sk, sv = _proj_call(
        x2, pos_col,
        (row2(norm1), wcq, wckv, wkr, wsb, row2(q_a_norm), wuq, row2(kv_a_norm), wuk, wuv, vone,
         gq, gk, inv_slot, qaug, kaug), tm=PROJ_TM)

    pos_c3 = positions.reshape(B, S, 1)
    pos_r3 = positions.reshape(B, 1, S)
    r3 = lambda a: a.reshape(B, S, a.shape[-1])

    mla_flags = _block_flags(positions, MLA_TQ, MLA_TK, True)
    sb_flags = _block_flags(positions, SB_TQ, SB_TK, False)
    kstart = jnp.max(jnp.where(sb_flags != 0, jnp.arange(S // SB_TK, dtype=jnp.int32), -1),
                     axis=-1).astype(jnp.int32)
    attn_args = (mla_flags, kstart, r3(q), r3(k), r3(v), r3(sq), r3(sk), r3(sv), pos_c3, pos_r3)

    def separate(mla_flags, kstart, q, k, v, sq, sk, sv, pos_c3, pos_r3):
        o_mla = _mla_call(mla_flags, q, k, v, pos_c3, pos_r3, MLA_TQ, MLA_TK, hp=2, online=True)
        o_sb = _sb_call(kstart, sq, sk, sv, pos_c3, pos_r3, SB_TQ, SB_TK, win=SB_WIN)
        return o_mla, o_sb

    o_mla, o_sb = lax.cond(
        bounded,
        lambda *a: tuple(_attn_call(*a, MLA_TQ, MLA_TK, hp=MLA_HEADS_PER_STEP)),
        separate, *attn_args)

    wr_t = w_router.T
    wrh = wr_t.astype(BF16)
    wrl = (wr_t - wrh.astype(F32)).astype(BF16)
    base, hn, eidx, gate, rank, cnt = _merge_call(
        x2, o_mla.reshape(T, MLA_WIDTH), o_sb.reshape(T, SB_WIDTH), row2(out_norm_mla),
        row2(out_norm_sb), w_o.astype(BF16), row2(norm2), wrh, wrl,
        router_bias.reshape(N_EXPERTS, 1).astype(F32), w_shared_gate_up.astype(BF16),
        w_shared_down.astype(BF16), tm=MERGE_TM)

    off, tiles, pos, gates = _route_tables(eidx, gate, rank, cnt)
    out = _moe_call(off, tiles, pos, gates, hn, base, w_gate_up.astype(BF16),
                    w_down.astype(BF16))
    return out.reshape(B, S, D)
```

```python
import functools
import math

import jax
import jax.numpy as jnp
from jax import lax
from jax.experimental import pallas as pl
from jax.experimental.pallas import tpu as pltpu

D_MODEL = 1024
CHUNK = 64
MLA_HEADS = 8
MLA_NOPE = 64
MLA_ROPE = 32
MLA_QK = MLA_NOPE + MLA_ROPE
MLA_V = 64
Q_LORA = 256
KV_LORA = 128
ROPE_THETA = 10000.0
SB_HEADS = 8
SB_HEAD_DIM = 64
SB_WIDTH = SB_HEADS * SB_HEAD_DIM
MLA_WIDTH = MLA_HEADS * MLA_V
N_EXPERTS = 64
TOP_K = 8
EXPERT_FF = 256
SHARED_FF = 256
ROUTED_SCALE = 2.5
EPS = 1e-6
NEG = -1e30

LANES = 128
VMEM_LIMIT_BYTES = 56 * 1024 * 1024

SLOT = LANES
HALF_ROPE = MLA_ROPE // 2

SB_LOG_ZERO = -90.0

LOG2E = 1.4426950408889634
BOUND_MARGIN = 1.02
MLA_MAX_SHIFT = 50.0

BF16 = jnp.bfloat16
F32 = jnp.float32
HI_MASK = -65536
CHUNK_SHIFT = CHUNK.bit_length() - 1
assert 1 << CHUNK_SHIFT == CHUNK

PROJ_TM = 512
MERGE_TM = 1024
MLA_TQ, MLA_TK = 512, 512
MLA_HEADS_PER_STEP = 4
SB_TQ, SB_TK, SB_WIN = 256, 128, 4


def _rms(x, g):
    return x * lax.rsqrt(jnp.mean(x * x, axis=-1, keepdims=True) + EPS) * g


def _dot(a, b):
    return jnp.dot(a, b, preferred_element_type=F32)


def _dot_nt(a, b):
    return lax.dot_general(a, b, (((1,), (1,)), ((), ())), preferred_element_type=F32)


def _proj_kernel(x_ref, pos_ref, g1_ref, wcq_ref, wckv_ref, wkr_ref, wsb_ref, gqa_ref, wuq_ref,
                 gkva_ref, wuk_ref, wuv_ref, vone_ref, gq_ref, gk_ref, inv_ref, qaug_ref, kaug_ref,
                 q_ref, k_ref, v_ref, sq_ref, sk_ref, sv_ref):
    x = x_ref[...]
    xn = _rms(x, g1_ref[...]).astype(BF16)

    cq = _rms(_dot(xn, wcq_ref[...]), gqa_ref[...]).astype(BF16)
    q = _dot(cq, wuq_ref[...])
    ckv = _rms(_dot(xn, wckv_ref[...]), gkva_ref[...]).astype(BF16)
    kn = _dot(ckv, wuk_ref[...])
    v_ref[...] = (_dot(ckv, wuv_ref[...]) + vone_ref[...]).astype(BF16)
    kr = _dot(xn, wkr_ref[...])

    sb = _dot(xn, wsb_ref[...])
    sq_ref[...] = (sb[:, :SB_WIDTH] * (1.0 / math.sqrt(SB_HEAD_DIM))).astype(BF16)
    sk_ref[...] = sb[:, SB_WIDTH:2 * SB_WIDTH].astype(BF16)
    sv_ref[...] = sb[:, 2 * SB_WIDTH:].astype(BF16)

    ang = pos_ref[...].astype(F32) * inv_ref[...]
    cos = jnp.cos(ang)
    sin = jnp.sin(ang)
    lane = lax.broadcasted_iota(jnp.int32, ang.shape, 1)
    first_half = lane < MLA_NOPE + HALF_ROPE

    def rope(t):
        up = pltpu.roll(t, SLOT - HALF_ROPE, 1)
        down = pltpu.roll(t, HALF_ROPE, 1)
        return t * cos + jnp.where(first_half, -up, down) * sin

    def head_norm(t, g):
        ss = jnp.sum(t * t, axis=-1, keepdims=True) * (1.0 / MLA_QK)
        return t * lax.rsqrt(ss + EPS) * g

    scale = LOG2E / math.sqrt(MLA_QK)
    for h in range(MLA_HEADS):
        sl = slice(h * SLOT, (h + 1) * SLOT)
        qh = rope(head_norm(q[:, sl], gq_ref[...])) * scale + qaug_ref[...]
        q_ref[:, sl] = qh.astype(BF16)
        kh = rope(head_norm(kn[:, sl] + kr, gk_ref[...])) + kaug_ref[...]
        k_ref[:, sl] = kh.astype(BF16)


def _proj_call(x2, pos_col, weights, tm):
    T = x2.shape[0]
    full = lambda a: pl.BlockSpec(a.shape, lambda i: (0,) * a.ndim)
    row = lambda w: pl.BlockSpec((tm, w), lambda i: (i, 0))
    out_w = (MLA_HEADS * SLOT, MLA_HEADS * SLOT, MLA_HEADS * SLOT, SB_WIDTH, SB_WIDTH, SB_WIDTH)
    return pl.pallas_call(
        _proj_kernel,
        grid=(T // tm,),
        in_specs=[row(D_MODEL), row(1)] + [full(w) for w in weights],
        out_specs=[row(w) for w in out_w],
        out_shape=[jax.ShapeDtypeStruct((T, w), BF16) for w in out_w],
        compiler_params=pltpu.CompilerParams(dimension_semantics=("arbitrary",),
                                             vmem_limit_bytes=VMEM_LIMIT_BYTES),
        name="proj",
    )(x2, pos_col, *weights)


def _mla_kernel(flags_ref, q_ref, k_ref, v_ref, posq_ref, posk_ref, o_ref, acc_ref, m_ref,
                *, tq, tk, nk, hp, online):
    b = pl.program_id(0)
    qi = pl.program_id(2)
    qchunk = jnp.right_shift(posq_ref[0], CHUNK_SHIFT)
    acc_ref[...] = jnp.zeros(acc_ref.shape, F32)
    if online:
        m_ref[...] = jnp.full(m_ref.shape, NEG, F32)

    def block(kb, masked):
        k0 = pl.multiple_of(kb * tk, tk)
        if masked:
            kchunk = jnp.right_shift(posk_ref[0, :, pl.ds(k0, tk)], CHUNK_SHIFT)
            vis = kchunk <= qchunk
        for h in range(hp):
            hs = slice(h * SLOT, (h + 1) * SLOT)
            s = _dot_nt(q_ref[0, :, hs], k_ref[0, pl.ds(k0, tk), hs])
            if masked:
                s = jnp.where(vis, s, NEG)
            vh = v_ref[0, pl.ds(k0, tk), hs]
            if online:
                m_old = m_ref[h]
                m_new = jnp.maximum(m_old, jnp.max(s, axis=-1, keepdims=True))
                p = jnp.exp2(s - m_new)
                acc_ref[h] = jnp.exp2(m_old - m_new) * acc_ref[h] + _dot(p.astype(BF16), vh)
                m_ref[h] = m_new
            else:
                acc_ref[h] += _dot(jnp.exp2(s).astype(BF16), vh)

    def body(kb, carry):
        flag = flags_ref[b, qi, kb]

        @pl.when(flag == 1)
        def _():
            block(kb, False)

        @pl.when(flag == 2)
        def _():
            block(kb, True)

        return carry

    lax.fori_loop(0, nk, body, 0)
    lane = lax.broadcasted_iota(jnp.int32, (tq, LANES), 1)
    for j in range(hp // 2):
        o0 = acc_ref[2 * j] / pltpu.roll(acc_ref[2 * j], MLA_V, 1)
        o1 = acc_ref[2 * j + 1] / pltpu.roll(acc_ref[2 * j + 1], MLA_V, 1)
        o_ref[0, :, j * LANES:(j + 1) * LANES] = jnp.where(lane < MLA_V, o0, pltpu.roll(o1, MLA_V, 1))


def _mla_call(flags, q, k, v, pos_col, pos_row, tq, tk, hp, online):
    B, S, _ = q.shape
    nq, nk = S // tq, S // tk
    grid_spec = pltpu.PrefetchScalarGridSpec(
        num_scalar_prefetch=1,
        grid=(B, MLA_HEADS // hp, nq),
        in_specs=[
            pl.BlockSpec((1, tq, hp * SLOT), lambda b, p, i, f: (b, i, p)),
            pl.BlockSpec((1, S, hp * SLOT), lambda b, p, i, f: (b, 0, p)),
            pl.BlockSpec((1, S, hp * SLOT), lambda b, p, i, f: (b, 0, p)),
            pl.BlockSpec((1, tq, 1), lambda b, p, i, f: (b, i, 0)),
            pl.BlockSpec((1, 1, S), lambda b, p, i, f: (b, 0, 0)),
        ],
        out_specs=pl.BlockSpec((1, tq, hp * MLA_V), lambda b, p, i, f: (b, i, p)),
        scratch_shapes=[pltpu.VMEM((hp, tq, LANES), F32), pltpu.VMEM((hp, tq, 1), F32)],
    )
    return pl.pallas_call(
        functools.partial(_mla_kernel, tq=tq, tk=tk, nk=nk, hp=hp, online=online),
        grid_spec=grid_spec,
        out_shape=jax.ShapeDtypeStruct((B, S, MLA_WIDTH), F32),
        compiler_params=pltpu.CompilerParams(
            dimension_semantics=("arbitrary", "arbitrary", "arbitrary"),
            vmem_limit_bytes=VMEM_LIMIT_BYTES),
        name="mla_attn_online" if online else "mla_attn",
    )(flags, q, k, v, pos_col, pos_row)


def _softplus(z):
    return jnp.maximum(z, 0.0) + jnp.log(1.0 + jnp.exp(-jnp.abs(z)))


def _sb_kernel(kstart_ref, q_ref, k_ref, v_ref, posq_ref, posk_ref, o_ref, run_ref, acc_ref,
               *, tq, tk, win):
    b = pl.program_id(0)
    qi = pl.program_id(2)
    lane = lax.broadcasted_iota(jnp.int32, (tq, LANES), 1)
    row_i = lax.broadcasted_iota(jnp.int32, (tk, tk), 0)
    col_i = lax.broadcasted_iota(jnp.int32, (tk, tk), 1)
    tri = (row_i >= col_i).astype(BF16)
    kstart = kstart_ref[b, qi]
    q_pair = q_ref[0]
    zero = jnp.zeros_like(q_pair)
    q2 = jnp.concatenate([jnp.where(lane < SB_HEAD_DIM, q_pair, zero),
                          jnp.where(lane >= SB_HEAD_DIM, q_pair, zero)], axis=0)
    qpos = jnp.concatenate([posq_ref[0], posq_ref[0]], axis=0)

    def suffix_sums(l1m):
        hi = lax.bitcast_convert_type(
            lax.bitcast_convert_type(l1m, jnp.int32) & jnp.int32(HI_MASK), F32)
        lo = l1m - hi
        both = _dot(jnp.concatenate([hi.astype(BF16), lo.astype(BF16)], axis=0), tri)
        return both[:2 * tq] + both[2 * tq:]

    run_ref[...] = jnp.zeros(run_ref.shape, F32)
    acc_ref[...] = jnp.zeros(acc_ref.shape, F32)

    @pl.when(kstart >= win - 1)
    def _():
        k0 = pl.multiple_of((kstart - (win - 1)) * tk, tk)
        causal = posk_ref[0, :, pl.ds(k0, win * tk)] < qpos
        z = _dot_nt(q2, k_ref[0, pl.ds(k0, win * tk), :])
        l1m = jnp.where(causal, -_softplus(z), 0.0)
        run = jnp.zeros((2 * tq, 1), F32)
        a_blocks = [None] * win
        for j in reversed(range(win)):
            cs = slice(j * tk, (j + 1) * tk)
            suffix = suffix_sums(l1m[:, cs])
            a = jnp.exp(z[:, cs] + suffix + run)
            a_blocks[j] = jnp.where(causal[:, cs], a, 0.0).astype(BF16)
            run = run + suffix[:, 0:1]
        acc_ref[...] = _dot(jnp.concatenate(a_blocks, axis=1), v_ref[0, pl.ds(k0, win * tk), :])
        run_ref[...] = run

    def block(kb):
        k0 = pl.multiple_of(kb * tk, tk)
        z = _dot_nt(q2, k_ref[0, pl.ds(k0, tk), :])
        causal = posk_ref[0, :, pl.ds(k0, tk)] < qpos
        l1m = jnp.where(causal, -_softplus(z), 0.0)
        suffix = suffix_sums(l1m)
        run = run_ref[...]
        a = jnp.where(causal, jnp.exp(z + suffix + run), 0.0)
        acc_ref[...] += _dot(a.astype(BF16), v_ref[0, pl.ds(k0, tk), :])
        run_new = run + suffix[:, 0:1]
        run_ref[...] = run_new
        return jnp.max(run_new)

    def cond(c):
        kb, top = c
        return (kb >= 0) & (top >= SB_LOG_ZERO)

    def body(c):
        kb, _ = c
        return kb - 1, block(kb)

    kb_first = jnp.where(kstart >= win - 1, kstart - win, kstart)
    lax.while_loop(cond, body, (kb_first, jnp.max(run_ref[...])))

    o_ref[0] = jnp.where(lane < SB_HEAD_DIM, acc_ref[:tq], acc_ref[tq:])


def _sb_call(kstart, q, k, v, pos_col, pos_row, tq, tk, win):
    B, S, _ = q.shape
    nq = S // tq
    grid_spec = pltpu.PrefetchScalarGridSpec(
        num_scalar_prefetch=1,
        grid=(B, SB_HEADS // 2, nq),
        in_specs=[
            pl.BlockSpec((1, tq, LANES), lambda b, p, i, s: (b, i, p)),
            pl.BlockSpec((1, S, LANES), lambda b, p, i, s: (b, 0, p)),
            pl.BlockSpec((1, S, LANES), lambda b, p, i, s: (b, 0, p)),
            pl.BlockSpec((1, tq, 1), lambda b, p, i, s: (b, i, 0)),
            pl.BlockSpec((1, 1, S), lambda b, p, i, s: (b, 0, 0)),
        ],
        out_specs=pl.BlockSpec((1, tq, LANES), lambda b, p, i, s: (b, i, p)),
        scratch_shapes=[pltpu.VMEM((2 * tq, 1), F32), pltpu.VMEM((2 * tq, LANES), F32)],
    )
    return pl.pallas_call(
        functools.partial(_sb_kernel, tq=tq, tk=tk, win=win),
        grid_spec=grid_spec,
        out_shape=jax.ShapeDtypeStruct((B, S, SB_WIDTH), F32),
        compiler_params=pltpu.CompilerParams(
            dimension_semantics=("arbitrary", "arbitrary", "arbitrary"),
            vmem_limit_bytes=VMEM_LIMIT_BYTES),
        name="sb_attn",
    )(kstart, q, k, v, pos_col, pos_row)


def _round_robin(*gens):
    gens = list(gens)
    while gens:
        for g in list(gens):
            if next(g, StopIteration) is StopIteration:
                gens.remove(g)


def _attn_kernel(flags_ref, kstart_ref, q_ref, k_ref, v_ref, sq_ref, sk_ref, sv_ref, posq_ref,
                 posk_ref, om_ref, os_ref, acc_ref, run_ref, sacc_ref, *, tq, tk, nk, hp):
    b = pl.program_id(0)
    qi = pl.program_id(2)
    stq, stk, win = SB_TQ, SB_TK, SB_WIN
    halves = tq // stq
    units = halves * (hp // 2)
    qchunk = jnp.right_shift(posq_ref[0], CHUNK_SHIFT)
    acc_ref[...] = jnp.zeros(acc_ref.shape, F32)
    lane = lax.broadcasted_iota(jnp.int32, (stq, LANES), 1)
    tri2 = (lax.broadcasted_iota(jnp.int32, (2 * stk, 2 * stk), 0)
            >= lax.broadcasted_iota(jnp.int32, (2 * stk, 2 * stk), 1)).astype(BF16)
    tri = tri2[:stk, :stk]

    def mla_heads(kb, masked):
        k0 = pl.multiple_of(kb * tk, tk)
        if masked:
            kchunk = jnp.right_shift(posk_ref[0, :, pl.ds(k0, tk)], CHUNK_SHIFT)
            vis = kchunk <= qchunk
        for h in range(hp):
            hs = slice(h * SLOT, (h + 1) * SLOT)
            s = _dot_nt(q_ref[0, :, hs], k_ref[0, pl.ds(k0, tk), hs])
            if masked:
                s = jnp.where(vis, s, NEG)
            acc_ref[h] += _dot(jnp.exp2(s).astype(BF16), v_ref[0, pl.ds(k0, tk), hs])
            yield

    def sb_operands(u):
        rows = pl.ds(pl.multiple_of((u % halves) * stq, stq), stq)
        cols = pl.ds(pl.multiple_of((u // halves) * LANES, LANES), LANES)
        q_pair = sq_ref[0, rows, cols]
        zero = jnp.zeros_like(q_pair)
        q2 = jnp.concatenate([jnp.where(lane < SB_HEAD_DIM, q_pair, zero),
                              jnp.where(lane >= SB_HEAD_DIM, q_pair, zero)], axis=0)
        qpos = jnp.concatenate([posq_ref[0, rows], posq_ref[0, rows]], axis=0)
        return q2, qpos, rows, cols

    def suffix_sums(l1m, triangle):
        hi = lax.bitcast_convert_type(
            lax.bitcast_convert_type(l1m, jnp.int32) & jnp.int32(HI_MASK), F32)
        lo = l1m - hi
        both = _dot(jnp.concatenate([hi.astype(BF16), lo.astype(BF16)], axis=0), triangle)
        return both[:2 * stq] + both[2 * stq:]

    def sb_fast(u, kstart):
        q2, qpos, _, cols = sb_operands(u)
        k0 = pl.multiple_of((kstart - (win - 1)) * stk, stk)
        causal = posk_ref[0, :, pl.ds(k0, win * stk)] < qpos
        z = _dot_nt(q2, sk_ref[0, pl.ds(k0, win * stk), cols])
        l1m = jnp.where(causal, -_softplus(z), 0.0)
        yield
        run = jnp.zeros((2 * stq, 1), F32)
        a_blocks = [None] * (win // 2)
        for j in reversed(range(win // 2)):
            cs = slice(j * 2 * stk, (j + 1) * 2 * stk)
            suffix = suffix_sums(l1m[:, cs], tri2)
            a = jnp.exp(z[:, cs] + suffix + run)
            a_blocks[j] = jnp.where(causal[:, cs], a, 0.0).astype(BF16)
            run = run + suffix[:, 0:1]
            yield
        sacc_ref[...] = _dot(jnp.concatenate(a_blocks, axis=1), sv_ref[0, pl.ds(k0, win * stk), cols])
        run_ref[...] = run
        yield

    def sb_finish(u, kstart, fast):
        q2, qpos, rows, cols = sb_operands(u)

        def block(kb):
            k0 = pl.multiple_of(kb * stk, stk)
            z = _dot_nt(q2, sk_ref[0, pl.ds(k0, stk), cols])
            causal = posk_ref[0, :, pl.ds(k0, stk)] < qpos
            l1m = jnp.where(causal, -_softplus(z), 0.0)
            suffix = suffix_sums(l1m, tri)
            run = run_ref[...]
            a = jnp.where(causal, jnp.exp(z + suffix + run), 0.0)
            sacc_ref[...] += _dot(a.astype(BF16), sv_ref[0, pl.ds(k0, stk), cols])
            run_new = run + suffix[:, 0:1]
            run_ref[...] = run_new
            return jnp.max(run_new)

        def cond(c):
            kb, top = c
            return (kb >= 0) & (top >= SB_LOG_ZERO)

        def body(c):
            kb, _ = c
            return kb - 1, block(kb)

        kb_first = jnp.where(fast, kstart - win, kstart)
        lax.while_loop(cond, body, (kb_first, jnp.max(run_ref[...])))
        os_ref[0, rows, cols] = jnp.where(lane < SB_HEAD_DIM, sacc_ref[:stq], sacc_ref[stq:])

    def with_unit(kb, carry):
        flag = flags_ref[b, qi, kb]
        kstart = kstart_ref[b, qi * halves + kb % halves]
        fast = kstart >= win - 1
        run_ref[...] = jnp.zeros(run_ref.shape, F32)
        sacc_ref[...] = jnp.zeros(sacc_ref.shape, F32)

        @pl.when((flag == 1) & fast)
        def _():
            _round_robin(sb_fast(kb, kstart), mla_heads(kb, False))

        @pl.when((flag == 2) & fast)
        def _():
            _round_robin(sb_fast(kb, kstart), mla_heads(kb, True))

        @pl.when((flag == 0) & fast)
        def _():
            _round_robin(sb_fast(kb, kstart))

        @pl.when((flag == 1) & jnp.logical_not(fast))
        def _():
            _round_robin(mla_heads(kb, False))

        @pl.when((flag == 2) & jnp.logical_not(fast))
        def _():
            _round_robin(mla_heads(kb, True))

        sb_finish(kb, kstart, fast)
        return carry

    def mla_only(kb, carry):
        flag = flags_ref[b, qi, kb]

        @pl.when(flag == 1)
        def _():
            _round_robin(mla_heads(kb, False))

        @pl.when(flag == 2)
        def _():
            _round_robin(mla_heads(kb, True))

        return carry

    lax.fori_loop(0, units, with_unit, 0)
    lax.fori_loop(units, nk, mla_only, 0)
    lane_q = lax.broadcasted_iota(jnp.int32, (tq, LANES), 1)
    for j in range(hp // 2):
        o0 = acc_ref[2 * j] / pltpu.roll(acc_ref[2 * j], MLA_V, 1)
        o1 = acc_ref[2 * j + 1] / pltpu.roll(acc_ref[2 * j + 1], MLA_V, 1)
        om_ref[0, :, j * LANES:(j + 1) * LANES] = jnp.where(lane_q < MLA_V, o0, pltpu.roll(o1, MLA_V, 1))


def _attn_call(flags, kstart, q, k, v, sq, sk, sv, pos_col, pos_row, tq, tk, hp):
    B, S, _ = q.shape
    nq, nk = S // tq, S // tk
    assert (tq // SB_TQ) * (hp // 2) <= nk and SB_HEADS == MLA_HEADS and SB_WIN % 2 == 0
    sbw = (hp // 2) * LANES
    grid_spec = pltpu.PrefetchScalarGridSpec(
        num_scalar_prefetch=2,
        grid=(B, MLA_HEADS // hp, nq),
        in_specs=[
            pl.BlockSpec((1, tq, hp * SLOT), lambda b, p, i, f, s: (b, i, p)),
            pl.BlockSpec((1, S, hp * SLOT), lambda b, p, i, f, s: (b, 0, p)),
            pl.BlockSpec((1, S, hp * SLOT), lambda b, p, i, f, s: (b, 0, p)),
            pl.BlockSpec((1, tq, sbw), lambda b, p, i, f, s: (b, i, p)),
            pl.BlockSpec((1, S, sbw), lambda b, p, i, f, s: (b, 0, p)),
            pl.BlockSpec((1, S, sbw), lambda b, p, i, f, s: (b, 0, p)),
            pl.BlockSpec((1, tq, 1), lambda b, p, i, f, s: (b, i, 0)),
            pl.BlockSpec((1, 1, S), lambda b, p, i, f, s: (b, 0, 0)),
        ],
        out_specs=[pl.BlockSpec((1, tq, hp * MLA_V), lambda b, p, i, f, s: (b, i, p)),
                   pl.BlockSpec((1, tq, sbw), lambda b, p, i, f, s: (b, i, p))],
        scratch_shapes=[pltpu.VMEM((hp, tq, LANES), F32), pltpu.VMEM((2 * SB_TQ, 1), F32),
                        pltpu.VMEM((2 * SB_TQ, LANES), F32)],
    )
    return pl.pallas_call(
        functools.partial(_attn_kernel, tq=tq, tk=tk, nk=nk, hp=hp),
        grid_spec=grid_spec,
        out_shape=[jax.ShapeDtypeStruct((B, S, MLA_WIDTH), F32),
                   jax.ShapeDtypeStruct((B, S, SB_WIDTH), F32)],
        compiler_params=pltpu.CompilerParams(
            dimension_semantics=("arbitrary", "arbitrary", "arbitrary"),
            vmem_limit_bytes=VMEM_LIMIT_BYTES),
        name="attn",
    )(flags, kstart, q, k, v, sq, sk, sv, pos_col, pos_row)


def _split_bf16(a):
    hi = a.astype(BF16)
    lo = (a - hi.astype(F32)).astype(BF16)
    return hi, lo


def _merge_kernel(x_ref, om_ref, os_ref, gm_ref, gs_ref, wo_ref, g2_ref, wrh_ref, wrl_ref,
                  bias_ref, wsgu_ref, wsd_ref, base_ref, hn_ref, eidx_ref, gate_ref, rank_ref,
                  cnt_ref):
    mm = _rms(om_ref[...], gm_ref[...]).astype(BF16)
    ms = _rms(os_ref[...], gs_ref[...]).astype(BF16)
    h = x_ref[...] + _dot(mm, wo_ref[:MLA_WIDTH, :]) + _dot(ms, wo_ref[MLA_WIDTH:, :])
    hn = _rms(h, g2_ref[...])
    hn_hi, hn_lo = _split_bf16(hn)
    bits = lax.bitcast_convert_type(hn_hi.astype(F32), jnp.int32)
    hn_ref[...] = (lax.shift_right_logical(bits[:, :D_MODEL // 2], 16)
                   | (bits[:, D_MODEL // 2:] & jnp.int32(HI_MASK)))

    logits = (_dot_nt(wrh_ref[...], hn_hi) + _dot_nt(wrh_ref[...], hn_lo)
              + _dot_nt(wrl_ref[...], hn_hi))
    scores = jax.nn.sigmoid(logits)
    work = scores + bias_ref[...]
    eidx = lax.broadcasted_iota(jnp.int32, work.shape, 0)
    chosen, chosen_score, hits = [], [], []
    for _ in range(TOP_K):
        top = jnp.max(work, axis=0, keepdims=True)
        first = jnp.min(jnp.where(work == top, eidx, N_EXPERTS), axis=0, keepdims=True)
        hit = eidx == first
        hits.append(hit)
        chosen.append(first)
        chosen_score.append(jnp.sum(jnp.where(hit, scores, 0.0), axis=0, keepdims=True))
        work = jnp.where(hit, -jnp.inf, work)
    sel = jnp.concatenate(chosen_score, axis=0)
    eidx_ref[...] = jnp.concatenate(chosen, axis=0)
    gate_ref[...] = sel / jnp.sum(sel, axis=0, keepdims=True) * ROUTED_SCALE

    tm = work.shape[1]
    member = functools.reduce(jnp.logical_or, hits)
    member = jnp.where(member, 1.0, 0.0)
    tri = (lax.broadcasted_iota(jnp.int32, (tm, tm), 0)
           <= lax.broadcasted_iota(jnp.int32, (tm, tm), 1)).astype(BF16)
    upto = _dot(member.astype(BF16), tri)
    before = upto - member
    rank_ref[...] = jnp.concatenate(
        [jnp.sum(jnp.where(hit, before, 0.0), axis=0, keepdims=True) for hit in hits],
        axis=0).astype(jnp.int32)
    cnt_ref[0] = upto[:, tm - 1:tm].astype(jnp.int32)

    sgu = _dot(hn_hi, wsgu_ref[...])
    act = (jax.nn.silu(sgu[:, :SHARED_FF]) * sgu[:, SHARED_FF:]).astype(BF16)
    base_ref[...] = h + _dot(act, wsd_ref[...])


def _merge_call(x2, o_mla, o_sb, gm, gs, wo, g2, wrh, wrl, bias, wsgu, wsd, tm):
    T = x2.shape[0]
    full = lambda a: pl.BlockSpec(a.shape, lambda i: (0,) * a.ndim)
    row = lambda w: pl.BlockSpec((tm, w), lambda i: (i, 0))
    weights = (gm, gs, wo, g2, wrh, wrl, bias, wsgu, wsd)
    topk = pl.BlockSpec((TOP_K, tm), lambda i: (0, i))
    return pl.pallas_call(
        _merge_kernel,
        grid=(T // tm,),
        in_specs=[row(D_MODEL), row(MLA_WIDTH), row(SB_WIDTH)] + [full(w) for w in weights],
        out_specs=[row(D_MODEL), row(D_MODEL // 2), topk, topk, topk,
                   pl.BlockSpec((1, N_EXPERTS, 1), lambda i: (i, 0, 0))],
        out_shape=[jax.ShapeDtypeStruct((T, D_MODEL), F32),
                   jax.ShapeDtypeStruct((T, D_MODEL // 2), jnp.int32),
                   jax.ShapeDtypeStruct((TOP_K, T), jnp.int32),
                   jax.ShapeDtypeStruct((TOP_K, T), F32),
                   jax.ShapeDtypeStruct((TOP_K, T), jnp.int32),
                   jax.ShapeDtypeStruct((T // tm, N_EXPERTS, 1), jnp.int32)],
        compiler_params=pltpu.CompilerParams(dimension_semantics=("arbitrary",),
                                             vmem_limit_bytes=VMEM_LIMIT_BYTES),
        name="merge",
    )(x2, o_mla, o_sb, *weights)


MOE_TB = 2048
MOE_RT = 64
MOE_MAX_TILES = 6
MOE_ROWS = MOE_TB * TOP_K + N_EXPERTS * (MOE_RT - 1)
MOE_PACK = D_MODEL // 2
MOE_SUB = MOE_PACK // LANES
MOE_UNROLL = 8
MOE_XCH = 256
MOE_OCH = 128
MOE_WBUF = 3
MOE_CLEAR_ROWS = 1024


def _unpack_lo(w):
    return lax.bitcast_convert_type(w << 16, F32)


def _unpack_hi(w):
    return lax.bitcast_convert_type(w & jnp.int32(HI_MASK), F32)


def _moe_kernel(off_ref, nt_ref, pos_ref, gate_ref, xp_hbm, base_hbm, wgu_hbm, wd_hbm, o_hbm,
                xy_ref, xbuf, bbuf, obuf, wgu_buf, wd_buf, xsem, bsem, osem, wsem):
    i = pl.program_id(0)

    n_visits = pl.num_programs(0) * N_EXPERTS

    def w_copies(step):
        slot = step % MOE_WBUF
        ex = step % N_EXPERTS
        return (pltpu.make_async_copy(wgu_hbm.at[ex], wgu_buf.at[slot], wsem.at[0, slot]),
                pltpu.make_async_copy(wd_hbm.at[ex], wd_buf.at[slot], wsem.at[1, slot]))

    @pl.when(i == 0)
    def _():
        for ahead in range(MOE_WBUF - 1):
            for cp in w_copies(ahead):
                cp.start()

        def clear(r, c):
            xy_ref[pl.ds(pl.multiple_of(r * MOE_CLEAR_ROWS, MOE_CLEAR_ROWS), MOE_CLEAR_ROWS), :] = (
                jnp.zeros((MOE_CLEAR_ROWS, LANES), jnp.int32))
            return c

        lax.fori_loop(0, MOE_ROWS * MOE_SUB // MOE_CLEAR_ROWS, clear, 0)
        tail = MOE_ROWS * MOE_SUB % MOE_CLEAR_ROWS
        if tail:
            xy_ref[pl.ds(MOE_ROWS * MOE_SUB - tail, tail), :] = jnp.zeros((tail, LANES), jnp.int32)

    def x_copy(c, slot, block=i):
        return pltpu.make_async_copy(
            xp_hbm.at[pl.ds(block * MOE_TB + c * MOE_XCH, MOE_XCH), :], xbuf.at[slot], xsem.at[slot])

    def base_copy(c, slot):
        return pltpu.make_async_copy(
            base_hbm.at[pl.ds(i * MOE_TB + c * MOE_OCH, MOE_OCH), :], bbuf.at[slot], bsem.at[slot])

    def out_copy(c, slot):
        return pltpu.make_async_copy(
            obuf.at[slot], o_hbm.at[pl.ds(i * MOE_TB + c * MOE_OCH, MOE_OCH), :], osem.at[slot])

    def dispatch():
        n_chunks = MOE_TB // MOE_XCH
        for c in range(n_chunks):
            slot = c % 2
            if c + 1 < n_chunks:
                x_copy(c + 1, 1 - slot).start()
            x_copy(c, slot).wait()

            def step(tt, carry, c=c, slot=slot):
                slab = xbuf[slot, pl.ds(pl.multiple_of(tt * MOE_UNROLL, MOE_UNROLL), MOE_UNROLL), :]
                for u in range(MOE_UNROLL):
                    t = tt * MOE_UNROLL + u
                    row = jnp.concatenate([slab[u:u + 1, j * LANES:(j + 1) * LANES]
                                           for j in range(MOE_SUB)], axis=0)
                    for k in range(TOP_K):
                        p = pl.multiple_of(pos_ref[TOP_K * (c * MOE_XCH + t) + k], MOE_SUB)
                        xy_ref[pl.ds(p, MOE_SUB), :] = row
                return carry

            lax.fori_loop(0, MOE_XCH // MOE_UNROLL, step, 0)

    def ffn(r0, m, wslot):
        base_row = pl.multiple_of(MOE_SUB * r0, 8)
        words = jnp.concatenate(
            [xy_ref[pl.ds(base_row + j, m, stride=MOE_SUB), :] for j in range(MOE_SUB)], axis=1)
        gu = (_dot(_unpack_lo(words).astype(BF16), wgu_buf[wslot, :MOE_PACK, :])
              + _dot(_unpack_hi(words).astype(BF16), wgu_buf[wslot, MOE_PACK:, :]))
        act = (jax.nn.silu(gu[:, :EXPERT_FF]) * gu[:, EXPERT_FF:]).astype(BF16)
        y = _dot(act, wd_buf[wslot])
        ya = lax.bitcast_convert_type(y[:, :MOE_PACK].astype(BF16).astype(F32), jnp.int32)
        yb = lax.bitcast_convert_type(y[:, MOE_PACK:].astype(BF16).astype(F32), jnp.int32)
        packed = lax.shift_right_logical(ya, 16) | yb
        for j in range(MOE_SUB):
            xy_ref[pl.ds(base_row + j, m, stride=MOE_SUB), :] = packed[:, j * LANES:(j + 1) * LANES]

    def expert(e, carry):
        visit = i * N_EXPERTS + e

        @pl.when(visit + (MOE_WBUF - 1) < n_visits)
        def _():
            for cp in w_copies(visit + (MOE_WBUF - 1)):
                cp.start()

        for cp in w_copies(visit):
            cp.wait()
        wslot = visit % MOE_WBUF
        off = off_ref[visit]
        nt = nt_ref[visit]

        def chunk(c, inner):
            ffn(off + c * (MOE_MAX_TILES * MOE_RT), MOE_MAX_TILES * MOE_RT, wslot)
            return inner

        full = nt // MOE_MAX_TILES
        lax.fori_loop(0, full, chunk, 0)
        rest = nt - full * MOE_MAX_TILES
        for tiles in range(1, MOE_MAX_TILES):
            @pl.when(rest == tiles)
            def _():
                ffn(off + full * (MOE_MAX_TILES * MOE_RT), tiles * MOE_RT, wslot)

        return carry

    def combine():
        n_chunks = MOE_TB // MOE_OCH
        for c in range(n_chunks):
            slot = c % 2
            if c + 1 < n_chunks:
                base_copy(c + 1, 1 - slot).start()
            base_copy(c, slot).wait()
            if c >= 2:
                out_copy(c - 2, slot).wait()

            def step(tt, carry, c=c, slot=slot):
                for u in range(MOE_UNROLL):
                    t = tt * MOE_UNROLL + u
                    lo = jnp.zeros((MOE_SUB, LANES), F32)
                    hi = jnp.zeros((MOE_SUB, LANES), F32)
                    for k in range(TOP_K):
                        s = TOP_K * (c * MOE_OCH + t) + k
                        w = xy_ref[pl.ds(pl.multiple_of(pos_ref[s], MOE_SUB), MOE_SUB), :]
                        g = gate_ref[s]
                        lo = lo + g * _unpack_lo(w)
                        hi = hi + g * _unpack_hi(w)
                    routed = jnp.concatenate([lo[j:j + 1] for j in range(MOE_SUB)]
                                             + [hi[j:j + 1] for j in range(MOE_SUB)], axis=1)
                    obuf[slot, pl.ds(t, 1), :] = bbuf[slot, pl.ds(t, 1), :] + routed
                return carry

            lax.fori_loop(0, MOE_OCH // MOE_UNROLL, step, 0)
            out_copy(c, slot).start()
        out_copy(n_chunks - 2, n_chunks % 2).wait()
        out_copy(n_chunks - 1, (n_chunks - 1) % 2).wait()

    @pl.when(i == 0)
    def _():
        x_copy(0, 0).start()

    dispatch()
    base_copy(0, 0).start()

    @pl.when(i + 1 < pl.num_programs(0))
    def _():
        x_copy(0, 0, block=i + 1).start()
    lax.fori_loop(0, N_EXPERTS, expert, 0)
    combine()


def _moe_call(off, nt, pos, gates, xp, base, wgu, wd):
    T = base.shape[0]
    slots = MOE_TB * TOP_K
    grid_spec = pltpu.PrefetchScalarGridSpec(
        num_scalar_prefetch=2,
        grid=(T // MOE_TB,),
        in_specs=[
            pl.BlockSpec((slots,), lambda i, o, n: (i,), memory_space=pltpu.SMEM),
            pl.BlockSpec((slots,), lambda i, o, n: (i,), memory_space=pltpu.SMEM),
            pl.BlockSpec(memory_space=pl.ANY),
            pl.BlockSpec(memory_space=pl.ANY),
            pl.BlockSpec(memory_space=pl.ANY),
            pl.BlockSpec(memory_space=pl.ANY),
        ],
        out_specs=pl.BlockSpec(memory_space=pl.ANY),
        scratch_shapes=[
            pltpu.VMEM((MOE_ROWS * MOE_SUB, LANES), jnp.int32),
            pltpu.VMEM((2, MOE_XCH, MOE_PACK), jnp.int32),
            pltpu.VMEM((2, MOE_OCH, D_MODEL), F32),
            pltpu.VMEM((2, MOE_OCH, D_MODEL), F32),
            pltpu.VMEM((MOE_WBUF, D_MODEL, 2 * EXPERT_FF), BF16),
            pltpu.VMEM((MOE_WBUF, EXPERT_FF, D_MODEL), BF16),
            pltpu.SemaphoreType.DMA((2,)),
            pltpu.SemaphoreType.DMA((2,)),
            pltpu.SemaphoreType.DMA((2,)),
            pltpu.SemaphoreType.DMA((2, MOE_WBUF)),
        ],
    )
    return pl.pallas_call(
        _moe_kernel,
        grid_spec=grid_spec,
        out_shape=jax.ShapeDtypeStruct((T, D_MODEL), F32),
        compiler_params=pltpu.CompilerParams(dimension_semantics=("arbitrary",),
                                             vmem_limit_bytes=VMEM_LIMIT_BYTES),
        name="moe",
    )(off, nt, pos, gates, xp, base, wgu, wd)


def _route_tables(eidx, gate, rank, cnt):
    T = eidx.shape[1]
    nblk = T // MOE_TB
    cnt = cnt.reshape(nblk, -1, N_EXPERTS)
    per_blk = cnt.shape[1]
    earlier = jnp.cumsum(cnt, axis=1) - cnt
    tiles = (cnt.sum(axis=1) + MOE_RT - 1) // MOE_RT
    off = jnp.cumsum(tiles, axis=1) * MOE_RT - tiles * MOE_RT
    start = (off[:, None, :] + earlier).reshape(nblk * per_blk, N_EXPERTS)
    e_tk = eidx.T.reshape(nblk * per_blk, T // (nblk * per_blk), TOP_K)
    chosen = e_tk[..., None] == jnp.arange(N_EXPERTS, dtype=jnp.int32)
    pos = jnp.where(chosen, start[:, None, None, :], 0).sum(axis=-1).reshape(T, TOP_K) + rank.T
    return (off.reshape(-1).astype(jnp.int32), tiles.reshape(-1).astype(jnp.int32),
            (pos * MOE_SUB).reshape(-1).astype(jnp.int32), gate.T.reshape(-1))


def _slot_cols(w, width):
    k = w.shape[0]
    w = w.reshape(k, MLA_HEADS, width)
    return jnp.pad(w, ((0, 0), (0, 0), (0, SLOT - width))).reshape(k, MLA_HEADS * SLOT)


def _block_flags(pos, tq, tk, chunked):
    B, S = pos.shape
    p = jnp.right_shift(pos, CHUNK_SHIFT) if chunked else pos
    qmin = p.reshape(B, S // tq, tq).min(-1)[:, :, None]
    qmax = p.reshape(B, S // tq, tq).max(-1)[:, :, None]
    kmin = p.reshape(B, S // tk, tk).min(-1)[:, None, :]
    kmax = p.reshape(B, S // tk, tk).max(-1)[:, None, :]
    if chunked:
        none, all_ = kmin > qmax, kmax <= qmin
    else:
        none, all_ = kmin >= qmax, kmax < qmin
    return jnp.where(none, 0, jnp.where(all_, 1, 2)).astype(jnp.int32)


def kernel(x, positions, norm1, w_in, q_a_norm, w_uq, kv_a_norm, w_ukv, q_norm, k_norm,
           out_norm_mla, out_norm_sb, w_o, norm2, w_router, router_bias, w_gate_up, w_down,
           w_shared_gate_up, w_shared_down):
    B, S, D = x.shape
    T = B * S
    c0 = Q_LORA
    c1 = c0 + KV_LORA
    c2 = c1 + MLA_ROPE

    row2 = lambda g: g.reshape(1, -1).astype(F32)
    wcq = w_in[:, :c0].astype(BF16)
    wckv = w_in[:, c0:c1].astype(BF16)
    wkr = jnp.pad(w_in[:, c1:c2], ((0, 0), (MLA_NOPE, SLOT - MLA_QK))).astype(BF16)
    wsb = w_in[:, c2:].astype(BF16)
    wuq = _slot_cols(w_uq, MLA_QK).astype(BF16)
    w_ukv3 = w_ukv.reshape(KV_LORA, MLA_HEADS, MLA_NOPE + MLA_V)
    wuk = _slot_cols(w_ukv3[:, :, :MLA_NOPE].reshape(KV_LORA, -1), MLA_NOPE).astype(BF16)
    wuv = _slot_cols(w_ukv3[:, :, MLA_NOPE:].reshape(KV_LORA, MLA_WIDTH), MLA_V).astype(BF16)
    slot_lane = jnp.arange(SLOT)
    vone = jnp.tile((slot_lane >= MLA_V).astype(F32), MLA_HEADS).reshape(1, MLA_HEADS * SLOT)
    gq = jnp.pad(q_norm, (0, SLOT - MLA_QK)).reshape(1, SLOT)
    gk = jnp.pad(k_norm, (0, SLOT - MLA_QK)).reshape(1, SLOT)
    inv = ROPE_THETA ** (-jnp.arange(HALF_ROPE, dtype=F32) / HALF_ROPE)
    inv_slot = jnp.pad(jnp.concatenate([inv, inv]), (MLA_NOPE, SLOT - MLA_QK)).reshape(1, SLOT)

    bound = (math.sqrt(MLA_QK) * LOG2E * BOUND_MARGIN) * jnp.max(jnp.abs(q_norm)) * jnp.max(jnp.abs(k_norm))
    bounded = bound <= MLA_MAX_SHIFT
    shift = jnp.where(bounded, bound, 0.0)
    qaug = (slot_lane == MLA_QK).astype(F32).reshape(1, SLOT)
    kaug = -shift * qaug

    x2 = x.reshape(T, D)
    pos_col = positions.reshape(T, 1)
    q, k, v, sq, sk, sv = _proj_call(
        x2, pos_col,
        (row2(norm1), wcq, wckv, wkr, wsb, row2(q_a_norm), wuq, row2(kv_a_norm), wuk, wuv, vone,
         gq, gk, inv_slot, qaug, kaug), tm=PROJ_TM)

    pos_c3 = positions.reshape(B, S, 1)
    pos_r3 = positions.reshape(B, 1, S)
    r3 = lambda a: a.reshape(B, S, a.shape[-1])

    mla_flags = _block_flags(positions, MLA_TQ, MLA_TK, True)
    sb_flags = _block_flags(positions, SB_TQ, SB_TK, False)
    kstart = jnp.max(jnp.where(sb_flags != 0, jnp.arange(S // SB_TK, dtype=jnp.int32), -1),
                     axis=-1).astype(jnp.int32)
    attn_args = (mla_flags, kstart, r3(q), r3(k), r3(v), r3(sq), r3(sk), r3(sv), pos_c3, pos_r3)

    def separate(mla_flags, kstart, q, k, v, sq, sk, sv, pos_c3, pos_r3):
        o_mla = _mla_call(mla_flags, q, k, v, pos_c3, pos_r3, MLA_TQ, MLA_TK, hp=2, online=True)
        o_sb = _sb_call(kstart, sq, sk, sv, pos_c3, pos_r3, SB_TQ, SB_TK, win=SB_WIN)
        return o_mla, o_sb

    o_mla, o_sb = lax.cond(
        bounded,
        lambda *a: tuple(_attn_call(*a, MLA_TQ, MLA_TK, hp=MLA_HEADS_PER_STEP)),
        separate, *attn_args)

    wr_t = w_router.T
    wrh = wr_t.astype(BF16)
    wrl = (wr_t - wrh.astype(F32)).astype(BF16)
    base, hn, eidx, gate, rank, cnt = _merge_call(
        x2, o_mla.reshape(T, MLA_WIDTH), o_sb.reshape(T, SB_WIDTH), row2(out_norm_mla),
        row2(out_norm_sb), w_o.astype(BF16), row2(norm2), wrh, wrl,
        router_bias.reshape(N_EXPERTS, 1).astype(F32), w_shared_gate_up.astype(BF16),
        w_shared_down.astype(BF16), tm=MERGE_TM)

    off, tiles, pos, gates = _route_tables(eidx, gate, rank, cnt)
    out = _moe_call(off, tiles, pos, gates, hn, base, w_gate_up.astype(BF16),
                    w_down.astype(BF16))
    return out.reshape(B, S, D)
```

```python
import functools
import math

import jax
import jax.numpy as jnp
from jax import lax
from jax.experimental import pallas as pl
from jax.experimental.pallas import tpu as pltpu

D_MODEL = 1024
CHUNK = 64
MLA_HEADS = 8
MLA_NOPE = 64
MLA_ROPE = 32
MLA_QK = MLA_NOPE + MLA_ROPE
MLA_V = 64
Q_LORA = 256
KV_LORA = 128
ROPE_THETA = 10000.0
SB_HEADS = 8
SB_HEAD_DIM = 64
SB_WIDTH = SB_HEADS * SB_HEAD_DIM
MLA_WIDTH = MLA_HEADS * MLA_V
N_EXPERTS = 64
TOP_K = 8
EXPERT_FF = 256
SHARED_FF = 256
ROUTED_SCALE = 2.5
EPS = 1e-6
NEG = -1e30

LANES = 128
VMEM_LIMIT_BYTES = 56 * 1024 * 1024

SLOT = LANES
HALF_ROPE = MLA_ROPE // 2

SB_LOG_ZERO = -90.0

LOG2E = 1.4426950408889634
BOUND_MARGIN = 1.02
MLA_MAX_SHIFT = 50.0

BF16 = jnp.bfloat16
F32 = jnp.float32
HI_MASK = -65536
CHUNK_SHIFT = CHUNK.bit_length() - 1
assert 1 << CHUNK_SHIFT == CHUNK

PROJ_TM = 512
MERGE_TM = 1024
MLA_TQ, MLA_TK = 512, 512
MLA_HEADS_PER_STEP = 4
SB_TQ, SB_TK, SB_WIN = 256, 128, 4


def _rms(x, g):
    return x * lax.rsqrt(jnp.mean(x * x, axis=-1, keepdims=True) + EPS) * g


def _dot(a, b):
    return jnp.dot(a, b, preferred_element_type=F32)


def _dot_nt(a, b):
    return lax.dot_general(a, b, (((1,), (1,)), ((), ())), preferred_element_type=F32)


def _proj_kernel(x_ref, pos_ref, g1_ref, wcq_ref, wckv_ref, wkr_ref, wsb_ref, gqa_ref, wuq_ref,
                 gkva_ref, wuk_ref, wuv_ref, vone_ref, gq_ref, gk_ref, inv_ref, qaug_ref, kaug_ref,
                 q_ref, k_ref, v_ref, sq_ref, sk_ref, sv_ref):
    x = x_ref[...]
    xn = _rms(x, g1_ref[...]).astype(BF16)

    cq = _rms(_dot(xn, wcq_ref[...]), gqa_ref[...]).astype(BF16)
    q = _dot(cq, wuq_ref[...])
    ckv = _rms(_dot(xn, wckv_ref[...]), gkva_ref[...]).astype(BF16)
    kn = _dot(ckv, wuk_ref[...])
    v_ref[...] = (_dot(ckv, wuv_ref[...]) + vone_ref[...]).astype(BF16)
    kr = _dot(xn, wkr_ref[...])

    sb = _dot(xn, wsb_ref[...])
    sq_ref[...] = (sb[:, :SB_WIDTH] * (1.0 / math.sqrt(SB_HEAD_DIM))).astype(BF16)
    sk_ref[...] = sb[:, SB_WIDTH:2 * SB_WIDTH].astype(BF16)
    sv_ref[...] = sb[:, 2 * SB_WIDTH:].astype(BF16)

    ang = pos_ref[...].astype(F32) * inv_ref[...]
    cos = jnp.cos(ang)
    sin = jnp.sin(ang)
    lane = lax.broadcasted_iota(jnp.int32, ang.shape, 1)
    first_half = lane < MLA_NOPE + HALF_ROPE

    def rope(t):
        up = pltpu.roll(t, SLOT - HALF_ROPE, 1)
        down = pltpu.roll(t, HALF_ROPE, 1)
        return t * cos + jnp.where(first_half, -up, down) * sin

    def head_norm(t, g):
        ss = jnp.sum(t * t, axis=-1, keepdims=True) * (1.0 / MLA_QK)
        return t * lax.rsqrt(ss + EPS) * g

    scale = LOG2E / math.sqrt(MLA_QK)
    for h in range(MLA_HEADS):
        sl = slice(h * SLOT, (h + 1) * SLOT)
        qh = rope(head_norm(q[:, sl], gq_ref[...])) * scale + qaug_ref[...]
        q_ref[:, sl] = qh.astype(BF16)
        kh = rope(head_norm(kn[:, sl] + kr, gk_ref[...])) + kaug_ref[...]
        k_ref[:, sl] = kh.astype(BF16)


def _proj_call(x2, pos_col, weights, tm):
    T = x2.shape[0]
    full = lambda a: pl.BlockSpec(a.shape, lambda i: (0,) * a.ndim)
    row = lambda w: pl.BlockSpec((tm, w), lambda i: (i, 0))
    out_w = (MLA_HEADS * SLOT, MLA_HEADS * SLOT, MLA_HEADS * SLOT, SB_WIDTH, SB_WIDTH, SB_WIDTH)
    return pl.pallas_call(
        _proj_kernel,
        grid=(T // tm,),
        in_specs=[row(D_MODEL), row(1)] + [full(w) for w in weights],
        out_specs=[row(w) for w in out_w],
        out_shape=[jax.ShapeDtypeStruct((T, w), BF16) for w in out_w],
        compiler_params=pltpu.CompilerParams(dimension_semantics=("arbitrary",),
                                             vmem_limit_bytes=VMEM_LIMIT_BYTES),
        name="proj",
    )(x2, pos_col, *weights)


def _mla_kernel(flags_ref, q_ref, k_ref, v_ref, posq_ref, posk_ref, o_ref, acc_ref, m_ref,
                *, tq, tk, nk, hp, online):
    b = pl.program_id(0)
    qi = pl.program_id(2)
    qchunk = jnp.right_shift(posq_ref[0], CHUNK_SHIFT)
    acc_ref[...] = jnp.zeros(acc_ref.shape, F32)
    if online:
        m_ref[...] = jnp.full(m_ref.shape, NEG, F32)

    def block(kb, masked):
        k0 = pl.multiple_of(kb * tk, tk)
        if masked:
            kchunk = jnp.right_shift(posk_ref[0, :, pl.ds(k0, tk)], CHUNK_SHIFT)
            vis = kchunk <= qchunk
        for h in range(hp):
            hs = slice(h * SLOT, (h + 1) * SLOT)
            s = _dot_nt(q_ref[0, :, hs], k_ref[0, pl.ds(k0, tk), hs])
            if masked:
                s = jnp.where(vis, s, NEG)
            vh = v_ref[0, pl.ds(k0, tk), hs]
            if online:
                m_old = m_ref[h]
                m_new = jnp.maximum(m_old, jnp.max(s, axis=-1, keepdims=True))
                p = jnp.exp2(s - m_new)
                acc_ref[h] = jnp.exp2(m_old - m_new) * acc_ref[h] + _dot(p.astype(BF16), vh)
                m_ref[h] = m_new
            else:
                acc_ref[h] += _dot(jnp.exp2(s).astype(BF16), vh)

    def body(kb, carry):
        flag = flags_ref[b, qi, kb]

        @pl.when(flag == 1)
        def _():
            block(kb, False)

        @pl.when(flag == 2)
        def _():
            block(kb, True)

        return carry

    lax.fori_loop(0, nk, body, 0)
    lane = lax.broadcasted_iota(jnp.int32, (tq, LANES), 1)
    for j in range(hp // 2):
        o0 = acc_ref[2 * j] / pltpu.roll(acc_ref[2 * j], MLA_V, 1)
        o1 = acc_ref[2 * j + 1] / pltpu.roll(acc_ref[2 * j + 1], MLA_V, 1)
        o_ref[0, :, j * LANES:(j + 1) * LANES] = jnp.where(lane < MLA_V, o0, pltpu.roll(o1, MLA_V, 1))


def _mla_call(flags, q, k, v, pos_col, pos_row, tq, tk, hp, online):
    B, S, _ = q.shape
    nq, nk = S // tq, S // tk
    grid_spec = pltpu.PrefetchScalarGridSpec(
        num_scalar_prefetch=1,
        grid=(B, MLA_HEADS // hp, nq),
        in_specs=[
            pl.BlockSpec((1, tq, hp * SLOT), lambda b, p, i, f: (b, i, p)),
            pl.BlockSpec((1, S, hp * SLOT), lambda b, p, i, f: (b, 0, p)),
            pl.BlockSpec((1, S, hp * SLOT), lambda b, p, i, f: (b, 0, p)),
            pl.BlockSpec((1, tq, 1), lambda b, p, i, f: (b, i, 0)),
            pl.BlockSpec((1, 1, S), lambda b, p, i, f: (b, 0, 0)),
        ],
        out_specs=pl.BlockSpec((1, tq, hp * MLA_V), lambda b, p, i, f: (b, i, p)),
        scratch_shapes=[pltpu.VMEM((hp, tq, LANES), F32), pltpu.VMEM((hp, tq, 1), F32)],
    )
    return pl.pallas_call(
        functools.partial(_mla_kernel, tq=tq, tk=tk, nk=nk, hp=hp, online=online),
        grid_spec=grid_spec,
        out_shape=jax.ShapeDtypeStruct((B, S, MLA_WIDTH), F32),
        compiler_params=pltpu.CompilerParams(
            dimension_semantics=("arbitrary", "arbitrary", "arbitrary"),
            vmem_limit_bytes=VMEM_LIMIT_BYTES),
        name="mla_attn_online" if online else "mla_attn",
    )(flags, q, k, v, pos_col, pos_row)


def _softplus(z):
    return jnp.maximum(z, 0.0) + jnp.log(1.0 + jnp.exp(-jnp.abs(z)))


def _sb_kernel(kstart_ref, q_ref, k_ref, v_ref, posq_ref, posk_ref, o_ref, run_ref, acc_ref,
               *, tq, tk, win):
    b = pl.program_id(0)
    qi = pl.program_id(2)
    lane = lax.broadcasted_iota(jnp.int32, (tq, LANES), 1)
    row_i = lax.broadcasted_iota(jnp.int32, (tk, tk), 0)
    col_i = lax.broadcasted_iota(jnp.int32, (tk, tk), 1)
    tri = (row_i >= col_i).astype(BF16)
    kstart = kstart_ref[b, qi]
    q_pair = q_ref[0]
    zero = jnp.zeros_like(q_pair)
    q2 = jnp.concatenate([jnp.where(lane < SB_HEAD_DIM, q_pair, zero),
                          jnp.where(lane >= SB_HEAD_DIM, q_pair, zero)], axis=0)
    qpos = jnp.concatenate([posq_ref[0], posq_ref[0]], axis=0)

    def suffix_sums(l1m):
        hi = lax.bitcast_convert_type(
            lax.bitcast_convert_type(l1m, jnp.int32) & jnp.int32(HI_MASK), F32)
        lo = l1m - hi
        both = _dot(jnp.concatenate([hi.astype(BF16), lo.astype(BF16)], axis=0), tri)
        return both[:2 * tq] + both[2 * tq:]

    run_ref[...] = jnp.zeros(run_ref.shape, F32)
    acc_ref[...] = jnp.zeros(acc_ref.shape, F32)

    @pl.when(kstart >= win - 1)
    def _():
        k0 = pl.multiple_of((kstart - (win - 1)) * tk, tk)
        causal = posk_ref[0, :, pl.ds(k0, win * tk)] < qpos
        z = _dot_nt(q2, k_ref[0, pl.ds(k0, win * tk), :])
        l1m = jnp.where(causal, -_softplus(z), 0.0)
        run = jnp.zeros((2 * tq, 1), F32)
        a_blocks = [None] * win
        for j in reversed(range(win)):
            cs = slice(j * tk, (j + 1) * tk)
            suffix = suffix_sums(l1m[:, cs])
            a = jnp.exp(z[:, cs] + suffix + run)
            a_blocks[j] = jnp.where(causal[:, cs], a, 0.0).astype(BF16)
            run = run + suffix[:, 0:1]
        acc_ref[...] = _dot(jnp.concatenate(a_blocks, axis=1), v_ref[0, pl.ds(k0, win * tk), :])
        run_ref[...] = run

    def block(kb):
        k0 = pl.multiple_of(kb * tk, tk)
        z = _dot_nt(q2, k_ref[0, pl.ds(k0, tk), :])
        causal = posk_ref[0, :, pl.ds(k0, tk)] < qpos
        l1m = jnp.where(causal, -_softplus(z), 0.0)
        suffix = suffix_sums(l1m)
        run = run_ref[...]
        a = jnp.where(causal, jnp.exp(z + suffix + run), 0.0)
        acc_ref[...] += _dot(a.astype(BF16), v_ref[0, pl.ds(k0, tk), :])
        run_new = run + suffix[:, 0:1]
        run_ref[...] = run_new
        return jnp.max(run_new)

    def cond(c):
        kb, top = c
        return (kb >= 0) & (top >= SB_LOG_ZERO)

    def body(c):
        kb, _ = c
        return kb - 1, block(kb)

    kb_first = jnp.where(kstart >= win - 1, kstart - win, kstart)
    lax.while_loop(cond, body, (kb_first, jnp.max(run_ref[...])))

    o_ref[0] = jnp.where(lane < SB_HEAD_DIM, acc_ref[:tq], acc_ref[tq:])


def _sb_call(kstart, q, k, v, pos_col, pos_row, tq, tk, win):
    B, S, _ = q.shape
    nq = S // tq
    grid_spec = pltpu.PrefetchScalarGridSpec(
        num_scalar_prefetch=1,
        grid=(B, SB_HEADS // 2, nq),
        in_specs=[
            pl.BlockSpec((1, tq, LANES), lambda b, p, i, s: (b, i, p)),
            pl.BlockSpec((1, S, LANES), lambda b, p, i, s: (b, 0, p)),
            pl.BlockSpec((1, S, LANES), lambda b, p, i, s: (b, 0, p)),
            pl.BlockSpec((1, tq, 1), lambda b, p, i, s: (b, i, 0)),
            pl.BlockSpec((1, 1, S), lambda b, p, i, s: (b, 0, 0)),
        ],
        out_specs=pl.BlockSpec((1, tq, LANES), lambda b, p, i, s: (b, i, p)),
        scratch_shapes=[pltpu.VMEM((2 * tq, 1), F32), pltpu.VMEM((2 * tq, LANES), F32)],
    )
    return pl.pallas_call(
        functools.partial(_sb_kernel, tq=tq, tk=tk, win=win),
        grid_spec=grid_spec,
        out_shape=jax.ShapeDtypeStruct((B, S, SB_WIDTH), F32),
        compiler_params=pltpu.CompilerParams(
            dimension_semantics=("arbitrary", "arbitrary", "arbitrary"),
            vmem_limit_bytes=VMEM_LIMIT_BYTES),
        name="sb_attn",
    )(kstart, q, k, v, pos_col, pos_row)


def _round_robin(*gens):
    gens = list(gens)
    while gens:
        for g in list(gens):
            if next(g, StopIteration) is StopIteration:
                gens.remove(g)


def _attn_kernel(flags_ref, kstart_ref, q_ref, k_ref, v_ref, sq_ref, sk_ref, sv_ref, posq_ref,
                 posk_ref, om_ref, os_ref, acc_ref, run_ref, sacc_ref, *, tq, tk, nk, hp):
    b = pl.program_id(0)
    qi = pl.program_id(2)
    stq, stk, win = SB_TQ, SB_TK, SB_WIN
    halves = tq // stq
    units = halves * (hp // 2)
    qchunk = jnp.right_shift(posq_ref[0], CHUNK_SHIFT)
    acc_ref[...] = jnp.zeros(acc_ref.shape, F32)
    lane = lax.broadcasted_iota(jnp.int32, (stq, LANES), 1)
    tri2 = (lax.broadcasted_iota(jnp.int32, (2 * stk, 2 * stk), 0)
            >= lax.broadcasted_iota(jnp.int32, (2 * stk, 2 * stk), 1)).astype(BF16)
    tri = tri2[:stk, :stk]

    def mla_heads(kb, masked):
        k0 = pl.multiple_of(kb * tk, tk)
        if masked:
            kchunk = jnp.right_shift(posk_ref[0, :, pl.ds(k0, tk)], CHUNK_SHIFT)
            vis = kchunk <= qchunk
        for h in range(hp):
            hs = slice(h * SLOT, (h + 1) * SLOT)
            s = _dot_nt(q_ref[0, :, hs], k_ref[0, pl.ds(k0, tk), hs])
            if masked:
                s = jnp.where(vis, s, NEG)
            acc_ref[h] += _dot(jnp.exp2(s).astype(BF16), v_ref[0, pl.ds(k0, tk), hs])
            yield

    def sb_operands(u):
        rows = pl.ds(pl.multiple_of((u % halves) * stq, stq), stq)
        cols = pl.ds(pl.multiple_of((u // halves) * LANES, LANES), LANES)
        q_pair = sq_ref[0, rows, cols]
        zero = jnp.zeros_like(q_pair)
        q2 = jnp.concatenate([jnp.where(lane < SB_HEAD_DIM, q_pair, zero),
                              jnp.where(lane >= SB_HEAD_DIM, q_pair, zero)], axis=0)
        qpos = jnp.concatenate([posq_ref[0, rows], posq_ref[0, rows]], axis=0)
        return q2, qpos, rows, cols

    def suffix_sums(l1m, triangle):
        hi = lax.bitcast_convert_type(
            lax.bitcast_convert_type(l1m, jnp.int32) & jnp.int32(HI_MASK), F32)
        lo = l1m - hi
        both = _dot(jnp.concatenate([hi.astype(BF16), lo.astype(BF16)], axis=0), triangle)
        return both[:2 * stq] + both[2 * stq:]

    def sb_fast(u, kstart):
        q2, qpos, _, cols = sb_operands(u)
        k0 = pl.multiple_of((kstart - (win - 1)) * stk, stk)
        causal = posk_ref[0, :, pl.ds(k0, win * stk)] < qpos
        z = _dot_nt(q2, sk_ref[0, pl.ds(k0, win * stk), cols])
        l1m = jnp.where(causal, -_softplus(z), 0.0)
        yield
        run = jnp.zeros((2 * stq, 1), F32)
        a_blocks = [None] * (win // 2)
        for j in reversed(range(win // 2)):
            cs = slice(j * 2 * stk, (j + 1) * 2 * stk)
            suffix = suffix_sums(l1m[:, cs], tri2)
            a = jnp.exp(z[:, cs] + suffix + run)
            a_blocks[j] = jnp.where(causal[:, cs], a, 0.0).astype(BF16)
            run = run + suffix[:, 0:1]
            yield
        sacc_ref[...] = _dot(jnp.concatenate(a_blocks, axis=1), sv_ref[0, pl.ds(k0, win * stk), cols])
        run_ref[...] = run
        yield

    def sb_finish(u, kstart, fast):
        q2, qpos, rows, cols = sb_operands(u)

        def block(kb):
            k0 = pl.multiple_of(kb * stk, stk)
            z = _dot_nt(q2, sk_ref[0, pl.ds(k0, stk), cols])
            causal = posk_ref[0, :, pl.ds(k0, stk)] < qpos
            l1m = jnp.where(causal, -_softplus(z), 0.0)
            suffix = suffix_sums(l1m, tri)
            run = run_ref[...]
            a = jnp.where(causal, jnp.exp(z + suffix + run), 0.0)
            sacc_ref[...] += _dot(a.astype(BF16), sv_ref[0, pl.ds(k0, stk), cols])
            run_new = run + suffix[:, 0:1]
            run_ref[...] = run_new
            return jnp.max(run_new)

        def cond(c):
            kb, top = c
            return (kb >= 0) & (top >= SB_LOG_ZERO)

        def body(c):
            kb, _ = c
            return kb - 1, block(kb)

        kb_first = jnp.where(fast, kstart - win, kstart)
        lax.while_loop(cond, body, (kb_first, jnp.max(run_ref[...])))
        os_ref[0, rows, cols] = jnp.where(lane < SB_HEAD_DIM, sacc_ref[:stq], sacc_ref[stq:])

    def with_unit(kb, carry):
        flag = flags_ref[b, qi, kb]
        kstart = kstart_ref[b, qi * halves + kb % halves]
        fast = kstart >= win - 1
        run_ref[...] = jnp.zeros(run_ref.shape, F32)
        sacc_ref[...] = jnp.zeros(sacc_ref.shape, F32)

        @pl.when((flag == 1) & fast)
        def _():
            _round_robin(sb_fast(kb, kstart), mla_heads(kb, False))

        @pl.when((flag == 2) & fast)
        def _():
            _round_robin(sb_fast(kb, kstart), mla_heads(kb, True))

        @pl.when((flag == 0) & fast)
        def _():
            _round_robin(sb_fast(kb, kstart))

        @pl.when((flag == 1) & jnp.logical_not(fast))
        def _():
            _round_robin(mla_heads(kb, False))

        @pl.when((flag == 2) & jnp.logical_not(fast))
        def _():
            _round_robin(mla_heads(kb, True))

        sb_finish(kb, kstart, fast)
        return carry

    def mla_only(kb, carry):
        flag = flags_ref[b, qi, kb]

        @pl.when(flag == 1)
        def _():
            _round_robin(mla_heads(kb, False))

        @pl.when(flag == 2)
        def _():
            _round_robin(mla_heads(kb, True))

        return carry

    lax.fori_loop(0, units, with_unit, 0)
    lax.fori_loop(units, nk, mla_only, 0)
    lane_q = lax.broadcasted_iota(jnp.int32, (tq, LANES), 1)
    for j in range(hp // 2):
        o0 = acc_ref[2 * j] / pltpu.roll(acc_ref[2 * j], MLA_V, 1)
        o1 = acc_ref[2 * j + 1] / pltpu.roll(acc_ref[2 * j + 1], MLA_V, 1)
        om_ref[0, :, j * LANES:(j + 1) * LANES] = jnp.where(lane_q < MLA_V, o0, pltpu.roll(o1, MLA_V, 1))


def _attn_call(flags, kstart, q, k, v, sq, sk, sv, pos_col, pos_row, tq, tk, hp):
    B, S, _ = q.shape
    nq, nk = S // tq, S // tk
    assert (tq // SB_TQ) * (hp // 2) <= nk and SB_HEADS == MLA_HEADS and SB_WIN % 2 == 0
    sbw = (hp // 2) * LANES
    grid_spec = pltpu.PrefetchScalarGridSpec(
        num_scalar_prefetch=2,
        grid=(B, MLA_HEADS // hp, nq),
        in_specs=[
            pl.BlockSpec((1, tq, hp * SLOT), lambda b, p, i, f, s: (b, i, p)),
            pl.BlockSpec((1, S, hp * SLOT), lambda b, p, i, f, s: (b, 0, p)),
            pl.BlockSpec((1, S, hp * SLOT), lambda b, p, i, f, s: (b, 0, p)),
            pl.BlockSpec((1, tq, sbw), lambda b, p, i, f, s: (b, i, p)),
            pl.BlockSpec((1, S, sbw), lambda b, p, i, f, s: (b, 0, p)),
            pl.BlockSpec((1, S, sbw), lambda b, p, i, f, s: (b, 0, p)),
            pl.BlockSpec((1, tq, 1), lambda b, p, i, f, s: (b, i, 0)),
            pl.BlockSpec((1, 1, S), lambda b, p, i, f, s: (b, 0, 0)),
        ],
        out_specs=[pl.BlockSpec((1, tq, hp * MLA_V), lambda b, p, i, f, s: (b, i, p)),
                   pl.BlockSpec((1, tq, sbw), lambda b, p, i, f, s: (b, i, p))],
        scratch_shapes=[pltpu.VMEM((hp, tq, LANES), F32), pltpu.VMEM((2 * SB_TQ, 1), F32),
                        pltpu.VMEM((2 * SB_TQ, LANES), F32)],
    )
    return pl.pallas_call(
        functools.partial(_attn_kernel, tq=tq, tk=tk, nk=nk, hp=hp),
        grid_spec=grid_spec,
        out_shape=[jax.ShapeDtypeStruct((B, S, MLA_WIDTH), F32),
                   jax.ShapeDtypeStruct((B, S, SB_WIDTH), F32)],
        compiler_params=pltpu.CompilerParams(
            dimension_semantics=("arbitrary", "arbitrary", "arbitrary"),
            vmem_limit_bytes=VMEM_LIMIT_BYTES),
        name="attn",
    )(flags, kstart, q, k, v, sq, sk, sv, pos_col, pos_row)


def _split_bf16(a):
    hi = a.astype(BF16)
    lo = (a - hi.astype(F32)).astype(BF16)
    return hi, lo


def _merge_kernel(x_ref, om_ref, os_ref, gm_ref, gs_ref, wo_ref, g2_ref, wrh_ref, wrl_ref,
                  bias_ref, wsgu_ref, wsd_ref, base_ref, hn_ref, eidx_ref, gate_ref, rank_ref,
                  cnt_ref):
    mm = _rms(om_ref[...], gm_ref[...]).astype(BF16)
    ms = _rms(os_ref[...], gs_ref[...]).astype(BF16)
    h = x_ref[...] + _dot(mm, wo_ref[:MLA_WIDTH, :]) + _dot(ms, wo_ref[MLA_WIDTH:, :])
    hn = _rms(h, g2_ref[...])
    hn_hi, hn_lo = _split_bf16(hn)
    bits = lax.bitcast_convert_type(hn_hi.astype(F32), jnp.int32)
    hn_ref[...] = (lax.shift_right_logical(bits[:, :D_MODEL // 2], 16)
                   | (bits[:, D_MODEL // 2:] & jnp.int32(HI_MASK)))

    logits = (_dot_nt(wrh_ref[...], hn_hi) + _dot_nt(wrh_ref[...], hn_lo)
              + _dot_nt(wrl_ref[...], hn_hi))
    scores = jax.nn.sigmoid(logits)
    work = scores + bias_ref[...]
    eidx = lax.broadcasted_iota(jnp.int32, work.shape, 0)
    chosen, chosen_score, hits = [], [], []
    for _ in range(TOP_K):
        top = jnp.max(work, axis=0, keepdims=True)
        first = jnp.min(jnp.where(work == top, eidx, N_EXPERTS), axis=0, keepdims=True)
        hit = eidx == first
        hits.append(hit)
        chosen.append(first)
        chosen_score.append(jnp.sum(jnp.where(hit, scores, 0.0), axis=0, keepdims=True))
        work = jnp.where(hit, -jnp.inf, work)
    sel = jnp.concatenate(chosen_score, axis=0)
    eidx_ref[...] = jnp.concatenate(chosen, axis=0)
    gate_ref[...] = sel / jnp.sum(sel, axis=0, keepdims=True) * ROUTED_SCALE

    tm = work.shape[1]
    member = functools.reduce(jnp.logical_or, hits)
    member = jnp.where(member, 1.0, 0.0)
    tri = (lax.broadcasted_iota(jnp.int32, (tm, tm), 0)
           <= lax.broadcasted_iota(jnp.int32, (tm, tm), 1)).astype(BF16)
    upto = _dot(member.astype(BF16), tri)
    before = upto - member
    rank_ref[...] = jnp.concatenate(
        [jnp.sum(jnp.where(hit, before, 0.0), axis=0, keepdims=True) for hit in hits],
        axis=0).astype(jnp.int32)
    cnt_ref[0] = upto[:, tm - 1:tm].astype(jnp.int32)

    sgu = _dot(hn_hi, wsgu_ref[...])
    act = (jax.nn.silu(sgu[:, :SHARED_FF]) * sgu[:, SHARED_FF:]).astype(BF16)
    base_ref[...] = h + _dot(act, wsd_ref[...])


def _merge_call(x2, o_mla, o_sb, gm, gs, wo, g2, wrh, wrl, bias, wsgu, wsd, tm):
    T = x2.shape[0]
    full = lambda a: pl.BlockSpec(a.shape, lambda i: (0,) * a.ndim)
    row = lambda w: pl.BlockSpec((tm, w), lambda i: (i, 0))
    weights = (gm, gs, wo, g2, wrh, wrl, bias, wsgu, wsd)
    topk = pl.BlockSpec((TOP_K, tm), lambda i: (0, i))
    return pl.pallas_call(
        _merge_kernel,
        grid=(T // tm,),
        in_specs=[row(D_MODEL), row(MLA_WIDTH), row(SB_WIDTH)] + [full(w) for w in weights],
        out_specs=[row(D_MODEL), row(D_MODEL // 2), topk, topk, topk,
                   pl.BlockSpec((1, N_EXPERTS, 1), lambda i: (i, 0, 0))],
        out_shape=[jax.ShapeDtypeStruct((T, D_MODEL), F32),
                   jax.ShapeDtypeStruct((T, D_MODEL // 2), jnp.int32),
                   jax.ShapeDtypeStruct((TOP_K, T), jnp.int32),
                   jax.ShapeDtypeStruct((TOP_K, T), F32),
                   jax.ShapeDtypeStruct((TOP_K, T), jnp.int32),
                   jax.ShapeDtypeStruct((T // tm, N_EXPERTS, 1), jnp.int32)],
        compiler_params=pltpu.CompilerParams(dimension_semantics=("arbitrary",),
                                             vmem_limit_bytes=VMEM_LIMIT_BYTES),
        name="merge",
    )(x2, o_mla, o_sb, *weights)


MOE_TB = 2048
MOE_RT = 64
MOE_MAX_TILES = 6
MOE_ROWS = MOE_TB * TOP_K + N_EXPERTS * (MOE_RT - 1)
MOE_PACK = D_MODEL // 2
MOE_SUB = MOE_PACK // LANES
MOE_UNROLL = 8
MOE_XCH = 256
MOE_OCH = 128
MOE_WBUF = 4
MOE_PAIR_TILES = (4, 5)
MOE_CLEAR_ROWS = 1024


def _unpack_lo(w):
    return lax.bitcast_convert_type(w << 16, F32)


def _unpack_hi(w):
    return lax.bitcast_convert_type(w & jnp.int32(HI_MASK), F32)


def _moe_kernel(off_ref, nt_ref, pos_ref, gate_ref, xp_hbm, base_hbm, wgu_hbm, wd_hbm, o_hbm,
                xy_ref, xbuf, bbuf, obuf, wgu_buf, wd_buf, xsem, bsem, osem, wsem):
    i = pl.program_id(0)

    n_visits = pl.num_programs(0) * N_EXPERTS

    def w_copies(step):
        slot = step % MOE_WBUF
        ex = step % N_EXPERTS
        return (pltpu.make_async_copy(wgu_hbm.at[ex], wgu_buf.at[slot], wsem.at[0, slot]),
                pltpu.make_async_copy(wd_hbm.at[ex], wd_buf.at[slot], wsem.at[1, slot]))

    @pl.when(i == 0)
    def _():
        for ahead in range(2):
            for cp in w_copies(ahead):
                cp.start()

        def clear(r, c):
            xy_ref[pl.ds(pl.multiple_of(r * MOE_CLEAR_ROWS, MOE_CLEAR_ROWS), MOE_CLEAR_ROWS), :] = (
                jnp.zeros((MOE_CLEAR_ROWS, LANES), jnp.int32))
            return c

        lax.fori_loop(0, MOE_ROWS * MOE_SUB // MOE_CLEAR_ROWS, clear, 0)
        tail = MOE_ROWS * MOE_SUB % MOE_CLEAR_ROWS
        if tail:
            xy_ref[pl.ds(MOE_ROWS * MOE_SUB - tail, tail), :] = jnp.zeros((tail, LANES), jnp.int32)

    def x_copy(c, slot, block=i):
        return pltpu.make_async_copy(
            xp_hbm.at[pl.ds(block * MOE_TB + c * MOE_XCH, MOE_XCH), :], xbuf.at[slot], xsem.at[slot])

    def base_copy(c, slot):
        return pltpu.make_async_copy(
            base_hbm.at[pl.ds(i * MOE_TB + c * MOE_OCH, MOE_OCH), :], bbuf.at[slot], bsem.at[slot])

    def out_copy(c, slot):
        return pltpu.make_async_copy(
            obuf.at[slot], o_hbm.at[pl.ds(i * MOE_TB + c * MOE_OCH, MOE_OCH), :], osem.at[slot])

    def dispatch():
        n_chunks = MOE_TB // MOE_XCH
        for c in range(n_chunks):
            slot = c % 2
            if c + 1 < n_chunks:
                x_copy(c + 1, 1 - slot).start()
            x_copy(c, slot).wait()

            def step(tt, carry, c=c, slot=slot):
                slab = xbuf[slot, pl.ds(pl.multiple_of(tt * MOE_UNROLL, MOE_UNROLL), MOE_UNROLL), :]
                for u in range(MOE_UNROLL):
                    t = tt * MOE_UNROLL + u
                    row = jnp.concatenate([slab[u:u + 1, j * LANES:(j + 1) * LANES]
                                           for j in range(MOE_SUB)], axis=0)
                    for k in range(TOP_K):
                        p = pl.multiple_of(pos_ref[TOP_K * (c * MOE_XCH + t) + k], MOE_SUB)
                        xy_ref[pl.ds(p, MOE_SUB), :] = row
                return carry

            lax.fori_loop(0, MOE_XCH // MOE_UNROLL, step, 0)

    def ffn_stages(r0, m, wslot):
        base_row = pl.multiple_of(MOE_SUB * r0, 8)
        words = jnp.concatenate(
            [xy_ref[pl.ds(base_row + j, m, stride=MOE_SUB), :] for j in range(MOE_SUB)], axis=1)
        lo = _unpack_lo(words).astype(BF16)
        hi = _unpack_hi(words).astype(BF16)
        yield
        gu = _dot(lo, wgu_buf[wslot, :MOE_PACK, :]) + _dot(hi, wgu_buf[wslot, MOE_PACK:, :])
        yield
        act = (jax.nn.silu(gu[:, :EXPERT_FF]) * gu[:, EXPERT_FF:]).astype(BF16)
        y = _dot(act, wd_buf[wslot])
        yield
        ya = lax.bitcast_convert_type(y[:, :MOE_PACK].astype(BF16).astype(F32), jnp.int32)
        yb = lax.bitcast_convert_type(y[:, MOE_PACK:].astype(BF16).astype(F32), jnp.int32)
        packed = lax.shift_right_logical(ya, 16) | yb
        for j in range(MOE_SUB):
            xy_ref[pl.ds(base_row + j, m, stride=MOE_SUB), :] = packed[:, j * LANES:(j + 1) * LANES]
        yield

    def ffn(r0, m, wslot):
        _round_robin(ffn_stages(r0, m, wslot))

    def segment(visit):
        wslot = visit % MOE_WBUF
        off = off_ref[visit]
        nt = nt_ref[visit]

        def chunk(c, inner):
            ffn(off + c * (MOE_MAX_TILES * MOE_RT), MOE_MAX_TILES * MOE_RT, wslot)
            return inner

        full = nt // MOE_MAX_TILES
        lax.fori_loop(0, full, chunk, 0)
        rest = nt - full * MOE_MAX_TILES
        for tiles in range(1, MOE_MAX_TILES):
            @pl.when(rest == tiles)
            def _():
                ffn(off + full * (MOE_MAX_TILES * MOE_RT), tiles * MOE_RT, wslot)

    def expert_pair(j, carry):
        v0 = i * N_EXPERTS + 2 * j
        for v in (v0 + 2, v0 + 3):
            @pl.when(v < n_visits)
            def _(v=v):
                for cp in w_copies(v):
                    cp.start()

        for v in (v0, v0 + 1):
            for cp in w_copies(v):
                cp.wait()
        nt0, nt1 = nt_ref[v0], nt_ref[v0 + 1]
        paired = jnp.bool_(False)
        for ta in MOE_PAIR_TILES:
            for tb in MOE_PAIR_TILES:
                hit = (nt0 == ta) & (nt1 == tb)
                paired = paired | hit

                @pl.when(hit)
                def _(ta=ta, tb=tb):
                    _round_robin(ffn_stages(off_ref[v0], ta * MOE_RT, v0 % MOE_WBUF),
                                 ffn_stages(off_ref[v0 + 1], tb * MOE_RT, (v0 + 1) % MOE_WBUF))

        @pl.when(jnp.logical_not(paired))
        def _():
            segment(v0)
            segment(v0 + 1)

        return carry

    def combine():
        n_chunks = MOE_TB // MOE_OCH
        for c in range(n_chunks):
            slot = c % 2
            if c + 1 < n_chunks:
                base_copy(c + 1, 1 - slot).start()
            base_copy(c, slot).wait()
            if c >= 2:
                out_copy(c - 2, slot).wait()

            def step(tt, carry, c=c, slot=slot):
                for u in range(MOE_UNROLL):
                    t = tt * MOE_UNROLL + u
                    lo = jnp.zeros((MOE_SUB, LANES), F32)
                    hi = jnp.zeros((MOE_SUB, LANES), F32)
                    for k in range(TOP_K):
                        s = TOP_K * (c * MOE_OCH + t) + k
                        w = xy_ref[pl.ds(pl.multiple_of(pos_ref[s], MOE_SUB), MOE_SUB), :]
                        g = gate_ref[s]
                        lo = lo + g * _unpack_lo(w)
                        hi = hi + g * _unpack_hi(w)
                    routed = jnp.concatenate([lo[j:j + 1] for j in range(MOE_SUB)]
                                             + [hi[j:j + 1] for j in range(MOE_SUB)], axis=1)
                    obuf[slot, pl.ds(t, 1), :] = bbuf[slot, pl.ds(t, 1), :] + routed
                return carry

            lax.fori_loop(0, MOE_OCH // MOE_UNROLL, step, 0)
            out_copy(c, slot).start()
        out_copy(n_chunks - 2, n_chunks % 2).wait()
        out_copy(n_chunks - 1, (n_chunks - 1) % 2).wait()

    @pl.when(i == 0)
    def _():
        x_copy(0, 0).start()

    dispatch()
    base_copy(0, 0).start()

    @pl.when(i + 1 < pl.num_programs(0))
    def _():
        x_copy(0, 0, block=i + 1).start()
    lax.fori_loop(0, N_EXPERTS // 2, expert_pair, 0)
    combine()


def _moe_call(off, nt, pos, gates, xp, base, wgu, wd):
    T = base.shape[0]
    slots = MOE_TB * TOP_K
    grid_spec = pltpu.PrefetchScalarGridSpec(
        num_scalar_prefetch=2,
        grid=(T // MOE_TB,),
        in_specs=[
            pl.BlockSpec((slots,), lambda i, o, n: (i,), memory_space=pltpu.SMEM),
            pl.BlockSpec((slots,), lambda i, o, n: (i,), memory_space=pltpu.SMEM),
            pl.BlockSpec(memory_space=pl.ANY),
            pl.BlockSpec(memory_space=pl.ANY),
            pl.BlockSpec(memory_space=pl.ANY),
            pl.BlockSpec(memory_space=pl.ANY),
        ],
        out_specs=pl.BlockSpec(memory_space=pl.ANY),
        scratch_shapes=[
            pltpu.VMEM((MOE_ROWS * MOE_SUB, LANES), jnp.int32),
            pltpu.VMEM((2, MOE_XCH, MOE_PACK), jnp.int32),
            pltpu.VMEM((2, MOE_OCH, D_MODEL), F32),
            pltpu.VMEM((2, MOE_OCH, D_MODEL), F32),
            pltpu.VMEM((MOE_WBUF, D_MODEL, 2 * EXPERT_FF), BF16),
            pltpu.VMEM((MOE_WBUF, EXPERT_FF, D_MODEL), BF16),
            pltpu.SemaphoreType.DMA((2,)),
            pltpu.SemaphoreType.DMA((2,)),
            pltpu.SemaphoreType.DMA((2,)),
            pltpu.SemaphoreType.DMA((2, MOE_WBUF)),
        ],
    )
    return pl.pallas_call(
        _moe_kernel,
        grid_spec=grid_spec,
        out_shape=jax.ShapeDtypeStruct((T, D_MODEL), F32),
        compiler_params=pltpu.CompilerParams(dimension_semantics=("arbitrary",),
                                             vmem_limit_bytes=VMEM_LIMIT_BYTES),
        name="moe",
    )(off, nt, pos, gates, xp, base, wgu, wd)


def _route_tables(eidx, gate, rank, cnt):
    T = eidx.shape[1]
    nblk = T // MOE_TB
    cnt = cnt.reshape(nblk, -1, N_EXPERTS)
    per_blk = cnt.shape[1]
    earlier = jnp.cumsum(cnt, axis=1) - cnt
    tiles = (cnt.sum(axis=1) + MOE_RT - 1) // MOE_RT
    off = jnp.cumsum(tiles, axis=1) * MOE_RT - tiles * MOE_RT
    start = (off[:, None, :] + earlier).reshape(nblk * per_blk, N_EXPERTS)
    e_tk = eidx.T.reshape(nblk * per_blk, T // (nblk * per_blk), TOP_K)
    chosen = e_tk[..., None] == jnp.arange(N_EXPERTS, dtype=jnp.int32)
    pos = jnp.where(chosen, start[:, None, None, :], 0).sum(axis=-1).reshape(T, TOP_K) + rank.T
    return (off.reshape(-1).astype(jnp.int32), tiles.reshape(-1).astype(jnp.int32),
            (pos * MOE_SUB).reshape(-1).astype(jnp.int32), gate.T.reshape(-1))


def _slot_cols(w, width):
    k = w.shape[0]
    w = w.reshape(k, MLA_HEADS, width)
    return jnp.pad(w, ((0, 0), (0, 0), (0, SLOT - width))).reshape(k, MLA_HEADS * SLOT)


def _block_flags(pos, tq, tk, chunked):
    B, S = pos.shape
    p = jnp.right_shift(pos, CHUNK_SHIFT) if chunked else pos
    qmin = p.reshape(B, S // tq, tq).min(-1)[:, :, None]
    qmax = p.reshape(B, S // tq, tq).max(-1)[:, :, None]
    kmin = p.reshape(B, S // tk, tk).min(-1)[:, None, :]
    kmax = p.reshape(B, S // tk, tk).max(-1)[:, None, :]
    if chunked:
        none, all_ = kmin > qmax, kmax <= qmin
    else:
        none, all_ = kmin >= qmax, kmax < qmin
    return jnp.where(none, 0, jnp.where(all_, 1, 2)).astype(jnp.int32)


def kernel(x, positions, norm1, w_in, q_a_norm, w_uq, kv_a_norm, w_ukv, q_norm, k_norm,
           out_norm_mla, out_norm_sb, w_o, norm2, w_router, router_bias, w_gate_up, w_down,
           w_shared_gate_up, w_shared_down):
    B, S, D = x.shape
    T = B * S
    c0 = Q_LORA
    c1 = c0 + KV_LORA
    c2 = c1 + MLA_ROPE

    row2 = lambda g: g.reshape(1, -1).astype(F32)
    wcq = w_in[:, :c0].astype(BF16)
    wckv = w_in[:, c0:c1].astype(BF16)
    wkr = jnp.pad(w_in[:, c1:c2], ((0, 0), (MLA_NOPE, SLOT - MLA_QK))).astype(BF16)
    wsb = w_in[:, c2:].astype(BF16)
    wuq = _slot_cols(w_uq, MLA_QK).astype(BF16)
    w_ukv3 = w_ukv.reshape(KV_LORA, MLA_HEADS, MLA_NOPE + MLA_V)
    wuk = _slot_cols(w_ukv3[:, :, :MLA_NOPE].reshape(KV_LORA, -1), MLA_NOPE).astype(BF16)
    wuv = _slot_cols(w_ukv3[:, :, MLA_NOPE:].reshape(KV_LORA, MLA_WIDTH), MLA_V).astype(BF16)
    slot_lane = jnp.arange(SLOT)
    vone = jnp.tile((slot_lane >= MLA_V).astype(F32), MLA_HEADS).reshape(1, MLA_HEADS * SLOT)
    gq = jnp.pad(q_norm, (0, SLOT - MLA_QK)).reshape(1, SLOT)
    gk = jnp.pad(k_norm, (0, SLOT - MLA_QK)).reshape(1, SLOT)
    inv = ROPE_THETA ** (-jnp.arange(HALF_ROPE, dtype=F32) / HALF_ROPE)
    inv_slot = jnp.pad(jnp.concatenate([inv, inv]), (MLA_NOPE, SLOT - MLA_QK)).reshape(1, SLOT)

    bound = (math.sqrt(MLA_QK) * LOG2E * BOUND_MARGIN) * jnp.max(jnp.abs(q_norm)) * jnp.max(jnp.abs(k_norm))
    bounded = bound <= MLA_MAX_SHIFT
    shift = jnp.where(bounded, bound, 0.0)
    qaug = (slot_lane == MLA_QK).astype(F32).reshape(1, SLOT)
    kaug = -shift * qaug

    x2 = x.reshape(T, D)
    pos_col = positions.reshape(T, 1)
    q, k, v, sq, sk, sv = _proj_call(
        x2, pos_col,
        (row2(norm1), wcq, wckv, wkr, wsb, row2(q_a_norm), wuq, row2(kv_a_norm), wuk, wuv, vone,
         gq, gk, inv_slot, qaug, kaug), tm=PROJ_TM)

    pos_c3 = positions.reshape(B, S, 1)
    pos_r3 = positions.reshape(B, 1, S)
    r3 = lambda a: a.reshape(B, S, a.shape[-1])

    mla_flags = _block_flags(positions, MLA_TQ, MLA_TK, True)
    sb_flags = _block_flags(positions, SB_TQ, SB_TK, False)
    kstart = jnp.max(jnp.where(sb_flags != 0, jnp.arange(S // SB_TK, dtype=jnp.int32), -1),
                     axis=-1).astype(jnp.int32)
    attn_args = (mla_flags, kstart, r3(q), r3(k), r3(v), r3(sq), r3(sk), r3(sv), pos_c3, pos_r3)

    def separate(mla_flags, kstart, q, k, v, sq, sk, sv, pos_c3, pos_r3):
        o_mla = _mla_call(mla_flags, q, k, v, pos_c3, pos_r3, MLA_TQ, MLA_TK, hp=2, online=True)
        o_sb = _sb_call(kstart, sq, sk, sv, pos_c3, pos_r3, SB_TQ, SB_TK, win=SB_WIN)
        return o_mla, o_sb

    o_mla, o_sb = lax.cond(
        bounded,
        lambda *a: tuple(_attn_call(*a, MLA_TQ, MLA_TK, hp=MLA_HEADS_PER_STEP)),
        separate, *attn_args)

    wr_t = w_router.T
    wrh = wr_t.astype(BF16)
    wrl = (wr_t - wrh.astype(F32)).astype(BF16)
    base, hn, eidx, gate, rank, cnt = _merge_call(
        x2, o_mla.reshape(T, MLA_WIDTH), o_sb.reshape(T, SB_WIDTH), row2(out_norm_mla),
        row2(out_norm_sb), w_o.astype(BF16), row2(norm2), wrh, wrl,
        router_bias.reshape(N_EXPERTS, 1).astype(F32), w_shared_gate_up.astype(BF16),
        w_shared_down.astype(BF16), tm=MERGE_TM)

    off, tiles, pos, gates = _route_tables(eidx, gate, rank, cnt)
    out = _moe_call(off, tiles, pos, gates, hn, base, w_gate_up.astype(BF16),
                    w_down.astype(BF16))
    return out.reshape(B, S, D)
```

```python
import functools
import math

import jax
import jax.numpy as jnp
from jax import lax
from jax.experimental import pallas as pl
from jax.experimental.pallas import tpu as pltpu

D_MODEL = 1024
CHUNK = 64
MLA_HEADS = 8
MLA_NOPE = 64
MLA_ROPE = 32
MLA_QK = MLA_NOPE + MLA_ROPE
MLA_V = 64
Q_LORA = 256
KV_LORA = 128
ROPE_THETA = 10000.0
SB_HEADS = 8
SB_HEAD_DIM = 64
SB_WIDTH = SB_HEADS * SB_HEAD_DIM
MLA_WIDTH = MLA_HEADS * MLA_V
N_EXPERTS = 64
TOP_K = 8
EXPERT_FF = 256
SHARED_FF = 256
ROUTED_SCALE = 2.5
EPS = 1e-6
NEG = -1e30

LANES = 128
VMEM_LIMIT_BYTES = 56 * 1024 * 1024

SLOT = LANES
HALF_ROPE = MLA_ROPE // 2

SB_LOG_ZERO = -90.0

LOG2E = 1.4426950408889634
BOUND_MARGIN = 1.02
MLA_MAX_SHIFT = 50.0

BF16 = jnp.bfloat16
F32 = jnp.float32
HI_MASK = -65536
CHUNK_SHIFT = CHUNK.bit_length() - 1
assert 1 << CHUNK_SHIFT == CHUNK

PROJ_TM = 512
MERGE_TM = 1024
MLA_TQ, MLA_TK = 512, 512
MLA_HEADS_PER_STEP = 4
SB_TQ, SB_TK, SB_WIN = 256, 128, 4


def _rms(x, g):
    return x * lax.rsqrt(jnp.mean(x * x, axis=-1, keepdims=True) + EPS) * g


def _dot(a, b):
    return jnp.dot(a, b, preferred_element_type=F32)


def _dot_nt(a, b):
    return lax.dot_general(a, b, (((1,), (1,)), ((), ())), preferred_element_type=F32)


ROPE_PACK = LANES // HALF_ROPE


def _rope_table_kernel(pos_ref, inv_ref, cos_ref, sin_ref):
    ang = pos_ref[...].astype(F32) * inv_ref[...]
    cos_ref[...] = jnp.cos(ang)
    sin_ref[...] = jnp.sin(ang)


def _rope_tables(positions, inv):
    T = positions.shape[0]
    rows = T // ROPE_PACK
    pos = jnp.repeat(positions.reshape(rows, ROPE_PACK), HALF_ROPE, axis=1)
    inv_row = jnp.tile(inv, ROPE_PACK).reshape(1, LANES)
    tr = min(rows, 1024)
    cos, sin = pl.pallas_call(
        _rope_table_kernel,
        grid=(rows // tr,),
        in_specs=[pl.BlockSpec((tr, LANES), lambda i: (i, 0)), pl.BlockSpec((1, LANES), lambda i: (0, 0))],
        out_specs=[pl.BlockSpec((tr, LANES), lambda i: (i, 0))] * 2,
        out_shape=[jax.ShapeDtypeStruct((rows, LANES), F32)] * 2,
        name="rope_tables",
    )(pos, inv_row)
    return cos.reshape(T, HALF_ROPE), sin.reshape(T, HALF_ROPE)


def _proj_kernel(x_ref, cos_ref, sin_ref, g1_ref, wcq_ref, wckv_ref, wkr_ref, wsb_ref, gqa_ref, wuq_ref,
                 gkva_ref, wuk_ref, wuv_ref, vone_ref, gq_ref, gk_ref, qaug_ref, kaug_ref,
                 q_ref, k_ref, v_ref, sq_ref, sk_ref, sv_ref):
    x = x_ref[...]
    xn = _rms(x, g1_ref[...]).astype(BF16)

    cq = _rms(_dot(xn, wcq_ref[...]), gqa_ref[...]).astype(BF16)
    q = _dot(cq, wuq_ref[...])
    ckv = _rms(_dot(xn, wckv_ref[...]), gkva_ref[...]).astype(BF16)
    kn = _dot(ckv, wuk_ref[...])
    v_ref[...] = (_dot(ckv, wuv_ref[...]) + vone_ref[...]).astype(BF16)
    kr = _dot(xn, wkr_ref[...])

    sb = _dot(xn, wsb_ref[...])
    sq_ref[...] = (sb[:, :SB_WIDTH] * (1.0 / math.sqrt(SB_HEAD_DIM))).astype(BF16)
    sk_ref[...] = sb[:, SB_WIDTH:2 * SB_WIDTH].astype(BF16)
    sv_ref[...] = sb[:, 2 * SB_WIDTH:].astype(BF16)

    cos = cos_ref[...]
    sin = sin_ref[...]

    def rope(t):
        return t * cos + pltpu.roll(t, SLOT // 2, 1) * sin

    def head_norm(t, g):
        ss = jnp.sum(t * t, axis=-1, keepdims=True) * (1.0 / MLA_QK)
        return t * lax.rsqrt(ss + EPS) * g

    scale = LOG2E / math.sqrt(MLA_QK)
    for h in range(MLA_HEADS):
        sl = slice(h * SLOT, (h + 1) * SLOT)
        qh = rope(head_norm(q[:, sl], gq_ref[...])) * scale + qaug_ref[...]
        q_ref[:, sl] = qh.astype(BF16)
        kh = rope(head_norm(kn[:, sl] + kr, gk_ref[...])) + kaug_ref[...]
        k_ref[:, sl] = kh.astype(BF16)


def _proj_call(x2, cos_slot, sin_slot, weights, tm):
    T = x2.shape[0]
    full = lambda a: pl.BlockSpec(a.shape, lambda i: (0,) * a.ndim)
    row = lambda w: pl.BlockSpec((tm, w), lambda i: (i, 0))
    out_w = (MLA_HEADS * SLOT, MLA_HEADS * SLOT, MLA_HEADS * SLOT, SB_WIDTH, SB_WIDTH, SB_WIDTH)
    return pl.pallas_call(
        _proj_kernel,
        grid=(T // tm,),
        in_specs=[row(D_MODEL), row(SLOT), row(SLOT)] + [full(w) for w in weights],
        out_specs=[row(w) for w in out_w],
        out_shape=[jax.ShapeDtypeStruct((T, w), BF16) for w in out_w],
        compiler_params=pltpu.CompilerParams(dimension_semantics=("arbitrary",),
                                             vmem_limit_bytes=VMEM_LIMIT_BYTES),
        name="proj",
    )(x2, cos_slot, sin_slot, *weights)


def _mla_kernel(flags_ref, q_ref, k_ref, v_ref, posq_ref, posk_ref, o_ref, acc_ref, m_ref,
                *, tq, tk, nk, hp, online):
    b = pl.program_id(0)
    qi = pl.program_id(2)
    qchunk = jnp.right_shift(posq_ref[0], CHUNK_SHIFT)
    acc_ref[...] = jnp.zeros(acc_ref.shape, F32)
    if online:
        m_ref[...] = jnp.full(m_ref.shape, NEG, F32)

    def block(kb, masked):
        k0 = pl.multiple_of(kb * tk, tk)
        if masked:
            kchunk = jnp.right_shift(posk_ref[0, :, pl.ds(k0, tk)], CHUNK_SHIFT)
            vis = kchunk <= qchunk
        for h in range(hp):
            hs = slice(h * SLOT, (h + 1) * SLOT)
            s = _dot_nt(q_ref[0, :, hs], k_ref[0, pl.ds(k0, tk), hs])
            if masked:
                s = jnp.where(vis, s, NEG)
            vh = v_ref[0, pl.ds(k0, tk), hs]
            if online:
                m_old = m_ref[h]
                m_new = jnp.maximum(m_old, jnp.max(s, axis=-1, keepdims=True))
                p = jnp.exp2(s - m_new)
                acc_ref[h] = jnp.exp2(m_old - m_new) * acc_ref[h] + _dot(p.astype(BF16), vh)
                m_ref[h] = m_new
            else:
                acc_ref[h] += _dot(jnp.exp2(s).astype(BF16), vh)

    def body(kb, carry):
        flag = flags_ref[b, qi, kb]

        @pl.when(flag == 1)
        def _():
            block(kb, False)

        @pl.when(flag == 2)
        def _():
            block(kb, True)

        return carry

    lax.fori_loop(0, nk, body, 0)
    lane = lax.broadcasted_iota(jnp.int32, (tq, LANES), 1)
    for j in range(hp // 2):
        o0 = acc_ref[2 * j] / pltpu.roll(acc_ref[2 * j], MLA_V, 1)
        o1 = acc_ref[2 * j + 1] / pltpu.roll(acc_ref[2 * j + 1], MLA_V, 1)
        o_ref[0, :, j * LANES:(j + 1) * LANES] = jnp.where(lane < MLA_V, o0, pltpu.roll(o1, MLA_V, 1))


def _mla_call(flags, q, k, v, pos_col, pos_row, tq, tk, hp, online):
    B, S, _ = q.shape
    nq, nk = S // tq, S // tk
    grid_spec = pltpu.PrefetchScalarGridSpec(
        num_scalar_prefetch=1,
        grid=(B, MLA_HEADS // hp, nq),
        in_specs=[
            pl.BlockSpec((1, tq, hp * SLOT), lambda b, p, i, f: (b, i, p)),
            pl.BlockSpec((1, S, hp * SLOT), lambda b, p, i, f: (b, 0, p)),
            pl.BlockSpec((1, S, hp * SLOT), lambda b, p, i, f: (b, 0, p)),
            pl.BlockSpec((1, tq, 1), lambda b, p, i, f: (b, i, 0)),
            pl.BlockSpec((1, 1, S), lambda b, p, i, f: (b, 0, 0)),
        ],
        out_specs=pl.BlockSpec((1, tq, hp * MLA_V), lambda b, p, i, f: (b, i, p)),
        scratch_shapes=[pltpu.VMEM((hp, tq, LANES), F32), pltpu.VMEM((hp, tq, 1), F32)],
    )
    return pl.pallas_call(
        functools.partial(_mla_kernel, tq=tq, tk=tk, nk=nk, hp=hp, online=online),
        grid_spec=grid_spec,
        out_shape=jax.ShapeDtypeStruct((B, S, MLA_WIDTH), F32),
        compiler_params=pltpu.CompilerParams(
            dimension_semantics=("arbitrary", "arbitrary", "arbitrary"),
            vmem_limit_bytes=VMEM_LIMIT_BYTES),
        name="mla_attn_online" if online else "mla_attn",
    )(flags, q, k, v, pos_col, pos_row)


def _softplus(z):
    return jnp.maximum(z, 0.0) + jnp.log(1.0 + jnp.exp(-jnp.abs(z)))


def _sb_kernel(kstart_ref, q_ref, k_ref, v_ref, posq_ref, posk_ref, o_ref, run_ref, acc_ref,
               *, tq, tk, win):
    b = pl.program_id(0)
    qi = pl.program_id(2)
    lane = lax.broadcasted_iota(jnp.int32, (tq, LANES), 1)
    row_i = lax.broadcasted_iota(jnp.int32, (tk, tk), 0)
    col_i = lax.broadcasted_iota(jnp.int32, (tk, tk), 1)
    tri = (row_i >= col_i).astype(BF16)
    kstart = kstart_ref[b, qi]
    q_pair = q_ref[0]
    zero = jnp.zeros_like(q_pair)
    q2 = jnp.concatenate([jnp.where(lane < SB_HEAD_DIM, q_pair, zero),
                          jnp.where(lane >= SB_HEAD_DIM, q_pair, zero)], axis=0)
    qpos = jnp.concatenate([posq_ref[0], posq_ref[0]], axis=0)

    def suffix_sums(l1m):
        hi = lax.bitcast_convert_type(
            lax.bitcast_convert_type(l1m, jnp.int32) & jnp.int32(HI_MASK), F32)
        lo = l1m - hi
        both = _dot(jnp.concatenate([hi.astype(BF16), lo.astype(BF16)], axis=0), tri)
        return both[:2 * tq] + both[2 * tq:]

    run_ref[...] = jnp.zeros(run_ref.shape, F32)
    acc_ref[...] = jnp.zeros(acc_ref.shape, F32)

    @pl.when(kstart >= win - 1)
    def _():
        k0 = pl.multiple_of((kstart - (win - 1)) * tk, tk)
        causal = posk_ref[0, :, pl.ds(k0, win * tk)] < qpos
        z = _dot_nt(q2, k_ref[0, pl.ds(k0, win * tk), :])
        l1m = jnp.where(causal, -_softplus(z), 0.0)
        run = jnp.zeros((2 * tq, 1), F32)
        a_blocks = [None] * win
        for j in reversed(range(win)):
            cs = slice(j * tk, (j + 1) * tk)
            suffix = suffix_sums(l1m[:, cs])
            a = jnp.exp(z[:, cs] + suffix + run)
            a_blocks[j] = jnp.where(causal[:, cs], a, 0.0).astype(BF16)
            run = run + suffix[:, 0:1]
        acc_ref[...] = _dot(jnp.concatenate(a_blocks, axis=1), v_ref[0, pl.ds(k0, win * tk), :])
        run_ref[...] = run

    def block(kb):
        k0 = pl.multiple_of(kb * tk, tk)
        z = _dot_nt(q2, k_ref[0, pl.ds(k0, tk), :])
        causal = posk_ref[0, :, pl.ds(k0, tk)] < qpos
        l1m = jnp.where(causal, -_softplus(z), 0.0)
        suffix = suffix_sums(l1m)
        run = run_ref[...]
        a = jnp.where(causal, jnp.exp(z + suffix + run), 0.0)
        acc_ref[...] += _dot(a.astype(BF16), v_ref[0, pl.ds(k0, tk), :])
        run_new = run + suffix[:, 0:1]
        run_ref[...] = run_new
        return jnp.max(run_new)

    def cond(c):
        kb, top = c
        return (kb >= 0) & (top >= SB_LOG_ZERO)

    def body(c):
        kb, _ = c
        return kb - 1, block(kb)

    kb_first = jnp.where(kstart >= win - 1, kstart - win, kstart)
    lax.while_loop(cond, body, (kb_first, jnp.max(run_ref[...])))

    o_ref[0] = jnp.where(lane < SB_HEAD_DIM, acc_ref[:tq], acc_ref[tq:])


def _sb_call(kstart, q, k, v, pos_col, pos_row, tq, tk, win):
    B, S, _ = q.shape
    nq = S // tq
    grid_spec = pltpu.PrefetchScalarGridSpec(
        num_scalar_prefetch=1,
        grid=(B, SB_HEADS // 2, nq),
        in_specs=[
            pl.BlockSpec((1, tq, LANES), lambda b, p, i, s: (b, i, p)),
            pl.BlockSpec((1, S, LANES), lambda b, p, i, s: (b, 0, p)),
            pl.BlockSpec((1, S, LANES), lambda b, p, i, s: (b, 0, p)),
            pl.BlockSpec((1, tq, 1), lambda b, p, i, s: (b, i, 0)),
            pl.BlockSpec((1, 1, S), lambda b, p, i, s: (b, 0, 0)),
        ],
        out_specs=pl.BlockSpec((1, tq, LANES), lambda b, p, i, s: (b, i, p)),
        scratch_shapes=[pltpu.VMEM((2 * tq, 1), F32), pltpu.VMEM((2 * tq, LANES), F32)],
    )
    return pl.pallas_call(
        functools.partial(_sb_kernel, tq=tq, tk=tk, win=win),
        grid_spec=grid_spec,
        out_shape=jax.ShapeDtypeStruct((B, S, SB_WIDTH), F32),
        compiler_params=pltpu.CompilerParams(
            dimension_semantics=("arbitrary", "arbitrary", "arbitrary"),
            vmem_limit_bytes=VMEM_LIMIT_BYTES),
        name="sb_attn",
    )(kstart, q, k, v, pos_col, pos_row)


def _round_robin(*gens):
    gens = list(gens)
    while gens:
        for g in list(gens):
            if next(g, StopIteration) is StopIteration:
                gens.remove(g)


def _attn_kernel(flags_ref, kstart_ref, q_ref, k_ref, v_ref, sq_ref, sk_ref, sv_ref, posq_ref,
                 posk_ref, om_ref, os_ref, acc_ref, run_ref, sacc_ref, *, tq, tk, nk, hp):
    b = pl.program_id(0)
    qi = pl.program_id(2)
    stq, stk, win = SB_TQ, SB_TK, SB_WIN
    halves = tq // stq
    units = halves * (hp // 2)
    qchunk = jnp.right_shift(posq_ref[0], CHUNK_SHIFT)
    acc_ref[...] = jnp.zeros(acc_ref.shape, F32)
    lane = lax.broadcasted_iota(jnp.int32, (stq, LANES), 1)
    tri2 = (lax.broadcasted_iota(jnp.int32, (2 * stk, 2 * stk), 0)
            >= lax.broadcasted_iota(jnp.int32, (2 * stk, 2 * stk), 1)).astype(BF16)
    tri = tri2[:stk, :stk]

    def mla_heads(kb, masked):
        k0 = pl.multiple_of(kb * tk, tk)
        if masked:
            kchunk = jnp.right_shift(posk_ref[0, :, pl.ds(k0, tk)], CHUNK_SHIFT)
            vis = kchunk <= qchunk
        for h in range(hp):
            hs = slice(h * SLOT, (h + 1) * SLOT)
            s = _dot_nt(q_ref[0, :, hs], k_ref[0, pl.ds(k0, tk), hs])
            if masked:
                s = jnp.where(vis, s, NEG)
            acc_ref[h] += _dot(jnp.exp2(s).astype(BF16), v_ref[0, pl.ds(k0, tk), hs])
            yield

    def sb_operands(u):
        rows = pl.ds(pl.multiple_of((u % halves) * stq, stq), stq)
        cols = pl.ds(pl.multiple_of((u // halves) * LANES, LANES), LANES)
        q_pair = sq_ref[0, rows, cols]
        zero = jnp.zeros_like(q_pair)
        q2 = jnp.concatenate([jnp.where(lane < SB_HEAD_DIM, q_pair, zero),
                              jnp.where(lane >= SB_HEAD_DIM, q_pair, zero)], axis=0)
        qpos = jnp.concatenate([posq_ref[0, rows], posq_ref[0, rows]], axis=0)
        return q2, qpos, rows, cols

    def suffix_sums(l1m, triangle):
        hi = lax.bitcast_convert_type(
            lax.bitcast_convert_type(l1m, jnp.int32) & jnp.int32(HI_MASK), F32)
        lo = l1m - hi
        both = _dot(jnp.concatenate([hi.astype(BF16), lo.astype(BF16)], axis=0), triangle)
        return both[:2 * stq] + both[2 * stq:]

    def sb_fast(u, kstart):
        q2, qpos, _, cols = sb_operands(u)
        k0 = pl.multiple_of((kstart - (win - 1)) * stk, stk)
        causal = posk_ref[0, :, pl.ds(k0, win * stk)] < qpos
        z = _dot_nt(q2, sk_ref[0, pl.ds(k0, win * stk), cols])
        l1m = jnp.where(causal, -_softplus(z), 0.0)
        yield
        run = jnp.zeros((2 * stq, 1), F32)
        a_blocks = [None] * (win // 2)
        for j in reversed(range(win // 2)):
            cs = slice(j * 2 * stk, (j + 1) * 2 * stk)
            suffix = suffix_sums(l1m[:, cs], tri2)
            a = jnp.exp(z[:, cs] + suffix + run)
            a_blocks[j] = jnp.where(causal[:, cs], a, 0.0).astype(BF16)
            run = run + suffix[:, 0:1]
            yield
        sacc_ref[...] = _dot(jnp.concatenate(a_blocks, axis=1), sv_ref[0, pl.ds(k0, win * stk), cols])
        run_ref[...] = run
        yield

    def sb_finish(u, kstart, fast):
        q2, qpos, rows, cols = sb_operands(u)

        def block(kb):
            k0 = pl.multiple_of(kb * stk, stk)
            z = _dot_nt(q2, sk_ref[0, pl.ds(k0, stk), cols])
            causal = posk_ref[0, :, pl.ds(k0, stk)] < qpos
            l1m = jnp.where(causal, -_softplus(z), 0.0)
            suffix = suffix_sums(l1m, tri)
            run = run_ref[...]
            a = jnp.where(causal, jnp.exp(z + suffix + run), 0.0)
            sacc_ref[...] += _dot(a.astype(BF16), sv_ref[0, pl.ds(k0, stk), cols])
            run_new = run + suffix[:, 0:1]
            run_ref[...] = run_new
            return jnp.max(run_new)

        def cond(c):
            kb, top = c
            return (kb >= 0) & (top >= SB_LOG_ZERO)

        def body(c):
            kb, _ = c
            return kb - 1, block(kb)

        kb_first = jnp.where(fast, kstart - win, kstart)
        lax.while_loop(cond, body, (kb_first, jnp.max(run_ref[...])))
        os_ref[0, rows, cols] = jnp.where(lane < SB_HEAD_DIM, sacc_ref[:stq], sacc_ref[stq:])

    def with_unit(kb, carry):
        flag = flags_ref[b, qi, kb]
        kstart = kstart_ref[b, qi * halves + kb % halves]
        fast = kstart >= win - 1
        run_ref[...] = jnp.zeros(run_ref.shape, F32)
        sacc_ref[...] = jnp.zeros(sacc_ref.shape, F32)

        @pl.when((flag == 1) & fast)
        def _():
            _round_robin(sb_fast(kb, kstart), mla_heads(kb, False))

        @pl.when((flag == 2) & fast)
        def _():
            _round_robin(sb_fast(kb, kstart), mla_heads(kb, True))

        @pl.when((flag == 0) & fast)
        def _():
            _round_robin(sb_fast(kb, kstart))

        @pl.when((flag == 1) & jnp.logical_not(fast))
        def _():
            _round_robin(mla_heads(kb, False))

        @pl.when((flag == 2) & jnp.logical_not(fast))
        def _():
            _round_robin(mla_heads(kb, True))

        sb_finish(kb, kstart, fast)
        return carry

    def mla_only(kb, carry):
        flag = flags_ref[b, qi, kb]

        @pl.when(flag == 1)
        def _():
            _round_robin(mla_heads(kb, False))

        @pl.when(flag == 2)
        def _():
            _round_robin(mla_heads(kb, True))

        return carry

    lax.fori_loop(0, units, with_unit, 0)
    lax.fori_loop(units, nk, mla_only, 0)
    lane_q = lax.broadcasted_iota(jnp.int32, (tq, LANES), 1)
    for j in range(hp // 2):
        o0 = acc_ref[2 * j] / pltpu.roll(acc_ref[2 * j], MLA_V, 1)
        o1 = acc_ref[2 * j + 1] / pltpu.roll(acc_ref[2 * j + 1], MLA_V, 1)
        om_ref[0, :, j * LANES:(j + 1) * LANES] = jnp.where(lane_q < MLA_V, o0, pltpu.roll(o1, MLA_V, 1))


def _attn_call(flags, kstart, q, k, v, sq, sk, sv, pos_col, pos_row, tq, tk, hp):
    B, S, _ = q.shape
    nq, nk = S // tq, S // tk
    assert (tq // SB_TQ) * (hp // 2) <= nk and SB_HEADS == MLA_HEADS and SB_WIN % 2 == 0
    sbw = (hp // 2) * LANES
    grid_spec = pltpu.PrefetchScalarGridSpec(
        num_scalar_prefetch=2,
        grid=(B, MLA_HEADS // hp, nq),
        in_specs=[
            pl.BlockSpec((1, tq, hp * SLOT), lambda b, p, i, f, s: (b, i, p)),
            pl.BlockSpec((1, S, hp * SLOT), lambda b, p, i, f, s: (b, 0, p)),
            pl.BlockSpec((1, S, hp * SLOT), lambda b, p, i, f, s: (b, 0, p)),
            pl.BlockSpec((1, tq, sbw), lambda b, p, i, f, s: (b, i, p)),
            pl.BlockSpec((1, S, sbw), lambda b, p, i, f, s: (b, 0, p)),
            pl.BlockSpec((1, S, sbw), lambda b, p, i, f, s: (b, 0, p)),
            pl.BlockSpec((1, tq, 1), lambda b, p, i, f, s: (b, i, 0)),
            pl.BlockSpec((1, 1, S), lambda b, p, i, f, s: (b, 0, 0)),
        ],
        out_specs=[pl.BlockSpec((1, tq, hp * MLA_V), lambda b, p, i, f, s: (b, i, p)),
                   pl.BlockSpec((1, tq, sbw), lambda b, p, i, f, s: (b, i, p))],
        scratch_shapes=[pltpu.VMEM((hp, tq, LANES), F32), pltpu.VMEM((2 * SB_TQ, 1), F32),
                        pltpu.VMEM((2 * SB_TQ, LANES), F32)],
    )
    return pl.pallas_call(
        functools.partial(_attn_kernel, tq=tq, tk=tk, nk=nk, hp=hp),
        grid_spec=grid_spec,
        out_shape=[jax.ShapeDtypeStruct((B, S, MLA_WIDTH), F32),
                   jax.ShapeDtypeStruct((B, S, SB_WIDTH), F32)],
        compiler_params=pltpu.CompilerParams(
            dimension_semantics=("arbitrary", "arbitrary", "arbitrary"),
            vmem_limit_bytes=VMEM_LIMIT_BYTES),
        name="attn",
    )(flags, kstart, q, k, v, sq, sk, sv, pos_col, pos_row)


def _split_bf16(a):
    hi = a.astype(BF16)
    lo = (a - hi.astype(F32)).astype(BF16)
    return hi, lo


def _merge_kernel(x_ref, om_ref, os_ref, gm_ref, gs_ref, wo_ref, g2_ref, wrh_ref, wrl_ref,
                  bias_ref, wsgu_ref, wsd_ref, base_ref, hn_ref, eidx_ref, gate_ref, rank_ref,
                  cnt_ref):
    mm = _rms(om_ref[...], gm_ref[...]).astype(BF16)
    ms = _rms(os_ref[...], gs_ref[...]).astype(BF16)
    h = x_ref[...] + _dot(mm, wo_ref[:MLA_WIDTH, :]) + _dot(ms, wo_ref[MLA_WIDTH:, :])
    hn = _rms(h, g2_ref[...])
    hn_hi, hn_lo = _split_bf16(hn)
    bits = lax.bitcast_convert_type(hn_hi.astype(F32), jnp.int32)
    hn_ref[...] = (lax.shift_right_logical(bits[:, :D_MODEL // 2], 16)
                   | (bits[:, D_MODEL // 2:] & jnp.int32(HI_MASK)))

    logits = (_dot_nt(wrh_ref[...], hn_hi) + _dot_nt(wrh_ref[...], hn_lo)
              + _dot_nt(wrl_ref[...], hn_hi))
    scores = jax.nn.sigmoid(logits)
    work = scores + bias_ref[...]
    eidx = lax.broadcasted_iota(jnp.int32, work.shape, 0)
    chosen, chosen_score, hits = [], [], []
    for _ in range(TOP_K):
        top = jnp.max(work, axis=0, keepdims=True)
        first = jnp.min(jnp.where(work == top, eidx, N_EXPERTS), axis=0, keepdims=True)
        hit = eidx == first
        hits.append(hit)
        chosen.append(first)
        chosen_score.append(jnp.sum(jnp.where(hit, scores, 0.0), axis=0, keepdims=True))
        work = jnp.where(hit, -jnp.inf, work)
    sel = jnp.concatenate(chosen_score, axis=0)
    eidx_ref[...] = jnp.concatenate(chosen, axis=0)
    gate_ref[...] = sel / jnp.sum(sel, axis=0, keepdims=True) * ROUTED_SCALE

    tm = work.shape[1]
    member = functools.reduce(jnp.logical_or, hits)
    member = jnp.where(member, 1.0, 0.0)
    tri = (lax.broadcasted_iota(jnp.int32, (tm, tm), 0)
           <= lax.broadcasted_iota(jnp.int32, (tm, tm), 1)).astype(BF16)
    upto = _dot(member.astype(BF16), tri)
    before = upto - member
    rank_ref[...] = jnp.concatenate(
        [jnp.sum(jnp.where(hit, before, 0.0), axis=0, keepdims=True) for hit in hits],
        axis=0).astype(jnp.int32)
    cnt_ref[0] = upto[:, tm - 1:tm].astype(jnp.int32)

    sgu = _dot(hn_hi, wsgu_ref[...])
    act = (jax.nn.silu(sgu[:, :SHARED_FF]) * sgu[:, SHARED_FF:]).astype(BF16)
    base_ref[...] = h + _dot(act, wsd_ref[...])


def _merge_call(x2, o_mla, o_sb, gm, gs, wo, g2, wrh, wrl, bias, wsgu, wsd, tm):
    T = x2.shape[0]
    full = lambda a: pl.BlockSpec(a.shape, lambda i: (0,) * a.ndim)
    row = lambda w: pl.BlockSpec((tm, w), lambda i: (i, 0))
    weights = (gm, gs, wo, g2, wrh, wrl, bias, wsgu, wsd)
    topk = pl.BlockSpec((TOP_K, tm), lambda i: (0, i))
    return pl.pallas_call(
        _merge_kernel,
        grid=(T // tm,),
        in_specs=[row(D_MODEL), row(MLA_WIDTH), row(SB_WIDTH)] + [full(w) for w in weights],
        out_specs=[row(D_MODEL), row(D_MODEL // 2), topk, topk, topk,
                   pl.BlockSpec((1, N_EXPERTS, 1), lambda i: (i, 0, 0))],
        out_shape=[jax.ShapeDtypeStruct((T, D_MODEL), F32),
                   jax.ShapeDtypeStruct((T, D_MODEL // 2), jnp.int32),
                   jax.ShapeDtypeStruct((TOP_K, T), jnp.int32),
                   jax.ShapeDtypeStruct((TOP_K, T), F32),
                   jax.ShapeDtypeStruct((TOP_K, T), jnp.int32),
                   jax.ShapeDtypeStruct((T // tm, N_EXPERTS, 1), jnp.int32)],
        compiler_params=pltpu.CompilerParams(dimension_semantics=("arbitrary",),
                                             vmem_limit_bytes=VMEM_LIMIT_BYTES),
        name="merge",
    )(x2, o_mla, o_sb, *weights)


MOE_TB = 2048
MOE_RT = 64
MOE_MAX_TILES = 6
MOE_ROWS = MOE_TB * TOP_K + N_EXPERTS * (MOE_RT - 1)
MOE_PACK = D_MODEL // 2
MOE_SUB = MOE_PACK // LANES
MOE_UNROLL = 8
MOE_XCH = 256
MOE_OCH = 128
MOE_WBUF = 4
MOE_PAIR_TILES = (4, 5)
MOE_CLEAR_ROWS = 1024


def _unpack_lo(w):
    return lax.bitcast_convert_type(w << 16, F32)


def _unpack_hi(w):
    return lax.bitcast_convert_type(w & jnp.int32(HI_MASK), F32)


def _moe_kernel(off_ref, nt_ref, pos_ref, gate_ref, xp_hbm, base_hbm, wgu_hbm, wd_hbm, o_hbm,
                xy_ref, xbuf, bbuf, obuf, wgu_buf, wd_buf, xsem, bsem, osem, wsem):
    i = pl.program_id(0)

    n_visits = pl.num_programs(0) * N_EXPERTS

    def w_copies(step):
        slot = step % MOE_WBUF
        ex = step % N_EXPERTS
        return (pltpu.make_async_copy(wgu_hbm.at[ex], wgu_buf.at[slot], wsem.at[0, slot]),
                pltpu.make_async_copy(wd_hbm.at[ex], wd_buf.at[slot], wsem.at[1, slot]))

    @pl.when(i == 0)
    def _():
        for ahead in range(2):
            for cp in w_copies(ahead):
                cp.start()

        def clear(r, c):
            xy_ref[pl.ds(pl.multiple_of(r * MOE_CLEAR_ROWS, MOE_CLEAR_ROWS), MOE_CLEAR_ROWS), :] = (
                jnp.zeros((MOE_CLEAR_ROWS, LANES), jnp.int32))
            return c

        lax.fori_loop(0, MOE_ROWS * MOE_SUB // MOE_CLEAR_ROWS, clear, 0)
        tail = MOE_ROWS * MOE_SUB % MOE_CLEAR_ROWS
        if tail:
            xy_ref[pl.ds(MOE_ROWS * MOE_SUB - tail, tail), :] = jnp.zeros((tail, LANES), jnp.int32)

    def x_copy(c, slot, block=i):
        return pltpu.make_async_copy(
            xp_hbm.at[pl.ds(block * MOE_TB + c * MOE_XCH, MOE_XCH), :], xbuf.at[slot], xsem.at[slot])

    def base_copy(c, slot):
        return pltpu.make_async_copy(
            base_hbm.at[pl.ds(i * MOE_TB + c * MOE_OCH, MOE_OCH), :], bbuf.at[slot], bsem.at[slot])

    def out_copy(c, slot):
        return pltpu.make_async_copy(
            obuf.at[slot], o_hbm.at[pl.ds(i * MOE_TB + c * MOE_OCH, MOE_OCH), :], osem.at[slot])

    def dispatch():
        n_chunks = MOE_TB // MOE_XCH
        for c in range(n_chunks):
            slot = c % 2
            if c + 1 < n_chunks:
                x_copy(c + 1, 1 - slot).start()
            x_copy(c, slot).wait()

            def step(tt, carry, c=c, slot=slot):
                slab = xbuf[slot, pl.ds(pl.multiple_of(tt * MOE_UNROLL, MOE_UNROLL), MOE_UNROLL), :]
                for u in range(MOE_UNROLL):
                    t = tt * MOE_UNROLL + u
                    row = jnp.concatenate([slab[u:u + 1, j * LANES:(j + 1) * LANES]
                                           for j in range(MOE_SUB)], axis=0)
                    for k in range(TOP_K):
                        p = pl.multiple_of(pos_ref[TOP_K * (c * MOE_XCH + t) + k], MOE_SUB)
                        xy_ref[pl.ds(p, MOE_SUB), :] = row
                return carry

            lax.fori_loop(0, MOE_XCH // MOE_UNROLL, step, 0)

    def ffn_stages(r0, m, wslot):
        base_row = pl.multiple_of(MOE_SUB * r0, 8)
        words = jnp.concatenate(
            [xy_ref[pl.ds(base_row + j, m, stride=MOE_SUB), :] for j in range(MOE_SUB)], axis=1)
        lo = _unpack_lo(words).astype(BF16)
        hi = _unpack_hi(words).astype(BF16)
        yield
        gu = _dot(lo, wgu_buf[wslot, :MOE_PACK, :]) + _dot(hi, wgu_buf[wslot, MOE_PACK:, :])
        yield
        act = (jax.nn.silu(gu[:, :EXPERT_FF]) * gu[:, EXPERT_FF:]).astype(BF16)
        y = _dot(act, wd_buf[wslot])
        yield
        ya = lax.bitcast_convert_type(y[:, :MOE_PACK].astype(BF16).astype(F32), jnp.int32)
        yb = lax.bitcast_convert_type(y[:, MOE_PACK:].astype(BF16).astype(F32), jnp.int32)
        packed = lax.shift_right_logical(ya, 16) | yb
        for j in range(MOE_SUB):
            xy_ref[pl.ds(base_row + j, m, stride=MOE_SUB), :] = packed[:, j * LANES:(j + 1) * LANES]
        yield

    def ffn(r0, m, wslot):
        _round_robin(ffn_stages(r0, m, wslot))

    def segment(visit):
        wslot = visit % MOE_WBUF
        off = off_ref[visit]
        nt = nt_ref[visit]

        def chunk(c, inner):
            ffn(off + c * (MOE_MAX_TILES * MOE_RT), MOE_MAX_TILES * MOE_RT, wslot)
            return inner

        full = nt // MOE_MAX_TILES
        lax.fori_loop(0, full, chunk, 0)
        rest = nt - full * MOE_MAX_TILES
        for tiles in range(1, MOE_MAX_TILES):
            @pl.when(rest == tiles)
            def _():
                ffn(off + full * (MOE_MAX_TILES * MOE_RT), tiles * MOE_RT, wslot)

    def expert_pair(j, carry):
        v0 = i * N_EXPERTS + 2 * j
        for v in (v0 + 2, v0 + 3):
            @pl.when(v < n_visits)
            def _(v=v):
                for cp in w_copies(v):
                    cp.start()

        for v in (v0, v0 + 1):
            for cp in w_copies(v):
                cp.wait()
        nt0, nt1 = nt_ref[v0], nt_ref[v0 + 1]
        paired = jnp.bool_(False)
        for ta in MOE_PAIR_TILES:
            for tb in MOE_PAIR_TILES:
                hit = (nt0 == ta) & (nt1 == tb)
                paired = paired | hit

                @pl.when(hit)
                def _(ta=ta, tb=tb):
                    _round_robin(ffn_stages(off_ref[v0], ta * MOE_RT, v0 % MOE_WBUF),
                                 ffn_stages(off_ref[v0 + 1], tb * MOE_RT, (v0 + 1) % MOE_WBUF))

        @pl.when(jnp.logical_not(paired))
        def _():
            segment(v0)
            segment(v0 + 1)

        return carry

    def combine():
        n_chunks = MOE_TB // MOE_OCH
        for c in range(n_chunks):
            slot = c % 2
            if c + 1 < n_chunks:
                base_copy(c + 1, 1 - slot).start()
            base_copy(c, slot).wait()
            if c >= 2:
                out_copy(c - 2, slot).wait()

            def step(tt, carry, c=c, slot=slot):
                for u in range(MOE_UNROLL):
                    t = tt * MOE_UNROLL + u
                    lo = jnp.zeros((MOE_SUB, LANES), F32)
                    hi = jnp.zeros((MOE_SUB, LANES), F32)
                    for k in range(TOP_K):
                        s = TOP_K * (c * MOE_OCH + t) + k
                        w = xy_ref[pl.ds(pl.multiple_of(pos_ref[s], MOE_SUB), MOE_SUB), :]
                        g = gate_ref[s]
                        lo = lo + g * _unpack_lo(w)
                        hi = hi + g * _unpack_hi(w)
                    routed = jnp.concatenate([lo[j:j + 1] for j in range(MOE_SUB)]
                                             + [hi[j:j + 1] for j in range(MOE_SUB)], axis=1)
                    obuf[slot, pl.ds(t, 1), :] = bbuf[slot, pl.ds(t, 1), :] + routed
                return carry

            lax.fori_loop(0, MOE_OCH // MOE_UNROLL, step, 0)
            out_copy(c, slot).start()
        out_copy(n_chunks - 2, n_chunks % 2).wait()
        out_copy(n_chunks - 1, (n_chunks - 1) % 2).wait()

    @pl.when(i == 0)
    def _():
        x_copy(0, 0).start()

    dispatch()
    base_copy(0, 0).start()

    @pl.when(i + 1 < pl.num_programs(0))
    def _():
        x_copy(0, 0, block=i + 1).start()
    lax.fori_loop(0, N_EXPERTS // 2, expert_pair, 0)
    combine()


def _moe_call(off, nt, pos, gates, xp, base, wgu, wd):
    T = base.shape[0]
    slots = MOE_TB * TOP_K
    grid_spec = pltpu.PrefetchScalarGridSpec(
        num_scalar_prefetch=2,
        grid=(T // MOE_TB,),
        in_specs=[
            pl.BlockSpec((slots,), lambda i, o, n: (i,), memory_space=pltpu.SMEM),
            pl.BlockSpec((slots,), lambda i, o, n: (i,), memory_space=pltpu.SMEM),
            pl.BlockSpec(memory_space=pl.ANY),
            pl.BlockSpec(memory_space=pl.ANY),
            pl.BlockSpec(memory_space=pl.ANY),
            pl.BlockSpec(memory_space=pl.ANY),
        ],
        out_specs=pl.BlockSpec(memory_space=pl.ANY),
        scratch_shapes=[
            pltpu.VMEM((MOE_ROWS * MOE_SUB, LANES), jnp.int32),
            pltpu.VMEM((2, MOE_XCH, MOE_PACK), jnp.int32),
            pltpu.VMEM((2, MOE_OCH, D_MODEL), F32),
            pltpu.VMEM((2, MOE_OCH, D_MODEL), F32),
            pltpu.VMEM((MOE_WBUF, D_MODEL, 2 * EXPERT_FF), BF16),
            pltpu.VMEM((MOE_WBUF, EXPERT_FF, D_MODEL), BF16),
            pltpu.SemaphoreType.DMA((2,)),
            pltpu.SemaphoreType.DMA((2,)),
            pltpu.SemaphoreType.DMA((2,)),
            pltpu.SemaphoreType.DMA((2, MOE_WBUF)),
        ],
    )
    return pl.pallas_call(
        _moe_kernel,
        grid_spec=grid_spec,
        out_shape=jax.ShapeDtypeStruct((T, D_MODEL), F32),
        compiler_params=pltpu.CompilerParams(dimension_semantics=("arbitrary",),
                                             vmem_limit_bytes=VMEM_LIMIT_BYTES),
        name="moe",
    )(off, nt, pos, gates, xp, base, wgu, wd)


def _route_tables(eidx, gate, rank, cnt):
    T = eidx.shape[1]
    nblk = T // MOE_TB
    cnt = cnt.reshape(nblk, -1, N_EXPERTS)
    per_blk = cnt.shape[1]
    earlier = jnp.cumsum(cnt, axis=1) - cnt
    tiles = (cnt.sum(axis=1) + MOE_RT - 1) // MOE_RT
    off = jnp.cumsum(tiles, axis=1) * MOE_RT - tiles * MOE_RT
    start = (off[:, None, :] + earlier).reshape(nblk * per_blk, N_EXPERTS)
    e_tk = eidx.T.reshape(nblk * per_blk, T // (nblk * per_blk), TOP_K)
    chosen = e_tk[..., None] == jnp.arange(N_EXPERTS, dtype=jnp.int32)
    pos = jnp.where(chosen, start[:, None, None, :], 0).sum(axis=-1).reshape(T, TOP_K) + rank.T
    return (off.reshape(-1).astype(jnp.int32), tiles.reshape(-1).astype(jnp.int32),
            (pos * MOE_SUB).reshape(-1).astype(jnp.int32), gate.T.reshape(-1))


def _qk_slot(a):
    first = SLOT // 2 - HALF_ROPE
    nope, r1, r2 = a[..., :MLA_NOPE], a[..., MLA_NOPE:MLA_NOPE + HALF_ROPE], a[..., MLA_NOPE + HALF_ROPE:]
    pad = jnp.zeros(a.shape[:-1] + (SLOT - MLA_QK,), a.dtype)
    return jnp.concatenate([r1, nope[..., :first], r2, nope[..., first:], pad], axis=-1)


def _slot_cols(w, width):
    k = w.shape[0]
    w = w.reshape(k, MLA_HEADS, width)
    return jnp.pad(w, ((0, 0), (0, 0), (0, SLOT - width))).reshape(k, MLA_HEADS * SLOT)


def _block_flags(pos, tq, tk, chunked):
    B, S = pos.shape
    p = jnp.right_shift(pos, CHUNK_SHIFT) if chunked else pos
    qmin = p.reshape(B, S // tq, tq).min(-1)[:, :, None]
    qmax = p.reshape(B, S // tq, tq).max(-1)[:, :, None]
    kmin = p.reshape(B, S // tk, tk).min(-1)[:, None, :]
    kmax = p.reshape(B, S // tk, tk).max(-1)[:, None, :]
    if chunked:
        none, all_ = kmin > qmax, kmax <= qmin
    else:
        none, all_ = kmin >= qmax, kmax < qmin
    return jnp.where(none, 0, jnp.where(all_, 1, 2)).astype(jnp.int32)


def kernel(x, positions, norm1, w_in, q_a_norm, w_uq, kv_a_norm, w_ukv, q_norm, k_norm,
           out_norm_mla, out_norm_sb, w_o, norm2, w_router, router_bias, w_gate_up, w_down,
           w_shared_gate_up, w_shared_down):
    B, S, D = x.shape
    T = B * S
    c0 = Q_LORA
    c1 = c0 + KV_LORA
    c2 = c1 + MLA_ROPE

    row2 = lambda g: g.reshape(1, -1).astype(F32)
    wcq = w_in[:, :c0].astype(BF16)
    wckv = w_in[:, c0:c1].astype(BF16)
    wkr = _qk_slot(jnp.pad(w_in[:, c1:c2], ((0, 0), (MLA_NOPE, 0)))).astype(BF16)
    wsb = w_in[:, c2:].astype(BF16)
    wuq = _qk_slot(w_uq.reshape(Q_LORA, MLA_HEADS, MLA_QK)).reshape(Q_LORA, MLA_HEADS * SLOT).astype(BF16)
    w_ukv3 = w_ukv.reshape(KV_LORA, MLA_HEADS, MLA_NOPE + MLA_V)
    wuk = _qk_slot(jnp.pad(w_ukv3[:, :, :MLA_NOPE], ((0, 0), (0, 0), (0, MLA_ROPE)))
                   ).reshape(KV_LORA, MLA_HEADS * SLOT).astype(BF16)
    wuv = _slot_cols(w_ukv3[:, :, MLA_NOPE:].reshape(KV_LORA, MLA_WIDTH), MLA_V).astype(BF16)
    slot_lane = jnp.arange(SLOT)
    vone = jnp.tile((slot_lane >= MLA_V).astype(F32), MLA_HEADS).reshape(1, MLA_HEADS * SLOT)
    gq = _qk_slot(q_norm).reshape(1, SLOT)
    gk = _qk_slot(k_norm).reshape(1, SLOT)
    inv = ROPE_THETA ** (-jnp.arange(HALF_ROPE, dtype=F32) / HALF_ROPE)
    cos, sin = _rope_tables(positions.reshape(-1), inv)
    cos_slot = _qk_slot(jnp.concatenate([jnp.ones((T, MLA_NOPE), F32), cos, cos], axis=1))
    sin_slot = _qk_slot(jnp.concatenate([jnp.zeros((T, MLA_NOPE), F32), -sin, sin], axis=1))

    bound = (math.sqrt(MLA_QK) * LOG2E * BOUND_MARGIN) * jnp.max(jnp.abs(q_norm)) * jnp.max(jnp.abs(k_norm))
    bounded = bound <= MLA_MAX_SHIFT
    shift = jnp.where(bounded, bound, 0.0)
    qaug = (slot_lane == MLA_QK).astype(F32).reshape(1, SLOT)
    kaug = -shift * qaug

    x2 = x.reshape(T, D)
    q, k, v, sq, sk, sv = _proj_call(
        x2, cos_slot, sin_slot,
        (row2(norm1), wcq, wckv, wkr, wsb, row2(q_a_norm), wuq, row2(kv_a_norm), wuk, wuv, vone,
         gq, gk, qaug, kaug), tm=PROJ_TM)

    pos_c3 = positions.reshape(B, S, 1)
    pos_r3 = positions.reshape(B, 1, S)
    r3 = lambda a: a.reshape(B, S, a.shape[-1])

    mla_flags = _block_flags(positions, MLA_TQ, MLA_TK, True)
    sb_flags = _block_flags(positions, SB_TQ, SB_TK, False)
    kstart = jnp.max(jnp.where(sb_flags != 0, jnp.arange(S // SB_TK, dtype=jnp.int32), -1),
                     axis=-1).astype(jnp.int32)
    attn_args = (mla_flags, kstart, r3(q), r3(k), r3(v), r3(sq), r3(sk), r3(sv), pos_c3, pos_r3)

    def separate(mla_flags, kstart, q, k, v, sq, sk, sv, pos_c3, pos_r3):
        o_mla = _mla_call(mla_flags, q, k, v, pos_c3, pos_r3, MLA_TQ, MLA_TK, hp=2, online=True)
        o_sb = _sb_call(kstart, sq, sk, sv, pos_c3, pos_r3, SB_TQ, SB_TK, win=SB_WIN)
        return o_mla, o_sb

    o_mla, o_sb = lax.cond(
        bounded,
        lambda *a: tuple(_attn_call(*a, MLA_TQ, MLA_TK, hp=MLA_HEADS_PER_STEP)),
        separate, *attn_args)

    wr_t = w_router.T
    wrh = wr_t.astype(BF16)
    wrl = (wr_t - wrh.astype(F32)).astype(BF16)
    base, hn, eidx, gate, rank, cnt = _merge_call(
        x2, o_mla.reshape(T, MLA_WIDTH), o_sb.reshape(T, SB_WIDTH), row2(out_norm_mla),
        row2(out_norm_sb), w_o.astype(BF16), row2(norm2), wrh, wrl,
        router_bias.reshape(N_EXPERTS, 1).astype(F32), w_shared_gate_up.astype(BF16),
        w_shared_down.astype(BF16), tm=MERGE_TM)

    off, tiles, pos, gates = _route_tables(eidx, gate, rank, cnt)
    out = _moe_call(off, tiles, pos, gates, hn, base, w_gate_up.astype(BF16),
                    w_down.astype(BF16))
    return out.reshape(B, S, D)
```

```python
import functools
import math

import jax
import jax.numpy as jnp
from jax import lax
from jax.experimental import pallas as pl
from jax.experimental.pallas import tpu as pltpu

D_MODEL = 1024
CHUNK = 64
MLA_HEADS = 8
MLA_NOPE = 64
MLA_ROPE = 32
MLA_QK = MLA_NOPE + MLA_ROPE
MLA_V = 64
Q_LORA = 256
KV_LORA = 128
ROPE_THETA = 10000.0
SB_HEADS = 8
SB_HEAD_DIM = 64
SB_WIDTH = SB_HEADS * SB_HEAD_DIM
MLA_WIDTH = MLA_HEADS * MLA_V
N_EXPERTS = 64
TOP_K = 8
EXPERT_FF = 256
SHARED_FF = 256
ROUTED_SCALE = 2.5
EPS = 1e-6
NEG = -1e30

LANES = 128
VMEM_LIMIT_BYTES = 56 * 1024 * 1024

SLOT = LANES
HALF_ROPE = MLA_ROPE // 2

SB_LOG_ZERO = -90.0

LOG2E = 1.4426950408889634
BOUND_MARGIN = 1.02
MLA_MAX_SHIFT = 50.0

BF16 = jnp.bfloat16
F32 = jnp.float32
HI_MASK = -65536
CHUNK_SHIFT = CHUNK.bit_length() - 1
assert 1 << CHUNK_SHIFT == CHUNK

PROJ_TM = 512
MERGE_TM = 1024
MLA_TQ, MLA_TK = 512, 512
MLA_HEADS_PER_STEP = 4
SB_TQ, SB_TK, SB_WIN = 256, 128, 4


def _rms(x, g):
    return x * lax.rsqrt(jnp.mean(x * x, axis=-1, keepdims=True) + EPS) * g


def _dot(a, b):
    return jnp.dot(a, b, preferred_element_type=F32)


def _dot_nt(a, b):
    return lax.dot_general(a, b, (((1,), (1,)), ((), ())), preferred_element_type=F32)


ROPE_PACK = LANES // HALF_ROPE


def _rope_table_kernel(pos_ref, inv_ref, cos_ref, sin_ref):
    ang = pos_ref[...].astype(F32) * inv_ref[...]
    cos_ref[...] = jnp.cos(ang)
    sin_ref[...] = jnp.sin(ang)


def _rope_tables(positions, inv):
    T = positions.shape[0]
    rows = T // ROPE_PACK
    pos = jnp.repeat(positions.reshape(rows, ROPE_PACK), HALF_ROPE, axis=1)
    inv_row = jnp.tile(inv, ROPE_PACK).reshape(1, LANES)
    tr = min(rows, 1024)
    cos, sin = pl.pallas_call(
        _rope_table_kernel,
        grid=(rows // tr,),
        in_specs=[pl.BlockSpec((tr, LANES), lambda i: (i, 0)), pl.BlockSpec((1, LANES), lambda i: (0, 0))],
        out_specs=[pl.BlockSpec((tr, LANES), lambda i: (i, 0))] * 2,
        out_shape=[jax.ShapeDtypeStruct((rows, LANES), F32)] * 2,
        name="rope_tables",
    )(pos, inv_row)
    return cos.reshape(T, HALF_ROPE), sin.reshape(T, HALF_ROPE)


def _proj_kernel(x_ref, cos_ref, sin_ref, g1_ref, wcq_ref, wckv_ref, wkr_ref, wsb_ref, gqa_ref, wuq_ref,
                 gkva_ref, wuk_ref, wuv_ref, vone_ref, gq_ref, gk_ref, qaug_ref, kaug_ref, pcos_ref,
                 psin_ref, cbase_ref,
                 q_ref, k_ref, v_ref, sq_ref, sk_ref, sv_ref):
    x = x_ref[...]
    xn = _rms(x, g1_ref[...]).astype(BF16)

    cq = _rms(_dot(xn, wcq_ref[...]), gqa_ref[...]).astype(BF16)
    q = _dot(cq, wuq_ref[...])
    ckv = _rms(_dot(xn, wckv_ref[...]), gkva_ref[...]).astype(BF16)
    kn = _dot(ckv, wuk_ref[...])
    v_ref[...] = (_dot(ckv, wuv_ref[...]) + vone_ref[...]).astype(BF16)
    kr = _dot(xn, wkr_ref[...])

    sb = _dot(xn, wsb_ref[...])
    sq_ref[...] = (sb[:, :SB_WIDTH] * (1.0 / math.sqrt(SB_HEAD_DIM))).astype(BF16)
    sk_ref[...] = sb[:, SB_WIDTH:2 * SB_WIDTH].astype(BF16)
    sv_ref[...] = sb[:, 2 * SB_WIDTH:].astype(BF16)

    def place(t, pm):
        h1 = t.astype(BF16)
        r1 = t - h1.astype(F32)
        h2 = r1.astype(BF16)
        h3 = (r1 - h2.astype(F32)).astype(BF16)
        return _dot(h1, pm) + _dot(h2, pm) + _dot(h3, pm)

    cos = place(cos_ref[...], pcos_ref[...]) + cbase_ref[...]
    sin = place(sin_ref[...], psin_ref[...])

    def rope(t):
        return t * cos + pltpu.roll(t, SLOT // 2, 1) * sin

    def head_norm(t, g):
        ss = jnp.sum(t * t, axis=-1, keepdims=True) * (1.0 / MLA_QK)
        return t * lax.rsqrt(ss + EPS) * g

    scale = LOG2E / math.sqrt(MLA_QK)
    for h in range(MLA_HEADS):
        sl = slice(h * SLOT, (h + 1) * SLOT)
        qh = rope(head_norm(q[:, sl], gq_ref[...])) * scale + qaug_ref[...]
        q_ref[:, sl] = qh.astype(BF16)
        kh = rope(head_norm(kn[:, sl] + kr, gk_ref[...])) + kaug_ref[...]
        k_ref[:, sl] = kh.astype(BF16)


def _proj_call(x2, cos_slot, sin_slot, weights, tm):
    T = x2.shape[0]
    full = lambda a: pl.BlockSpec(a.shape, lambda i: (0,) * a.ndim)
    row = lambda w: pl.BlockSpec((tm, w), lambda i: (i, 0))
    out_w = (MLA_HEADS * SLOT, MLA_HEADS * SLOT, MLA_HEADS * SLOT, SB_WIDTH, SB_WIDTH, SB_WIDTH)
    return pl.pallas_call(
        _proj_kernel,
        grid=(T // tm,),
        in_specs=[row(D_MODEL), row(SLOT), row(SLOT)] + [full(w) for w in weights],
        out_specs=[row(w) for w in out_w],
        out_shape=[jax.ShapeDtypeStruct((T, w), BF16) for w in out_w],
        compiler_params=pltpu.CompilerParams(dimension_semantics=("arbitrary",),
                                             vmem_limit_bytes=VMEM_LIMIT_BYTES),
        name="proj",
    )(x2, cos_slot, sin_slot, *weights)


def _mla_kernel(flags_ref, q_ref, k_ref, v_ref, posq_ref, posk_ref, o_ref, acc_ref, m_ref,
                *, tq, tk, nk, hp, online):
    b = pl.program_id(0)
    qi = pl.program_id(2)
    qchunk = jnp.right_shift(posq_ref[0], CHUNK_SHIFT)
    acc_ref[...] = jnp.zeros(acc_ref.shape, F32)
    if online:
        m_ref[...] = jnp.full(m_ref.shape, NEG, F32)

    def block(kb, masked):
        k0 = pl.multiple_of(kb * tk, tk)
        if masked:
            kchunk = jnp.right_shift(posk_ref[0, :, pl.ds(k0, tk)], CHUNK_SHIFT)
            vis = kchunk <= qchunk
        for h in range(hp):
            hs = slice(h * SLOT, (h + 1) * SLOT)
            s = _dot_nt(q_ref[0, :, hs], k_ref[0, pl.ds(k0, tk), hs])
            if masked:
                s = jnp.where(vis, s, NEG)
            vh = v_ref[0, pl.ds(k0, tk), hs]
            if online:
                m_old = m_ref[h]
                m_new = jnp.maximum(m_old, jnp.max(s, axis=-1, keepdims=True))
                p = jnp.exp2(s - m_new)
                acc_ref[h] = jnp.exp2(m_old - m_new) * acc_ref[h] + _dot(p.astype(BF16), vh)
                m_ref[h] = m_new
            else:
                acc_ref[h] += _dot(jnp.exp2(s).astype(BF16), vh)

    def body(kb, carry):
        flag = flags_ref[b, qi, kb]

        @pl.when(flag == 1)
        def _():
            block(kb, False)

        @pl.when(flag == 2)
        def _():
            block(kb, True)

        return carry

    lax.fori_loop(0, nk, body, 0)
    lane = lax.broadcasted_iota(jnp.int32, (tq, LANES), 1)
    for j in range(hp // 2):
        o0 = acc_ref[2 * j] / pltpu.roll(acc_ref[2 * j], MLA_V, 1)
        o1 = acc_ref[2 * j + 1] / pltpu.roll(acc_ref[2 * j + 1], MLA_V, 1)
        o_ref[0, :, j * LANES:(j + 1) * LANES] = jnp.where(lane < MLA_V, o0, pltpu.roll(o1, MLA_V, 1))


def _mla_call(flags, q, k, v, pos_col, pos_row, tq, tk, hp, online):
    B, S, _ = q.shape
    nq, nk = S // tq, S // tk
    grid_spec = pltpu.PrefetchScalarGridSpec(
        num_scalar_prefetch=1,
        grid=(B, MLA_HEADS // hp, nq),
        in_specs=[
            pl.BlockSpec((1, tq, hp * SLOT), lambda b, p, i, f: (b, i, p)),
            pl.BlockSpec((1, S, hp * SLOT), lambda b, p, i, f: (b, 0, p)),
            pl.BlockSpec((1, S, hp * SLOT), lambda b, p, i, f: (b, 0, p)),
            pl.BlockSpec((1, tq, 1), lambda b, p, i, f: (b, i, 0)),
            pl.BlockSpec((1, 1, S), lambda b, p, i, f: (b, 0, 0)),
        ],
        out_specs=pl.BlockSpec((1, tq, hp * MLA_V), lambda b, p, i, f: (b, i, p)),
        scratch_shapes=[pltpu.VMEM((hp, tq, LANES), F32), pltpu.VMEM((hp, tq, 1), F32)],
    )
    return pl.pallas_call(
        functools.partial(_mla_kernel, tq=tq, tk=tk, nk=nk, hp=hp, online=online),
        grid_spec=grid_spec,
        out_shape=jax.ShapeDtypeStruct((B, S, MLA_WIDTH), F32),
        compiler_params=pltpu.CompilerParams(
            dimension_semantics=("arbitrary", "arbitrary", "arbitrary"),
            vmem_limit_bytes=VMEM_LIMIT_BYTES),
        name="mla_attn_online" if online else "mla_attn",
    )(flags, q, k, v, pos_col, pos_row)


def _softplus(z):
    return jnp.maximum(z, 0.0) + jnp.log(1.0 + jnp.exp(-jnp.abs(z)))


def _sb_kernel(kstart_ref, q_ref, k_ref, v_ref, posq_ref, posk_ref, o_ref, run_ref, acc_ref,
               *, tq, tk, win):
    b = pl.program_id(0)
    qi = pl.program_id(2)
    lane = lax.broadcasted_iota(jnp.int32, (tq, LANES), 1)
    row_i = lax.broadcasted_iota(jnp.int32, (tk, tk), 0)
    col_i = lax.broadcasted_iota(jnp.int32, (tk, tk), 1)
    tri = (row_i >= col_i).astype(BF16)
    kstart = kstart_ref[b, qi]
    q_pair = q_ref[0]
    zero = jnp.zeros_like(q_pair)
    q2 = jnp.concatenate([jnp.where(lane < SB_HEAD_DIM, q_pair, zero),
                          jnp.where(lane >= SB_HEAD_DIM, q_pair, zero)], axis=0)
    qpos = jnp.concatenate([posq_ref[0], posq_ref[0]], axis=0)

    def suffix_sums(l1m):
        hi = lax.bitcast_convert_type(
            lax.bitcast_convert_type(l1m, jnp.int32) & jnp.int32(HI_MASK), F32)
        lo = l1m - hi
        both = _dot(jnp.concatenate([hi.astype(BF16), lo.astype(BF16)], axis=0), tri)
        return both[:2 * tq] + both[2 * tq:]

    run_ref[...] = jnp.zeros(run_ref.shape, F32)
    acc_ref[...] = jnp.zeros(acc_ref.shape, F32)

    @pl.when(kstart >= win - 1)
    def _():
        k0 = pl.multiple_of((kstart - (win - 1)) * tk, tk)
        causal = posk_ref[0, :, pl.ds(k0, win * tk)] < qpos
        z = _dot_nt(q2, k_ref[0, pl.ds(k0, win * tk), :])
        l1m = jnp.where(causal, -_softplus(z), 0.0)
        run = jnp.zeros((2 * tq, 1), F32)
        a_blocks = [None] * win
        for j in reversed(range(win)):
            cs = slice(j * tk, (j + 1) * tk)
            suffix = suffix_sums(l1m[:, cs])
            a = jnp.exp(z[:, cs] + suffix + run)
            a_blocks[j] = jnp.where(causal[:, cs], a, 0.0).astype(BF16)
            run = run + suffix[:, 0:1]
        acc_ref[...] = _dot(jnp.concatenate(a_blocks, axis=1), v_ref[0, pl.ds(k0, win * tk), :])
        run_ref[...] = run

    def block(kb):
        k0 = pl.multiple_of(kb * tk, tk)
        z = _dot_nt(q2, k_ref[0, pl.ds(k0, tk), :])
        causal = posk_ref[0, :, pl.ds(k0, tk)] < qpos
        l1m = jnp.where(causal, -_softplus(z), 0.0)
        suffix = suffix_sums(l1m)
        run = run_ref[...]
        a = jnp.where(causal, jnp.exp(z + suffix + run), 0.0)
        acc_ref[...] += _dot(a.astype(BF16), v_ref[0, pl.ds(k0, tk), :])
        run_new = run + suffix[:, 0:1]
        run_ref[...] = run_new
        return jnp.max(run_new)

    def cond(c):
        kb, top = c
        return (kb >= 0) & (top >= SB_LOG_ZERO)

    def body(c):
        kb, _ = c
        return kb - 1, block(kb)

    kb_first = jnp.where(kstart >= win - 1, kstart - win, kstart)
    lax.while_loop(cond, body, (kb_first, jnp.max(run_ref[...])))

    o_ref[0] = jnp.where(lane < SB_HEAD_DIM, acc_ref[:tq], acc_ref[tq:])


def _sb_call(kstart, q, k, v, pos_col, pos_row, tq, tk, win):
    B, S, _ = q.shape
    nq = S // tq
    grid_spec = pltpu.PrefetchScalarGridSpec(
        num_scalar_prefetch=1,
        grid=(B, SB_HEADS // 2, nq),
        in_specs=[
            pl.BlockSpec((1, tq, LANES), lambda b, p, i, s: (b, i, p)),
            pl.BlockSpec((1, S, LANES), lambda b, p, i, s: (b, 0, p)),
            pl.BlockSpec((1, S, LANES), lambda b, p, i, s: (b, 0, p)),
            pl.BlockSpec((1, tq, 1), lambda b, p, i, s: (b, i, 0)),
            pl.BlockSpec((1, 1, S), lambda b, p, i, s: (b, 0, 0)),
        ],
        out_specs=pl.BlockSpec((1, tq, LANES), lambda b, p, i, s: (b, i, p)),
        scratch_shapes=[pltpu.VMEM((2 * tq, 1), F32), pltpu.VMEM((2 * tq, LANES), F32)],
    )
    return pl.pallas_call(
        functools.partial(_sb_kernel, tq=tq, tk=tk, win=win),
        grid_spec=grid_spec,
        out_shape=jax.ShapeDtypeStruct((B, S, SB_WIDTH), F32),
        compiler_params=pltpu.CompilerParams(
            dimension_semantics=("arbitrary", "arbitrary", "arbitrary"),
            vmem_limit_bytes=VMEM_LIMIT_BYTES),
        name="sb_attn",
    )(kstart, q, k, v, pos_col, pos_row)


def _round_robin(*gens):
    gens = list(gens)
    while gens:
        for g in list(gens):
            if next(g, StopIteration) is StopIteration:
                gens.remove(g)


def _attn_kernel(flags_ref, kstart_ref, q_ref, k_ref, v_ref, sq_ref, sk_ref, sv_ref, posq_ref,
                 posk_ref, om_ref, os_ref, acc_ref, run_ref, sacc_ref, *, tq, tk, nk, hp):
    b = pl.program_id(0)
    qi = pl.program_id(2)
    stq, stk, win = SB_TQ, SB_TK, SB_WIN
    halves = tq // stq
    units = halves * (hp // 2)
    qchunk = jnp.right_shift(posq_ref[0], CHUNK_SHIFT)
    acc_ref[...] = jnp.zeros(acc_ref.shape, F32)
    lane = lax.broadcasted_iota(jnp.int32, (stq, LANES), 1)
    tri2 = (lax.broadcasted_iota(jnp.int32, (2 * stk, 2 * stk), 0)
            >= lax.broadcasted_iota(jnp.int32, (2 * stk, 2 * stk), 1)).astype(BF16)
    tri = tri2[:stk, :stk]

    def mla_heads(kb, masked):
        k0 = pl.multiple_of(kb * tk, tk)
        if masked:
            kchunk = jnp.right_shift(posk_ref[0, :, pl.ds(k0, tk)], CHUNK_SHIFT)
            vis = kchunk <= qchunk
        for h in range(hp):
            hs = slice(h * SLOT, (h + 1) * SLOT)
            s = _dot_nt(q_ref[0, :, hs], k_ref[0, pl.ds(k0, tk), hs])
            if masked:
                s = jnp.where(vis, s, NEG)
            acc_ref[h] += _dot(jnp.exp2(s).astype(BF16), v_ref[0, pl.ds(k0, tk), hs])
            yield

    def sb_operands(u):
        rows = pl.ds(pl.multiple_of((u % halves) * stq, stq), stq)
        cols = pl.ds(pl.multiple_of((u // halves) * LANES, LANES), LANES)
        q_pair = sq_ref[0, rows, cols]
        zero = jnp.zeros_like(q_pair)
        q2 = jnp.concatenate([jnp.where(lane < SB_HEAD_DIM, q_pair, zero),
                              jnp.where(lane >= SB_HEAD_DIM, q_pair, zero)], axis=0)
        qpos = jnp.concatenate([posq_ref[0, rows], posq_ref[0, rows]], axis=0)
        return q2, qpos, rows, cols

    def suffix_sums(l1m, triangle):
        hi = lax.bitcast_convert_type(
            lax.bitcast_convert_type(l1m, jnp.int32) & jnp.int32(HI_MASK), F32)
        lo = l1m - hi
        both = _dot(jnp.concatenate([hi.astype(BF16), lo.astype(BF16)], axis=0), triangle)
        return both[:2 * stq] + both[2 * stq:]

    def sb_fast(u, kstart):
        q2, qpos, _, cols = sb_operands(u)
        k0 = pl.multiple_of((kstart - (win - 1)) * stk, stk)
        causal = posk_ref[0, :, pl.ds(k0, win * stk)] < qpos
        z = _dot_nt(q2, sk_ref[0, pl.ds(k0, win * stk), cols])
        l1m = jnp.where(causal, -_softplus(z), 0.0)
        yield
        run = jnp.zeros((2 * stq, 1), F32)
        a_blocks = [None] * (win // 2)
        for j in reversed(range(win // 2)):
            cs = slice(j * 2 * stk, (j + 1) * 2 * stk)
            suffix = suffix_sums(l1m[:, cs], tri2)
            a = jnp.exp(z[:, cs] + suffix + run)
            a_blocks[j] = jnp.where(causal[:, cs], a, 0.0).astype(BF16)
            run = run + suffix[:, 0:1]
            yield
        sacc_ref[...] = _dot(jnp.concatenate(a_blocks, axis=1), sv_ref[0, pl.ds(k0, win * stk), cols])
        run_ref[...] = run
        yield

    def sb_finish(u, kstart, fast):
        q2, qpos, rows, cols = sb_operands(u)

        def block(kb):
            k0 = pl.multiple_of(kb * stk, stk)
            z = _dot_nt(q2, sk_ref[0, pl.ds(k0, stk), cols])
            causal = posk_ref[0, :, pl.ds(k0, stk)] < qpos
            l1m = jnp.where(causal, -_softplus(z), 0.0)
            suffix = suffix_sums(l1m, tri)
            run = run_ref[...]
            a = jnp.where(causal, jnp.exp(z + suffix + run), 0.0)
            sacc_ref[...] += _dot(a.astype(BF16), sv_ref[0, pl.ds(k0, stk), cols])
            run_new = run + suffix[:, 0:1]
            run_ref[...] = run_new
            return jnp.max(run_new)

        def cond(c):
            kb, top = c
            return (kb >= 0) & (top >= SB_LOG_ZERO)

        def body(c):
            kb, _ = c
            return kb - 1, block(kb)

        kb_first = jnp.where(fast, kstart - win, kstart)
        lax.while_loop(cond, body, (kb_first, jnp.max(run_ref[...])))
        os_ref[0, rows, cols] = jnp.where(lane < SB_HEAD_DIM, sacc_ref[:stq], sacc_ref[stq:])

    def with_unit(kb, carry):
        flag = flags_ref[b, qi, kb]
        kstart = kstart_ref[b, qi * halves + kb % halves]
        fast = kstart >= win - 1
        run_ref[...] = jnp.zeros(run_ref.shape, F32)
        sacc_ref[...] = jnp.zeros(sacc_ref.shape, F32)

        @pl.when((flag == 1) & fast)
        def _():
            _round_robin(sb_fast(kb, kstart), mla_heads(kb, False))

        @pl.when((flag == 2) & fast)
        def _():
            _round_robin(sb_fast(kb, kstart), mla_heads(kb, True))

        @pl.when((flag == 0) & fast)
        def _():
            _round_robin(sb_fast(kb, kstart))

        @pl.when((flag == 1) & jnp.logical_not(fast))
        def _():
            _round_robin(mla_heads(kb, False))

        @pl.when((flag == 2) & jnp.logical_not(fast))
        def _():
            _round_robin(mla_heads(kb, True))

        sb_finish(kb, kstart, fast)
        return carry

    def mla_only(kb, carry):
        flag = flags_ref[b, qi, kb]

        @pl.when(flag == 1)
        def _():
            _round_robin(mla_heads(kb, False))

        @pl.when(flag == 2)
        def _():
            _round_robin(mla_heads(kb, True))

        return carry

    lax.fori_loop(0, units, with_unit, 0)
    lax.fori_loop(units, nk, mla_only, 0)
    lane_q = lax.broadcasted_iota(jnp.int32, (tq, LANES), 1)
    for j in range(hp // 2):
        o0 = acc_ref[2 * j] / pltpu.roll(acc_ref[2 * j], MLA_V, 1)
        o1 = acc_ref[2 * j + 1] / pltpu.roll(acc_ref[2 * j + 1], MLA_V, 1)
        om_ref[0, :, j * LANES:(j + 1) * LANES] = jnp.where(lane_q < MLA_V, o0, pltpu.roll(o1, MLA_V, 1))


def _attn_call(flags, kstart, q, k, v, sq, sk, sv, pos_col, pos_row, tq, tk, hp):
    B, S, _ = q.shape
    nq, nk = S // tq, S // tk
    assert (tq // SB_TQ) * (hp // 2) <= nk and SB_HEADS == MLA_HEADS and SB_WIN % 2 == 0
    sbw = (hp // 2) * LANES
    grid_spec = pltpu.PrefetchScalarGridSpec(
        num_scalar_prefetch=2,
        grid=(B, MLA_HEADS // hp, nq),
        in_specs=[
            pl.BlockSpec((1, tq, hp * SLOT), lambda b, p, i, f, s: (b, i, p)),
            pl.BlockSpec((1, S, hp * SLOT), lambda b, p, i, f, s: (b, 0, p)),
            pl.BlockSpec((1, S, hp * SLOT), lambda b, p, i, f, s: (b, 0, p)),
            pl.BlockSpec((1, tq, sbw), lambda b, p, i, f, s: (b, i, p)),
            pl.BlockSpec((1, S, sbw), lambda b, p, i, f, s: (b, 0, p)),
            pl.BlockSpec((1, S, sbw), lambda b, p, i, f, s: (b, 0, p)),
            pl.BlockSpec((1, tq, 1), lambda b, p, i, f, s: (b, i, 0)),
            pl.BlockSpec((1, 1, S), lambda b, p, i, f, s: (b, 0, 0)),
        ],
        out_specs=[pl.BlockSpec((1, tq, hp * MLA_V), lambda b, p, i, f, s: (b, i, p)),
                   pl.BlockSpec((1, tq, sbw), lambda b, p, i, f, s: (b, i, p))],
        scratch_shapes=[pltpu.VMEM((hp, tq, LANES), F32), pltpu.VMEM((2 * SB_TQ, 1), F32),
                        pltpu.VMEM((2 * SB_TQ, LANES), F32)],
    )
    return pl.pallas_call(
        functools.partial(_attn_kernel, tq=tq, tk=tk, nk=nk, hp=hp),
        grid_spec=grid_spec,
        out_shape=[jax.ShapeDtypeStruct((B, S, MLA_WIDTH), F32),
                   jax.ShapeDtypeStruct((B, S, SB_WIDTH), F32)],
        compiler_params=pltpu.CompilerParams(
            dimension_semantics=("arbitrary", "arbitrary", "arbitrary"),
            vmem_limit_bytes=VMEM_LIMIT_BYTES),
        name="attn",
    )(flags, kstart, q, k, v, sq, sk, sv, pos_col, pos_row)


def _split_bf16(a):
    hi = a.astype(BF16)
    lo = (a - hi.astype(F32)).astype(BF16)
    return hi, lo


def _merge_kernel(x_ref, om_ref, os_ref, gm_ref, gs_ref, wo_ref, g2_ref, wrh_ref, wrl_ref,
                  bias_ref, wsgu_ref, wsd_ref, base_ref, hn_ref, eidx_ref, gate_ref, rank_ref,
                  cnt_ref):
    mm = _rms(om_ref[...], gm_ref[...]).astype(BF16)
    ms = _rms(os_ref[...], gs_ref[...]).astype(BF16)
    h = x_ref[...] + _dot(mm, wo_ref[:MLA_WIDTH, :]) + _dot(ms, wo_ref[MLA_WIDTH:, :])
    hn = _rms(h, g2_ref[...])
    hn_hi, hn_lo = _split_bf16(hn)
    bits = lax.bitcast_convert_type(hn_hi.astype(F32), jnp.int32)
    hn_ref[...] = (lax.shift_right_logical(bits[:, :D_MODEL // 2], 16)
                   | (bits[:, D_MODEL // 2:] & jnp.int32(HI_MASK)))

    logits = (_dot_nt(wrh_ref[...], hn_hi) + _dot_nt(wrh_ref[...], hn_lo)
              + _dot_nt(wrl_ref[...], hn_hi))
    scores = jax.nn.sigmoid(logits)
    work = scores + bias_ref[...]
    eidx = lax.broadcasted_iota(jnp.int32, work.shape, 0)
    chosen, chosen_score, hits = [], [], []
    for _ in range(TOP_K):
        top = jnp.max(work, axis=0, keepdims=True)
        first = jnp.min(jnp.where(work == top, eidx, N_EXPERTS), axis=0, keepdims=True)
        hit = eidx == first
        hits.append(hit)
        chosen.append(first)
        chosen_score.append(jnp.sum(jnp.where(hit, scores, 0.0), axis=0, keepdims=True))
        work = jnp.where(hit, -jnp.inf, work)
    sel = jnp.concatenate(chosen_score, axis=0)
    eidx_ref[...] = jnp.concatenate(chosen, axis=0)
    gate_ref[...] = sel / jnp.sum(sel, axis=0, keepdims=True) * ROUTED_SCALE

    tm = work.shape[1]
    member = functools.reduce(jnp.logical_or, hits)
    member = jnp.where(member, 1.0, 0.0)
    tri = (lax.broadcasted_iota(jnp.int32, (tm, tm), 0)
           <= lax.broadcasted_iota(jnp.int32, (tm, tm), 1)).astype(BF16)
    upto = _dot(member.astype(BF16), tri)
    before = upto - member
    rank_ref[...] = jnp.concatenate(
        [jnp.sum(jnp.where(hit, before, 0.0), axis=0, keepdims=True) for hit in hits],
        axis=0).astype(jnp.int32)
    cnt_ref[0] = upto[:, tm - 1:tm].astype(jnp.int32)

    sgu = _dot(hn_hi, wsgu_ref[...])
    act = (jax.nn.silu(sgu[:, :SHARED_FF]) * sgu[:, SHARED_FF:]).astype(BF16)
    base_ref[...] = h + _dot(act, wsd_ref[...])


def _merge_call(x2, o_mla, o_sb, gm, gs, wo, g2, wrh, wrl, bias, wsgu, wsd, tm):
    T = x2.shape[0]
    full = lambda a: pl.BlockSpec(a.shape, lambda i: (0,) * a.ndim)
    row = lambda w: pl.BlockSpec((tm, w), lambda i: (i, 0))
    weights = (gm, gs, wo, g2, wrh, wrl, bias, wsgu, wsd)
    topk = pl.BlockSpec((TOP_K, tm), lambda i: (0, i))
    return pl.pallas_call(
        _merge_kernel,
        grid=(T // tm,),
        in_specs=[row(D_MODEL), row(MLA_WIDTH), row(SB_WIDTH)] + [full(w) for w in weights],
        out_specs=[row(D_MODEL), row(D_MODEL // 2), topk, topk, topk,
                   pl.BlockSpec((1, N_EXPERTS, 1), lambda i: (i, 0, 0))],
        out_shape=[jax.ShapeDtypeStruct((T, D_MODEL), F32),
                   jax.ShapeDtypeStruct((T, D_MODEL // 2), jnp.int32),
                   jax.ShapeDtypeStruct((TOP_K, T), jnp.int32),
                   jax.ShapeDtypeStruct((TOP_K, T), F32),
                   jax.ShapeDtypeStruct((TOP_K, T), jnp.int32),
                   jax.ShapeDtypeStruct((T // tm, N_EXPERTS, 1), jnp.int32)],
        compiler_params=pltpu.CompilerParams(dimension_semantics=("arbitrary",),
                                             vmem_limit_bytes=VMEM_LIMIT_BYTES),
        name="merge",
    )(x2, o_mla, o_sb, *weights)


MOE_TB = 2048
MOE_RT = 64
MOE_MAX_TILES = 6
MOE_ROWS = MOE_TB * TOP_K + N_EXPERTS * (MOE_RT - 1)
MOE_PACK = D_MODEL // 2
MOE_SUB = MOE_PACK // LANES
MOE_UNROLL = 8
MOE_XCH = 256
MOE_OCH = 128
MOE_WBUF = 4
MOE_PAIR_TILES = (4, 5)
MOE_CLEAR_ROWS = 1024


def _unpack_lo(w):
    return lax.bitcast_convert_type(w << 16, F32)


def _unpack_hi(w):
    return lax.bitcast_convert_type(w & jnp.int32(HI_MASK), F32)


def _moe_kernel(off_ref, nt_ref, pos_ref, gate_ref, xp_hbm, base_hbm, wgu_hbm, wd_hbm, o_hbm,
                xy_ref, xbuf, bbuf, obuf, wgu_buf, wd_buf, xsem, bsem, osem, wsem):
    i = pl.program_id(0)

    n_visits = pl.num_programs(0) * N_EXPERTS

    def w_copies(step):
        slot = step % MOE_WBUF
        ex = step % N_EXPERTS
        return (pltpu.make_async_copy(wgu_hbm.at[ex], wgu_buf.at[slot], wsem.at[0, slot]),
                pltpu.make_async_copy(wd_hbm.at[ex], wd_buf.at[slot], wsem.at[1, slot]))

    @pl.when(i == 0)
    def _():
        for ahead in range(2):
            for cp in w_copies(ahead):
                cp.start()

        def clear(r, c):
            xy_ref[pl.ds(pl.multiple_of(r * MOE_CLEAR_ROWS, MOE_CLEAR_ROWS), MOE_CLEAR_ROWS), :] = (
                jnp.zeros((MOE_CLEAR_ROWS, LANES), jnp.int32))
            return c

        lax.fori_loop(0, MOE_ROWS * MOE_SUB // MOE_CLEAR_ROWS, clear, 0)
        tail = MOE_ROWS * MOE_SUB % MOE_CLEAR_ROWS
        if tail:
            xy_ref[pl.ds(MOE_ROWS * MOE_SUB - tail, tail), :] = jnp.zeros((tail, LANES), jnp.int32)

    def x_copy(c, slot, block=i):
        return pltpu.make_async_copy(
            xp_hbm.at[pl.ds(block * MOE_TB + c * MOE_XCH, MOE_XCH), :], xbuf.at[slot], xsem.at[slot])

    def base_copy(c, slot):
        return pltpu.make_async_copy(
            base_hbm.at[pl.ds(i * MOE_TB + c * MOE_OCH, MOE_OCH), :], bbuf.at[slot], bsem.at[slot])

    def out_copy(c, slot):
        return pltpu.make_async_copy(
            obuf.at[slot], o_hbm.at[pl.ds(i * MOE_TB + c * MOE_OCH, MOE_OCH), :], osem.at[slot])

    def dispatch():
        n_chunks = MOE_TB // MOE_XCH
        for c in range(n_chunks):
            slot = c % 2
            if c + 1 < n_chunks:
                x_copy(c + 1, 1 - slot).start()
            x_copy(c, slot).wait()

            def step(tt, carry, c=c, slot=slot):
                slab = xbuf[slot, pl.ds(pl.multiple_of(tt * MOE_UNROLL, MOE_UNROLL), MOE_UNROLL), :]
                for u in range(MOE_UNROLL):
                    t = tt * MOE_UNROLL + u
                    row = jnp.concatenate([slab[u:u + 1, j * LANES:(j + 1) * LANES]
                                           for j in range(MOE_SUB)], axis=0)
                    for k in range(TOP_K):
                        p = pl.multiple_of(pos_ref[TOP_K * (c * MOE_XCH + t) + k], MOE_SUB)
                        xy_ref[pl.ds(p, MOE_SUB), :] = row
                return carry

            lax.fori_loop(0, MOE_XCH // MOE_UNROLL, step, 0)

    def ffn_stages(r0, m, wslot):
        base_row = pl.multiple_of(MOE_SUB * r0, 8)
        words = jnp.concatenate(
            [xy_ref[pl.ds(base_row + j, m, stride=MOE_SUB), :] for j in range(MOE_SUB)], axis=1)
        lo = _unpack_lo(words).astype(BF16)
        hi = _unpack_hi(words).astype(BF16)
        yield
        gu = _dot(lo, wgu_buf[wslot, :MOE_PACK, :]) + _dot(hi, wgu_buf[wslot, MOE_PACK:, :])
        yield
        act = (jax.nn.silu(gu[:, :EXPERT_FF]) * gu[:, EXPERT_FF:]).astype(BF16)
        y = _dot(act, wd_buf[wslot])
        yield
        ya = lax.bitcast_convert_type(y[:, :MOE_PACK].astype(BF16).astype(F32), jnp.int32)
        yb = lax.bitcast_convert_type(y[:, MOE_PACK:].astype(BF16).astype(F32), jnp.int32)
        packed = lax.shift_right_logical(ya, 16) | yb
        for j in range(MOE_SUB):
            xy_ref[pl.ds(base_row + j, m, stride=MOE_SUB), :] = packed[:, j * LANES:(j + 1) * LANES]
        yield

    def ffn(r0, m, wslot):
        _round_robin(ffn_stages(r0, m, wslot))

    def segment(visit):
        wslot = visit % MOE_WBUF
        off = off_ref[visit]
        nt = nt_ref[visit]

        def chunk(c, inner):
            ffn(off + c * (MOE_MAX_TILES * MOE_RT), MOE_MAX_TILES * MOE_RT, wslot)
            return inner

        full = nt // MOE_MAX_TILES
        lax.fori_loop(0, full, chunk, 0)
        rest = nt - full * MOE_MAX_TILES
        for tiles in range(1, MOE_MAX_TILES):
            @pl.when(rest == tiles)
            def _():
                ffn(off + full * (MOE_MAX_TILES * MOE_RT), tiles * MOE_RT, wslot)

    def expert_pair(j, carry):
        v0 = i * N_EXPERTS + 2 * j
        for v in (v0 + 2, v0 + 3):
            @pl.when(v < n_visits)
            def _(v=v):
                for cp in w_copies(v):
                    cp.start()

        for v in (v0, v0 + 1):
            for cp in w_copies(v):
                cp.wait()
        nt0, nt1 = nt_ref[v0], nt_ref[v0 + 1]
        paired = jnp.bool_(False)
        for ta in MOE_PAIR_TILES:
            for tb in MOE_PAIR_TILES:
                hit = (nt0 == ta) & (nt1 == tb)
                paired = paired | hit

                @pl.when(hit)
                def _(ta=ta, tb=tb):
                    _round_robin(ffn_stages(off_ref[v0], ta * MOE_RT, v0 % MOE_WBUF),
                                 ffn_stages(off_ref[v0 + 1], tb * MOE_RT, (v0 + 1) % MOE_WBUF))

        @pl.when(jnp.logical_not(paired))
        def _():
            segment(v0)
            segment(v0 + 1)

        return carry

    def combine():
        n_chunks = MOE_TB // MOE_OCH
        for c in range(n_chunks):
            slot = c % 2
            if c + 1 < n_chunks:
                base_copy(c + 1, 1 - slot).start()
            base_copy(c, slot).wait()
            if c >= 2:
                out_copy(c - 2, slot).wait()

            def step(tt, carry, c=c, slot=slot):
                for u in range(MOE_UNROLL):
                    t = tt * MOE_UNROLL + u
                    lo = jnp.zeros((MOE_SUB, LANES), F32)
                    hi = jnp.zeros((MOE_SUB, LANES), F32)
                    for k in range(TOP_K):
                        s = TOP_K * (c * MOE_OCH + t) + k
                        w = xy_ref[pl.ds(pl.multiple_of(pos_ref[s], MOE_SUB), MOE_SUB), :]
                        g = gate_ref[s]
                        lo = lo + g * _unpack_lo(w)
                        hi = hi + g * _unpack_hi(w)
                    routed = jnp.concatenate([lo[j:j + 1] for j in range(MOE_SUB)]
                                             + [hi[j:j + 1] for j in range(MOE_SUB)], axis=1)
                    obuf[slot, pl.ds(t, 1), :] = bbuf[slot, pl.ds(t, 1), :] + routed
                return carry

            lax.fori_loop(0, MOE_OCH // MOE_UNROLL, step, 0)
            out_copy(c, slot).start()
        out_copy(n_chunks - 2, n_chunks % 2).wait()
        out_copy(n_chunks - 1, (n_chunks - 1) % 2).wait()

    @pl.when(i == 0)
    def _():
        x_copy(0, 0).start()

    dispatch()
    base_copy(0, 0).start()

    @pl.when(i + 1 < pl.num_programs(0))
    def _():
        x_copy(0, 0, block=i + 1).start()
    lax.fori_loop(0, N_EXPERTS // 2, expert_pair, 0)
    combine()


def _moe_call(off, nt, pos, gates, xp, base, wgu, wd):
    T = base.shape[0]
    slots = MOE_TB * TOP_K
    grid_spec = pltpu.PrefetchScalarGridSpec(
        num_scalar_prefetch=2,
        grid=(T // MOE_TB,),
        in_specs=[
            pl.BlockSpec((slots,), lambda i, o, n: (i,), memory_space=pltpu.SMEM),
            pl.BlockSpec((slots,), lambda i, o, n: (i,), memory_space=pltpu.SMEM),
            pl.BlockSpec(memory_space=pl.ANY),
            pl.BlockSpec(memory_space=pl.ANY),
            pl.BlockSpec(memory_space=pl.ANY),
            pl.BlockSpec(memory_space=pl.ANY),
        ],
        out_specs=pl.BlockSpec(memory_space=pl.ANY),
        scratch_shapes=[
            pltpu.VMEM((MOE_ROWS * MOE_SUB, LANES), jnp.int32),
            pltpu.VMEM((2, MOE_XCH, MOE_PACK), jnp.int32),
            pltpu.VMEM((2, MOE_OCH, D_MODEL), F32),
            pltpu.VMEM((2, MOE_OCH, D_MODEL), F32),
            pltpu.VMEM((MOE_WBUF, D_MODEL, 2 * EXPERT_FF), BF16),
            pltpu.VMEM((MOE_WBUF, EXPERT_FF, D_MODEL), BF16),
            pltpu.SemaphoreType.DMA((2,)),
            pltpu.SemaphoreType.DMA((2,)),
            pltpu.SemaphoreType.DMA((2,)),
            pltpu.SemaphoreType.DMA((2, MOE_WBUF)),
        ],
    )
    return pl.pallas_call(
        _moe_kernel,
        grid_spec=grid_spec,
        out_shape=jax.ShapeDtypeStruct((T, D_MODEL), F32),
        compiler_params=pltpu.CompilerParams(dimension_semantics=("arbitrary",),
                                             vmem_limit_bytes=VMEM_LIMIT_BYTES),
        name="moe",
    )(off, nt, pos, gates, xp, base, wgu, wd)


def _route_tables(eidx, gate, rank, cnt):
    T = eidx.shape[1]
    nblk = T // MOE_TB
    cnt = cnt.reshape(nblk, -1, N_EXPERTS)
    per_blk = cnt.shape[1]
    earlier = jnp.cumsum(cnt, axis=1) - cnt
    tiles = (cnt.sum(axis=1) + MOE_RT - 1) // MOE_RT
    off = jnp.cumsum(tiles, axis=1) * MOE_RT - tiles * MOE_RT
    start = (off[:, None, :] + earlier).reshape(nblk * per_blk, N_EXPERTS)
    e_tk = eidx.T.reshape(nblk * per_blk, T // (nblk * per_blk), TOP_K)
    chosen = e_tk[..., None] == jnp.arange(N_EXPERTS, dtype=jnp.int32)
    pos = jnp.where(chosen, start[:, None, None, :], 0).sum(axis=-1).reshape(T, TOP_K) + rank.T
    return (off.reshape(-1).astype(jnp.int32), tiles.reshape(-1).astype(jnp.int32),
            (pos * MOE_SUB).reshape(-1).astype(jnp.int32), gate.T.reshape(-1))


def _qk_slot(a):
    first = SLOT // 2 - HALF_ROPE
    nope, r1, r2 = a[..., :MLA_NOPE], a[..., MLA_NOPE:MLA_NOPE + HALF_ROPE], a[..., MLA_NOPE + HALF_ROPE:]
    pad = jnp.zeros(a.shape[:-1] + (SLOT - MLA_QK,), a.dtype)
    return jnp.concatenate([r1, nope[..., :first], r2, nope[..., first:], pad], axis=-1)


def _slot_cols(w, width):
    k = w.shape[0]
    w = w.reshape(k, MLA_HEADS, width)
    return jnp.pad(w, ((0, 0), (0, 0), (0, SLOT - width))).reshape(k, MLA_HEADS * SLOT)


def _block_flags(pos, tq, tk, chunked):
    B, S = pos.shape
    p = jnp.right_shift(pos, CHUNK_SHIFT) if chunked else pos
    qmin = p.reshape(B, S // tq, tq).min(-1)[:, :, None]
    qmax = p.reshape(B, S // tq, tq).max(-1)[:, :, None]
    kmin = p.reshape(B, S // tk, tk).min(-1)[:, None, :]
    kmax = p.reshape(B, S // tk, tk).max(-1)[:, None, :]
    if chunked:
        none, all_ = kmin > qmax, kmax <= qmin
    else:
        none, all_ = kmin >= qmax, kmax < qmin
    return jnp.where(none, 0, jnp.where(all_, 1, 2)).astype(jnp.int32)


def kernel(x, positions, norm1, w_in, q_a_norm, w_uq, kv_a_norm, w_ukv, q_norm, k_norm,
           out_norm_mla, out_norm_sb, w_o, norm2, w_router, router_bias, w_gate_up, w_down,
           w_shared_gate_up, w_shared_down):
    B, S, D = x.shape
    T = B * S
    c0 = Q_LORA
    c1 = c0 + KV_LORA
    c2 = c1 + MLA_ROPE

    row2 = lambda g: g.reshape(1, -1).astype(F32)
    wcq = w_in[:, :c0].astype(BF16)
    wckv = w_in[:, c0:c1].astype(BF16)
    wkr = _qk_slot(jnp.pad(w_in[:, c1:c2], ((0, 0), (MLA_NOPE, 0)))).astype(BF16)
    wsb = w_in[:, c2:].astype(BF16)
    wuq = _qk_slot(w_uq.reshape(Q_LORA, MLA_HEADS, MLA_QK)).reshape(Q_LORA, MLA_HEADS * SLOT).astype(BF16)
    w_ukv3 = w_ukv.reshape(KV_LORA, MLA_HEADS, MLA_NOPE + MLA_V)
    wuk = _qk_slot(jnp.pad(w_ukv3[:, :, :MLA_NOPE], ((0, 0), (0, 0), (0, MLA_ROPE)))
                   ).reshape(KV_LORA, MLA_HEADS * SLOT).astype(BF16)
    wuv = _slot_cols(w_ukv3[:, :, MLA_NOPE:].reshape(KV_LORA, MLA_WIDTH), MLA_V).astype(BF16)
    slot_lane = jnp.arange(SLOT)
    vone = jnp.tile((slot_lane >= MLA_V).astype(F32), MLA_HEADS).reshape(1, MLA_HEADS * SLOT)
    gq = _qk_slot(q_norm).reshape(1, SLOT)
    gk = _qk_slot(k_norm).reshape(1, SLOT)
    inv = ROPE_THETA ** (-jnp.arange(HALF_ROPE, dtype=F32) / HALF_ROPE)
    cos, sin = _rope_tables(positions.reshape(-1), inv)
    cos_slot = jnp.pad(cos, ((0, 0), (0, SLOT - HALF_ROPE)))
    sin_slot = jnp.pad(sin, ((0, 0), (0, SLOT - HALF_ROPE)))
    eye = jnp.eye(HALF_ROPE, dtype=F32)
    to_slot = lambda a, b: jnp.pad(_qk_slot(jnp.concatenate([jnp.zeros((HALF_ROPE, MLA_NOPE), F32), a, b], axis=1)),
                                   ((0, SLOT - HALF_ROPE), (0, 0))).astype(BF16)
    pcos, psin = to_slot(eye, eye), to_slot(-eye, eye)
    cbase = _qk_slot(jnp.concatenate([jnp.ones((1, MLA_NOPE), F32), jnp.zeros((1, MLA_ROPE), F32)], axis=1))

    bound = (math.sqrt(MLA_QK) * LOG2E * BOUND_MARGIN) * jnp.max(jnp.abs(q_norm)) * jnp.max(jnp.abs(k_norm))
    bounded = bound <= MLA_MAX_SHIFT
    shift = jnp.where(bounded, bound, 0.0)
    qaug = (slot_lane == MLA_QK).astype(F32).reshape(1, SLOT)
    kaug = -shift * qaug

    x2 = x.reshape(T, D)
    q, k, v, sq, sk, sv = _proj_call(
        x2, cos_slot, sin_slot,
        (row2(norm1), wcq, wckv, wkr, wsb, row2(q_a_norm), wuq, row2(kv_a_norm), wuk, wuv, vone,
         gq, gk, qaug, kaug, pcos, psin, cbase), tm=PROJ_TM)

    pos_c3 = positions.reshape(B, S, 1)
    pos_r3 = positions.reshape(B, 1, S)
    r3 = lambda a: a.reshape(B, S, a.shape[-1])

    mla_flags = _block_flags(positions, MLA_TQ, MLA_TK, True)
    sb_flags = _block_flags(positions, SB_TQ, SB_TK, False)
    kstart = jnp.max(jnp.where(sb_flags != 0, jnp.arange(S // SB_TK, dtype=jnp.int32), -1),
                     axis=-1).astype(jnp.int32)
    attn_args = (mla_flags, kstart, r3(q), r3(k), r3(v), r3(sq), r3(sk), r3(sv), pos_c3, pos_r3)

    def separate(mla_flags, kstart, q, k, v, sq, sk, sv, pos_c3, pos_r3):
        o_mla = _mla_call(mla_flags, q, k, v, pos_c3, pos_r3, MLA_TQ, MLA_TK, hp=2, online=True)
        o_sb = _sb_call(kstart, sq, sk, sv, pos_c3, pos_r3, SB_TQ, SB_TK, win=SB_WIN)
        return o_mla, o_sb

    o_mla, o_sb = lax.cond(
        bounded,
        lambda *a: tuple(_attn_call(*a, MLA_TQ, MLA_TK, hp=MLA_HEADS_PER_STEP)),
        separate, *attn_args)

    wr_t = w_router.T
    wrh = wr_t.astype(BF16)
    wrl = (wr_t - wrh.astype(F32)).astype(BF16)
    base, hn, eidx, gate, rank, cnt = _merge_call(
        x2, o_mla.reshape(T, MLA_WIDTH), o_sb.reshape(T, SB_WIDTH), row2(out_norm_mla),
        row2(out_norm_sb), w_o.astype(BF16), row2(norm2), wrh, wrl,
        router_bias.reshape(N_EXPERTS, 1).astype(F32), w_shared_gate_up.astype(BF16),
        w_shared_down.astype(BF16), tm=MERGE_TM)

    off, tiles, pos, gates = _route_tables(eidx, gate, rank, cnt)
    out = _moe_call(off, tiles, pos, gates, hn, base, w_gate_up.astype(BF16),
                    w_down.astype(BF16))
    return out.reshape(B, S, D)
```
